```python
import jax
import jax.numpy as jnp
from jax import lax
import numpy as np


D_MODEL = 1024
BATCH = 16
SEQ = 4096
DEPTH = 2

D_FF = 2816
A_HEADS = 8
A_KV_HEADS = 2
A_HEAD_DIM = 64
WINDOW = 128
A_BLOCK = 128
B_HEADS = 4
B_KEY_DIM = 128
B_VAL_DIM = 128
B_CHUNK = 64
C_HEADS = 16
C_Q_RANK = 256
C_KV_RANK = 256
C_NOPE = 64
C_ROPE = 32
C_V = 64
C_QBLOCK = 128
ROPE_THETA = 10000.0
LN_EPS = 1e-5
RMS_EPS = 1e-6
DEEPNORM_ALPHA = (2 * DEPTH) ** 0.25
DEEPNORM_BETA = (8 * DEPTH) ** -0.25
N_EVEN = (DEPTH + 1) // 2
N_ODD = DEPTH // 2
N_SUB = 3
A_Q_W = A_HEADS * A_HEAD_DIM
A_KV_W = A_KV_HEADS * A_HEAD_DIM
B_K_W = B_HEADS * B_KEY_DIM
B_V_W = B_HEADS * B_VAL_DIM
HYB_SPLITS = (A_Q_W, A_KV_W, A_KV_W, B_K_W, B_K_W, B_K_W, B_V_W, B_V_W)
HYB_IN = sum(HYB_SPLITS)
HYB_OUT = A_Q_W + B_V_W
MLA_DOWN = C_Q_RANK + C_KV_RANK + C_ROPE
F32 = jnp.float32

kernel_name = 'hybrid_swa_hgrn2_mla_macaron_deepnorm_adaln'


def layer_norm(x, g, b):
    xf = x.astype(F32)
    mu = jnp.mean(xf, -1, keepdims=True)
    var = jnp.mean(jnp.square(xf - mu), -1, keepdims=True)
    return ((xf - mu) * lax.rsqrt(var + LN_EPS) * g + b).astype(x.dtype)


def rms_norm(x, w):
    xf = x.astype(F32)
    return (xf * lax.rsqrt(jnp.mean(xf * xf, -1, keepdims=True) + RMS_EPS) * w).astype(x.dtype)


def swiglu(h, w_gate, w_up, w_down):
    return (jax.nn.silu(h @ w_gate) * (h @ w_up)) @ w_down


def rope(x, pos):
    half = x.shape[-1] // 2
    freqs = ROPE_THETA ** (-jnp.arange(half, dtype=F32) / half)
    ang = pos.astype(F32)[..., None] * freqs
    ang = ang.reshape(ang.shape[:2] + (1,) * (x.ndim - 3) + (half,))
    cos, sin = jnp.cos(ang), jnp.sin(ang)
    x1, x2 = x[..., :half].astype(F32), x[..., half:].astype(F32)
    return jnp.concatenate([x1 * cos - x2 * sin, x1 * sin + x2 * cos], -1).astype(x.dtype)


def alibi_slopes(n):
    return 2.0 ** (-8.0 * jnp.arange(1, n + 1, dtype=F32) / n)


def window_attention(q, k, v, pos, sink):
    bsz, seq = q.shape[:2]
    nb = seq // A_BLOCK
    grp = A_HEADS // A_KV_HEADS

    def band(t):
        tp = jnp.pad(t, [(0, 0), (A_BLOCK, A_BLOCK)] + [(0, 0)] * (t.ndim - 2))
        tb = tp.reshape((bsz, nb + 2, A_BLOCK) + t.shape[2:])
        return jnp.concatenate([tb[:, :-2], tb[:, 1:-1], tb[:, 2:]], axis=2)

    kb, vb, pk = band(k), band(v), band(pos)
    qb = q.reshape(bsz, nb, A_BLOCK, A_KV_HEADS, grp, A_HEAD_DIM)
    pq = pos.reshape(bsz, nb, A_BLOCK)
    s = jnp.einsum('bnqhgd,bnkhd->bnhgqk', qb, kb).astype(F32) * (A_HEAD_DIM ** -0.5)
    qi = jnp.arange(A_BLOCK)[:, None] + A_BLOCK
    ki = jnp.arange(3 * A_BLOCK)[None, :]
    abs_k = jnp.arange(nb)[:, None, None] * A_BLOCK + ki[None] - A_BLOCK
    valid = (jnp.abs(qi - ki) <= WINDOW)[None] & (abs_k >= 0) & (abs_k < seq)
    dist = jnp.abs(pq[..., :, None] - pk[..., None, :]).astype(F32)
    slopes = alibi_slopes(A_HEADS).reshape(A_KV_HEADS, grp)[None, None, :, :, None, None]
    s = s - slopes * dist[:, :, None, None]
    s = jnp.where(valid[None, :, None, None], s, -jnp.inf)
    sink_l = sink.astype(F32).reshape(A_KV_HEADS, grp)[None, None, :, :, None, None]
    m = jnp.maximum(jnp.max(s, -1, keepdims=True), sink_l)
    p = jnp.exp(s - m)
    p = p / (jnp.sum(p, -1, keepdims=True) + jnp.exp(sink_l - m))
    o = jnp.einsum('bnhgqk,bnkhd->bnqhgd', p.astype(v.dtype), vb)
    return o.reshape(bsz, seq, A_Q_W)


def hgrn2_scan(q, k, v, logf):
    bsz, nh, seq, dk = q.shape
    dv = v.shape[-1]
    n = seq // B_CHUNK
    q, k, logf = [t.reshape(bsz, nh, n, B_CHUNK, dk) for t in (q, k, logf)]
    v = v.reshape(bsz, nh, n, B_CHUNK, dv)
    b = jnp.cumsum(logf, axis=3)
    b_mid = b[:, :, :, B_CHUNK // 2 - 1:B_CHUNK // 2]
    b_end = b[:, :, :, -1:]
    a = jnp.einsum('bhnck,bhnsk->bhncs', q * jnp.exp(b - b_mid), k * jnp.exp(b_mid - b))
    tri = jnp.tril(jnp.ones((B_CHUNK, B_CHUNK), bool))
    o_intra = jnp.einsum('bhncs,bhnsv->bhncv', jnp.where(tri, a, 0.0), v)
    d_state = jnp.einsum('bhnck,bhncv->bhnkv', k * jnp.exp(b_end - b), v)
    decay = jnp.exp(b_end[:, :, :, 0])

    def step(s_prev, inp):
        dec, ds = inp
        return dec[..., None] * s_prev + ds, s_prev

    _, s0 = lax.scan(step, jnp.zeros((bsz, nh, dk, dv), q.dtype),
                     (jnp.moveaxis(decay, 2, 0), jnp.moveaxis(d_state, 2, 0)))
    s0 = jnp.moveaxis(s0, 0, 2)
    o_inter = jnp.einsum('bhnck,bhnkv->bhncv', q * jnp.exp(b), s0)
    return (o_intra + o_inter).reshape(bsz, nh, seq, dv)


def hgrn2_bidir(hq, hff, hfb, hi, hg, lb, norm_w):
    bsz, seq = hq.shape[:2]

    def heads(t, d):
        return t.reshape(bsz, seq, B_HEADS, d).transpose(0, 2, 1, 3).astype(F32)

    q = jax.nn.silu(heads(hq, B_KEY_DIM))
    v = heads(hi, B_VAL_DIM)
    lbh = lb.astype(F32).reshape(B_HEADS, 1, B_KEY_DIM)

    def gates(hf):
        f = lbh + (1.0 - lbh) * jax.nn.sigmoid(heads(hf, B_KEY_DIM))
        return 1.0 - f, jnp.log(f)

    k_f, g_f = gates(hff)
    k_b, g_b = gates(hfb)
    flip = lambda t: jnp.flip(t, axis=2)
    o = hgrn2_scan(q, k_f, v, g_f) + flip(hgrn2_scan(flip(q), flip(k_b), flip(v), flip(g_b)))
    o = rms_norm(o, norm_w[:, None, :])
    o = o.transpose(0, 2, 1, 3).reshape(bsz, seq, B_V_W).astype(hg.dtype)
    return o * jax.nn.silu(hg)


def hybrid_mixer(h, pos, w_in, w_out, sink, lb, norm_w):
    bsz, seq, _ = h.shape
    offs = [int(o) for o in np.cumsum(HYB_SPLITS)[:-1]]
    aq, ak, av, bq, bff, bfb, bi, bg = jnp.split(h @ w_in, offs, axis=-1)
    o_a = window_attention(aq.reshape(bsz, seq, A_HEADS, A_HEAD_DIM),
                           ak.reshape(bsz, seq, A_KV_HEADS, A_HEAD_DIM),
                           av.reshape(bsz, seq, A_KV_HEADS, A_HEAD_DIM), pos, sink)
    o_b = hgrn2_bidir(bq, bff, bfb, bi, bg, lb, norm_w)
    return jnp.concatenate([o_a, o_b], axis=-1) @ w_out


def mla(h, pos, w_down, q_norm, kv_norm, w_uq, w_ukv, w_out):
    bsz, seq, _ = h.shape
    cq, ckv, kr = jnp.split(h @ w_down, [C_Q_RANK, C_Q_RANK + C_KV_RANK], axis=-1)
    q = (rms_norm(cq, q_norm) @ w_uq).reshape(bsz, seq, C_HEADS, C_NOPE + C_ROPE)
    q_nope, q_rope = q[..., :C_NOPE], rope(q[..., C_NOPE:], pos)
    kv = (rms_norm(ckv, kv_norm) @ w_ukv).reshape(bsz, seq, C_HEADS, C_NOPE + C_V)
    k_nope, v = kv[..., :C_NOPE], kv[..., C_NOPE:]
    k_rope = rope(kr, pos)
    scale = (C_NOPE + C_ROPE) ** -0.5
    nq = seq // C_QBLOCK
    qn_b = q_nope.reshape(bsz, nq, C_QBLOCK, C_HEADS, C_NOPE).swapaxes(0, 1)
    qr_b = q_rope.reshape(bsz, nq, C_QBLOCK, C_HEADS, C_ROPE).swapaxes(0, 1)

    def block(args):
        qn_i, qr_i = args
        s = (jnp.einsum('bqhd,bkhd->bhqk', qn_i, k_nope)
             + jnp.einsum('bqhr,bkr->bhqk', qr_i, k_rope)).astype(F32) * scale
        p = jax.nn.softmax(s, axis=-1).astype(v.dtype)
        return jnp.einsum('bhqk,bkhd->bqhd', p, v)

    o = lax.map(block, (qn_b, qr_b))
    return o.swapaxes(0, 1).reshape(bsz, seq, C_HEADS * C_V) @ w_out


def modulated_sublayer(x, mod, g, b, fn, res_w):
    shift, scale, gate = mod[:, 0, None, :], mod[:, 1, None, :], mod[:, 2, None, :]
    y = fn(x * (1.0 + scale) + shift)
    return layer_norm(DEEPNORM_ALPHA * x + res_w * (1.0 + gate) * y, g, b)


def _fwd_setup_inputs(seed: int = 0) -> dict:
    key = jax.random.key(seed)
    ks = jax.random.split(key, 24)
    nrm = lambda k, shape, s: jax.random.normal(k, shape, F32) * s
    d = D_MODEL
    return {
        'x': nrm(ks[0], (BATCH, SEQ, d), 1.0),
        'c': nrm(ks[1], (BATCH, d), 1.0),
        'positions': jnp.broadcast_to(jnp.arange(SEQ, dtype=jnp.int32), (BATCH, SEQ)),
        'ada_w': nrm(ks[2], (DEPTH, d, N_SUB * 3 * d), 0.1 * d ** -0.5),
        'ada_b': nrm(ks[3], (DEPTH, N_SUB * 3 * d), 0.01),
        'ln_g': 1.0 + nrm(ks[4], (DEPTH, N_SUB, d), 0.01),
        'ln_b': nrm(ks[5], (DEPTH, N_SUB, d), 0.01),
        'ffn_w_gate': nrm(ks[6], (DEPTH, 2, d, D_FF), d ** -0.5),
        'ffn_w_up': nrm(ks[7], (DEPTH, 2, d, D_FF), d ** -0.5),
        'ffn_w_down': nrm(ks[8], (DEPTH, 2, D_FF, d), DEEPNORM_BETA * D_FF ** -0.5),
        'hyb_w_in': nrm(ks[9], (N_EVEN, d, HYB_IN), d ** -0.5),
        'hyb_w_out': nrm(ks[10], (N_EVEN, HYB_OUT, d), DEEPNORM_BETA * HYB_OUT ** -0.5),
        'attn_sink': nrm(ks[11], (N_EVEN, A_HEADS), 0.5),
        'hgrn_lb_logits': nrm(ks[12], (DEPTH + 1, B_K_W), 0.1),
        'hgrn_norm_w': 1.0 + nrm(ks[13], (N_EVEN, B_HEADS, B_VAL_DIM), 0.01),
        'mla_w_down': nrm(ks[14], (N_ODD, d, MLA_DOWN), d ** -0.5),
        'mla_q_norm': 1.0 + nrm(ks[15], (N_ODD, C_Q_RANK), 0.01),
        'mla_kv_norm': 1.0 + nrm(ks[16], (N_ODD, C_KV_RANK), 0.01),
        'mla_w_uq': nrm(ks[17], (N_ODD, C_Q_RANK, C_HEADS * (C_NOPE + C_ROPE)), C_Q_RANK ** -0.5),
        'mla_w_ukv': nrm(ks[18], (N_ODD, C_KV_RANK, C_HEADS * (C_NOPE + C_V)), C_KV_RANK ** -0.5),
        'mla_w_out': nrm(ks[19], (N_ODD, C_HEADS * C_V, d), DEEPNORM_BETA * (C_HEADS * C_V) ** -0.5),
    }


def _fwd_reference(x, c, positions, ada_w, ada_b, ln_g, ln_b, ffn_w_gate, ffn_w_up, ffn_w_down,
              hyb_w_in, hyb_w_out, attn_sink, hgrn_lb_logits, hgrn_norm_w,
              mla_w_down, mla_q_norm, mla_kv_norm, mla_w_uq, mla_w_ukv, mla_w_out):
    bsz = x.shape[0]
    cond = jax.nn.silu(c)
    lb_all = jnp.cumsum(jax.nn.softmax(hgrn_lb_logits.astype(F32), axis=0), axis=0)
    for layer in range(DEPTH):
        mod = (cond @ ada_w[layer] + ada_b[layer]).reshape(bsz, N_SUB, 3, D_MODEL)
        ffn_pre = lambda h, l=layer: swiglu(h, ffn_w_gate[l, 0], ffn_w_up[l, 0], ffn_w_down[l, 0])
        ffn_post = lambda h, l=layer: swiglu(h, ffn_w_gate[l, 1], ffn_w_up[l, 1], ffn_w_down[l, 1])
        if layer % 2 == 0:
            e = layer // 2
            mixer = lambda h, e=e, l=layer: hybrid_mixer(h, positions, hyb_w_in[e], hyb_w_out[e],
                                                         attn_sink[e], lb_all[l], hgrn_norm_w[e])
        else:
            o = layer // 2
            mixer = lambda h, o=o: mla(h, positions, mla_w_down[o], mla_q_norm[o], mla_kv_norm[o],
                                       mla_w_uq[o], mla_w_ukv[o], mla_w_out[o])
        x = modulated_sublayer(x, mod[:, 0], ln_g[layer, 0], ln_b[layer, 0], ffn_pre, 0.5)
        x = modulated_sublayer(x, mod[:, 1], ln_g[layer, 1], ln_b[layer, 1], mixer, 1.0)
        x = modulated_sublayer(x, mod[:, 2], ln_g[layer, 2], ln_b[layer, 2], ffn_post, 0.5)
    return x


import jax as _jax
import jax.numpy as _jnp

TWIN_FORMAT = 'train_step'
FWD_PARAMS = ['x', 'c', 'positions', 'ada_w', 'ada_b', 'ln_g', 'ln_b', 'ffn_w_gate', 'ffn_w_up', 'ffn_w_down', 'hyb_w_in', 'hyb_w_out', 'attn_sink', 'hgrn_lb_logits', 'hgrn_norm_w', 'mla_w_down', 'mla_q_norm', 'mla_kv_norm', 'mla_w_uq', 'mla_w_ukv', 'mla_w_out']
TWIN_WEIGHTS = ['ada_w', 'ada_b', 'ln_g', 'ln_b', 'ffn_w_gate', 'ffn_w_up', 'ffn_w_down', 'hyb_w_in', 'hyb_w_out', 'attn_sink', 'hgrn_lb_logits', 'hgrn_norm_w', 'mla_w_down', 'mla_q_norm', 'mla_kv_norm', 'mla_w_uq', 'mla_w_ukv', 'mla_w_out']
TWIN_DIFF_INPUT = 'x'
TWIN_INPUTS = ['x', 'c', 'positions', 'ada_w', 'ada_b', 'ln_g', 'ln_b', 'ffn_w_gate', 'ffn_w_up', 'ffn_w_down', 'hyb_w_in', 'hyb_w_out', 'attn_sink', 'hgrn_lb_logits', 'hgrn_norm_w', 'mla_w_down', 'mla_q_norm', 'mla_kv_norm', 'mla_w_uq', 'mla_w_ukv', 'mla_w_out', 'loss_target', 'm_ada_w', 'm_ada_b', 'm_ln_g', 'm_ln_b', 'm_ffn_w_gate', 'm_ffn_w_up', 'm_ffn_w_down', 'm_hyb_w_in', 'm_hyb_w_out', 'm_attn_sink', 'm_hgrn_lb_logits', 'm_hgrn_norm_w', 'm_mla_w_down', 'm_mla_q_norm', 'm_mla_kv_norm', 'm_mla_w_uq', 'm_mla_w_ukv', 'm_mla_w_out', 'v_ada_w', 'v_ada_b', 'v_ln_g', 'v_ln_b', 'v_ffn_w_gate', 'v_ffn_w_up', 'v_ffn_w_down', 'v_hyb_w_in', 'v_hyb_w_out', 'v_attn_sink', 'v_hgrn_lb_logits', 'v_hgrn_norm_w', 'v_mla_w_down', 'v_mla_q_norm', 'v_mla_kv_norm', 'v_mla_w_uq', 'v_mla_w_ukv', 'v_mla_w_out']
TWIN_OUTPUTS = ['loss', 'grad_x', 'grad_ada_w', 'grad_ada_b', 'grad_ln_g', 'grad_ln_b', 'grad_ffn_w_gate', 'grad_ffn_w_up', 'grad_ffn_w_down', 'grad_hyb_w_in', 'grad_hyb_w_out', 'grad_attn_sink', 'grad_hgrn_lb_logits', 'grad_hgrn_norm_w', 'grad_mla_w_down', 'grad_mla_q_norm', 'grad_mla_kv_norm', 'grad_mla_w_uq', 'grad_mla_w_ukv', 'grad_mla_w_out', 'delta_ada_w', 'delta_ada_b', 'delta_ln_g', 'delta_ln_b', 'delta_ffn_w_gate', 'delta_ffn_w_up', 'delta_ffn_w_down', 'delta_hyb_w_in', 'delta_hyb_w_out', 'delta_attn_sink', 'delta_hgrn_lb_logits', 'delta_hgrn_norm_w', 'delta_mla_w_down', 'delta_mla_q_norm', 'delta_mla_kv_norm', 'delta_mla_w_uq', 'delta_mla_w_ukv', 'delta_mla_w_out', 'new_m_ada_w', 'new_m_ada_b', 'new_m_ln_g', 'new_m_ln_b', 'new_m_ffn_w_gate', 'new_m_ffn_w_up', 'new_m_ffn_w_down', 'new_m_hyb_w_in', 'new_m_hyb_w_out', 'new_m_attn_sink', 'new_m_hgrn_lb_logits', 'new_m_hgrn_norm_w', 'new_m_mla_w_down', 'new_m_mla_q_norm', 'new_m_mla_kv_norm', 'new_m_mla_w_uq', 'new_m_mla_w_ukv', 'new_m_mla_w_out', 'new_v_ada_w', 'new_v_ada_b', 'new_v_ln_g', 'new_v_ln_b', 'new_v_ffn_w_gate', 'new_v_ffn_w_up', 'new_v_ffn_w_down', 'new_v_hyb_w_in', 'new_v_hyb_w_out', 'new_v_attn_sink', 'new_v_hgrn_lb_logits', 'new_v_hgrn_norm_w', 'new_v_mla_w_down', 'new_v_mla_q_norm', 'new_v_mla_kv_norm', 'new_v_mla_w_uq', 'new_v_mla_w_ukv', 'new_v_mla_w_out']
TWIN_LEAF_KINDS = {'loss': 'loss', 'grad_x': 'grad_x', 'grad_ada_w': 'grad_w', 'grad_ada_b': 'grad_w', 'grad_ln_g': 'grad_w', 'grad_ln_b': 'grad_w', 'grad_ffn_w_gate': 'grad_w', 'grad_ffn_w_up': 'grad_w', 'grad_ffn_w_down': 'grad_w', 'grad_hyb_w_in': 'grad_w', 'grad_hyb_w_out': 'grad_w', 'grad_attn_sink': 'grad_w', 'grad_hgrn_lb_logits': 'grad_w', 'grad_hgrn_norm_w': 'grad_w', 'grad_mla_w_down': 'grad_w', 'grad_mla_q_norm': 'grad_w', 'grad_mla_kv_norm': 'grad_w', 'grad_mla_w_uq': 'grad_w', 'grad_mla_w_ukv': 'grad_w', 'grad_mla_w_out': 'grad_w', 'delta_ada_w': 'delta_w', 'delta_ada_b': 'delta_w', 'delta_ln_g': 'delta_w', 'delta_ln_b': 'delta_w', 'delta_ffn_w_gate': 'delta_w', 'delta_ffn_w_up': 'delta_w', 'delta_ffn_w_down': 'delta_w', 'delta_hyb_w_in': 'delta_w', 'delta_hyb_w_out': 'delta_w', 'delta_attn_sink': 'delta_w', 'delta_hgrn_lb_logits': 'delta_w', 'delta_hgrn_norm_w': 'delta_w', 'delta_mla_w_down': 'delta_w', 'delta_mla_q_norm': 'delta_w', 'delta_mla_kv_norm': 'delta_w', 'delta_mla_w_uq': 'delta_w', 'delta_mla_w_ukv': 'delta_w', 'delta_mla_w_out': 'delta_w', 'new_m_ada_w': 'new_m', 'new_m_ada_b': 'new_m', 'new_m_ln_g': 'new_m', 'new_m_ln_b': 'new_m', 'new_m_ffn_w_gate': 'new_m', 'new_m_ffn_w_up': 'new_m', 'new_m_ffn_w_down': 'new_m', 'new_m_hyb_w_in': 'new_m', 'new_m_hyb_w_out': 'new_m', 'new_m_attn_sink': 'new_m', 'new_m_hgrn_lb_logits': 'new_m', 'new_m_hgrn_norm_w': 'new_m', 'new_m_mla_w_down': 'new_m', 'new_m_mla_q_norm': 'new_m', 'new_m_mla_kv_norm': 'new_m', 'new_m_mla_w_uq': 'new_m', 'new_m_mla_w_ukv': 'new_m', 'new_m_mla_w_out': 'new_m', 'new_v_ada_w': 'new_v', 'new_v_ada_b': 'new_v', 'new_v_ln_g': 'new_v', 'new_v_ln_b': 'new_v', 'new_v_ffn_w_gate': 'new_v', 'new_v_ffn_w_up': 'new_v', 'new_v_ffn_w_down': 'new_v', 'new_v_hyb_w_in': 'new_v', 'new_v_hyb_w_out': 'new_v', 'new_v_attn_sink': 'new_v', 'new_v_hgrn_lb_logits': 'new_v', 'new_v_hgrn_norm_w': 'new_v', 'new_v_mla_w_down': 'new_v', 'new_v_mla_q_norm': 'new_v', 'new_v_mla_kv_norm': 'new_v', 'new_v_mla_w_uq': 'new_v', 'new_v_mla_w_ukv': 'new_v', 'new_v_mla_w_out': 'new_v'}


def _forward(args):
    return _fwd_reference(*[args[k] for k in FWD_PARAMS])


def _output_shape():
    out = _jax.eval_shape(lambda: _forward(_fwd_setup_inputs(0)))
    return out.shape, out.dtype

N_MICROBATCH = 1
ADAM_LR = 0.001
ADAM_B1 = 0.9
ADAM_B2 = 0.999
ADAM_EPS = 1e-08
ADAM_WD = 0.01
ADAM_STEP = 10
PER_EXAMPLE_BATCH_AXIS = {'x': 0, 'c': 0, 'positions': 0, 'loss_target': 0}
SHARED_INPUTS = []
_WEIGHT_DTYPES = {'ada_w': _jnp.float32, 'ada_b': _jnp.float32, 'ln_g': _jnp.float32, 'ln_b': _jnp.float32, 'ffn_w_gate': _jnp.float32, 'ffn_w_up': _jnp.float32, 'ffn_w_down': _jnp.float32, 'hyb_w_in': _jnp.float32, 'hyb_w_out': _jnp.float32, 'attn_sink': _jnp.float32, 'hgrn_lb_logits': _jnp.float32, 'hgrn_norm_w': _jnp.float32, 'mla_w_down': _jnp.float32, 'mla_q_norm': _jnp.float32, 'mla_kv_norm': _jnp.float32, 'mla_w_uq': _jnp.float32, 'mla_w_ukv': _jnp.float32, 'mla_w_out': _jnp.float32}
MOMENT_SCALE = {'ada_w': 3.139823e-02, 'ada_b': 5.824599e-02, 'ln_g': 2.605197e+01, 'ln_b': 5.650485e-01, 'ffn_w_gate': 1.669764e-02, 'ffn_w_up': 1.616659e-02, 'ffn_w_down': 5.365190e-02, 'hyb_w_in': 3.305984e-02, 'hyb_w_out': 8.178895e-02, 'attn_sink': 3.037079e-02, 'hgrn_lb_logits': 2.123036e-03, 'hgrn_norm_w': 5.469060e-02, 'mla_w_down': 2.434603e-02, 'mla_q_norm': 2.033992e-02, 'mla_kv_norm': 3.200845e-02, 'mla_w_uq': 8.255512e-03, 'mla_w_ukv': 1.053760e-02, 'mla_w_out': 2.433603e-02}


def _to_microbatches(a, axis):
    t = _jnp.moveaxis(a, axis, 0)
    t = t.reshape((N_MICROBATCH, t.shape[0] // N_MICROBATCH) + t.shape[1:])
    return _jnp.moveaxis(t, 1, axis + 1)


def setup_inputs(seed: int = 0) -> dict:
    inp = _fwd_setup_inputs(seed)
    key = _jax.random.fold_in(_jax.random.key(seed), 7919)
    shape, _ = _output_shape()
    out = dict(inp)
    out["loss_target"] = _jax.random.normal(_jax.random.fold_in(key, 0), shape, _jnp.float32)
    for i, name in enumerate(TWIN_WEIGHTS):
        w = inp[name].astype(_jnp.float32)
        if MOMENT_SCALE is None:
            s = _jnp.sqrt(_jnp.mean(_jnp.square(w)) + 1e-30)
        else:
            s = MOMENT_SCALE[name]
        km, kv = _jax.random.split(_jax.random.fold_in(key, i + 1))
        out[name] = w
        out["m_" + name] = s * _jax.random.normal(km, w.shape, _jnp.float32)
        out["v_" + name] = (s * s) * _jax.random.uniform(kv, w.shape, _jnp.float32, 0.5, 1.5)
    if N_MICROBATCH > 1:
        for name, axis in PER_EXAMPLE_BATCH_AXIS.items():
            out[name] = _to_microbatches(out[name], axis)
    return {'x': out['x'], 'c': out['c'], 'positions': out['positions'], 'ada_w': out['ada_w'], 'ada_b': out['ada_b'], 'ln_g': out['ln_g'], 'ln_b': out['ln_b'], 'ffn_w_gate': out['ffn_w_gate'], 'ffn_w_up': out['ffn_w_up'], 'ffn_w_down': out['ffn_w_down'], 'hyb_w_in': out['hyb_w_in'], 'hyb_w_out': out['hyb_w_out'], 'attn_sink': out['attn_sink'], 'hgrn_lb_logits': out['hgrn_lb_logits'], 'hgrn_norm_w': out['hgrn_norm_w'], 'mla_w_down': out['mla_w_down'], 'mla_q_norm': out['mla_q_norm'], 'mla_kv_norm': out['mla_kv_norm'], 'mla_w_uq': out['mla_w_uq'], 'mla_w_ukv': out['mla_w_ukv'], 'mla_w_out': out['mla_w_out'], 'loss_target': out['loss_target'], 'm_ada_w': out['m_ada_w'], 'm_ada_b': out['m_ada_b'], 'm_ln_g': out['m_ln_g'], 'm_ln_b': out['m_ln_b'], 'm_ffn_w_gate': out['m_ffn_w_gate'], 'm_ffn_w_up': out['m_ffn_w_up'], 'm_ffn_w_down': out['m_ffn_w_down'], 'm_hyb_w_in': out['m_hyb_w_in'], 'm_hyb_w_out': out['m_hyb_w_out'], 'm_attn_sink': out['m_attn_sink'], 'm_hgrn_lb_logits': out['m_hgrn_lb_logits'], 'm_hgrn_norm_w': out['m_hgrn_norm_w'], 'm_mla_w_down': out['m_mla_w_down'], 'm_mla_q_norm': out['m_mla_q_norm'], 'm_mla_kv_norm': out['m_mla_kv_norm'], 'm_mla_w_uq': out['m_mla_w_uq'], 'm_mla_w_ukv': out['m_mla_w_ukv'], 'm_mla_w_out': out['m_mla_w_out'], 'v_ada_w': out['v_ada_w'], 'v_ada_b': out['v_ada_b'], 'v_ln_g': out['v_ln_g'], 'v_ln_b': out['v_ln_b'], 'v_ffn_w_gate': out['v_ffn_w_gate'], 'v_ffn_w_up': out['v_ffn_w_up'], 'v_ffn_w_down': out['v_ffn_w_down'], 'v_hyb_w_in': out['v_hyb_w_in'], 'v_hyb_w_out': out['v_hyb_w_out'], 'v_attn_sink': out['v_attn_sink'], 'v_hgrn_lb_logits': out['v_hgrn_lb_logits'], 'v_hgrn_norm_w': out['v_hgrn_norm_w'], 'v_mla_w_down': out['v_mla_w_down'], 'v_mla_q_norm': out['v_mla_q_norm'], 'v_mla_kv_norm': out['v_mla_kv_norm'], 'v_mla_w_uq': out['v_mla_w_uq'], 'v_mla_w_ukv': out['v_mla_w_ukv'], 'v_mla_w_out': out['v_mla_w_out']}


def _loss(weights, diff, rest, loss_target):
    with _jax.named_scope("forward"):
        args = {**rest, TWIN_DIFF_INPUT: diff, **{k: w.astype(_WEIGHT_DTYPES[k]) for k, w in weights.items()}}
        y = _forward(args)
    with _jax.named_scope("loss_head"):
        err = _jnp.square(y.astype(_jnp.float32) - loss_target)
        return 0.5 * _jnp.sum(_jnp.mean(err, axis=-1)) if err.ndim else 0.5 * err


def _adamw(w, g, m, v):
    m = ADAM_B1 * m + (1.0 - ADAM_B1) * g
    v = ADAM_B2 * v + (1.0 - ADAM_B2) * _jnp.square(g)
    m_hat = m / (1.0 - ADAM_B1 ** ADAM_STEP)
    v_hat = v / (1.0 - ADAM_B2 ** ADAM_STEP)
    delta = -ADAM_LR * (m_hat / (_jnp.sqrt(v_hat) + ADAM_EPS) + ADAM_WD * w)
    return delta, m, v


def reference(x, c, positions, ada_w, ada_b, ln_g, ln_b, ffn_w_gate, ffn_w_up, ffn_w_down, hyb_w_in, hyb_w_out, attn_sink, hgrn_lb_logits, hgrn_norm_w, mla_w_down, mla_q_norm, mla_kv_norm, mla_w_uq, mla_w_ukv, mla_w_out, loss_target, m_ada_w, m_ada_b, m_ln_g, m_ln_b, m_ffn_w_gate, m_ffn_w_up, m_ffn_w_down, m_hyb_w_in, m_hyb_w_out, m_attn_sink, m_hgrn_lb_logits, m_hgrn_norm_w, m_mla_w_down, m_mla_q_norm, m_mla_kv_norm, m_mla_w_uq, m_mla_w_ukv, m_mla_w_out, v_ada_w, v_ada_b, v_ln_g, v_ln_b, v_ffn_w_gate, v_ffn_w_up, v_ffn_w_down, v_hyb_w_in, v_hyb_w_out, v_attn_sink, v_hgrn_lb_logits, v_hgrn_norm_w, v_mla_w_down, v_mla_q_norm, v_mla_kv_norm, v_mla_w_uq, v_mla_w_ukv, v_mla_w_out):
    given = dict(x=x, c=c, positions=positions, ada_w=ada_w, ada_b=ada_b, ln_g=ln_g, ln_b=ln_b, ffn_w_gate=ffn_w_gate, ffn_w_up=ffn_w_up, ffn_w_down=ffn_w_down, hyb_w_in=hyb_w_in, hyb_w_out=hyb_w_out, attn_sink=attn_sink, hgrn_lb_logits=hgrn_lb_logits, hgrn_norm_w=hgrn_norm_w, mla_w_down=mla_w_down, mla_q_norm=mla_q_norm, mla_kv_norm=mla_kv_norm, mla_w_uq=mla_w_uq, mla_w_ukv=mla_w_ukv, mla_w_out=mla_w_out, loss_target=loss_target, m_ada_w=m_ada_w, m_ada_b=m_ada_b, m_ln_g=m_ln_g, m_ln_b=m_ln_b, m_ffn_w_gate=m_ffn_w_gate, m_ffn_w_up=m_ffn_w_up, m_ffn_w_down=m_ffn_w_down, m_hyb_w_in=m_hyb_w_in, m_hyb_w_out=m_hyb_w_out, m_attn_sink=m_attn_sink, m_hgrn_lb_logits=m_hgrn_lb_logits, m_hgrn_norm_w=m_hgrn_norm_w, m_mla_w_down=m_mla_w_down, m_mla_q_norm=m_mla_q_norm, m_mla_kv_norm=m_mla_kv_norm, m_mla_w_uq=m_mla_w_uq, m_mla_w_ukv=m_mla_w_ukv, m_mla_w_out=m_mla_w_out, v_ada_w=v_ada_w, v_ada_b=v_ada_b, v_ln_g=v_ln_g, v_ln_b=v_ln_b, v_ffn_w_gate=v_ffn_w_gate, v_ffn_w_up=v_ffn_w_up, v_ffn_w_down=v_ffn_w_down, v_hyb_w_in=v_hyb_w_in, v_hyb_w_out=v_hyb_w_out, v_attn_sink=v_attn_sink, v_hgrn_lb_logits=v_hgrn_lb_logits, v_hgrn_norm_w=v_hgrn_norm_w, v_mla_w_down=v_mla_w_down, v_mla_q_norm=v_mla_q_norm, v_mla_kv_norm=v_mla_kv_norm, v_mla_w_uq=v_mla_w_uq, v_mla_w_ukv=v_mla_w_ukv, v_mla_w_out=v_mla_w_out)
    weights = {n: given[n] for n in TWIN_WEIGHTS}
    shared = {n: given[n] for n in SHARED_INPUTS}
    per_example = {n: given[n] for n in ['x', 'c', 'positions']}
    grad_fn = _jax.value_and_grad(_loss, argnums=(0, 1))

    def one_microbatch(ex, loss_target):
        ex = dict(ex)
        diff = ex.pop(TWIN_DIFF_INPUT)
        return grad_fn(weights, diff, {**shared, **ex}, loss_target)

    if N_MICROBATCH == 1:
        loss, (grad_w, grad_x) = one_microbatch(per_example, given["loss_target"])
    else:
        def body(carry, xs):
            loss_sum, grad_sum = carry
            l_k, (gw_k, gx_k) = one_microbatch(xs[0], xs[1])
            with _jax.named_scope("update"):
                return (loss_sum + l_k, _jax.tree.map(_jnp.add, grad_sum, gw_k)), gx_k

        init = (_jnp.zeros((), _jnp.float32), _jax.tree.map(_jnp.zeros_like, weights))
        (loss, grad_w), grad_x = _jax.lax.scan(body, init, (per_example, given["loss_target"]))
    with _jax.named_scope("update"):
        delta_w, new_m, new_v = {}, {}, {}
        for n in TWIN_WEIGHTS:
            delta_w[n], new_m[n], new_v[n] = _adamw(weights[n], grad_w[n], given["m_" + n], given["v_" + n])
    return (loss, grad_x, *[grad_w[n] for n in TWIN_WEIGHTS], *[delta_w[n] for n in TWIN_WEIGHTS],
            *[new_m[n] for n in TWIN_WEIGHTS], *[new_v[n] for n in TWIN_WEIGHTS])
```

```python
import functools
import math

import jax
import jax.numpy as jnp
import numpy as np
from jax import lax
from jax.experimental import pallas as pl
from jax.experimental.pallas import tpu as pltpu

F32 = jnp.float32
BF16 = jnp.bfloat16

D_MODEL = 1024
DEPTH = 2
D_FF = 2816
A_HEADS = 8
A_KV_HEADS = 2
A_HEAD_DIM = 64
WINDOW = 128
A_BLOCK = 128
B_HEADS = 4
B_KEY_DIM = 128
B_CHUNK = 64
C_HEADS = 16
C_Q_RANK = 256
C_KV_RANK = 256
C_NOPE = 64
C_ROPE = 32
C_V = 64
ROPE_THETA = 10000.0
LN_EPS = 1e-5
RMS_EPS = 1e-6
DEEPNORM_ALPHA = (2 * DEPTH) ** 0.25
N_SUB = 3
N_DEV = 8
LANES = 128
HYB_PAD = 3840
MO_PAD = 1536
MLA_DOWN_PAD = 640
MLA_HEAD_PAD = C_HEADS * LANES
ADAM_LR = 0.001
ADAM_B1 = 0.9
ADAM_B2 = 0.999
ADAM_EPS = 1e-08
ADAM_WD = 0.01
ADAM_STEP = 10
VMEM_LIMIT = 48 * 1024 * 1024
MESH_ID = pl.DeviceIdType.MESH
HIGHEST = lax.Precision.HIGHEST


def _cparams(sem=None):
    return pltpu.CompilerParams(dimension_semantics=sem, vmem_limit_bytes=VMEM_LIMIT)


def _tile(n, cap):
    if n <= cap:
        return n
    t = (cap // LANES) * LANES
    while t >= LANES:
        if n % t == 0:
            return t
        t -= LANES
    return n


def _rows_tile(n, cap):
    if n <= cap:
        return n
    t = cap
    while t >= 8:
        if n % t == 0:
            return t
        t -= 8
    return n


def _sigmoid(x):
    return 1.0 / (1.0 + jnp.exp(-x))


def _dot(a, b, dims, precision=None):
    return lax.dot_general(a, b, (dims, ((), ())), precision=precision,
                           preferred_element_type=F32)


NN = ((1,), (0,))
NT = ((1,), (1,))
TN = ((0,), (0,))


def matmul(a, b, form, out_dtype, name, bias=None, tm_cap=512, tn_cap=1024, tk_cap=1024):
    if form == "nn":
        (m, k), (k2, n) = a.shape, b.shape
    elif form == "nt":
        (m, k), (n, k2) = a.shape, b.shape
    else:
        (k, m), (k2, n) = a.shape, b.shape
    assert k == k2, (a.shape, b.shape, form)
    tm = _tile(m, tm_cap)
    tn = _tile(n, tn_cap)
    tk = k if k <= 4096 else _tile(k, tk_cap)
    nk = k // tk
    dims = {"nn": NN, "nt": NT, "tn": TN}[form]

    def body(*refs):
        if bias is None:
            a_ref, b_ref, o_ref, acc_ref = refs
            bias_ref = None
        else:
            a_ref, b_ref, bias_ref, o_ref, acc_ref = refs
        kk = pl.program_id(2)
        part = _dot(a_ref[...].astype(BF16), b_ref[...].astype(BF16), dims)

        def finish(total):
            if bias_ref is not None:
                total = total + bias_ref[...]
            o_ref[...] = total.astype(o_ref.dtype)

        if nk == 1:
            finish(part)
        else:
            @pl.when(kk == 0)
            def _():
                acc_ref[...] = part

            @pl.when(jnp.logical_and(kk > 0, kk < nk - 1))
            def _():
                acc_ref[...] += part

            @pl.when(kk == nk - 1)
            def _():
                finish(acc_ref[...] + part)

    if form == "nn":
        a_spec = pl.BlockSpec((tm, tk), lambda i, j, kk: (i, kk))
        b_spec = pl.BlockSpec((tk, tn), lambda i, j, kk: (kk, j))
    elif form == "nt":
        a_spec = pl.BlockSpec((tm, tk), lambda i, j, kk: (i, kk))
        b_spec = pl.BlockSpec((tn, tk), lambda i, j, kk: (j, kk))
    else:
        a_spec = pl.BlockSpec((tk, tm), lambda i, j, kk: (kk, i))
        b_spec = pl.BlockSpec((tk, tn), lambda i, j, kk: (kk, j))
    in_specs = [a_spec, b_spec]
    args = [a, b]
    if bias is not None:
        in_specs.append(pl.BlockSpec((1, tn), lambda i, j, kk: (0, j)))
        args.append(bias.reshape(1, n))
    return pl.pallas_call(
        body, name=name,
        out_shape=jax.ShapeDtypeStruct((m, n), out_dtype),
        grid=(m // tm, n // tn, nk),
        in_specs=in_specs,
        out_specs=pl.BlockSpec((tm, tn), lambda i, j, kk: (i, j)),
        scratch_shapes=[pltpu.VMEM((tm, tn), F32)],
        compiler_params=_cparams(("parallel", "parallel", "arbitrary")),
    )(*args)


TOK_TILE = 256


def _vec_spec(d):
    return pl.BlockSpec((1, 1, d), lambda b, s: (b, 0, 0))


def _tok_spec(ts, d):
    return pl.BlockSpec((1, ts, d), lambda b, s: (b, s, 0))


def mod_fwd(x, scale, shift):
    bsz, seq, d = x.shape
    ts = _rows_tile(seq, TOK_TILE)

    def body(x_ref, sc_ref, sh_ref, h_ref):
        h_ref[0] = (x_ref[0] * (1.0 + sc_ref[0]) + sh_ref[0]).astype(BF16)

    return pl.pallas_call(
        body, name="mod_fwd",
        out_shape=jax.ShapeDtypeStruct((bsz, seq, d), BF16),
        grid=(bsz, seq // ts),
        in_specs=[_tok_spec(ts, d), _vec_spec(d), _vec_spec(d)],
        out_specs=_tok_spec(ts, d),
        compiler_params=_cparams(("parallel", "parallel")),
    )(x, scale, shift)


def _ln_stats(z):
    mu = jnp.mean(z, axis=-1, keepdims=True)
    zc = z - mu
    var = jnp.mean(zc * zc, axis=-1, keepdims=True)
    return zc, lax.rsqrt(var + LN_EPS)


def resln_fwd(x, y, gate, g, b, res_w, nxt):
    bsz, seq, d = x.shape
    ts = _rows_tile(seq, TOK_TILE)

    def body(*refs):
        if nxt is None:
            x_ref, y_ref, gt_ref, g_ref, b_ref, xn_ref = refs
        else:
            x_ref, y_ref, gt_ref, g_ref, b_ref, sc_ref, sh_ref, xn_ref, hn_ref = refs
        z = DEEPNORM_ALPHA * x_ref[0] + (res_w * (1.0 + gt_ref[0])) * y_ref[0]
        zc, r = _ln_stats(z)
        xn = zc * r * g_ref[...] + b_ref[...]
        xn_ref[0] = xn
        if nxt is not None:
            hn_ref[0] = (xn * (1.0 + sc_ref[0]) + sh_ref[0]).astype(BF16)

    row = pl.BlockSpec((1, d), lambda bb, s: (0, 0))
    in_specs = [_tok_spec(ts, d), _tok_spec(ts, d), _vec_spec(d), row, row]
    args = [x, y, gate, g.reshape(1, d), b.reshape(1, d)]
    out_shape = [jax.ShapeDtypeStruct((bsz, seq, d), F32)]
    out_specs = [_tok_spec(ts, d)]
    if nxt is not None:
        in_specs += [_vec_spec(d), _vec_spec(d)]
        args += list(nxt)
        out_shape.append(jax.ShapeDtypeStruct((bsz, seq, d), BF16))
        out_specs.append(_tok_spec(ts, d))
    out = pl.pallas_call(
        body, name="resln_fwd",
        out_shape=out_shape, grid=(bsz, seq // ts),
        in_specs=in_specs, out_specs=out_specs,
        compiler_params=_cparams(("parallel", "parallel")),
    )(*args)
    return (out[0], None) if nxt is None else (out[0], out[1])


def resln_bwd(gout, x, y, gate, g, res_w):
    bsz, seq, d = x.shape
    ts = _rows_tile(seq, TOK_TILE)

    def body(go_ref, x_ref, y_ref, gt_ref, g_ref, dxa_ref, dy_ref, dgt_ref, dg_ref, db_ref):
        bb, s = pl.program_id(0), pl.program_id(1)
        yv = y_ref[0]
        rw = res_w * (1.0 + gt_ref[0])
        z = DEEPNORM_ALPHA * x_ref[0] + rw * yv
        zc, r = _ln_stats(z)
        xhat = zc * r
        go = go_ref[0]
        dxh = go * g_ref[...]
        dz = r * (dxh - jnp.mean(dxh, axis=-1, keepdims=True)
                  - xhat * jnp.mean(dxh * xhat, axis=-1, keepdims=True))
        dxa_ref[0] = DEEPNORM_ALPHA * dz
        dy_ref[0] = (rw * dz).astype(BF16)
        dgt = res_w * jnp.sum(dz * yv, axis=0, keepdims=True)
        dg = jnp.sum(go * xhat, axis=0, keepdims=True)
        db = jnp.sum(go, axis=0, keepdims=True)

        @pl.when(s == 0)
        def _():
            dgt_ref[0] = dgt

        @pl.when(s > 0)
        def _():
            dgt_ref[0] += dgt

        first = jnp.logical_and(bb == 0, s == 0)

        @pl.when(first)
        def _():
            dg_ref[...] = dg
            db_ref[...] = db

        @pl.when(jnp.logical_not(first))
        def _():
            dg_ref[...] += dg
            db_ref[...] += db

    row = pl.BlockSpec((1, d), lambda bb, s: (0, 0))
    return pl.pallas_call(
        body, name="resln_bwd",
        out_shape=[jax.ShapeDtypeStruct((bsz, seq, d), F32),
                   jax.ShapeDtypeStruct((bsz, seq, d), BF16),
                   jax.ShapeDtypeStruct((bsz, 1, d), F32),
                   jax.ShapeDtypeStruct((1, d), F32),
                   jax.ShapeDtypeStruct((1, d), F32)],
        grid=(bsz, seq // ts),
        in_specs=[_tok_spec(ts, d), _tok_spec(ts, d), _tok_spec(ts, d), _vec_spec(d), row],
        out_specs=[_tok_spec(ts, d), _tok_spec(ts, d), _vec_spec(d), row, row],
        compiler_params=_cparams(("arbitrary", "arbitrary")),
    )(gout, x, y, gate, g.reshape(1, d))


def mod_bwd(dxa, dh, x, scale):
    bsz, seq, d = x.shape
    ts = _rows_tile(seq, TOK_TILE)

    def body(dxa_ref, dh_ref, x_ref, sc_ref, dx_ref, dsc_ref, dsh_ref):
        s = pl.program_id(1)
        dh = dh_ref[0].astype(F32)
        dx_ref[0] = dxa_ref[0] + dh * (1.0 + sc_ref[0])
        dsc = jnp.sum(dh * x_ref[0], axis=0, keepdims=True)
        dsh = jnp.sum(dh, axis=0, keepdims=True)

        @pl.when(s == 0)
        def _():
            dsc_ref[0] = dsc
            dsh_ref[0] = dsh

        @pl.when(s > 0)
        def _():
            dsc_ref[0] += dsc
            dsh_ref[0] += dsh

    return pl.pallas_call(
        body, name="mod_bwd",
        out_shape=[jax.ShapeDtypeStruct((bsz, seq, d), F32),
                   jax.ShapeDtypeStruct((bsz, 1, d), F32),
                   jax.ShapeDtypeStruct((bsz, 1, d), F32)],
        grid=(bsz, seq // ts),
        in_specs=[_tok_spec(ts, d), _tok_spec(ts, d), _tok_spec(ts, d), _vec_spec(d)],
        out_specs=[_tok_spec(ts, d), _vec_spec(d), _vec_spec(d)],
        compiler_params=_cparams(("parallel", "arbitrary")),
    )(dxa, dh, x, scale)


def swiglu_fwd(gu):
    t, two_f = gu.shape
    f = two_f // 2
    tt = _rows_tile(t, TOK_TILE)

    def body(g_ref, u_ref, a_ref):
        g = g_ref[...]
        a_ref[...] = (g * _sigmoid(g) * u_ref[...]).astype(BF16)

    return pl.pallas_call(
        body, name="swiglu_fwd",
        out_shape=jax.ShapeDtypeStruct((t, f), BF16),
        grid=(t // tt,),
        in_specs=[pl.BlockSpec((tt, f), lambda i: (i, 0)), pl.BlockSpec((tt, f), lambda i: (i, 1))],
        out_specs=pl.BlockSpec((tt, f), lambda i: (i, 0)),
        compiler_params=_cparams(("parallel",)),
    )(gu, gu)


def swiglu_bwd(da, gu):
    t, two_f = gu.shape
    f = two_f // 2
    tt = _rows_tile(t, TOK_TILE)

    def body(da_ref, g_ref, u_ref, o_ref):
        g = g_ref[...]
        sg = _sigmoid(g)
        da = da_ref[...]
        o_ref[:, :f] = (da * u_ref[...] * (sg * (1.0 + g * (1.0 - sg)))).astype(BF16)
        o_ref[:, f:] = (da * (g * sg)).astype(BF16)

    return pl.pallas_call(
        body, name="swiglu_bwd",
        out_shape=jax.ShapeDtypeStruct((t, two_f), BF16),
        grid=(t // tt,),
        in_specs=[pl.BlockSpec((tt, f), lambda i: (i, 0)),
                  pl.BlockSpec((tt, f), lambda i: (i, 0)), pl.BlockSpec((tt, f), lambda i: (i, 1))],
        out_specs=pl.BlockSpec((tt, two_f), lambda i: (i, 0)),
        compiler_params=_cparams(("parallel",)),
    )(da, gu, gu)


def loss_fwd(y, tgt):
    bsz, seq, d = y.shape
    ts = _rows_tile(seq, TOK_TILE)

    def body(y_ref, t_ref, dy_ref, l_ref, acc_ref):
        bb, s = pl.program_id(0), pl.program_id(1)
        e = y_ref[0] - t_ref[0]
        dy_ref[0] = e * (1.0 / d)
        part = jnp.sum((e * e).reshape(ts // 8, 8, d), axis=0)
        first = jnp.logical_and(bb == 0, s == 0)

        @pl.when(first)
        def _():
            acc_ref[...] = part

        @pl.when(jnp.logical_not(first))
        def _():
            acc_ref[...] += part

        @pl.when(jnp.logical_and(bb == pl.num_programs(0) - 1, s == pl.num_programs(1) - 1))
        def _():
            tot = jnp.sum(jnp.sum(acc_ref[...], axis=1, keepdims=True), axis=0, keepdims=True)
            l_ref[...] = jnp.broadcast_to(tot * (0.5 / d), (8, LANES))

    dy, l = pl.pallas_call(
        body, name="loss_fwd",
        out_shape=[jax.ShapeDtypeStruct((bsz, seq, d), F32), jax.ShapeDtypeStruct((8, LANES), F32)],
        grid=(bsz, seq // ts),
        in_specs=[_tok_spec(ts, d), _tok_spec(ts, d)],
        out_specs=[_tok_spec(ts, d), pl.BlockSpec((8, LANES), lambda bb, s: (0, 0))],
        scratch_shapes=[pltpu.VMEM((8, d), F32)],
        compiler_params=_cparams(("arbitrary", "arbitrary")),
    )(y, tgt)
    return l[0, 0], dy


def _rot_half(x):
    lane = lax.broadcasted_iota(jnp.int32, x.shape, 1)
    swapped = jnp.where(lane < 80, pltpu.roll(x, 112, axis=1), pltpu.roll(x, 16, axis=1))
    return jnp.where((lane >= C_NOPE) & (lane < C_NOPE + C_ROPE), swapped, 0.0)


def _rms(x, w):
    r = lax.rsqrt(jnp.mean(x * x, axis=-1, keepdims=True) + RMS_EPS)
    return x * r * w, r


def _rms_bwd(dy, x, w):
    r = lax.rsqrt(jnp.mean(x * x, axis=-1, keepdims=True) + RMS_EPS)
    wd = dy * w
    dx = r * wd - x * (r * r * r) * jnp.mean(x * wd, axis=-1, keepdims=True)
    return dx, jnp.sum(dy * x * r, axis=0, keepdims=True)


def mla_mid_fwd(cqkv, qw, kw, cq_tab, sg_tab):
    bsz, seq, _ = cqkv.shape
    ts = _rows_tile(seq, TOK_TILE)
    r = C_Q_RANK

    def body(x_ref, qw_ref, kw_ref, c_ref, s_ref, nq_ref, nkv_ref, kr_ref):
        x = x_ref[0]
        nq_ref[0] = _rms(x[:, :r], qw_ref[...])[0].astype(BF16)
        nkv_ref[0] = _rms(x[:, r:2 * r], kw_ref[...])[0].astype(BF16)
        xr = x[:, 2 * r:]
        kr_ref[0] = (xr * c_ref[0] + _rot_half(xr) * s_ref[0]).astype(BF16)

    row = pl.BlockSpec((1, r), lambda b, s: (0, 0))
    return pl.pallas_call(
        body, name="mla_mid_fwd",
        out_shape=[jax.ShapeDtypeStruct((bsz, seq, r), BF16), jax.ShapeDtypeStruct((bsz, seq, r), BF16),
                   jax.ShapeDtypeStruct((bsz, seq, LANES), BF16)],
        grid=(bsz, seq // ts),
        in_specs=[_tok_spec(ts, MLA_DOWN_PAD), row, row, _tok_spec(ts, LANES), _tok_spec(ts, LANES)],
        out_specs=[_tok_spec(ts, r), _tok_spec(ts, r), _tok_spec(ts, LANES)],
        compiler_params=_cparams(("parallel", "parallel")),
    )(cqkv, qw.reshape(1, r), kw.reshape(1, r), cq_tab, sg_tab)


def mla_mid_bwd(dnq, dnkv, dkr, cqkv, qw, kw, cq_tab, sg_tab):
    bsz, seq, _ = cqkv.shape
    ts = _rows_tile(seq, TOK_TILE)
    r = C_Q_RANK

    def body(dnq_ref, dnkv_ref, dkr_ref, x_ref, qw_ref, kw_ref, c_ref, s_ref, dx_ref, dqw_ref, dkw_ref):
        bb, s = pl.program_id(0), pl.program_id(1)
        x = x_ref[0]
        dcq, dqw = _rms_bwd(dnq_ref[0].astype(F32), x[:, :r], qw_ref[...])
        dckv, dkw = _rms_bwd(dnkv_ref[0].astype(F32), x[:, r:2 * r], kw_ref[...])
        dk = dkr_ref[0]
        dxr = dk * c_ref[0] + _rot_half(dk * s_ref[0])
        dx_ref[0, :, :r] = dcq.astype(BF16)
        dx_ref[0, :, r:2 * r] = dckv.astype(BF16)
        dx_ref[0, :, 2 * r:] = dxr.astype(BF16)
        first = jnp.logical_and(bb == 0, s == 0)

        @pl.when(first)
        def _():
            dqw_ref[...] = dqw
            dkw_ref[...] = dkw

        @pl.when(jnp.logical_not(first))
        def _():
            dqw_ref[...] += dqw
            dkw_ref[...] += dkw

    row = pl.BlockSpec((1, r), lambda b, s: (0, 0))
    return pl.pallas_call(
        body, name="mla_mid_bwd",
        out_shape=[jax.ShapeDtypeStruct((bsz, seq, MLA_DOWN_PAD), BF16),
                   jax.ShapeDtypeStruct((1, r), F32), jax.ShapeDtypeStruct((1, r), F32)],
        grid=(bsz, seq // ts),
        in_specs=[_tok_spec(ts, r), _tok_spec(ts, r), _tok_spec(ts, LANES), _tok_spec(ts, MLA_DOWN_PAD),
                  row, row, _tok_spec(ts, LANES), _tok_spec(ts, LANES)],
        out_specs=[_tok_spec(ts, MLA_DOWN_PAD), row, row],
        compiler_params=_cparams(("arbitrary", "arbitrary")),
    )(dnq, dnkv, dkr, cqkv, qw.reshape(1, r), kw.reshape(1, r), cq_tab, sg_tab)


def q_rope(q, cq_tab, sg_tab, transpose_rule, name):
    bsz, seq, w = q.shape
    ts = _rows_tile(seq, TOK_TILE)

    def body(q_ref, c_ref, s_ref, o_ref):
        c, s = c_ref[0], s_ref[0]
        for h in range(w // LANES):
            x = q_ref[0, :, h * LANES:(h + 1) * LANES].astype(F32)
            y = x * c + (_rot_half(x * s) if transpose_rule else _rot_half(x) * s)
            o_ref[0, :, h * LANES:(h + 1) * LANES] = y.astype(BF16)

    return pl.pallas_call(
        body, name=name,
        out_shape=jax.ShapeDtypeStruct((bsz, seq, w), BF16),
        grid=(bsz, seq // ts),
        in_specs=[_tok_spec(ts, w), _tok_spec(ts, LANES), _tok_spec(ts, LANES)],
        out_specs=_tok_spec(ts, w),
        compiler_params=_cparams(("parallel", "parallel")),
    )(q, cq_tab, sg_tab)


MLA_SCALE = (C_NOPE + C_ROPE) ** -0.5


def _eye_mask(n):
    return lax.broadcasted_iota(jnp.int32, (n, n), 0) == lax.broadcasted_iota(jnp.int32, (n, n), 1)


def mla_attn_fwd(qp, kv, kr):
    bsz, seq, _ = qp.shape
    tq = _rows_tile(seq, 256)

    def body(q_ref, kv_ref, kr_ref, o_ref, lse_ref, kc_ref):
        i = pl.program_id(2)
        lane = lax.broadcasted_iota(jnp.int32, (seq, LANES), 1)

        @pl.when(i == 0)
        def _():
            kc_ref[...] = jnp.where(lane < C_NOPE, kv_ref[0], kr_ref[0])

        s = _dot(q_ref[0], kc_ref[...], NT) * MLA_SCALE
        m = jnp.max(s, axis=-1, keepdims=True)
        p = jnp.exp(s - m)
        l = jnp.sum(p, axis=-1, keepdims=True)
        p = (p * (1.0 / l)).astype(BF16)
        o = _dot(p, kv_ref[0], NN)
        lane_q = lax.broadcasted_iota(jnp.int32, (tq, LANES), 1)
        o_ref[0] = jnp.where(lane_q >= C_NOPE, o, 0.0).astype(BF16)
        lse = m + jnp.log(l)
        lse_ref[0, 0] = jnp.sum(jnp.where(_eye_mask(tq), lse, 0.0), axis=0, keepdims=True)

    return pl.pallas_call(
        body, name="mla_attn_fwd",
        out_shape=[jax.ShapeDtypeStruct((bsz, seq, MLA_HEAD_PAD), BF16),
                   jax.ShapeDtypeStruct((bsz, C_HEADS, 1, seq), F32)],
        grid=(bsz, C_HEADS, seq // tq),
        in_specs=[pl.BlockSpec((1, tq, LANES), lambda b, h, i: (b, i, h)),
                  pl.BlockSpec((1, seq, LANES), lambda b, h, i: (b, 0, h)),
                  pl.BlockSpec((1, seq, LANES), lambda b, h, i: (b, 0, 0))],
        out_specs=[pl.BlockSpec((1, tq, LANES), lambda b, h, i: (b, i, h)),
                   pl.BlockSpec((1, 1, 1, tq), lambda b, h, i: (b, h, 0, i))],
        scratch_shapes=[pltpu.VMEM((seq, LANES), BF16)],
        compiler_params=_cparams(("parallel", "parallel", "arbitrary")),
    )(qp, kv, kr)


def mla_attn_bwd(qp, kv, kr, o, do, lse):
    bsz, seq, _ = qp.shape
    tq = _rows_tile(seq, 512)
    nt = seq // tq

    def body(q_ref, kv_ref, kr_ref, o_ref, do_ref, lse_ref, dq_ref, dkv_ref, dkr_ref,
             kc_ref, drow_ref, dqa_ref, dkc_ref, dkvv_ref):
        h = pl.program_id(1)
        lane_s = lax.broadcasted_iota(jnp.int32, (seq, LANES), 1)
        lane_t = lax.broadcasted_iota(jnp.int32, (tq, LANES), 1)
        kc_ref[...] = jnp.where(lane_s < C_NOPE, kv_ref[0], kr_ref[0])
        dqa_ref[...] = jnp.zeros_like(dqa_ref)
        ones = jnp.ones((8, LANES), F32)

        def delta_body(i, carry):
            rows = pl.ds(pl.multiple_of(i * tq, tq), tq)
            prod = do_ref[0, rows, :].astype(F32) * o_ref[0, rows, :].astype(F32)
            drow_ref[:, rows] = _dot(ones, prod, NT, precision=HIGHEST)
            return carry

        lax.fori_loop(0, nt, delta_body, 0)

        def key_body(j, carry):
            krows = pl.ds(pl.multiple_of(j * tq, tq), tq)
            kc_j = kc_ref[krows, :]
            kv_j = kv_ref[0, krows, :]
            dkc_ref[...] = jnp.zeros_like(dkc_ref)
            dkvv_ref[...] = jnp.zeros_like(dkvv_ref)

            def q_body(i, c2):
                qrows = pl.ds(pl.multiple_of(i * tq, tq), tq)
                q_i = q_ref[0, qrows, :]
                do_i = do_ref[0, qrows, :]
                st = _dot(kc_j, q_i, NT) * MLA_SCALE
                pt = jnp.exp(st - lse_ref[0, 0, :, qrows])
                dpt = _dot(kv_j, do_i, NT)
                dst = (pt * (dpt - drow_ref[0:1, qrows]) * MLA_SCALE).astype(BF16)
                dkvv_ref[...] += _dot(pt.astype(BF16), do_i, NN)
                dkc_ref[...] += _dot(dst, q_i, NN)
                dqa_ref[qrows, :] += _dot(dst, kc_j, TN)
                return c2

            lax.fori_loop(0, nt, q_body, 0)
            dkc = dkc_ref[...]
            dkv_ref[0, krows, :] = jnp.where(lane_t < C_NOPE, dkc, dkvv_ref[...]).astype(BF16)
            dkr_j = jnp.where(lane_t >= C_NOPE, dkc, 0.0)

            @pl.when(h == 0)
            def _():
                dkr_ref[0, krows, :] = dkr_j

            @pl.when(h > 0)
            def _():
                dkr_ref[0, krows, :] += dkr_j

            return carry

        lax.fori_loop(0, nt, key_body, 0)
        dq_ref[0] = dqa_ref[...].astype(BF16)

    head = pl.BlockSpec((1, seq, LANES), lambda b, h: (b, 0, h))
    shared = pl.BlockSpec((1, seq, LANES), lambda b, h: (b, 0, 0))
    return pl.pallas_call(
        body, name="mla_attn_bwd",
        out_shape=[jax.ShapeDtypeStruct((bsz, seq, MLA_HEAD_PAD), BF16),
                   jax.ShapeDtypeStruct((bsz, seq, MLA_HEAD_PAD), BF16),
                   jax.ShapeDtypeStruct((bsz, seq, LANES), F32)],
        grid=(bsz, C_HEADS),
        in_specs=[head, head, shared, head, head,
                  pl.BlockSpec((1, 1, 1, seq), lambda b, h: (b, h, 0, 0))],
        out_specs=[head, head, shared],
        scratch_shapes=[pltpu.VMEM((seq, LANES), BF16), pltpu.VMEM((8, seq), F32),
                        pltpu.VMEM((seq, LANES), F32), pltpu.VMEM((tq, LANES), F32),
                        pltpu.VMEM((tq, LANES), F32)],
        compiler_params=_cparams(("parallel", "arbitrary")),
    )(qp, kv, kr, o, do, lse)


A_SCALE = A_HEAD_DIM ** -0.5
A_GROUP = A_HEADS // A_KV_HEADS
A_BAND = 3 * A_BLOCK
NEG_BIG = -1e30


def _wattn_block(i, seq, posc_ref, posr_ref):
    start = pl.multiple_of(i * A_BLOCK, A_BLOCK)
    pq = posc_ref[0]
    pk = posr_ref[0, :, pl.ds(start, A_BAND)]
    dist = jnp.abs(pq - pk).astype(F32)
    qi = lax.broadcasted_iota(jnp.int32, (A_BLOCK, A_BAND), 0) + A_BLOCK
    ki = lax.broadcasted_iota(jnp.int32, (A_BLOCK, A_BAND), 1)
    absk = i * A_BLOCK + ki - A_BLOCK
    valid = (jnp.abs(qi - ki) <= WINDOW) & (absk >= 0) & (absk < seq)
    return start, dist, valid


def _wattn_probs(qh, kb, dist, valid, slope, sink):
    s = _dot(qh, kb, NT) * A_SCALE - slope * dist
    s = jnp.where(valid, s, NEG_BIG)
    m = jnp.maximum(jnp.max(s, axis=-1, keepdims=True), sink)
    p = jnp.exp(s - m)
    es = jnp.exp(sink - m)
    inv = 1.0 / (jnp.sum(p, axis=-1, keepdims=True) + es)
    return p * inv, es * inv


def _alibi_slope(h):
    return 2.0 ** (-8.0 * (h + 1) / A_HEADS)


def wattn_fwd(proj, kp, vp, posc, posr, sink):
    bsz, seq, _ = proj.shape
    nb = seq // A_BLOCK
    qw = A_HEADS * LANES

    def body(sink_ref, q_ref, kp_ref, vp_ref, posc_ref, posr_ref, o_ref):
        i = pl.program_id(1)
        start, dist, valid = _wattn_block(i, seq, posc_ref, posr_ref)
        kb = kp_ref[0, pl.ds(start, A_BAND), :]
        vb = vp_ref[0, pl.ds(start, A_BAND), :]
        lane = lax.broadcasted_iota(jnp.int32, (A_BLOCK, LANES), 1)
        for h in range(A_HEADS):
            g = h // A_GROUP
            qh = q_ref[0, :, h * LANES:(h + 1) * LANES].astype(BF16)
            p, _ = _wattn_probs(qh, kb, dist, valid, _alibi_slope(h), sink_ref[h])
            o = _dot(p.astype(BF16), vb, NN)
            mine = (lane >= A_HEAD_DIM) if g == 1 else (lane < A_HEAD_DIM)
            o_ref[0, :, h * LANES:(h + 1) * LANES] = jnp.where(mine, o, 0.0).astype(BF16)

    return pl.pallas_call(
        body, name="wattn_fwd",
        out_shape=jax.ShapeDtypeStruct((bsz, seq, MO_PAD), BF16),
        grid=(bsz, nb),
        in_specs=[pl.BlockSpec(memory_space=pltpu.SMEM),
                  pl.BlockSpec((1, A_BLOCK, qw), lambda b, i: (b, i, 0)),
                  pl.BlockSpec((1, seq + 2 * A_BLOCK, LANES), lambda b, i: (b, 0, 0)),
                  pl.BlockSpec((1, seq + 2 * A_BLOCK, LANES), lambda b, i: (b, 0, 0)),
                  pl.BlockSpec((1, A_BLOCK, 1), lambda b, i: (b, i, 0)),
                  pl.BlockSpec((1, 1, seq + 2 * A_BLOCK), lambda b, i: (b, 0, 0))],
        out_specs=pl.BlockSpec((1, A_BLOCK, qw), lambda b, i: (b, i, 0)),
        compiler_params=_cparams(("parallel", "parallel")),
    )(sink, proj, kp, vp, posc, posr)


def wattn_bwd(proj, kp, vp, posc, posr, sink, dmo):
    bsz, seq, _ = proj.shape
    nb = seq // A_BLOCK
    qw = A_HEADS * LANES
    sp = seq + 2 * A_BLOCK

    def body(sink_ref, q_ref, kp_ref, vp_ref, posc_ref, posr_ref, do_ref,
             dq_ref, dk_ref, dv_ref, ds_ref):
        i = pl.program_id(1)

        @pl.when(i == 0)
        def _():
            dk_ref[...] = jnp.zeros_like(dk_ref)
            dv_ref[...] = jnp.zeros_like(dv_ref)

        @pl.when(jnp.logical_and(i == 0, pl.program_id(0) == 0))
        def _():
            ds_ref[...] = jnp.zeros_like(ds_ref)

        start, dist, valid = _wattn_block(i, seq, posc_ref, posr_ref)
        kb = kp_ref[0, pl.ds(start, A_BAND), :]
        vb = vp_ref[0, pl.ds(start, A_BAND), :]
        lane = lax.broadcasted_iota(jnp.int32, (A_BLOCK, LANES), 1)
        lane1 = lax.broadcasted_iota(jnp.int32, (1, LANES), 1)
        dk_acc = jnp.zeros((A_BAND, LANES), F32)
        dv_acc = jnp.zeros((A_BAND, LANES), F32)
        dsink = jnp.zeros((1, LANES), F32)
        for h in range(A_HEADS):
            g = h // A_GROUP
            qh = q_ref[0, :, h * LANES:(h + 1) * LANES].astype(BF16)
            p, psink = _wattn_probs(qh, kb, dist, valid, _alibi_slope(h), sink_ref[h])
            pb = p.astype(BF16)
            do = do_ref[0, :, h * LANES:(h + 1) * LANES]
            dob = do.astype(BF16)
            mine = (lane >= A_HEAD_DIM) if g == 1 else (lane < A_HEAD_DIM)
            o = jnp.where(mine, _dot(pb, vb, NN), 0.0)
            delta = jnp.sum(do * o, axis=-1, keepdims=True)
            dp = _dot(dob, vb, NT)
            ds = (p * (dp - delta) * A_SCALE).astype(BF16)
            dq_ref[0, :, h * LANES:(h + 1) * LANES] = _dot(ds, kb, NN).astype(BF16)
            dk_acc = dk_acc + _dot(ds, qh, TN)
            dv_acc = dv_acc + _dot(pb, dob, TN)
            dsh = -jnp.sum(psink * delta, axis=0, keepdims=True)
            dsink = dsink + jnp.where(lane1 == h, dsh, 0.0)
        dk_ref[0, pl.ds(start, A_BAND), :] += dk_acc
        dv_ref[0, pl.ds(start, A_BAND), :] += dv_acc
        ds_ref[0] += dsink

    full = pl.BlockSpec((1, sp, LANES), lambda b, i: (b, 0, 0))
    return pl.pallas_call(
        body, name="wattn_bwd",
        out_shape=[jax.ShapeDtypeStruct((bsz, seq, qw), BF16),
                   jax.ShapeDtypeStruct((bsz, sp, LANES), F32),
                   jax.ShapeDtypeStruct((bsz, sp, LANES), F32),
                   jax.ShapeDtypeStruct((1, 1, LANES), F32)],
        grid=(bsz, nb),
        in_specs=[pl.BlockSpec(memory_space=pltpu.SMEM),
                  pl.BlockSpec((1, A_BLOCK, qw), lambda b, i: (b, i, 0)),
                  full, full,
                  pl.BlockSpec((1, A_BLOCK, 1), lambda b, i: (b, i, 0)),
                  pl.BlockSpec((1, 1, sp), lambda b, i: (b, 0, 0)),
                  pl.BlockSpec((1, A_BLOCK, qw), lambda b, i: (b, i, 0))],
        out_specs=[pl.BlockSpec((1, A_BLOCK, qw), lambda b, i: (b, i, 0)), full, full,
                   pl.BlockSpec((1, 1, LANES), lambda b, i: (0, 0, 0))],
        compiler_params=_cparams(("arbitrary", "arbitrary")),
    )(sink, proj, kp, vp, posc, posr, dmo)


HG_Q, HG_FF, HG_FB, HG_I, HG_G = 10, 14, 18, 22, 26
HG_OUT = 8
CH = B_CHUNK


def _hgrn_consts(reverse):
    r = lax.broadcasted_iota(jnp.int32, (CH, CH), 0)
    c = lax.broadcasted_iota(jnp.int32, (CH, CH), 1)
    incl = (c >= r) if reverse else (c <= r)
    incl_t = (c <= r) if reverse else (c >= r)
    row = lax.broadcasted_iota(jnp.int32, (CH, LANES), 0)
    mid = CH // 2 if reverse else CH // 2 - 1
    end = 0 if reverse else CH - 1
    return incl, incl_t, row == mid, row == end


def _pick_row(x, sel):
    return jnp.sum(jnp.where(sel, x, 0.0), axis=0, keepdims=True)


def _hgrn_gates(hf, lb):
    sig = _sigmoid(hf)
    f = lb + (1.0 - lb) * sig
    return 1.0 - f, jnp.log(f), sig, f


def _hgrn_decays(lf, consts):
    incl, _, is_mid, is_end = consts
    b = _dot(incl.astype(F32), lf, NN, precision=HIGHEST)
    return b, _pick_row(b, is_mid), _pick_row(b, is_end)


def _hgrn_chunk(q, k, v, lf, st, consts):
    b, bm, be = _hgrn_decays(lf, consts)
    qs = (q * jnp.exp(b - bm)).astype(BF16)
    ks = (k * jnp.exp(bm - b)).astype(BF16)
    a = jnp.where(consts[0], _dot(qs, ks, NT), 0.0).astype(BF16)
    qe = (q * jnp.exp(b)).astype(BF16)
    vb = v.astype(BF16)
    o = _dot(a, vb, NN) + _dot(qe, st.astype(BF16), NT)
    kd = (k * jnp.exp(be - b)).astype(BF16)
    return o, jnp.exp(be) * st + _dot(vb, kd, TN)


def _silu(x):
    return x * _sigmoid(x)


def hgrn_fwd(proj, mo, lb, nw):
    bsz, seq, _ = proj.shape
    n_ch = seq // CH
    ts = _rows_tile(seq, 512)

    def body(hq_ref, hff_ref, hfb_ref, hi_ref, hg_ref, mo_in_ref, lb_ref, nw_ref, y_ref, ot_ref):
        del mo_in_ref
        lb_v, nw_v = lb_ref[0], nw_ref[0]

        def run(hf_ref, reverse, first):
            consts = _hgrn_consts(reverse)

            def step(t, st):
                n = (n_ch - 1 - t) if reverse else t
                rows = pl.ds(pl.multiple_of(n * CH, CH), CH)
                k, lf, _, _ = _hgrn_gates(hf_ref[0, rows, :], lb_v)
                o, st = _hgrn_chunk(_silu(hq_ref[0, rows, :]), k, hi_ref[0, rows, :], lf, st, consts)
                if first:
                    ot_ref[0, rows, :] = o
                else:
                    ot_ref[0, rows, :] += o
                return st

            lax.fori_loop(0, n_ch, step, jnp.zeros((LANES, LANES), F32))

        run(hff_ref, False, True)
        run(hfb_ref, True, False)

        def finish(i, carry):
            rows = pl.ds(pl.multiple_of(i * ts, ts), ts)
            o = ot_ref[0, rows, :]
            r = lax.rsqrt(jnp.mean(o * o, axis=-1, keepdims=True) + RMS_EPS)
            y_ref[0, rows, :] = (o * r * nw_v * _silu(hg_ref[0, rows, :])).astype(BF16)
            return carry

        lax.fori_loop(0, seq // ts, finish, 0)

    def col(g):
        return pl.BlockSpec((1, seq, LANES), lambda b, h: (b, 0, g + h))

    par = pl.BlockSpec((1, 1, LANES), lambda b, h: (h, 0, 0))
    return pl.pallas_call(
        body, name="hgrn_fwd",
        out_shape=[jax.ShapeDtypeStruct(mo.shape, BF16),
                   jax.ShapeDtypeStruct((bsz, seq, B_HEADS * LANES), F32)],
        grid=(bsz, B_HEADS),
        in_specs=[col(HG_Q), col(HG_FF), col(HG_FB), col(HG_I), col(HG_G),
                  pl.BlockSpec(memory_space=pl.ANY), par, par],
        out_specs=[col(HG_OUT), col(0)],
        input_output_aliases={5: 0},
        compiler_params=_cparams(("parallel", "parallel")),
    )(proj, proj, proj, proj, proj, mo, lb, nw)


def hgrn_out_bwd(proj, otot, dmo, nw):
    bsz, seq, _ = proj.shape
    ts = _rows_tile(seq, 512)

    def body(hg_ref, ot_ref, dmo_ref, nw_ref, do_ref, dhg_ref, dnw_ref):
        nw_v = nw_ref[0]
        dy = dmo_ref[0].astype(F32)
        hg = hg_ref[0]
        o = ot_ref[0]
        sg = _sigmoid(hg)
        r = lax.rsqrt(jnp.mean(o * o, axis=-1, keepdims=True) + RMS_EPS)
        dhg_ref[0] = (dy * (o * r * nw_v) * (sg * (1.0 + hg * (1.0 - sg)))).astype(BF16)
        drn = dy * (hg * sg)
        dnw = jnp.sum(drn * o * r, axis=0, keepdims=True)
        wd = drn * nw_v
        do_ref[0] = r * wd - o * (r * r * r) * jnp.mean(o * wd, axis=-1, keepdims=True)
        first = jnp.logical_and(pl.program_id(1) == 0, pl.program_id(2) == 0)

        @pl.when(first)
        def _():
            dnw_ref[0] = dnw

        @pl.when(jnp.logical_not(first))
        def _():
            dnw_ref[0] += dnw

    def col(g):
        return pl.BlockSpec((1, ts, LANES), lambda h, b, s: (b, s, g + h))

    par = pl.BlockSpec((1, 1, LANES), lambda h, b, s: (h, 0, 0))
    return pl.pallas_call(
        body, name="hgrn_out_bwd",
        out_shape=[jax.ShapeDtypeStruct((bsz, seq, B_HEADS * LANES), F32),
                   jax.ShapeDtypeStruct((bsz, seq, B_HEADS * LANES), BF16),
                   jax.ShapeDtypeStruct((B_HEADS, 1, LANES), F32)],
        grid=(B_HEADS, bsz, seq // ts),
        in_specs=[col(HG_G), col(0), col(HG_OUT), par],
        out_specs=[col(0), col(0), par],
        compiler_params=_cparams(("parallel", "arbitrary", "arbitrary")),
    )(proj, otot, dmo, nw)


def hgrn_bwd(proj, do, lb):
    bsz, seq, _ = proj.shape
    n_ch = seq // CH
    ts = _rows_tile(seq, 512)

    def body(hq_ref, hff_ref, hfb_ref, hi_ref, do_scr, lb_ref,
             dhq_ref, dhff_ref, dhfb_ref, dhi_ref, dlb_ref, st_scr, dq_scr, dv_scr):
        lb_v = lb_ref[0]
        do_scr = do_scr.at[0]

        def run(hf_ref, dhf_ref, reverse, first, dlb0):
            consts = _hgrn_consts(reverse)
            incl, incl_t, is_mid, is_end = consts

            def load(n):
                rows = pl.ds(pl.multiple_of(n * CH, CH), CH)
                hf = hf_ref[0, rows, :]
                k, lf, sig, f = _hgrn_gates(hf, lb_v)
                return rows, _silu(hq_ref[0, rows, :]), k, hi_ref[0, rows, :], lf, sig, f

            def fwd_step(t, st):
                n = (n_ch - 1 - t) if reverse else t
                _, q, k, v, lf, _, _ = load(n)
                st_scr[n] = st.astype(BF16)
                return _hgrn_chunk(q, k, v, lf, st, consts)[1]

            lax.fori_loop(0, n_ch, fwd_step, jnp.zeros((LANES, LANES), F32))

            def bwd_step(t, carry):
                gt, dlb = carry
                n = t if reverse else (n_ch - 1 - t)
                rows, q, k, v, lf, sig, f = load(n)
                st = st_scr[n]
                do_c = do_scr[rows, :].astype(BF16)
                b, bm, be = _hgrn_decays(lf, consts)
                e_qs, e_ks, e_q, e_kd = jnp.exp(b - bm), jnp.exp(bm - b), jnp.exp(b), jnp.exp(be - b)
                dec = jnp.exp(be)
                qs, ks, qe, kd = q * e_qs, k * e_ks, q * e_q, k * e_kd
                qs_b, ks_b, qe_b, kd_b = qs.astype(BF16), ks.astype(BF16), qe.astype(BF16), kd.astype(BF16)
                vb = v.astype(BF16)
                gt_b = gt.astype(BF16)
                a = jnp.where(incl, _dot(qs_b, ks_b, NT), 0.0).astype(BF16)
                da = jnp.where(incl, _dot(do_c, vb, NT), 0.0).astype(BF16)
                dqe = _dot(do_c, st, NN)
                dv = _dot(a, do_c, TN) + _dot(kd_b, gt_b, NT)
                dqs = _dot(da, ks_b, NN)
                dks = _dot(da, qs_b, TN)
                dkd = _dot(vb, gt_b, NN)
                ddec = jnp.sum(gt * st.astype(F32), axis=0, keepdims=True)
                gt_new = dec * gt + _dot(do_c, qe_b, TN)
                dq = dqs * e_qs + dqe * e_q
                dk = dks * e_ks + dkd * e_kd
                t_qs, t_ks, t_kd = dqs * qs, dks * ks, dkd * kd
                db = t_qs - t_ks + dqe * qe - t_kd
                dbm = jnp.sum(t_ks - t_qs, axis=0, keepdims=True)
                dbe = jnp.sum(t_kd, axis=0, keepdims=True) + ddec * dec
                db = db + jnp.where(is_mid, dbm, 0.0) + jnp.where(is_end, dbe, 0.0)
                dlf = _dot(incl_t.astype(F32), db, NN, precision=HIGHEST)
                df = dlf / f - dk
                dhf_ref[0, rows, :] = (df * (1.0 - lb_v) * sig * (1.0 - sig)).astype(BF16)
                dlb = dlb + jnp.sum(df * (1.0 - sig), axis=0, keepdims=True)
                if first:
                    dq_scr[rows, :] = dq
                    dv_scr[rows, :] = dv
                else:
                    dq_scr[rows, :] += dq
                    dv_scr[rows, :] += dv
                return gt_new, dlb

            return lax.fori_loop(0, n_ch, bwd_step, (jnp.zeros((LANES, LANES), F32), dlb0))[1]

        dlb = run(hff_ref, dhff_ref, False, True, jnp.zeros((1, LANES), F32))
        dlb = run(hfb_ref, dhfb_ref, True, False, dlb)

        @pl.when(pl.program_id(1) == 0)
        def _():
            dlb_ref[0] = dlb

        @pl.when(pl.program_id(1) > 0)
        def _():
            dlb_ref[0] += dlb

        def finish(i, carry):
            rows = pl.ds(pl.multiple_of(i * ts, ts), ts)
            hq = hq_ref[0, rows, :]
            sq = _sigmoid(hq)
            dhq_ref[0, rows, :] = (dq_scr[rows, :] * (sq * (1.0 + hq * (1.0 - sq)))).astype(BF16)
            dhi_ref[0, rows, :] = dv_scr[rows, :].astype(BF16)
            return carry

        lax.fori_loop(0, seq // ts, finish, 0)

    def col(g):
        return pl.BlockSpec((1, seq, LANES), lambda h, b: (b, 0, g + h))

    par = pl.BlockSpec((1, 1, LANES), lambda h, b: (h, 0, 0))
    wide = jax.ShapeDtypeStruct((bsz, seq, B_HEADS * LANES), BF16)
    small = jax.ShapeDtypeStruct((B_HEADS, 1, LANES), F32)
    return pl.pallas_call(
        body, name="hgrn_bwd",
        out_shape=[wide, wide, wide, wide, small],
        grid=(B_HEADS, bsz),
        in_specs=[col(HG_Q), col(HG_FF), col(HG_FB), col(HG_I), col(0), par],
        out_specs=[col(0), col(0), col(0), col(0), par],
        scratch_shapes=[pltpu.VMEM((n_ch, LANES, LANES), BF16),
                        pltpu.VMEM((seq, LANES), F32), pltpu.VMEM((seq, LANES), F32)],
        compiler_params=_cparams(("parallel", "arbitrary")),
    )(proj, proj, proj, proj, do, lb)


def _whole(shape):
    return pl.BlockSpec(shape, lambda: (0,) * len(shape))


def silu_small(x):
    def body(x_ref, o_ref):
        o_ref[...] = _silu(x_ref[...])

    return pl.pallas_call(body, name="silu_small", out_shape=jax.ShapeDtypeStruct(x.shape, F32),
                          in_specs=[_whole(x.shape)], out_specs=_whole(x.shape))(x)


def _softmax_rows(x):
    e = jnp.exp(x - jnp.max(x, axis=0, keepdims=True))
    return e / jnp.sum(e, axis=0, keepdims=True)


def lb_fwd(logits, layer):
    n, w = logits.shape

    def body(x_ref, o_ref):
        p = _softmax_rows(x_ref[...])
        row = lax.broadcasted_iota(jnp.int32, (n, w), 0)
        o_ref[...] = jnp.sum(jnp.where(row <= layer, p, 0.0), axis=0, keepdims=True)

    return pl.pallas_call(body, name="lb_fwd", out_shape=jax.ShapeDtypeStruct((1, w), F32),
                          in_specs=[_whole((n, w))], out_specs=_whole((1, w)))(logits)


def lb_bwd(logits, dlb, layer):
    n, w = logits.shape

    def body(x_ref, d_ref, o_ref):
        p = _softmax_rows(x_ref[...])
        row = lax.broadcasted_iota(jnp.int32, (n, w), 0)
        dp = jnp.where(row <= layer, d_ref[...], 0.0)
        o_ref[...] = p * (dp - jnp.sum(p * dp, axis=0, keepdims=True))

    return pl.pallas_call(body, name="lb_bwd", out_shape=jax.ShapeDtypeStruct((n, w), F32),
                          in_specs=[_whole((n, w)), _whole((1, w))], out_specs=_whole((n, w)))(logits, dlb)


def sum_parts(x, name):
    p, r, c = x.shape
    tr = _rows_tile(r, max(8, (1 << 23) // (4 * c * p) // 8 * 8))

    def body(x_ref, o_ref):
        acc = x_ref[0].astype(F32)
        for j in range(1, p):
            acc = acc + x_ref[j].astype(F32)
        o_ref[...] = acc

    return pl.pallas_call(
        body, name=name, out_shape=jax.ShapeDtypeStruct((r, c), F32),
        grid=(r // tr,),
        in_specs=[pl.BlockSpec((p, tr, c), lambda i: (0, i, 0))],
        out_specs=pl.BlockSpec((tr, c), lambda i: (i, 0)),
        compiler_params=_cparams(("parallel",)),
    )(x)


def adamw(w, g, m, v, name):
    r, c = w.shape
    tr = _rows_tile(r, max(8, (1 << 20) // (4 * c) // 8 * 8))
    c1 = 1.0 - ADAM_B1 ** ADAM_STEP
    c2 = 1.0 - ADAM_B2 ** ADAM_STEP

    def body(w_ref, g_ref, m_ref, v_ref, d_ref, nm_ref, nv_ref):
        g = g_ref[...]
        nm = ADAM_B1 * m_ref[...] + (1.0 - ADAM_B1) * g
        nv = ADAM_B2 * v_ref[...] + (1.0 - ADAM_B2) * (g * g)
        nm_ref[...] = nm
        nv_ref[...] = nv
        d_ref[...] = -ADAM_LR * ((nm / c1) / (jnp.sqrt(nv / c2) + ADAM_EPS) + ADAM_WD * w_ref[...])

    spec = pl.BlockSpec((tr, c), lambda i: (i, 0))
    sds = jax.ShapeDtypeStruct((r, c), F32)
    return pl.pallas_call(
        body, name=name, out_shape=[sds, sds, sds], grid=(r // tr,),
        in_specs=[spec, spec, spec, spec], out_specs=[spec, spec, spec],
        compiler_params=_cparams(("parallel",)),
    )(w, g, m, v)


ANY = pl.BlockSpec(memory_space=pl.ANY)


def _my_place():
    return lax.axis_index("x"), lax.axis_index("y"), lax.axis_index("c")


def all_gather(x, name):
    def body(x_ref, out_ref, send_sems, recv_sems, local_sem):
        mx, my, mc = _my_place()
        me, sibling = (mx, my, mc), (mx, my, 1 - mc)
        chips = [(1 - mx, my), (mx, 1 - my), (1 - mx, 1 - my)]

        def slot(px, py, pc):
            return out_ref.at[4 * px + 2 * py + pc]

        def copy(k, block, to, src=None):
            return pltpu.make_async_remote_copy(
                src_ref=slot(*block) if src is None else src, dst_ref=slot(*block),
                send_sem=send_sems.at[k], recv_sem=recv_sems.at[k],
                device_id=to, device_id_type=MESH_ID)

        mine = pltpu.make_async_copy(x_ref, slot(*me), local_sem)
        mine.start()
        first = [copy(0, me, sibling, src=x_ref)]
        first += [copy(1 + j, me, (*chip, mc), src=x_ref) for j, chip in enumerate(chips)]
        for cp in first:
            cp.start()
        passed = [copy(4 + j, (*chip, mc), sibling) for j, chip in enumerate(chips)]
        for j, chip in enumerate(chips):
            copy(1 + j, (*chip, mc), me).wait_recv()
            passed[j].start()
        copy(0, sibling, me).wait_recv()
        for j, chip in enumerate(chips):
            copy(4 + j, (*chip, 1 - mc), me).wait_recv()
        for cp in first + passed:
            cp.wait_send()
        mine.wait()

    return pl.pallas_call(
        body, name=name,
        out_shape=jax.ShapeDtypeStruct((N_DEV,) + x.shape, x.dtype),
        in_specs=[ANY], out_specs=ANY,
        scratch_shapes=[pltpu.SemaphoreType.DMA((7,)), pltpu.SemaphoreType.DMA((7,)),
                        pltpu.SemaphoreType.DMA(())],
    )(x)


def all_to_all(x, name):
    def body(x_ref, out_ref, send_sems, recv_sems, local_sem):
        mx, my, mc = _my_place()
        me = 4 * mx + 2 * my + mc
        mine = pltpu.make_async_copy(x_ref.at[me], out_ref.at[me], local_sem)
        mine.start()
        copies = []
        for k in range(N_DEV - 1):
            fx, fy, fc = (k + 1) >> 2 & 1, (k + 1) >> 1 & 1, (k + 1) & 1
            px = 1 - mx if fx else mx
            py = 1 - my if fy else my
            pc = 1 - mc if fc else mc
            peer = 4 * px + 2 * py + pc
            cp = pltpu.make_async_remote_copy(
                src_ref=x_ref.at[peer], dst_ref=out_ref.at[me],
                send_sem=send_sems.at[k], recv_sem=recv_sems.at[k],
                device_id=(px, py, pc), device_id_type=MESH_ID)
            cp.start()
            copies.append((cp, peer, (px, py, pc), k))
        for cp, peer, to, k in copies:
            pltpu.make_async_remote_copy(
                src_ref=x_ref.at[peer], dst_ref=out_ref.at[peer],
                send_sem=send_sems.at[k], recv_sem=recv_sems.at[k],
                device_id=to, device_id_type=MESH_ID).wait_recv()
        for cp, _, _, _ in copies:
            cp.wait_send()
        mine.wait()

    return pl.pallas_call(
        body, name=name,
        out_shape=jax.ShapeDtypeStruct(x.shape, x.dtype),
        in_specs=[ANY], out_specs=ANY,
        scratch_shapes=[pltpu.SemaphoreType.DMA((7,)), pltpu.SemaphoreType.DMA((7,)),
                        pltpu.SemaphoreType.DMA(())],
    )(x)


ROW_W = D_MODEL
FF_SH = D_FF // N_DEV
N_FFN = 2 * DEPTH
SEC = {}
_off = 0
for _name, _rows in (("gate", N_FFN * FF_SH), ("up", N_FFN * FF_SH), ("down", N_FFN * FF_SH),
                     ("hin", 416), ("hout", 128), ("mout", 128), ("mdown", 80), ("uq", 48), ("ukv", 64)):
    SEC[_name] = (_off, _rows)
    _off += _rows
PACK_ROWS = _off
MDOWN_ROWS = 128 * 544 // ROW_W
A_Q_W = A_HEADS * A_HEAD_DIM
_KVHEAD = (np.arange(A_HEADS) // A_GROUP).reshape(A_HEADS, 1, 1)


def pack_shards(gate, up, down, hin, hout, mdown, uq, ukv, mout, dtype):
    def rows(a):
        return a.astype(dtype).reshape(-1, ROW_W)
    parts = [rows(jnp.swapaxes(gate, -1, -2)), rows(jnp.swapaxes(up, -1, -2)), rows(down),
             rows(jnp.swapaxes(hin, -1, -2)), rows(hout), rows(mout),
             jnp.pad(rows(mdown), ((0, SEC["mdown"][1] - MDOWN_ROWS), (0, 0))),
             rows(jnp.swapaxes(uq, -1, -2)), rows(jnp.swapaxes(ukv, -1, -2))]
    return jnp.concatenate(parts, axis=0)


def unpack_shards(p):
    def sec(name):
        o, n = SEC[name]
        return p[o:o + n]

    def col(name, r, c):
        return jnp.swapaxes(sec(name).reshape(-1, c, r), -1, -2)

    return dict(
        ffn_w_gate=col("gate", 1024, FF_SH).reshape(DEPTH, 2, 1024, FF_SH),
        ffn_w_up=col("up", 1024, FF_SH).reshape(DEPTH, 2, 1024, FF_SH),
        ffn_w_down=sec("down").reshape(DEPTH, 2, FF_SH, 1024),
        hyb_w_in=col("hin", 1024, 416),
        hyb_w_out=sec("hout").reshape(1, 128, 1024),
        mla_w_out=sec("mout").reshape(1, 128, 1024),
        mla_w_down=sec("mdown")[:MDOWN_ROWS].reshape(1, 128, 544),
        mla_w_uq=col("uq", 256, 192),
        mla_w_ukv=col("ukv", 256, 256),
    )


def _pad_qheads(w):
    a = w.reshape(A_HEADS, 1, A_HEAD_DIM, -1)
    kvh = _KVHEAD[..., None]
    both = jnp.concatenate([jnp.where(kvh == 0, a, 0), jnp.where(kvh == 1, a, 0)], axis=1)
    return both.reshape(A_HEADS * LANES, -1)


def _unpad_qheads(w):
    a = w.reshape(A_HEADS, 2, A_HEAD_DIM, -1)
    return jnp.where(_KVHEAD == 0, a[:, 0], a[:, 1]).reshape(A_Q_W, -1)


def unpack_full(g):
    def sec(name):
        o, n = SEC[name]
        return g[:, o:o + n]

    def ffn(name):
        return jnp.swapaxes(sec(name).reshape(N_DEV, N_FFN, FF_SH, ROW_W), 0, 1).reshape(N_FFN, D_FF, ROW_W)

    def z(*s):
        return jnp.zeros(s, g.dtype)

    gate_t, up_t, down = ffn("gate"), ffn("up"), ffn("down")
    hin_t = sec("hin").reshape(-1, ROW_W)
    hout = sec("hout").reshape(-1, ROW_W)
    mout = sec("mout").reshape(C_HEADS, C_V, ROW_W)
    mdown = sec("mdown")[:, :MDOWN_ROWS].reshape(D_MODEL, 544)
    uq_t = sec("uq").reshape(C_HEADS, C_NOPE + C_ROPE, C_Q_RANK)
    return dict(
        wgu_t=jnp.concatenate([gate_t, up_t], axis=1),
        wd=down,
        hin_t=jnp.concatenate([_pad_qheads(hin_t[:A_Q_W]), hin_t[A_Q_W:]], axis=0),
        hout=jnp.concatenate([_pad_qheads(hout[:A_Q_W]), hout[A_Q_W:]], axis=0),
        mout=jnp.concatenate([z(C_HEADS, C_NOPE, ROW_W), mout], axis=1).reshape(MLA_HEAD_PAD, ROW_W),
        mdown=jnp.concatenate([mdown[:, :512], z(D_MODEL, 64), mdown[:, 512:], z(D_MODEL, 32)], axis=1),
        uq_t=jnp.concatenate([uq_t, z(C_HEADS, 32, C_Q_RANK)], axis=1).reshape(MLA_HEAD_PAD, C_Q_RANK),
        ukv_t=sec("ukv").reshape(MLA_HEAD_PAD, C_KV_RANK),
    )


def pack_full(d):
    def split(a):
        return a.reshape(N_DEV, -1, ROW_W)

    def ffn(a):
        return jnp.swapaxes(a.reshape(N_FFN, N_DEV, FF_SH, ROW_W), 0, 1).reshape(N_DEV, N_FFN * FF_SH, ROW_W)

    hin = jnp.concatenate([_unpad_qheads(d["hin_t"][:A_HEADS * LANES]), d["hin_t"][A_HEADS * LANES:]], axis=0)
    hout = jnp.concatenate([_unpad_qheads(d["hout"][:A_HEADS * LANES]), d["hout"][A_HEADS * LANES:]], axis=0)
    mout = d["mout"].reshape(C_HEADS, LANES, ROW_W)[:, C_NOPE:].reshape(-1, ROW_W)
    mdown = jnp.concatenate([d["mdown"][:, :512], d["mdown"][:, 576:608]], axis=1)
    uq = d["uq_t"].reshape(C_HEADS, LANES, C_Q_RANK)[:, :C_NOPE + C_ROPE]
    parts = [ffn(d["wgu_t"][:, :D_FF]), ffn(d["wgu_t"][:, D_FF:]), ffn(d["wd"]),
             split(hin), split(hout), split(mout),
             jnp.pad(split(mdown), ((0, 0), (0, SEC["mdown"][1] - MDOWN_ROWS), (0, 0))),
             split(uq), split(d["ukv_t"])]
    return jnp.concatenate(parts, axis=1)


def rope_tables(positions):
    half = C_ROPE // 2
    freqs = ROPE_THETA ** (-jnp.arange(half, dtype=F32) / half)
    ang = positions.astype(F32)[..., None] * freqs
    cos, sin = jnp.cos(ang), jnp.sin(ang)
    shape = positions.shape
    cq = jnp.concatenate([jnp.ones(shape + (C_NOPE,), F32), cos, cos, jnp.zeros(shape + (32,), F32)], axis=-1)
    sg = jnp.concatenate([jnp.zeros(shape + (C_NOPE,), F32), -sin, sin, jnp.zeros(shape + (32,), F32)], axis=-1)
    return cq, sg


def ffn_fwd(h2, wgu_t, wd):
    gu = matmul(h2, wgu_t, "nt", F32, "ffn_up")
    a = swiglu_fwd(gu)
    return matmul(a, wd, "nn", F32, "ffn_down"), (gu, a)


def ffn_bwd(dy2, h2, wgu_t, wd, saved):
    gu, a = saved
    da = matmul(dy2, wd, "nt", F32, "ffn_down_dx")
    dgu = swiglu_bwd(da, gu)
    dwd = matmul(a, dy2, "tn", F32, "ffn_down_dw")
    dwgu_t = matmul(dgu, h2, "tn", F32, "ffn_up_dw")
    dh = matmul(dgu, wgu_t, "nn", F32, "ffn_up_dx")
    return dh, dwgu_t, dwd


def hybrid_fwd(h2, shape, w, aux):
    bsz, seq = shape
    proj = matmul(h2, w["hin_t"], "nt", F32, "hyb_in").reshape(bsz, seq, HYB_PAD)
    pad = ((0, 0), (A_BLOCK, A_BLOCK), (0, 0))
    kp = jnp.pad(proj[:, :, 8 * LANES:9 * LANES].astype(BF16), pad)
    vp = jnp.pad(proj[:, :, 9 * LANES:10 * LANES].astype(BF16), pad)
    mo = wattn_fwd(proj, kp, vp, aux["posc"], aux["posr"], aux["sink"])
    mo, otot = hgrn_fwd(proj, mo, aux["lb"], aux["nw"])
    y = matmul(mo.reshape(bsz * seq, MO_PAD), w["hout"], "nn", F32, "hyb_out")
    return y, (proj, kp, vp, mo, otot)


def hybrid_bwd(dy2, h2, shape, w, aux, saved):
    bsz, seq = shape
    proj, kp, vp, mo, otot = saved
    mo2 = mo.reshape(bsz * seq, MO_PAD)
    dmo = matmul(dy2, w["hout"], "nt", F32, "hyb_out_dx").reshape(bsz, seq, MO_PAD)
    dhout = matmul(mo2, dy2, "tn", F32, "hyb_out_dw")
    dq, dkp, dvp, dsink = wattn_bwd(proj, kp, vp, aux["posc"], aux["posr"], aux["sink"], dmo)
    do, dhg, dnw = hgrn_out_bwd(proj, otot, dmo, aux["nw"])
    dhq, dhff, dhfb, dhi, dlb = hgrn_bwd(proj, do, aux["lb"])
    dproj = jnp.concatenate([dq, dkp[:, A_BLOCK:-A_BLOCK].astype(BF16), dvp[:, A_BLOCK:-A_BLOCK].astype(BF16),
                             dhq, dhff, dhfb, dhi, dhg], axis=-1).reshape(bsz * seq, HYB_PAD)
    dhin_t = matmul(dproj, h2, "tn", F32, "hyb_in_dw")
    dh = matmul(dproj, w["hin_t"], "nn", F32, "hyb_in_dx")
    return dh, dict(hin_t=dhin_t, hout=dhout), dict(sink=dsink, lb=dlb, nw=dnw)


def mla_fwd(h2, shape, w, aux):
    bsz, seq = shape
    t = bsz * seq
    cqkv = matmul(h2, w["mdown"], "nn", F32, "mla_down").reshape(bsz, seq, MLA_DOWN_PAD)
    nq, nkv, kr = mla_mid_fwd(cqkv, aux["qn"], aux["kvn"], aux["cq"], aux["sg"])
    q = matmul(nq.reshape(t, C_Q_RANK), w["uq_t"], "nt", F32, "mla_uq").reshape(bsz, seq, MLA_HEAD_PAD)
    qp = q_rope(q, aux["cq"], aux["sg"], False, "q_rope_fwd")
    kv = matmul(nkv.reshape(t, C_KV_RANK), w["ukv_t"], "nt", BF16, "mla_ukv").reshape(bsz, seq, MLA_HEAD_PAD)
    o, lse = mla_attn_fwd(qp, kv, kr)
    y = matmul(o.reshape(t, MLA_HEAD_PAD), w["mout"], "nn", F32, "mla_out")
    return y, (cqkv, nq, nkv, kr, qp, kv, o, lse)


def mla_bwd(dy2, h2, shape, w, aux, saved):
    bsz, seq = shape
    t = bsz * seq
    cqkv, nq, nkv, kr, qp, kv, o, lse = saved
    do = matmul(dy2, w["mout"], "nt", BF16, "mla_out_dx").reshape(bsz, seq, MLA_HEAD_PAD)
    dmout = matmul(o.reshape(t, MLA_HEAD_PAD), dy2, "tn", F32, "mla_out_dw")
    dqp, dkv, dkr = mla_attn_bwd(qp, kv, kr, o, do, lse)
    dq = q_rope(dqp, aux["cq"], aux["sg"], True, "q_rope_bwd").reshape(t, MLA_HEAD_PAD)
    dkv2 = dkv.reshape(t, MLA_HEAD_PAD)
    dnq = matmul(dq, w["uq_t"], "nn", F32, "mla_uq_dx").reshape(bsz, seq, C_Q_RANK)
    duq_t = matmul(dq, nq.reshape(t, C_Q_RANK), "tn", F32, "mla_uq_dw")
    dnkv = matmul(dkv2, w["ukv_t"], "nn", F32, "mla_ukv_dx").reshape(bsz, seq, C_KV_RANK)
    dukv_t = matmul(dkv2, nkv.reshape(t, C_KV_RANK), "tn", F32, "mla_ukv_dw")
    dcqkv, dqn, dkvn = mla_mid_bwd(dnq, dnkv, dkr, cqkv, aux["qn"], aux["kvn"], aux["cq"], aux["sg"])
    dcqkv2 = dcqkv.reshape(t, MLA_DOWN_PAD)
    dmdown = matmul(h2, dcqkv2, "tn", F32, "mla_down_dw")
    dh = matmul(dcqkv2, w["mdown"], "nt", F32, "mla_down_dx")
    return dh, dict(mdown=dmdown, uq_t=duq_t, ukv_t=dukv_t, mout=dmout), dict(qn=dqn, kvn=dkvn)


W_NAMES = ['ada_w', 'ada_b', 'ln_g', 'ln_b', 'ffn_w_gate', 'ffn_w_up', 'ffn_w_down', 'hyb_w_in', 'hyb_w_out',
           'attn_sink', 'hgrn_lb_logits', 'hgrn_norm_w', 'mla_w_down', 'mla_q_norm', 'mla_kv_norm', 'mla_w_uq',
           'mla_w_ukv', 'mla_w_out']
SMALL_NAMES = ['ada_b', 'ln_g', 'ln_b', 'attn_sink', 'hgrn_lb_logits', 'hgrn_norm_w', 'mla_q_norm', 'mla_kv_norm']
MOD_W = N_SUB * 3 * D_MODEL
MOD_SH = MOD_W // N_DEV
RES_W = (0.5, 1.0, 0.5)


def _rows1024(a, rows=None):
    flat = a.astype(F32).reshape(-1)
    n = flat.shape[0]
    total = (-(-n // ROW_W) if rows is None else rows) * ROW_W
    return jnp.pad(flat, (0, total - n)).reshape(-1, ROW_W)


def kernel(x, c, positions, ada_w, ada_b, ln_g, ln_b, ffn_w_gate, ffn_w_up, ffn_w_down, hyb_w_in, hyb_w_out, attn_sink, hgrn_lb_logits, hgrn_norm_w, mla_w_down, mla_q_norm, mla_kv_norm, mla_w_uq, mla_w_ukv, mla_w_out, loss_target, m_ada_w, m_ada_b, m_ln_g, m_ln_b, m_ffn_w_gate, m_ffn_w_up, m_ffn_w_down, m_hyb_w_in, m_hyb_w_out, m_attn_sink, m_hgrn_lb_logits, m_hgrn_norm_w, m_mla_w_down, m_mla_q_norm, m_mla_kv_norm, m_mla_w_uq, m_mla_w_ukv, m_mla_w_out, v_ada_w, v_ada_b, v_ln_g, v_ln_b, v_ffn_w_gate, v_ffn_w_up, v_ffn_w_down, v_hyb_w_in, v_hyb_w_out, v_attn_sink, v_hgrn_lb_logits, v_hgrn_norm_w, v_mla_w_down, v_mla_q_norm, v_mla_kv_norm, v_mla_w_uq, v_mla_w_ukv, v_mla_w_out):
    weights = dict(zip(W_NAMES, (ada_w, ada_b, ln_g, ln_b, ffn_w_gate, ffn_w_up, ffn_w_down, hyb_w_in, hyb_w_out,
                                 attn_sink, hgrn_lb_logits, hgrn_norm_w, mla_w_down, mla_q_norm, mla_kv_norm,
                                 mla_w_uq, mla_w_ukv, mla_w_out)))
    mom1 = dict(zip(W_NAMES, (m_ada_w, m_ada_b, m_ln_g, m_ln_b, m_ffn_w_gate, m_ffn_w_up, m_ffn_w_down, m_hyb_w_in,
                              m_hyb_w_out, m_attn_sink, m_hgrn_lb_logits, m_hgrn_norm_w, m_mla_w_down, m_mla_q_norm,
                              m_mla_kv_norm, m_mla_w_uq, m_mla_w_ukv, m_mla_w_out)))
    mom2 = dict(zip(W_NAMES, (v_ada_w, v_ada_b, v_ln_g, v_ln_b, v_ffn_w_gate, v_ffn_w_up, v_ffn_w_down, v_hyb_w_in,
                              v_hyb_w_out, v_attn_sink, v_hgrn_lb_logits, v_hgrn_norm_w, v_mla_w_down, v_mla_q_norm,
                              v_mla_kv_norm, v_mla_w_uq, v_mla_w_ukv, v_mla_w_out)))
    bsz, seq, d = x.shape
    t = bsz * seq
    nb_tot = N_DEV * bsz
    me = 4 * lax.axis_index("x") + 2 * lax.axis_index("y") + lax.axis_index("c")

    c_all = all_gather(c, "gather_c").reshape(nb_tot, d)
    cond = silu_small(c_all)
    ada_b_mine = lax.dynamic_slice_in_dim(ada_b, me * MOD_SH, MOD_SH, axis=1)
    modp = jnp.concatenate([matmul(cond, ada_w[l], "nn", F32, "ada_fwd", bias=ada_b_mine[l])
                            for l in range(DEPTH)], axis=1)
    mod_rows = nb_tot * DEPTH * MOD_SH // ROW_W
    small1 = jnp.concatenate([_rows1024(modp), _rows1024(jnp.concatenate(
        [ln_g.reshape(-1), ln_b.reshape(-1), mla_q_norm.reshape(-1), mla_kv_norm.reshape(-1)]), rows=4)], axis=0)
    g1 = all_gather(small1, "gather_mod")
    mod_all = g1[:, :mod_rows].reshape(N_DEV, nb_tot, DEPTH, MOD_SH)
    mod_all = jnp.transpose(mod_all, (1, 2, 0, 3)).reshape(nb_tot, DEPTH, N_SUB, 3, d)
    mod = lax.dynamic_slice_in_dim(mod_all, me * bsz, bsz, axis=0)
    tail = g1[:, mod_rows:].reshape(N_DEV, -1)
    n_ln = DEPTH * N_SUB * LANES
    ln_g_full = jnp.transpose(tail[:, :n_ln].reshape(N_DEV, DEPTH, N_SUB, LANES), (1, 2, 0, 3)).reshape(DEPTH, N_SUB, d)
    ln_b_full = jnp.transpose(tail[:, n_ln:2 * n_ln].reshape(N_DEV, DEPTH, N_SUB, LANES), (1, 2, 0, 3)).reshape(DEPTH, N_SUB, d)
    qn_full = tail[:, 2 * n_ln:2 * n_ln + 32].reshape(C_Q_RANK)
    kvn_full = tail[:, 2 * n_ln + 32:2 * n_ln + 64].reshape(C_KV_RANK)

    def mvec(l, s, j):
        return mod[:, l, s, j].reshape(bsz, 1, d)

    packed = pack_shards(ffn_w_gate, ffn_w_up, ffn_w_down, hyb_w_in, hyb_w_out, mla_w_down, mla_w_uq, mla_w_ukv,
                         mla_w_out, BF16)
    w = unpack_full(all_gather(packed, "gather_weights"))

    cq_tab, sg_tab = rope_tables(positions)
    lb0 = lb_fwd(hgrn_lb_logits, 0)
    aux = [dict(posc=positions.reshape(bsz, seq, 1),
                posr=jnp.pad(positions, ((0, 0), (A_BLOCK, A_BLOCK))).reshape(bsz, 1, seq + 2 * A_BLOCK),
                sink=attn_sink[0], lb=lb0.reshape(B_HEADS, 1, LANES), nw=hgrn_norm_w[0].reshape(B_HEADS, 1, LANES)),
           dict(qn=qn_full, kvn=kvn_full, cq=cq_tab, sg=sg_tab)]
    mixers = [(hybrid_fwd, hybrid_bwd), (mla_fwd, mla_bwd)]

    tape = []
    xin = x
    h = mod_fwd(x, mvec(0, 0, 1), mvec(0, 0, 0))
    for l in range(DEPTH):
        for s in range(N_SUB):
            h2 = h.reshape(t, d)
            if s == 1:
                y, saved = mixers[l][0](h2, (bsz, seq), w, aux[l])
            else:
                i = 2 * l + s // 2
                y, saved = ffn_fwd(h2, w["wgu_t"][i], w["wd"][i])
            y = y.reshape(bsz, seq, d)
            last = l == DEPTH - 1 and s == N_SUB - 1
            ln, sn = (l, s + 1) if s + 1 < N_SUB else (l + 1, 0)
            nxt = None if last else (mvec(ln, sn, 1), mvec(ln, sn, 0))
            xn, hn = resln_fwd(xin, y, mvec(l, s, 2), ln_g_full[l, s], ln_b_full[l, s], RES_W[s], nxt)
            tape.append((l, s, xin, h2, y, saved))
            xin, h = xn, hn
    loss_part, gout = loss_fwd(xin, loss_target)
    loss = lax.psum(loss_part, ("x", "y", "c"))

    dmod = [[[None] * 3 for _ in range(N_SUB)] for _ in range(DEPTH)]
    dln_g = [[None] * N_SUB for _ in range(DEPTH)]
    dln_b = [[None] * N_SUB for _ in range(DEPTH)]
    big = dict(wgu_t=[None] * N_FFN, wd=[None] * N_FFN)
    small = {}
    for l, s, xs, h2, y, saved in reversed(tape):
        dxa, dy, dgate, dg, db = resln_bwd(gout, xs, y, mvec(l, s, 2), ln_g_full[l, s], RES_W[s])
        dy2 = dy.reshape(t, d)
        if s == 1:
            dh, dbig, dsmall = mixers[l][1](dy2, h2, (bsz, seq), w, aux[l], saved)
            big.update(dbig)
            small.update(dsmall)
        else:
            i = 2 * l + s // 2
            dh, big["wgu_t"][i], big["wd"][i] = ffn_bwd(dy2, h2, w["wgu_t"][i], w["wd"][i], saved)
        gout, dscale, dshift = mod_bwd(dxa, dh.reshape(bsz, seq, d), xs, mvec(l, s, 1))
        dmod[l][s] = [dshift, dscale, dgate]
        dln_g[l][s], dln_b[l][s] = dg, db
    grad_x = gout
    big["wgu_t"] = jnp.stack(big["wgu_t"])
    big["wd"] = jnp.stack(big["wd"])

    dmod_mine = jnp.stack([jnp.stack([jnp.concatenate(dmod[l][s], axis=1) for s in range(N_SUB)], axis=1)
                           for l in range(DEPTH)], axis=1)
    dmod_rows = bsz * DEPTH * MOD_W // ROW_W
    misc = jnp.concatenate([small["lb"].reshape(-1), small["nw"].reshape(-1), small["qn"].reshape(-1),
                            small["kvn"].reshape(-1), small["sink"].reshape(-1)[:A_HEADS]])
    small2 = jnp.concatenate([_rows1024(dmod_mine),
                              _rows1024(jnp.stack([jnp.stack(r) for r in dln_g])),
                              _rows1024(jnp.stack([jnp.stack(r) for r in dln_b])),
                              _rows1024(misc, rows=2)], axis=0)
    g2 = all_gather(small2, "gather_small_grads")
    dmod_all = g2[:, :dmod_rows].reshape(nb_tot, DEPTH * MOD_W // ROW_W, ROW_W)
    grad_ada_b = sum_parts(dmod_all, "sum_ada_b").reshape(DEPTH, MOD_W)
    dmod_cols = lax.dynamic_slice_in_dim(dmod_all.reshape(nb_tot, DEPTH, MOD_W), me * MOD_SH, MOD_SH, axis=2)
    grad_ada_w = jnp.stack([matmul(cond, dmod_cols[:, l], "tn", F32, "ada_dw") for l in range(DEPTH)])
    rest = sum_parts(g2[:, dmod_rows:], "sum_small")
    n6 = DEPTH * N_SUB
    gl_g = lax.dynamic_slice_in_dim(rest[:n6].reshape(DEPTH, N_SUB, d), me * LANES, LANES, axis=2)
    gl_b = lax.dynamic_slice_in_dim(rest[n6:2 * n6].reshape(DEPTH, N_SUB, d), me * LANES, LANES, axis=2)
    mrow = rest[2 * n6:].reshape(-1)
    dlb0 = mrow[:512].reshape(1, 512)
    g_nw = mrow[512:1024].reshape(1, B_HEADS, LANES)
    g_qn = lax.dynamic_slice_in_dim(mrow[1024:1280], me * 32, 32).reshape(1, 32)
    g_kvn = lax.dynamic_slice_in_dim(mrow[1280:1536], me * 32, 32).reshape(1, 32)
    g_sink = mrow[1536:1536 + A_HEADS].reshape(1, A_HEADS)
    g_lb = lb_bwd(hgrn_lb_logits, dlb0, 0)

    reduced = sum_parts(all_to_all(pack_full(big), "exchange_grads"), "sum_grads")
    grads = unpack_shards(reduced)
    grads.update(ada_w=grad_ada_w, ada_b=grad_ada_b, ln_g=gl_g, ln_b=gl_b, attn_sink=g_sink,
                 hgrn_lb_logits=g_lb, hgrn_norm_w=g_nw, mla_q_norm=g_qn, mla_kv_norm=g_kvn)

    delta, new_m, new_v = {}, {}, {}
    for name in W_NAMES:
        if name in SMALL_NAMES:
            continue
        shp = weights[name].shape
        two_d = (-1, shp[-1])
        out = adamw(weights[name].reshape(two_d), grads[name].reshape(two_d), mom1[name].reshape(two_d),
                    mom2[name].reshape(two_d), "adamw_" + name)
        delta[name], new_m[name], new_v[name] = [o.reshape(shp) for o in out]

    def pack_small(src):
        flat = jnp.concatenate([src[n].reshape(-1) for n in SMALL_NAMES])
        return jnp.pad(flat, (0, -flat.shape[0] % (8 * LANES))).reshape(-1, LANES)

    outs = adamw(pack_small(weights), pack_small(grads), pack_small(mom1), pack_small(mom2), "adamw_small")
    off = 0
    for name in SMALL_NAMES:
        n = weights[name].size
        for dst, o in zip((delta, new_m, new_v), outs):
            dst[name] = o.reshape(-1)[off:off + n].reshape(weights[name].shape)
        off += n

    return (loss, grad_x, *[grads[n] for n in W_NAMES], *[delta[n] for n in W_NAMES],
            *[new_m[n] for n in W_NAMES], *[new_v[n] for n in W_NAMES])
```

```python
import functools
import math

import jax
import jax.numpy as jnp
import numpy as np
from jax import lax
from jax.experimental import pallas as pl
from jax.experimental.pallas import tpu as pltpu

F32 = jnp.float32
BF16 = jnp.bfloat16

D_MODEL = 1024
DEPTH = 2
D_FF = 2816
A_HEADS = 8
A_KV_HEADS = 2
A_HEAD_DIM = 64
WINDOW = 128
A_BLOCK = 128
B_HEADS = 4
B_KEY_DIM = 128
B_CHUNK = 64
C_HEADS = 16
C_Q_RANK = 256
C_KV_RANK = 256
C_NOPE = 64
C_ROPE = 32
C_V = 64
ROPE_THETA = 10000.0
LN_EPS = 1e-5
RMS_EPS = 1e-6
DEEPNORM_ALPHA = (2 * DEPTH) ** 0.25
N_SUB = 3
N_DEV = 8
LANES = 128
HYB_PAD = 3840
MO_PAD = 1536
MLA_DOWN_PAD = 640
MLA_HEAD_PAD = C_HEADS * LANES
ADAM_LR = 0.001
ADAM_B1 = 0.9
ADAM_B2 = 0.999
ADAM_EPS = 1e-08
ADAM_WD = 0.01
ADAM_STEP = 10
VMEM_LIMIT = 48 * 1024 * 1024
MESH_ID = pl.DeviceIdType.MESH
HIGHEST = lax.Precision.HIGHEST


def _cparams(sem=None):
    return pltpu.CompilerParams(dimension_semantics=sem, vmem_limit_bytes=VMEM_LIMIT)


def _tile(n, cap):
    if n <= cap:
        return n
    t = (cap // LANES) * LANES
    while t >= LANES:
        if n % t == 0:
            return t
        t -= LANES
    return n


def _rows_tile(n, cap):
    if n <= cap:
        return n
    t = cap
    while t >= 8:
        if n % t == 0:
            return t
        t -= 8
    return n


def _sigmoid(x):
    return 1.0 / (1.0 + jnp.exp(-x))


def _dot(a, b, dims, precision=None):
    return lax.dot_general(a, b, (dims, ((), ())), precision=precision,
                           preferred_element_type=F32)


NN = ((1,), (0,))
NT = ((1,), (1,))
TN = ((0,), (0,))


def matmul(a, b, form, out_dtype, name, bias=None, tm_cap=512, tn_cap=1024, tk_cap=1024):
    if form == "nn":
        (m, k), (k2, n) = a.shape, b.shape
    elif form == "nt":
        (m, k), (n, k2) = a.shape, b.shape
    else:
        (k, m), (k2, n) = a.shape, b.shape
    assert k == k2, (a.shape, b.shape, form)
    tm = _tile(m, tm_cap)
    tn = _tile(n, tn_cap)
    tk = k if k <= 2 * D_FF else _tile(k, tk_cap)
    nk = k // tk
    dims = {"nn": NN, "nt": NT, "tn": TN}[form]

    def body(*refs):
        if bias is None:
            a_ref, b_ref, o_ref, acc_ref = refs
            bias_ref = None
        else:
            a_ref, b_ref, bias_ref, o_ref, acc_ref = refs
        kk = pl.program_id(2)
        part = _dot(a_ref[...].astype(BF16), b_ref[...].astype(BF16), dims)

        def finish(total):
            if bias_ref is not None:
                total = total + bias_ref[...]
            o_ref[...] = total.astype(o_ref.dtype)

        if nk == 1:
            finish(part)
        else:
            @pl.when(kk == 0)
            def _():
                acc_ref[...] = part

            @pl.when(jnp.logical_and(kk > 0, kk < nk - 1))
            def _():
                acc_ref[...] += part

            @pl.when(kk == nk - 1)
            def _():
                finish(acc_ref[...] + part)

    if form == "nn":
        a_spec = pl.BlockSpec((tm, tk), lambda i, j, kk: (i, kk))
        b_spec = pl.BlockSpec((tk, tn), lambda i, j, kk: (kk, j))
    elif form == "nt":
        a_spec = pl.BlockSpec((tm, tk), lambda i, j, kk: (i, kk))
        b_spec = pl.BlockSpec((tn, tk), lambda i, j, kk: (j, kk))
    else:
        a_spec = pl.BlockSpec((tk, tm), lambda i, j, kk: (kk, i))
        b_spec = pl.BlockSpec((tk, tn), lambda i, j, kk: (kk, j))
    in_specs = [a_spec, b_spec]
    args = [a, b]
    if bias is not None:
        in_specs.append(pl.BlockSpec((1, tn), lambda i, j, kk: (0, j)))
        args.append(bias.reshape(1, n))
    return pl.pallas_call(
        body, name=name,
        out_shape=jax.ShapeDtypeStruct((m, n), out_dtype),
        grid=(m // tm, n // tn, nk),
        in_specs=in_specs,
        out_specs=pl.BlockSpec((tm, tn), lambda i, j, kk: (i, j)),
        scratch_shapes=[pltpu.VMEM((tm, tn), F32)],
        compiler_params=_cparams(("parallel", "parallel", "arbitrary")),
    )(*args)


TOK_TILE = 256


def _vec_spec(d):
    return pl.BlockSpec((1, 1, d), lambda b, s: (b, 0, 0))


def _tok_spec(ts, d):
    return pl.BlockSpec((1, ts, d), lambda b, s: (b, s, 0))


def mod_fwd(x, scale, shift):
    bsz, seq, d = x.shape
    ts = _rows_tile(seq, TOK_TILE)

    def body(x_ref, sc_ref, sh_ref, h_ref):
        h_ref[0] = (x_ref[0] * (1.0 + sc_ref[0]) + sh_ref[0]).astype(BF16)

    return pl.pallas_call(
        body, name="mod_fwd",
        out_shape=jax.ShapeDtypeStruct((bsz, seq, d), BF16),
        grid=(bsz, seq // ts),
        in_specs=[_tok_spec(ts, d), _vec_spec(d), _vec_spec(d)],
        out_specs=_tok_spec(ts, d),
        compiler_params=_cparams(("parallel", "parallel")),
    )(x, scale, shift)


def _ln_stats(z):
    mu = jnp.mean(z, axis=-1, keepdims=True)
    zc = z - mu
    var = jnp.mean(zc * zc, axis=-1, keepdims=True)
    return zc, lax.rsqrt(var + LN_EPS)


def resln_fwd(x, y, gate, g, b, res_w, nxt):
    bsz, seq, d = x.shape
    ts = _rows_tile(seq, TOK_TILE)

    def body(*refs):
        if nxt is None:
            x_ref, y_ref, gt_ref, g_ref, b_ref, xn_ref = refs
        else:
            x_ref, y_ref, gt_ref, g_ref, b_ref, sc_ref, sh_ref, xn_ref, hn_ref = refs
        z = DEEPNORM_ALPHA * x_ref[0] + (res_w * (1.0 + gt_ref[0])) * y_ref[0]
        zc, r = _ln_stats(z)
        xn = zc * r * g_ref[...] + b_ref[...]
        xn_ref[0] = xn
        if nxt is not None:
            hn_ref[0] = (xn * (1.0 + sc_ref[0]) + sh_ref[0]).astype(BF16)

    row = pl.BlockSpec((1, d), lambda bb, s: (0, 0))
    in_specs = [_tok_spec(ts, d), _tok_spec(ts, d), _vec_spec(d), row, row]
    args = [x, y, gate, g.reshape(1, d), b.reshape(1, d)]
    out_shape = [jax.ShapeDtypeStruct((bsz, seq, d), F32)]
    out_specs = [_tok_spec(ts, d)]
    if nxt is not None:
        in_specs += [_vec_spec(d), _vec_spec(d)]
        args += list(nxt)
        out_shape.append(jax.ShapeDtypeStruct((bsz, seq, d), BF16))
        out_specs.append(_tok_spec(ts, d))
    out = pl.pallas_call(
        body, name="resln_fwd",
        out_shape=out_shape, grid=(bsz, seq // ts),
        in_specs=in_specs, out_specs=out_specs,
        compiler_params=_cparams(("parallel", "parallel")),
    )(*args)
    return (out[0], None) if nxt is None else (out[0], out[1])


def resln_bwd(gout, x, y, gate, g, res_w):
    bsz, seq, d = x.shape
    ts = _rows_tile(seq, TOK_TILE)

    def body(go_ref, x_ref, y_ref, gt_ref, g_ref, dxa_ref, dy_ref, dgt_ref, dg_ref, db_ref):
        bb, s = pl.program_id(0), pl.program_id(1)
        yv = y_ref[0]
        rw = res_w * (1.0 + gt_ref[0])
        z = DEEPNORM_ALPHA * x_ref[0] + rw * yv
        zc, r = _ln_stats(z)
        xhat = zc * r
        go = go_ref[0]
        dxh = go * g_ref[...]
        dz = r * (dxh - jnp.mean(dxh, axis=-1, keepdims=True)
                  - xhat * jnp.mean(dxh * xhat, axis=-1, keepdims=True))
        dxa_ref[0] = DEEPNORM_ALPHA * dz
        dy_ref[0] = (rw * dz).astype(BF16)
        dgt = res_w * jnp.sum(dz * yv, axis=0, keepdims=True)
        dg = jnp.sum(go * xhat, axis=0, keepdims=True)
        db = jnp.sum(go, axis=0, keepdims=True)

        @pl.when(s == 0)
        def _():
            dgt_ref[0] = dgt

        @pl.when(s > 0)
        def _():
            dgt_ref[0] += dgt

        first = jnp.logical_and(bb == 0, s == 0)

        @pl.when(first)
        def _():
            dg_ref[...] = dg
            db_ref[...] = db

        @pl.when(jnp.logical_not(first))
        def _():
            dg_ref[...] += dg
            db_ref[...] += db

    row = pl.BlockSpec((1, d), lambda bb, s: (0, 0))
    return pl.pallas_call(
        body, name="resln_bwd",
        out_shape=[jax.ShapeDtypeStruct((bsz, seq, d), F32),
                   jax.ShapeDtypeStruct((bsz, seq, d), BF16),
                   jax.ShapeDtypeStruct((bsz, 1, d), F32),
                   jax.ShapeDtypeStruct((1, d), F32),
                   jax.ShapeDtypeStruct((1, d), F32)],
        grid=(bsz, seq // ts),
        in_specs=[_tok_spec(ts, d), _tok_spec(ts, d), _tok_spec(ts, d), _vec_spec(d), row],
        out_specs=[_tok_spec(ts, d), _tok_spec(ts, d), _vec_spec(d), row, row],
        compiler_params=_cparams(("arbitrary", "arbitrary")),
    )(gout, x, y, gate, g.reshape(1, d))


def mod_bwd(dxa, dh, x, scale):
    bsz, seq, d = x.shape
    ts = _rows_tile(seq, TOK_TILE)

    def body(dxa_ref, dh_ref, x_ref, sc_ref, dx_ref, dsc_ref, dsh_ref):
        s = pl.program_id(1)
        dh = dh_ref[0].astype(F32)
        dx_ref[0] = dxa_ref[0] + dh * (1.0 + sc_ref[0])
        dsc = jnp.sum(dh * x_ref[0], axis=0, keepdims=True)
        dsh = jnp.sum(dh, axis=0, keepdims=True)

        @pl.when(s == 0)
        def _():
            dsc_ref[0] = dsc
            dsh_ref[0] = dsh

        @pl.when(s > 0)
        def _():
            dsc_ref[0] += dsc
            dsh_ref[0] += dsh

    return pl.pallas_call(
        body, name="mod_bwd",
        out_shape=[jax.ShapeDtypeStruct((bsz, seq, d), F32),
                   jax.ShapeDtypeStruct((bsz, 1, d), F32),
                   jax.ShapeDtypeStruct((bsz, 1, d), F32)],
        grid=(bsz, seq // ts),
        in_specs=[_tok_spec(ts, d), _tok_spec(ts, d), _tok_spec(ts, d), _vec_spec(d)],
        out_specs=[_tok_spec(ts, d), _vec_spec(d), _vec_spec(d)],
        compiler_params=_cparams(("parallel", "arbitrary")),
    )(dxa, dh, x, scale)


FF_HALF = D_FF // 2
FF_CHUNKS = ((0, 384), (384, 384), (768, 384), (1152, 256))


def ffn_up_fused(h2, wgu):
    t, d = h2.shape
    tm = _rows_tile(t, 512)

    def body(h_ref, w_ref, gu_ref, a_ref):
        h = h_ref[...]
        for c0, cw in FF_CHUNKS:
            g = _dot(h, w_ref[:, c0:c0 + cw], NN)
            u = _dot(h, w_ref[:, FF_HALF + c0:FF_HALF + c0 + cw], NN)
            gu_ref[:, c0:c0 + cw] = g.astype(BF16)
            gu_ref[:, FF_HALF + c0:FF_HALF + c0 + cw] = u.astype(BF16)
            a_ref[:, c0:c0 + cw] = (g * _sigmoid(g) * u).astype(BF16)

    return pl.pallas_call(
        body, name="ffn_up_fused",
        out_shape=[jax.ShapeDtypeStruct((t, 2 * D_FF), BF16), jax.ShapeDtypeStruct((t, D_FF), BF16)],
        grid=(2, t // tm),
        in_specs=[pl.BlockSpec((tm, d), lambda j, i: (i, 0)),
                  pl.BlockSpec((d, 2 * FF_HALF), lambda j, i: (0, j))],
        out_specs=[pl.BlockSpec((tm, 2 * FF_HALF), lambda j, i: (i, j)),
                   pl.BlockSpec((tm, FF_HALF), lambda j, i: (i, j))],
        compiler_params=_cparams(("parallel", "parallel")),
    )(h2, wgu)


def ffn_down_bwd_fused(dy2, wd_t, gu):
    t, d = dy2.shape
    tm = _rows_tile(t, 512)

    def body(dy_ref, w_ref, gu_ref, o_ref):
        dy = dy_ref[...]
        for c0, cw in FF_CHUNKS:
            da = _dot(dy, w_ref[:, c0:c0 + cw], NN)
            g = gu_ref[:, c0:c0 + cw].astype(F32)
            u = gu_ref[:, FF_HALF + c0:FF_HALF + c0 + cw].astype(F32)
            sg = _sigmoid(g)
            o_ref[:, c0:c0 + cw] = (da * u * (sg * (1.0 + g * (1.0 - sg)))).astype(BF16)
            o_ref[:, FF_HALF + c0:FF_HALF + c0 + cw] = (da * (g * sg)).astype(BF16)

    return pl.pallas_call(
        body, name="ffn_down_bwd_fused",
        out_shape=jax.ShapeDtypeStruct((t, 2 * D_FF), BF16),
        grid=(2, t // tm),
        in_specs=[pl.BlockSpec((tm, d), lambda j, i: (i, 0)),
                  pl.BlockSpec((d, FF_HALF), lambda j, i: (0, j)),
                  pl.BlockSpec((tm, 2 * FF_HALF), lambda j, i: (i, j))],
        out_specs=pl.BlockSpec((tm, 2 * FF_HALF), lambda j, i: (i, j)),
        compiler_params=_cparams(("parallel", "parallel")),
    )(dy2, wd_t, gu)


def loss_fwd(y, tgt):
    bsz, seq, d = y.shape
    ts = _rows_tile(seq, TOK_TILE)

    def body(y_ref, t_ref, dy_ref, l_ref, acc_ref):
        bb, s = pl.program_id(0), pl.program_id(1)
        e = y_ref[0] - t_ref[0]
        dy_ref[0] = e * (1.0 / d)
        part = jnp.sum((e * e).reshape(ts // 8, 8, d), axis=0)
        first = jnp.logical_and(bb == 0, s == 0)

        @pl.when(first)
        def _():
            acc_ref[...] = part

        @pl.when(jnp.logical_not(first))
        def _():
            acc_ref[...] += part

        @pl.when(jnp.logical_and(bb == pl.num_programs(0) - 1, s == pl.num_programs(1) - 1))
        def _():
            tot = jnp.sum(jnp.sum(acc_ref[...], axis=1, keepdims=True), axis=0, keepdims=True)
            l_ref[...] = jnp.broadcast_to(tot * (0.5 / d), (8, LANES))

    dy, l = pl.pallas_call(
        body, name="loss_fwd",
        out_shape=[jax.ShapeDtypeStruct((bsz, seq, d), F32), jax.ShapeDtypeStruct((8, LANES), F32)],
        grid=(bsz, seq // ts),
        in_specs=[_tok_spec(ts, d), _tok_spec(ts, d)],
        out_specs=[_tok_spec(ts, d), pl.BlockSpec((8, LANES), lambda bb, s: (0, 0))],
        scratch_shapes=[pltpu.VMEM((8, d), F32)],
        compiler_params=_cparams(("arbitrary", "arbitrary")),
    )(y, tgt)
    return l[0, 0], dy


def _rot_half(x):
    lane = lax.broadcasted_iota(jnp.int32, x.shape, 1)
    swapped = jnp.where(lane < 80, pltpu.roll(x, 112, axis=1), pltpu.roll(x, 16, axis=1))
    return jnp.where((lane >= C_NOPE) & (lane < C_NOPE + C_ROPE), swapped, 0.0)


def _rms(x, w):
    r = lax.rsqrt(jnp.mean(x * x, axis=-1, keepdims=True) + RMS_EPS)
    return x * r * w, r


def _rms_bwd(dy, x, w):
    r = lax.rsqrt(jnp.mean(x * x, axis=-1, keepdims=True) + RMS_EPS)
    wd = dy * w
    dx = r * wd - x * (r * r * r) * jnp.mean(x * wd, axis=-1, keepdims=True)
    return dx, jnp.sum(dy * x * r, axis=0, keepdims=True)


def mla_mid_fwd(cqkv, qw, kw, cq_tab, sg_tab):
    bsz, seq, _ = cqkv.shape
    ts = _rows_tile(seq, TOK_TILE)
    r = C_Q_RANK

    def body(x_ref, qw_ref, kw_ref, c_ref, s_ref, nq_ref, nkv_ref, kr_ref):
        x = x_ref[0]
        nq_ref[0] = _rms(x[:, :r], qw_ref[...])[0].astype(BF16)
        nkv_ref[0] = _rms(x[:, r:2 * r], kw_ref[...])[0].astype(BF16)
        xr = x[:, 2 * r:]
        kr_ref[0] = (xr * c_ref[0] + _rot_half(xr) * s_ref[0]).astype(BF16)

    row = pl.BlockSpec((1, r), lambda b, s: (0, 0))
    return pl.pallas_call(
        body, name="mla_mid_fwd",
        out_shape=[jax.ShapeDtypeStruct((bsz, seq, r), BF16), jax.ShapeDtypeStruct((bsz, seq, r), BF16),
                   jax.ShapeDtypeStruct((bsz, seq, LANES), BF16)],
        grid=(bsz, seq // ts),
        in_specs=[_tok_spec(ts, MLA_DOWN_PAD), row, row, _tok_spec(ts, LANES), _tok_spec(ts, LANES)],
        out_specs=[_tok_spec(ts, r), _tok_spec(ts, r), _tok_spec(ts, LANES)],
        compiler_params=_cparams(("parallel", "parallel")),
    )(cqkv, qw.reshape(1, r), kw.reshape(1, r), cq_tab, sg_tab)


def mla_mid_bwd(dnq, dnkv, dkr, cqkv, qw, kw, cq_tab, sg_tab):
    bsz, seq, _ = cqkv.shape
    ts = _rows_tile(seq, TOK_TILE)
    r = C_Q_RANK

    def body(dnq_ref, dnkv_ref, dkr_ref, x_ref, qw_ref, kw_ref, c_ref, s_ref, dx_ref, dqw_ref, dkw_ref):
        bb, s = pl.program_id(0), pl.program_id(1)
        x = x_ref[0]
        dcq, dqw = _rms_bwd(dnq_ref[0].astype(F32), x[:, :r], qw_ref[...])
        dckv, dkw = _rms_bwd(dnkv_ref[0].astype(F32), x[:, r:2 * r], kw_ref[...])
        dk = dkr_ref[0]
        dxr = dk * c_ref[0] + _rot_half(dk * s_ref[0])
        dx_ref[0, :, :r] = dcq.astype(BF16)
        dx_ref[0, :, r:2 * r] = dckv.astype(BF16)
        dx_ref[0, :, 2 * r:] = dxr.astype(BF16)
        first = jnp.logical_and(bb == 0, s == 0)

        @pl.when(first)
        def _():
            dqw_ref[...] = dqw
            dkw_ref[...] = dkw

        @pl.when(jnp.logical_not(first))
        def _():
            dqw_ref[...] += dqw
            dkw_ref[...] += dkw

    row = pl.BlockSpec((1, r), lambda b, s: (0, 0))
    return pl.pallas_call(
        body, name="mla_mid_bwd",
        out_shape=[jax.ShapeDtypeStruct((bsz, seq, MLA_DOWN_PAD), BF16),
                   jax.ShapeDtypeStruct((1, r), F32), jax.ShapeDtypeStruct((1, r), F32)],
        grid=(bsz, seq // ts),
        in_specs=[_tok_spec(ts, r), _tok_spec(ts, r), _tok_spec(ts, LANES), _tok_spec(ts, MLA_DOWN_PAD),
                  row, row, _tok_spec(ts, LANES), _tok_spec(ts, LANES)],
        out_specs=[_tok_spec(ts, MLA_DOWN_PAD), row, row],
        compiler_params=_cparams(("arbitrary", "arbitrary")),
    )(dnq, dnkv, dkr, cqkv, qw.reshape(1, r), kw.reshape(1, r), cq_tab, sg_tab)


def q_rope(q, cq_tab, sg_tab, transpose_rule, name, out_scale=1.0):
    bsz, seq, w = q.shape
    ts = _rows_tile(seq, TOK_TILE)

    def body(q_ref, c_ref, s_ref, o_ref):
        c, s = c_ref[0], s_ref[0]
        for h in range(w // LANES):
            x = q_ref[0, :, h * LANES:(h + 1) * LANES].astype(F32)
            y = x * c + (_rot_half(x * s) if transpose_rule else _rot_half(x) * s)
            o_ref[0, :, h * LANES:(h + 1) * LANES] = (y * out_scale).astype(BF16)

    return pl.pallas_call(
        body, name=name,
        out_shape=jax.ShapeDtypeStruct((bsz, seq, w), BF16),
        grid=(bsz, seq // ts),
        in_specs=[_tok_spec(ts, w), _tok_spec(ts, LANES), _tok_spec(ts, LANES)],
        out_specs=_tok_spec(ts, w),
        compiler_params=_cparams(("parallel", "parallel")),
    )(q, cq_tab, sg_tab)


MLA_SCALE = (C_NOPE + C_ROPE) ** -0.5
LOG2E = math.log2(math.e)
MLA_QSCALE = MLA_SCALE * LOG2E


def _eye_mask(n):
    return lax.broadcasted_iota(jnp.int32, (n, n), 0) == lax.broadcasted_iota(jnp.int32, (n, n), 1)


def mla_attn_fwd(qp, kv, kr):
    bsz, seq, _ = qp.shape
    tq = _rows_tile(seq, 256)

    def body(q_ref, kv_ref, kr_ref, o_ref, lse_ref, kc_ref):
        i = pl.program_id(2)
        lane = lax.broadcasted_iota(jnp.int32, (seq, LANES), 1)

        @pl.when(i == 0)
        def _():
            kc_ref[...] = jnp.where(lane < C_NOPE, kv_ref[0], kr_ref[0])

        s = _dot(q_ref[0], kc_ref[...], NT)
        m = jnp.max(s, axis=-1, keepdims=True)
        p = jnp.exp2(s - m)
        l = jnp.sum(p, axis=-1, keepdims=True)
        o = _dot(p.astype(BF16), kv_ref[0], NN) * (1.0 / l)
        lane_q = lax.broadcasted_iota(jnp.int32, (tq, LANES), 1)
        o_ref[0] = jnp.where(lane_q >= C_NOPE, o, 0.0).astype(BF16)
        lse = m + jnp.log2(l)
        lse_ref[0, 0] = jnp.sum(jnp.where(_eye_mask(tq), lse, 0.0), axis=0, keepdims=True)

    return pl.pallas_call(
        body, name="mla_attn_fwd",
        out_shape=[jax.ShapeDtypeStruct((bsz, seq, MLA_HEAD_PAD), BF16),
                   jax.ShapeDtypeStruct((bsz, C_HEADS, 1, seq), F32)],
        grid=(bsz, C_HEADS, seq // tq),
        in_specs=[pl.BlockSpec((1, tq, LANES), lambda b, h, i: (b, i, h)),
                  pl.BlockSpec((1, seq, LANES), lambda b, h, i: (b, 0, h)),
                  pl.BlockSpec((1, seq, LANES), lambda b, h, i: (b, 0, 0))],
        out_specs=[pl.BlockSpec((1, tq, LANES), lambda b, h, i: (b, i, h)),
                   pl.BlockSpec((1, 1, 1, tq), lambda b, h, i: (b, h, 0, i))],
        scratch_shapes=[pltpu.VMEM((seq, LANES), BF16)],
        compiler_params=_cparams(("parallel", "parallel", "arbitrary")),
    )(qp, kv, kr)


def mla_attn_bwd(qp, kv, kr, o, do, lse):
    bsz, seq, _ = qp.shape
    tq = _rows_tile(seq, 512)
    nt = seq // tq

    def body(q_ref, kv_ref, kr_ref, o_ref, do_ref, lse_ref, dq_ref, dkv_ref, dkr_ref,
             kc_ref, drow_ref, dqa_ref, dkc_ref, dkvv_ref):
        h = pl.program_id(1)
        lane_s = lax.broadcasted_iota(jnp.int32, (seq, LANES), 1)
        lane_t = lax.broadcasted_iota(jnp.int32, (tq, LANES), 1)
        kc_ref[...] = jnp.where(lane_s < C_NOPE, kv_ref[0], kr_ref[0])
        dqa_ref[...] = jnp.zeros_like(dqa_ref)
        ones = jnp.ones((8, LANES), F32)

        def delta_body(i, carry):
            rows = pl.ds(pl.multiple_of(i * tq, tq), tq)
            prod = do_ref[0, rows, :].astype(F32) * o_ref[0, rows, :].astype(F32)
            drow_ref[:, rows] = _dot(ones, prod, NT, precision=HIGHEST)
            return carry

        lax.fori_loop(0, nt, delta_body, 0)

        def key_body(j, carry):
            krows = pl.ds(pl.multiple_of(j * tq, tq), tq)
            kc_j = kc_ref[krows, :]
            kv_j = kv_ref[0, krows, :]
            dkc_ref[...] = jnp.zeros_like(dkc_ref)
            dkvv_ref[...] = jnp.zeros_like(dkvv_ref)

            def q_body(i, c2):
                qrows = pl.ds(pl.multiple_of(i * tq, tq), tq)
                q_i = q_ref[0, qrows, :]
                do_i = do_ref[0, qrows, :]
                st = _dot(kc_j, q_i, NT)
                pt = jnp.exp2(st - lse_ref[0, 0, :, qrows])
                dpt = _dot(kv_j, do_i, NT)
                dst = (pt * (dpt - drow_ref[0:1, qrows])).astype(BF16)
                dkvv_ref[...] += _dot(pt.astype(BF16), do_i, NN)
                dkc_ref[...] += _dot(dst, q_i, NN)
                dqa_ref[qrows, :] += _dot(dst, kc_j, TN)
                return c2

            lax.fori_loop(0, nt, q_body, 0)
            dkc = dkc_ref[...] * (MLA_SCALE / MLA_QSCALE)
            dkv_ref[0, krows, :] = jnp.where(lane_t < C_NOPE, dkc, dkvv_ref[...]).astype(BF16)
            dkr_j = jnp.where(lane_t >= C_NOPE, dkc, 0.0)

            @pl.when(h == 0)
            def _():
                dkr_ref[0, krows, :] = dkr_j

            @pl.when(h > 0)
            def _():
                dkr_ref[0, krows, :] += dkr_j

            return carry

        lax.fori_loop(0, nt, key_body, 0)
        dq_ref[0] = (dqa_ref[...] * MLA_SCALE).astype(BF16)

    head = pl.BlockSpec((1, seq, LANES), lambda b, h: (b, 0, h))
    shared = pl.BlockSpec((1, seq, LANES), lambda b, h: (b, 0, 0))
    return pl.pallas_call(
        body, name="mla_attn_bwd",
        out_shape=[jax.ShapeDtypeStruct((bsz, seq, MLA_HEAD_PAD), BF16),
                   jax.ShapeDtypeStruct((bsz, seq, MLA_HEAD_PAD), BF16),
                   jax.ShapeDtypeStruct((bsz, seq, LANES), F32)],
        grid=(bsz, C_HEADS),
        in_specs=[head, head, shared, head, head,
                  pl.BlockSpec((1, 1, 1, seq), lambda b, h: (b, h, 0, 0))],
        out_specs=[head, head, shared],
        scratch_shapes=[pltpu.VMEM((seq, LANES), BF16), pltpu.VMEM((8, seq), F32),
                        pltpu.VMEM((seq, LANES), F32), pltpu.VMEM((tq, LANES), F32),
                        pltpu.VMEM((tq, LANES), F32)],
        compiler_params=_cparams(("parallel", "arbitrary")),
    )(qp, kv, kr, o, do, lse)


A_SCALE = A_HEAD_DIM ** -0.5
A_GROUP = A_HEADS // A_KV_HEADS
A_BAND = 3 * A_BLOCK
NEG_BIG = -1e30


def _wattn_block(i, seq, posc_ref, posr_ref):
    start = pl.multiple_of(i * A_BLOCK, A_BLOCK)
    pq = posc_ref[0]
    pk = posr_ref[0, :, pl.ds(start, A_BAND)]
    dist = jnp.abs(pq - pk).astype(F32)
    qi = lax.broadcasted_iota(jnp.int32, (A_BLOCK, A_BAND), 0) + A_BLOCK
    ki = lax.broadcasted_iota(jnp.int32, (A_BLOCK, A_BAND), 1)
    absk = i * A_BLOCK + ki - A_BLOCK
    valid = (jnp.abs(qi - ki) <= WINDOW) & (absk >= 0) & (absk < seq)
    return start, dist, valid


def _wattn_probs(qh, kb, dist, valid, slope, sink):
    s = _dot(qh, kb, NT) * A_SCALE - slope * dist
    s = jnp.where(valid, s, NEG_BIG)
    m = jnp.maximum(jnp.max(s, axis=-1, keepdims=True), sink)
    p = jnp.exp(s - m)
    es = jnp.exp(sink - m)
    inv = 1.0 / (jnp.sum(p, axis=-1, keepdims=True) + es)
    return p * inv, es * inv


def _alibi_slope(h):
    return 2.0 ** (-8.0 * (h + 1) / A_HEADS)


def wattn_fwd(proj, kp, vp, posc, posr, sink):
    bsz, seq, _ = proj.shape
    nb = seq // A_BLOCK
    qw = A_HEADS * LANES

    def body(sink_ref, q_ref, kp_ref, vp_ref, posc_ref, posr_ref, o_ref):
        i = pl.program_id(1)
        start, dist, valid = _wattn_block(i, seq, posc_ref, posr_ref)
        kb = kp_ref[0, pl.ds(start, A_BAND), :]
        vb = vp_ref[0, pl.ds(start, A_BAND), :]
        lane = lax.broadcasted_iota(jnp.int32, (A_BLOCK, LANES), 1)
        for h in range(A_HEADS):
            g = h // A_GROUP
            qh = q_ref[0, :, h * LANES:(h + 1) * LANES].astype(BF16)
            p, _ = _wattn_probs(qh, kb, dist, valid, _alibi_slope(h), sink_ref[h])
            o = _dot(p.astype(BF16), vb, NN)
            mine = (lane >= A_HEAD_DIM) if g == 1 else (lane < A_HEAD_DIM)
            o_ref[0, :, h * LANES:(h + 1) * LANES] = jnp.where(mine, o, 0.0).astype(BF16)

    return pl.pallas_call(
        body, name="wattn_fwd",
        out_shape=jax.ShapeDtypeStruct((bsz, seq, MO_PAD), BF16),
        grid=(bsz, nb),
        in_specs=[pl.BlockSpec(memory_space=pltpu.SMEM),
                  pl.BlockSpec((1, A_BLOCK, qw), lambda b, i: (b, i, 0)),
                  pl.BlockSpec((1, seq + 2 * A_BLOCK, LANES), lambda b, i: (b, 0, 0)),
                  pl.BlockSpec((1, seq + 2 * A_BLOCK, LANES), lambda b, i: (b, 0, 0)),
                  pl.BlockSpec((1, A_BLOCK, 1), lambda b, i: (b, i, 0)),
                  pl.BlockSpec((1, 1, seq + 2 * A_BLOCK), lambda b, i: (b, 0, 0))],
        out_specs=pl.BlockSpec((1, A_BLOCK, qw), lambda b, i: (b, i, 0)),
        compiler_params=_cparams(("parallel", "parallel")),
    )(sink, proj, kp, vp, posc, posr)


def wattn_bwd(proj, kp, vp, posc, posr, sink, dmo):
    bsz, seq, _ = proj.shape
    nb = seq // A_BLOCK
    qw = A_HEADS * LANES
    sp = seq + 2 * A_BLOCK

    def body(sink_ref, q_ref, kp_ref, vp_ref, posc_ref, posr_ref, do_ref,
             dq_ref, dk_ref, dv_ref, ds_ref):
        i = pl.program_id(1)

        @pl.when(i == 0)
        def _():
            dk_ref[...] = jnp.zeros_like(dk_ref)
            dv_ref[...] = jnp.zeros_like(dv_ref)

        @pl.when(jnp.logical_and(i == 0, pl.program_id(0) == 0))
        def _():
            ds_ref[...] = jnp.zeros_like(ds_ref)

        start, dist, valid = _wattn_block(i, seq, posc_ref, posr_ref)
        kb = kp_ref[0, pl.ds(start, A_BAND), :]
        vb = vp_ref[0, pl.ds(start, A_BAND), :]
        lane = lax.broadcasted_iota(jnp.int32, (A_BLOCK, LANES), 1)
        lane1 = lax.broadcasted_iota(jnp.int32, (1, LANES), 1)
        dk_acc = jnp.zeros((A_BAND, LANES), F32)
        dv_acc = jnp.zeros((A_BAND, LANES), F32)
        dsink = jnp.zeros((1, LANES), F32)
        for h in range(A_HEADS):
            g = h // A_GROUP
            qh = q_ref[0, :, h * LANES:(h + 1) * LANES].astype(BF16)
            p, psink = _wattn_probs(qh, kb, dist, valid, _alibi_slope(h), sink_ref[h])
            pb = p.astype(BF16)
            do = do_ref[0, :, h * LANES:(h + 1) * LANES]
            dob = do.astype(BF16)
            mine = (lane >= A_HEAD_DIM) if g == 1 else (lane < A_HEAD_DIM)
            o = jnp.where(mine, _dot(pb, vb, NN), 0.0)
            delta = jnp.sum(do * o, axis=-1, keepdims=True)
            dp = _dot(dob, vb, NT)
            ds = (p * (dp - delta) * A_SCALE).astype(BF16)
            dq_ref[0, :, h * LANES:(h + 1) * LANES] = _dot(ds, kb, NN).astype(BF16)
            dk_acc = dk_acc + _dot(ds, qh, TN)
            dv_acc = dv_acc + _dot(pb, dob, TN)
            dsh = -jnp.sum(psink * delta, axis=0, keepdims=True)
            dsink = dsink + jnp.where(lane1 == h, dsh, 0.0)
        dk_ref[0, pl.ds(start, A_BAND), :] += dk_acc
        dv_ref[0, pl.ds(start, A_BAND), :] += dv_acc
        ds_ref[0] += dsink

    full = pl.BlockSpec((1, sp, LANES), lambda b, i: (b, 0, 0))
    return pl.pallas_call(
        body, name="wattn_bwd",
        out_shape=[jax.ShapeDtypeStruct((bsz, seq, qw), BF16),
                   jax.ShapeDtypeStruct((bsz, sp, LANES), F32),
                   jax.ShapeDtypeStruct((bsz, sp, LANES), F32),
                   jax.ShapeDtypeStruct((1, 1, LANES), F32)],
        grid=(bsz, nb),
        in_specs=[pl.BlockSpec(memory_space=pltpu.SMEM),
                  pl.BlockSpec((1, A_BLOCK, qw), lambda b, i: (b, i, 0)),
                  full, full,
                  pl.BlockSpec((1, A_BLOCK, 1), lambda b, i: (b, i, 0)),
                  pl.BlockSpec((1, 1, sp), lambda b, i: (b, 0, 0)),
                  pl.BlockSpec((1, A_BLOCK, qw), lambda b, i: (b, i, 0))],
        out_specs=[pl.BlockSpec((1, A_BLOCK, qw), lambda b, i: (b, i, 0)), full, full,
                   pl.BlockSpec((1, 1, LANES), lambda b, i: (0, 0, 0))],
        compiler_params=_cparams(("arbitrary", "arbitrary")),
    )(sink, proj, kp, vp, posc, posr, dmo)


HG_Q, HG_FF, HG_FB, HG_I, HG_G = 10, 14, 18, 22, 26
HG_OUT = 8
CH = B_CHUNK
HG_UNROLL = 4


def _hgrn_consts(reverse):
    r = lax.broadcasted_iota(jnp.int32, (CH, CH), 0)
    c = lax.broadcasted_iota(jnp.int32, (CH, CH), 1)
    incl = (c >= r) if reverse else (c <= r)
    incl_t = (c <= r) if reverse else (c >= r)
    row = lax.broadcasted_iota(jnp.int32, (CH, LANES), 0)
    mid = CH // 2 if reverse else CH // 2 - 1
    end = 0 if reverse else CH - 1
    return incl, incl_t, row == mid, row == end


def _pick_row(x, sel):
    return jnp.sum(jnp.where(sel, x, 0.0), axis=0, keepdims=True)


def _hgrn_gates(hf, lb):
    sig = _sigmoid(hf)
    f = lb + (1.0 - lb) * sig
    return 1.0 - f, jnp.log(f), sig, f


def _hgrn_decays(lf, consts):
    incl, _, is_mid, is_end = consts
    b = _dot(incl.astype(F32), lf, NN, precision=HIGHEST)
    return b, _pick_row(b, is_mid), _pick_row(b, is_end)


def _hgrn_chunk(q, k, v, lf, st, consts):
    b, bm, be = _hgrn_decays(lf, consts)
    qs = (q * jnp.exp(b - bm)).astype(BF16)
    ks = (k * jnp.exp(bm - b)).astype(BF16)
    a = jnp.where(consts[0], _dot(qs, ks, NT), 0.0).astype(BF16)
    qe = (q * jnp.exp(b)).astype(BF16)
    vb = v.astype(BF16)
    o = _dot(a, vb, NN) + _dot(qe, st.astype(BF16), NT)
    kd = (k * jnp.exp(be - b)).astype(BF16)
    return o, jnp.exp(be) * st + _dot(vb, kd, TN)


def _silu(x):
    return x * _sigmoid(x)


def hgrn_fwd(proj, mo, lb, nw):
    bsz, seq, _ = proj.shape
    n_ch = seq // CH
    ts = _rows_tile(seq, 512)

    def body(hq_ref, hff_ref, hfb_ref, hi_ref, hg_ref, mo_in_ref, lb_ref, nw_ref, y_ref, ot_ref):
        del mo_in_ref
        lb_v, nw_v = lb_ref[0], nw_ref[0]

        def run(hf_ref, reverse, first):
            consts = _hgrn_consts(reverse)

            def step(t, st):
                n = (n_ch - 1 - t) if reverse else t
                rows = pl.ds(pl.multiple_of(n * CH, CH), CH)
                k, lf, _, _ = _hgrn_gates(hf_ref[0, rows, :], lb_v)
                o, st = _hgrn_chunk(_silu(hq_ref[0, rows, :]), k, hi_ref[0, rows, :], lf, st, consts)
                if first:
                    ot_ref[0, rows, :] = o
                else:
                    ot_ref[0, rows, :] += o
                return st

            lax.fori_loop(0, n_ch, step, jnp.zeros((LANES, LANES), F32), unroll=HG_UNROLL)

        run(hff_ref, False, True)
        run(hfb_ref, True, False)

        def finish(i, carry):
            rows = pl.ds(pl.multiple_of(i * ts, ts), ts)
            o = ot_ref[0, rows, :]
            r = lax.rsqrt(jnp.mean(o * o, axis=-1, keepdims=True) + RMS_EPS)
            y_ref[0, rows, :] = (o * r * nw_v * _silu(hg_ref[0, rows, :])).astype(BF16)
            return carry

        lax.fori_loop(0, seq // ts, finish, 0)

    def col(g):
        return pl.BlockSpec((1, seq, LANES), lambda b, h: (b, 0, g + h))

    par = pl.BlockSpec((1, 1, LANES), lambda b, h: (h, 0, 0))
    return pl.pallas_call(
        body, name="hgrn_fwd",
        out_shape=[jax.ShapeDtypeStruct(mo.shape, BF16),
                   jax.ShapeDtypeStruct((bsz, seq, B_HEADS * LANES), F32)],
        grid=(bsz, B_HEADS),
        in_specs=[col(HG_Q), col(HG_FF), col(HG_FB), col(HG_I), col(HG_G),
                  pl.BlockSpec(memory_space=pl.ANY), par, par],
        out_specs=[col(HG_OUT), col(0)],
        input_output_aliases={5: 0},
        compiler_params=_cparams(("parallel", "parallel")),
    )(proj, proj, proj, proj, proj, mo, lb, nw)


def hgrn_out_bwd(proj, otot, dmo, nw):
    bsz, seq, _ = proj.shape
    ts = _rows_tile(seq, 512)

    def body(hg_ref, ot_ref, dmo_ref, nw_ref, do_ref, dhg_ref, dnw_ref):
        nw_v = nw_ref[0]
        dy = dmo_ref[0].astype(F32)
        hg = hg_ref[0]
        o = ot_ref[0]
        sg = _sigmoid(hg)
        r = lax.rsqrt(jnp.mean(o * o, axis=-1, keepdims=True) + RMS_EPS)
        dhg_ref[0] = (dy * (o * r * nw_v) * (sg * (1.0 + hg * (1.0 - sg)))).astype(BF16)
        drn = dy * (hg * sg)
        dnw = jnp.sum(drn * o * r, axis=0, keepdims=True)
        wd = drn * nw_v
        do_ref[0] = r * wd - o * (r * r * r) * jnp.mean(o * wd, axis=-1, keepdims=True)
        first = jnp.logical_and(pl.program_id(1) == 0, pl.program_id(2) == 0)

        @pl.when(first)
        def _():
            dnw_ref[0] = dnw

        @pl.when(jnp.logical_not(first))
        def _():
            dnw_ref[0] += dnw

    def col(g):
        return pl.BlockSpec((1, ts, LANES), lambda h, b, s: (b, s, g + h))

    par = pl.BlockSpec((1, 1, LANES), lambda h, b, s: (h, 0, 0))
    return pl.pallas_call(
        body, name="hgrn_out_bwd",
        out_shape=[jax.ShapeDtypeStruct((bsz, seq, B_HEADS * LANES), F32),
                   jax.ShapeDtypeStruct((bsz, seq, B_HEADS * LANES), BF16),
                   jax.ShapeDtypeStruct((B_HEADS, 1, LANES), F32)],
        grid=(B_HEADS, bsz, seq // ts),
        in_specs=[col(HG_G), col(0), col(HG_OUT), par],
        out_specs=[col(0), col(0), par],
        compiler_params=_cparams(("parallel", "arbitrary", "arbitrary")),
    )(proj, otot, dmo, nw)


def hgrn_bwd(proj, do, lb):
    bsz, seq, _ = proj.shape
    n_ch = seq // CH
    ts = _rows_tile(seq, 512)

    def body(hq_ref, hff_ref, hfb_ref, hi_ref, do_scr, lb_ref,
             dhq_ref, dhff_ref, dhfb_ref, dhi_ref, dlb_ref, st_scr, dq_scr, dv_scr):
        lb_v = lb_ref[0]
        do_scr = do_scr.at[0]

        def run(hf_ref, dhf_ref, reverse, first, dlb0):
            consts = _hgrn_consts(reverse)
            incl, incl_t, is_mid, is_end = consts

            def load(n):
                rows = pl.ds(pl.multiple_of(n * CH, CH), CH)
                hf = hf_ref[0, rows, :]
                k, lf, sig, f = _hgrn_gates(hf, lb_v)
                return rows, _silu(hq_ref[0, rows, :]), k, hi_ref[0, rows, :], lf, sig, f

            def fwd_step(t, st):
                n = (n_ch - 1 - t) if reverse else t
                _, q, k, v, lf, _, _ = load(n)
                st_scr[n] = st.astype(BF16)
                return _hgrn_chunk(q, k, v, lf, st, consts)[1]

            lax.fori_loop(0, n_ch, fwd_step, jnp.zeros((LANES, LANES), F32), unroll=HG_UNROLL)

            def bwd_step(t, carry):
                gt, dlb = carry
                n = t if reverse else (n_ch - 1 - t)
                rows, q, k, v, lf, sig, f = load(n)
                st = st_scr[n]
                do_c = do_scr[rows, :].astype(BF16)
                b, bm, be = _hgrn_decays(lf, consts)
                e_qs, e_ks, e_q, e_kd = jnp.exp(b - bm), jnp.exp(bm - b), jnp.exp(b), jnp.exp(be - b)
                dec = jnp.exp(be)
                qs, ks, qe, kd = q * e_qs, k * e_ks, q * e_q, k * e_kd
                qs_b, ks_b, qe_b, kd_b = qs.astype(BF16), ks.astype(BF16), qe.astype(BF16), kd.astype(BF16)
                vb = v.astype(BF16)
                gt_b = gt.astype(BF16)
                a = jnp.where(incl, _dot(qs_b, ks_b, NT), 0.0).astype(BF16)
                da = jnp.where(incl, _dot(do_c, vb, NT), 0.0).astype(BF16)
                dqe = _dot(do_c, st, NN)
                dv = _dot(a, do_c, TN) + _dot(kd_b, gt_b, NT)
                dqs = _dot(da, ks_b, NN)
                dks = _dot(da, qs_b, TN)
                dkd = _dot(vb, gt_b, NN)
                ddec = jnp.sum(gt * st.astype(F32), axis=0, keepdims=True)
                gt_new = dec * gt + _dot(do_c, qe_b, TN)
                dq = dqs * e_qs + dqe * e_q
                dk = dks * e_ks + dkd * e_kd
                t_qs, t_ks, t_kd = dqs * qs, dks * ks, dkd * kd
                db = t_qs - t_ks + dqe * qe - t_kd
                dbm = jnp.sum(t_ks - t_qs, axis=0, keepdims=True)
                dbe = jnp.sum(t_kd, axis=0, keepdims=True) + ddec * dec
                db = db + jnp.where(is_mid, dbm, 0.0) + jnp.where(is_end, dbe, 0.0)
                dlf = _dot(incl_t.astype(F32), db, NN, precision=HIGHEST)
                df = dlf / f - dk
                dhf_ref[0, rows, :] = (df * (1.0 - lb_v) * sig * (1.0 - sig)).astype(BF16)
                dlb = dlb + jnp.sum(df * (1.0 - sig), axis=0, keepdims=True)
                if first:
                    dq_scr[rows, :] = dq
                    dv_scr[rows, :] = dv
                else:
                    dq_scr[rows, :] += dq
                    dv_scr[rows, :] += dv
                return gt_new, dlb

            return lax.fori_loop(0, n_ch, bwd_step, (jnp.zeros((LANES, LANES), F32), dlb0),
                                 unroll=HG_UNROLL)[1]

        dlb = run(hff_ref, dhff_ref, False, True, jnp.zeros((1, LANES), F32))
        dlb = run(hfb_ref, dhfb_ref, True, False, dlb)

        @pl.when(pl.program_id(1) == 0)
        def _():
            dlb_ref[0] = dlb

        @pl.when(pl.program_id(1) > 0)
        def _():
            dlb_ref[0] += dlb

        def finish(i, carry):
            rows = pl.ds(pl.multiple_of(i * ts, ts), ts)
            hq = hq_ref[0, rows, :]
            sq = _sigmoid(hq)
            dhq_ref[0, rows, :] = (dq_scr[rows, :] * (sq * (1.0 + hq * (1.0 - sq)))).astype(BF16)
            dhi_ref[0, rows, :] = dv_scr[rows, :].astype(BF16)
            return carry

        lax.fori_loop(0, seq // ts, finish, 0)

    def col(g):
        return pl.BlockSpec((1, seq, LANES), lambda h, b: (b, 0, g + h))

    par = pl.BlockSpec((1, 1, LANES), lambda h, b: (h, 0, 0))
    wide = jax.ShapeDtypeStruct((bsz, seq, B_HEADS * LANES), BF16)
    small = jax.ShapeDtypeStruct((B_HEADS, 1, LANES), F32)
    return pl.pallas_call(
        body, name="hgrn_bwd",
        out_shape=[wide, wide, wide, wide, small],
        grid=(B_HEADS, bsz),
        in_specs=[col(HG_Q), col(HG_FF), col(HG_FB), col(HG_I), col(0), par],
        out_specs=[col(0), col(0), col(0), col(0), par],
        scratch_shapes=[pltpu.VMEM((n_ch, LANES, LANES), BF16),
                        pltpu.VMEM((seq, LANES), F32), pltpu.VMEM((seq, LANES), F32)],
        compiler_params=_cparams(("parallel", "arbitrary")),
    )(proj, proj, proj, proj, do, lb)


def _whole(shape):
    return pl.BlockSpec(shape, lambda: (0,) * len(shape))


def silu_small(x):
    def body(x_ref, o_ref):
        o_ref[...] = _silu(x_ref[...])

    return pl.pallas_call(body, name="silu_small", out_shape=jax.ShapeDtypeStruct(x.shape, F32),
                          in_specs=[_whole(x.shape)], out_specs=_whole(x.shape))(x)


def _softmax_rows(x):
    e = jnp.exp(x - jnp.max(x, axis=0, keepdims=True))
    return e / jnp.sum(e, axis=0, keepdims=True)


def lb_fwd(logits, layer):
    n, w = logits.shape

    def body(x_ref, o_ref):
        p = _softmax_rows(x_ref[...])
        row = lax.broadcasted_iota(jnp.int32, (n, w), 0)
        o_ref[...] = jnp.sum(jnp.where(row <= layer, p, 0.0), axis=0, keepdims=True)

    return pl.pallas_call(body, name="lb_fwd", out_shape=jax.ShapeDtypeStruct((1, w), F32),
                          in_specs=[_whole((n, w))], out_specs=_whole((1, w)))(logits)


def lb_bwd(logits, dlb, layer):
    n, w = logits.shape

    def body(x_ref, d_ref, o_ref):
        p = _softmax_rows(x_ref[...])
        row = lax.broadcasted_iota(jnp.int32, (n, w), 0)
        dp = jnp.where(row <= layer, d_ref[...], 0.0)
        o_ref[...] = p * (dp - jnp.sum(p * dp, axis=0, keepdims=True))

    return pl.pallas_call(body, name="lb_bwd", out_shape=jax.ShapeDtypeStruct((n, w), F32),
                          in_specs=[_whole((n, w)), _whole((1, w))], out_specs=_whole((n, w)))(logits, dlb)


def sum_parts(x, name):
    p, r, c = x.shape
    tr = _rows_tile(r, max(8, (1 << 23) // (4 * c * p) // 8 * 8))

    def body(x_ref, o_ref):
        acc = x_ref[0].astype(F32)
        for j in range(1, p):
            acc = acc + x_ref[j].astype(F32)
        o_ref[...] = acc

    return pl.pallas_call(
        body, name=name, out_shape=jax.ShapeDtypeStruct((r, c), F32),
        grid=(r // tr,),
        in_specs=[pl.BlockSpec((p, tr, c), lambda i: (0, i, 0))],
        out_specs=pl.BlockSpec((tr, c), lambda i: (i, 0)),
        compiler_params=_cparams(("parallel",)),
    )(x)


def adamw(w, g, m, v, name):
    r, c = w.shape
    tr = _rows_tile(r, max(8, (1 << 20) // (4 * c) // 8 * 8))
    c1 = 1.0 - ADAM_B1 ** ADAM_STEP
    c2 = 1.0 - ADAM_B2 ** ADAM_STEP

    def body(w_ref, g_ref, m_ref, v_ref, d_ref, nm_ref, nv_ref):
        g = g_ref[...]
        nm = ADAM_B1 * m_ref[...] + (1.0 - ADAM_B1) * g
        nv = ADAM_B2 * v_ref[...] + (1.0 - ADAM_B2) * (g * g)
        nm_ref[...] = nm
        nv_ref[...] = nv
        d_ref[...] = -ADAM_LR * ((nm / c1) / (jnp.sqrt(nv / c2) + ADAM_EPS) + ADAM_WD * w_ref[...])

    spec = pl.BlockSpec((tr, c), lambda i: (i, 0))
    sds = jax.ShapeDtypeStruct((r, c), F32)
    return pl.pallas_call(
        body, name=name, out_shape=[sds, sds, sds], grid=(r // tr,),
        in_specs=[spec, spec, spec, spec], out_specs=[spec, spec, spec],
        compiler_params=_cparams(("parallel",)),
    )(w, g, m, v)


ANY = pl.BlockSpec(memory_space=pl.ANY)


def _my_place():
    return lax.axis_index("x"), lax.axis_index("y"), lax.axis_index("c")


def all_gather(x, name):
    def body(x_ref, out_ref, send_sems, recv_sems, local_sem):
        mx, my, mc = _my_place()
        me, sibling = (mx, my, mc), (mx, my, 1 - mc)
        chips = [(1 - mx, my), (mx, 1 - my), (1 - mx, 1 - my)]

        def slot(px, py, pc):
            return out_ref.at[4 * px + 2 * py + pc]

        def copy(k, block, to, src=None):
            return pltpu.make_async_remote_copy(
                src_ref=slot(*block) if src is None else src, dst_ref=slot(*block),
                send_sem=send_sems.at[k], recv_sem=recv_sems.at[k],
                device_id=to, device_id_type=MESH_ID)

        mine = pltpu.make_async_copy(x_ref, slot(*me), local_sem)
        mine.start()
        first = [copy(0, me, sibling, src=x_ref)]
        first += [copy(1 + j, me, (*chip, mc), src=x_ref) for j, chip in enumerate(chips)]
        for cp in first:
            cp.start()
        passed = [copy(4 + j, (*chip, mc), sibling) for j, chip in enumerate(chips)]
        for j, chip in enumerate(chips):
            copy(1 + j, (*chip, mc), me).wait_recv()
            passed[j].start()
        copy(0, sibling, me).wait_recv()
        for j, chip in enumerate(chips):
            copy(4 + j, (*chip, 1 - mc), me).wait_recv()
        for cp in first + passed:
            cp.wait_send()
        mine.wait()

    return pl.pallas_call(
        body, name=name,
        out_shape=jax.ShapeDtypeStruct((N_DEV,) + x.shape, x.dtype),
        in_specs=[ANY], out_specs=ANY,
        scratch_shapes=[pltpu.SemaphoreType.DMA((7,)), pltpu.SemaphoreType.DMA((7,)),
                        pltpu.SemaphoreType.DMA(())],
    )(x)


def swap_with_sibling(x, name):
    def body(x_ref, out_ref, send_sem, recv_sem):
        mx, my, mc = _my_place()
        cp = pltpu.make_async_remote_copy(
            src_ref=x_ref.at[1 - mc], dst_ref=out_ref, send_sem=send_sem, recv_sem=recv_sem,
            device_id=(mx, my, 1 - mc), device_id_type=MESH_ID)
        cp.start()
        cp.wait()

    return pl.pallas_call(
        body, name=name,
        out_shape=jax.ShapeDtypeStruct(x.shape[1:], x.dtype),
        in_specs=[ANY], out_specs=ANY,
        scratch_shapes=[pltpu.SemaphoreType.DMA(()), pltpu.SemaphoreType.DMA(())],
    )(x)


def pair_add(x, r, mc):
    _, n, rows, c = x.shape
    tr = _rows_tile(rows, 512)

    def body(mc_ref, x_ref, r_ref, o_ref):
        del mc_ref
        o_ref[0] = (x_ref[0, 0].astype(F32) + r_ref[0].astype(F32)).astype(BF16)

    return pl.pallas_call(
        body, name="pair_add",
        out_shape=jax.ShapeDtypeStruct((n, rows, c), BF16),
        grid_spec=pltpu.PrefetchScalarGridSpec(
            num_scalar_prefetch=1, grid=(n, rows // tr),
            in_specs=[pl.BlockSpec((1, 1, tr, c), lambda j, i, s: (s[0], j, i, 0)),
                      pl.BlockSpec((1, tr, c), lambda j, i, s: (j, i, 0))],
            out_specs=pl.BlockSpec((1, tr, c), lambda j, i, s: (j, i, 0))),
        compiler_params=_cparams(("parallel", "parallel")),
    )(mc.reshape(1).astype(jnp.int32), x, r)


def chip_all_to_all(x, name):
    def body(x_ref, out_ref, send_sems, recv_sems, local_sem):
        mx, my, mc = _my_place()
        me = 2 * mx + my
        mine = pltpu.make_async_copy(x_ref.at[me], out_ref.at[me], local_sem)
        mine.start()
        copies = []
        for k, (fx, fy) in enumerate(((1, 0), (0, 1), (1, 1))):
            px = 1 - mx if fx else mx
            py = 1 - my if fy else my
            peer = 2 * px + py
            cp = pltpu.make_async_remote_copy(
                src_ref=x_ref.at[peer], dst_ref=out_ref.at[me],
                send_sem=send_sems.at[k], recv_sem=recv_sems.at[k],
                device_id=(px, py, mc), device_id_type=MESH_ID)
            cp.start()
            copies.append((cp, peer, (px, py, mc), k))
        for cp, peer, to, k in copies:
            pltpu.make_async_remote_copy(
                src_ref=x_ref.at[peer], dst_ref=out_ref.at[peer],
                send_sem=send_sems.at[k], recv_sem=recv_sems.at[k],
                device_id=to, device_id_type=MESH_ID).wait_recv()
        for cp, _, _, _ in copies:
            cp.wait_send()
        mine.wait()

    return pl.pallas_call(
        body, name=name,
        out_shape=jax.ShapeDtypeStruct(x.shape, x.dtype),
        in_specs=[ANY], out_specs=ANY,
        scratch_shapes=[pltpu.SemaphoreType.DMA((3,)), pltpu.SemaphoreType.DMA((3,)),
                        pltpu.SemaphoreType.DMA(())],
    )(x)


ROW_W = D_MODEL
FF_SH = D_FF // N_DEV
N_FFN = 2 * DEPTH
SEC = {}
_off = 0
for _name, _rows in (("gate", N_FFN * FF_SH), ("up", N_FFN * FF_SH), ("down", N_FFN * FF_SH),
                     ("hin", 416), ("hout", 128), ("mout", 128), ("mdown", 80), ("uq", 48), ("ukv", 64)):
    SEC[_name] = (_off, _rows)
    _off += _rows
PACK_ROWS = _off
MDOWN_ROWS = 128 * 544 // ROW_W
A_Q_W = A_HEADS * A_HEAD_DIM
_KVHEAD = (np.arange(A_HEADS) // A_GROUP).reshape(A_HEADS, 1, 1)


def pack_shards(gate, up, down, hin, hout, mdown, uq, ukv, mout, dtype):
    def rows(a):
        return a.astype(dtype).reshape(-1, ROW_W)
    parts = [rows(jnp.swapaxes(gate, -1, -2)), rows(jnp.swapaxes(up, -1, -2)), rows(down),
             rows(jnp.swapaxes(hin, -1, -2)), rows(hout), rows(mout),
             jnp.pad(rows(mdown), ((0, SEC["mdown"][1] - MDOWN_ROWS), (0, 0))),
             rows(jnp.swapaxes(uq, -1, -2)), rows(jnp.swapaxes(ukv, -1, -2))]
    return jnp.concatenate(parts, axis=0)


def unpack_shards(p):
    def sec(name):
        o, n = SEC[name]
        return p[o:o + n]

    def col(name, r, c):
        return jnp.swapaxes(sec(name).reshape(-1, c, r), -1, -2)

    return dict(
        ffn_w_gate=col("gate", 1024, FF_SH).reshape(DEPTH, 2, 1024, FF_SH),
        ffn_w_up=col("up", 1024, FF_SH).reshape(DEPTH, 2, 1024, FF_SH),
        ffn_w_down=sec("down").reshape(DEPTH, 2, FF_SH, 1024),
        hyb_w_in=col("hin", 1024, 416),
        hyb_w_out=sec("hout").reshape(1, 128, 1024),
        mla_w_out=sec("mout").reshape(1, 128, 1024),
        mla_w_down=sec("mdown")[:MDOWN_ROWS].reshape(1, 128, 544),
        mla_w_uq=col("uq", 256, 192),
        mla_w_ukv=col("ukv", 256, 256),
    )


def _pad_qheads(w):
    a = w.reshape(A_HEADS, 1, A_HEAD_DIM, -1)
    kvh = _KVHEAD[..., None]
    both = jnp.concatenate([jnp.where(kvh == 0, a, 0), jnp.where(kvh == 1, a, 0)], axis=1)
    return both.reshape(A_HEADS * LANES, -1)


def _unpad_qheads(w):
    a = w.reshape(A_HEADS, 2, A_HEAD_DIM, -1)
    return jnp.where(_KVHEAD == 0, a[:, 0], a[:, 1]).reshape(A_Q_W, -1)


def unpack_full(g):
    def sec(name):
        o, n = SEC[name]
        return g[:, o:o + n]

    def ffn(name):
        return jnp.swapaxes(sec(name).reshape(N_DEV, N_FFN, FF_SH, ROW_W), 0, 1).reshape(N_FFN, D_FF, ROW_W)

    def z(*s):
        return jnp.zeros(s, g.dtype)

    def halves(a):
        return a.reshape(N_FFN, 2, 1, FF_HALF, ROW_W)

    gate_t, up_t, down = ffn("gate"), ffn("up"), ffn("down")
    hin_t = sec("hin").reshape(-1, ROW_W)
    hout = sec("hout").reshape(-1, ROW_W)
    mout = sec("mout").reshape(C_HEADS, C_V, ROW_W)
    mdown = sec("mdown")[:, :MDOWN_ROWS].reshape(D_MODEL, 544)
    uq_t = sec("uq").reshape(C_HEADS, C_NOPE + C_ROPE, C_Q_RANK)
    w = dict(
        wgu_t=jnp.concatenate([halves(gate_t), halves(up_t)], axis=2).reshape(N_FFN, 2 * D_FF, ROW_W),
        wd=down,
        hin_t=jnp.concatenate([_pad_qheads(hin_t[:A_Q_W]), hin_t[A_Q_W:]], axis=0),
        hout=jnp.concatenate([_pad_qheads(hout[:A_Q_W]), hout[A_Q_W:]], axis=0),
        mout=jnp.concatenate([z(C_HEADS, C_NOPE, ROW_W), mout], axis=1).reshape(MLA_HEAD_PAD, ROW_W),
        mdown=jnp.concatenate([mdown[:, :512], z(D_MODEL, 64), mdown[:, 512:], z(D_MODEL, 32)], axis=1),
        uq_t=jnp.concatenate([uq_t, z(C_HEADS, 32, C_Q_RANK)], axis=1).reshape(MLA_HEAD_PAD, C_Q_RANK),
        ukv_t=sec("ukv").reshape(MLA_HEAD_PAD, C_KV_RANK),
    )
    w.update(wgu=jnp.swapaxes(w["wgu_t"], 1, 2), wd_t=jnp.swapaxes(down, 1, 2), hin=w["hin_t"].T,
             hout_t=w["hout"].T, mout_t=w["mout"].T, mdown_t=w["mdown"].T, uq=w["uq_t"].T, ukv=w["ukv_t"].T)
    return w


def pack_full(d, dtype):
    d = {k: v.astype(dtype) for k, v in d.items()}
    gu = d["wgu_t"].reshape(N_FFN, 2, 2, FF_HALF, ROW_W)
    d["wgu_t"] = jnp.concatenate([gu[:, :, 0].reshape(N_FFN, D_FF, ROW_W),
                                  gu[:, :, 1].reshape(N_FFN, D_FF, ROW_W)], axis=1)

    def split(a):
        return a.reshape(N_DEV, -1, ROW_W)

    def ffn(a):
        return jnp.swapaxes(a.reshape(N_FFN, N_DEV, FF_SH, ROW_W), 0, 1).reshape(N_DEV, N_FFN * FF_SH, ROW_W)

    hin = jnp.concatenate([_unpad_qheads(d["hin_t"][:A_HEADS * LANES]), d["hin_t"][A_HEADS * LANES:]], axis=0)
    hout = jnp.concatenate([_unpad_qheads(d["hout"][:A_HEADS * LANES]), d["hout"][A_HEADS * LANES:]], axis=0)
    mout = d["mout"].reshape(C_HEADS, LANES, ROW_W)[:, C_NOPE:].reshape(-1, ROW_W)
    mdown = jnp.concatenate([d["mdown"][:, :512], d["mdown"][:, 576:608]], axis=1)
    uq = d["uq_t"].reshape(C_HEADS, LANES, C_Q_RANK)[:, :C_NOPE + C_ROPE]
    parts = [ffn(d["wgu_t"][:, :D_FF]), ffn(d["wgu_t"][:, D_FF:]), ffn(d["wd"]),
             split(hin), split(hout), split(mout),
             jnp.pad(split(mdown), ((0, 0), (0, SEC["mdown"][1] - MDOWN_ROWS), (0, 0))),
             split(uq), split(d["ukv_t"])]
    return jnp.concatenate(parts, axis=1)


def rope_tables(positions):
    half = C_ROPE // 2
    freqs = ROPE_THETA ** (-jnp.arange(half, dtype=F32) / half)
    ang = positions.astype(F32)[..., None] * freqs
    cos, sin = jnp.cos(ang), jnp.sin(ang)
    shape = positions.shape
    cq = jnp.concatenate([jnp.ones(shape + (C_NOPE,), F32), cos, cos, jnp.zeros(shape + (32,), F32)], axis=-1)
    sg = jnp.concatenate([jnp.zeros(shape + (C_NOPE,), F32), -sin, sin, jnp.zeros(shape + (32,), F32)], axis=-1)
    return cq, sg


def ffn_fwd(h2, w, i):
    gu, a = ffn_up_fused(h2, w["wgu"][i])
    return matmul(a, w["wd"][i], "nn", F32, "ffn_down"), (gu, a)


def ffn_bwd(dy2, h2, w, i, saved):
    gu, a = saved
    dgu = ffn_down_bwd_fused(dy2, w["wd_t"][i], gu)
    dwd = matmul(a, dy2, "tn", F32, "ffn_down_dw", tm_cap=FF_HALF)
    dwgu_t = matmul(dgu, h2, "tn", F32, "ffn_up_dw", tm_cap=FF_HALF)
    dh = matmul(dgu, w["wgu_t"][i], "nn", F32, "ffn_up_dx")
    return dh, dwgu_t, dwd


def hybrid_fwd(h2, shape, w, aux):
    bsz, seq = shape
    proj = matmul(h2, w["hin"], "nn", F32, "hyb_in").reshape(bsz, seq, HYB_PAD)
    pad = ((0, 0), (A_BLOCK, A_BLOCK), (0, 0))
    kp = jnp.pad(proj[:, :, 8 * LANES:9 * LANES].astype(BF16), pad)
    vp = jnp.pad(proj[:, :, 9 * LANES:10 * LANES].astype(BF16), pad)
    mo = wattn_fwd(proj, kp, vp, aux["posc"], aux["posr"], aux["sink"])
    mo, otot = hgrn_fwd(proj, mo, aux["lb"], aux["nw"])
    y = matmul(mo.reshape(bsz * seq, MO_PAD), w["hout"], "nn", F32, "hyb_out")
    return y, (proj, kp, vp, mo, otot)


def hybrid_bwd(dy2, h2, shape, w, aux, saved):
    bsz, seq = shape
    proj, kp, vp, mo, otot = saved
    mo2 = mo.reshape(bsz * seq, MO_PAD)
    dmo = matmul(dy2, w["hout_t"], "nn", F32, "hyb_out_dx").reshape(bsz, seq, MO_PAD)
    dhout = matmul(mo2, dy2, "tn", F32, "hyb_out_dw")
    dq, dkp, dvp, dsink = wattn_bwd(proj, kp, vp, aux["posc"], aux["posr"], aux["sink"], dmo)
    do, dhg, dnw = hgrn_out_bwd(proj, otot, dmo, aux["nw"])
    dhq, dhff, dhfb, dhi, dlb = hgrn_bwd(proj, do, aux["lb"])
    dproj = jnp.concatenate([dq, dkp[:, A_BLOCK:-A_BLOCK].astype(BF16), dvp[:, A_BLOCK:-A_BLOCK].astype(BF16),
                             dhq, dhff, dhfb, dhi, dhg], axis=-1).reshape(bsz * seq, HYB_PAD)
    dhin_t = matmul(dproj, h2, "tn", F32, "hyb_in_dw")
    dh = matmul(dproj, w["hin_t"], "nn", F32, "hyb_in_dx")
    return dh, dict(hin_t=dhin_t, hout=dhout), dict(sink=dsink, lb=dlb, nw=dnw)


def mla_fwd(h2, shape, w, aux):
    bsz, seq = shape
    t = bsz * seq
    cqkv = matmul(h2, w["mdown"], "nn", F32, "mla_down").reshape(bsz, seq, MLA_DOWN_PAD)
    nq, nkv, kr = mla_mid_fwd(cqkv, aux["qn"], aux["kvn"], aux["cq"], aux["sg"])
    q = matmul(nq.reshape(t, C_Q_RANK), w["uq"], "nn", F32, "mla_uq").reshape(bsz, seq, MLA_HEAD_PAD)
    qp = q_rope(q, aux["cq"], aux["sg"], False, "q_rope_fwd", out_scale=MLA_QSCALE)
    kv = matmul(nkv.reshape(t, C_KV_RANK), w["ukv"], "nn", BF16, "mla_ukv").reshape(bsz, seq, MLA_HEAD_PAD)
    o, lse = mla_attn_fwd(qp, kv, kr)
    y = matmul(o.reshape(t, MLA_HEAD_PAD), w["mout"], "nn", F32, "mla_out")
    return y, (cqkv, nq, nkv, kr, qp, kv, o, lse)


def mla_bwd(dy2, h2, shape, w, aux, saved):
    bsz, seq = shape
    t = bsz * seq
    cqkv, nq, nkv, kr, qp, kv, o, lse = saved
    do = matmul(dy2, w["mout_t"], "nn", BF16, "mla_out_dx").reshape(bsz, seq, MLA_HEAD_PAD)
    dmout = matmul(o.reshape(t, MLA_HEAD_PAD), dy2, "tn", F32, "mla_out_dw")
    dqp, dkv, dkr = mla_attn_bwd(qp, kv, kr, o, do, lse)
    dq = q_rope(dqp, aux["cq"], aux["sg"], True, "q_rope_bwd").reshape(t, MLA_HEAD_PAD)
    dkv2 = dkv.reshape(t, MLA_HEAD_PAD)
    dnq = matmul(dq, w["uq_t"], "nn", F32, "mla_uq_dx").reshape(bsz, seq, C_Q_RANK)
    duq_t = matmul(dq, nq.reshape(t, C_Q_RANK), "tn", F32, "mla_uq_dw")
    dnkv = matmul(dkv2, w["ukv_t"], "nn", F32, "mla_ukv_dx").reshape(bsz, seq, C_KV_RANK)
    dukv_t = matmul(dkv2, nkv.reshape(t, C_KV_RANK), "tn", F32, "mla_ukv_dw")
    dcqkv, dqn, dkvn = mla_mid_bwd(dnq, dnkv, dkr, cqkv, aux["qn"], aux["kvn"], aux["cq"], aux["sg"])
    dcqkv2 = dcqkv.reshape(t, MLA_DOWN_PAD)
    dmdown = matmul(h2, dcqkv2, "tn", F32, "mla_down_dw")
    dh = matmul(dcqkv2, w["mdown_t"], "nn", F32, "mla_down_dx")
    return dh, dict(mdown=dmdown, uq_t=duq_t, ukv_t=dukv_t, mout=dmout), dict(qn=dqn, kvn=dkvn)


W_NAMES = ['ada_w', 'ada_b', 'ln_g', 'ln_b', 'ffn_w_gate', 'ffn_w_up', 'ffn_w_down', 'hyb_w_in', 'hyb_w_out',
           'attn_sink', 'hgrn_lb_logits', 'hgrn_norm_w', 'mla_w_down', 'mla_q_norm', 'mla_kv_norm', 'mla_w_uq',
           'mla_w_ukv', 'mla_w_out']
SMALL_NAMES = ['ada_b', 'ln_g', 'ln_b', 'attn_sink', 'hgrn_lb_logits', 'hgrn_norm_w', 'mla_q_norm', 'mla_kv_norm']
MOD_W = N_SUB * 3 * D_MODEL
MOD_SH = MOD_W // N_DEV
RES_W = (0.5, 1.0, 0.5)


def _rows1024(a, rows=None):
    flat = a.astype(F32).reshape(-1)
    n = flat.shape[0]
    total = (-(-n // ROW_W) if rows is None else rows) * ROW_W
    return jnp.pad(flat, (0, total - n)).reshape(-1, ROW_W)


def kernel(x, c, positions, ada_w, ada_b, ln_g, ln_b, ffn_w_gate, ffn_w_up, ffn_w_down, hyb_w_in, hyb_w_out, attn_sink, hgrn_lb_logits, hgrn_norm_w, mla_w_down, mla_q_norm, mla_kv_norm, mla_w_uq, mla_w_ukv, mla_w_out, loss_target, m_ada_w, m_ada_b, m_ln_g, m_ln_b, m_ffn_w_gate, m_ffn_w_up, m_ffn_w_down, m_hyb_w_in, m_hyb_w_out, m_attn_sink, m_hgrn_lb_logits, m_hgrn_norm_w, m_mla_w_down, m_mla_q_norm, m_mla_kv_norm, m_mla_w_uq, m_mla_w_ukv, m_mla_w_out, v_ada_w, v_ada_b, v_ln_g, v_ln_b, v_ffn_w_gate, v_ffn_w_up, v_ffn_w_down, v_hyb_w_in, v_hyb_w_out, v_attn_sink, v_hgrn_lb_logits, v_hgrn_norm_w, v_mla_w_down, v_mla_q_norm, v_mla_kv_norm, v_mla_w_uq, v_mla_w_ukv, v_mla_w_out):
    weights = dict(zip(W_NAMES, (ada_w, ada_b, ln_g, ln_b, ffn_w_gate, ffn_w_up, ffn_w_down, hyb_w_in, hyb_w_out,
                                 attn_sink, hgrn_lb_logits, hgrn_norm_w, mla_w_down, mla_q_norm, mla_kv_norm,
                                 mla_w_uq, mla_w_ukv, mla_w_out)))
    mom1 = dict(zip(W_NAMES, (m_ada_w, m_ada_b, m_ln_g, m_ln_b, m_ffn_w_gate, m_ffn_w_up, m_ffn_w_down, m_hyb_w_in,
                              m_hyb_w_out, m_attn_sink, m_hgrn_lb_logits, m_hgrn_norm_w, m_mla_w_down, m_mla_q_norm,
                              m_mla_kv_norm, m_mla_w_uq, m_mla_w_ukv, m_mla_w_out)))
    mom2 = dict(zip(W_NAMES, (v_ada_w, v_ada_b, v_ln_g, v_ln_b, v_ffn_w_gate, v_ffn_w_up, v_ffn_w_down, v_hyb_w_in,
                              v_hyb_w_out, v_attn_sink, v_hgrn_lb_logits, v_hgrn_norm_w, v_mla_w_down, v_mla_q_norm,
                              v_mla_kv_norm, v_mla_w_uq, v_mla_w_ukv, v_mla_w_out)))
    bsz, seq, d = x.shape
    t = bsz * seq
    nb_tot = N_DEV * bsz
    me = 4 * lax.axis_index("x") + 2 * lax.axis_index("y") + lax.axis_index("c")

    c_all = all_gather(c, "gather_c").reshape(nb_tot, d)
    cond = silu_small(c_all)
    ada_b_mine = lax.dynamic_slice_in_dim(ada_b, me * MOD_SH, MOD_SH, axis=1)
    modp = jnp.concatenate([matmul(cond, ada_w[l], "nn", F32, "ada_fwd", bias=ada_b_mine[l])
                            for l in range(DEPTH)], axis=1)
    mod_rows = nb_tot * DEPTH * MOD_SH // ROW_W
    small1 = jnp.concatenate([_rows1024(modp), _rows1024(jnp.concatenate(
        [ln_g.reshape(-1), ln_b.reshape(-1), mla_q_norm.reshape(-1), mla_kv_norm.reshape(-1)]), rows=4)], axis=0)
    g1 = all_gather(small1, "gather_mod")
    mod_all = g1[:, :mod_rows].reshape(N_DEV, nb_tot, DEPTH, MOD_SH)
    mod_all = jnp.transpose(mod_all, (1, 2, 0, 3)).reshape(nb_tot, DEPTH, N_SUB, 3, d)
    mod = lax.dynamic_slice_in_dim(mod_all, me * bsz, bsz, axis=0)
    tail = g1[:, mod_rows:].reshape(N_DEV, -1)
    n_ln = DEPTH * N_SUB * LANES
    ln_g_full = jnp.transpose(tail[:, :n_ln].reshape(N_DEV, DEPTH, N_SUB, LANES), (1, 2, 0, 3)).reshape(DEPTH, N_SUB, d)
    ln_b_full = jnp.transpose(tail[:, n_ln:2 * n_ln].reshape(N_DEV, DEPTH, N_SUB, LANES), (1, 2, 0, 3)).reshape(DEPTH, N_SUB, d)
    qn_full = tail[:, 2 * n_ln:2 * n_ln + 32].reshape(C_Q_RANK)
    kvn_full = tail[:, 2 * n_ln + 32:2 * n_ln + 64].reshape(C_KV_RANK)

    def mvec(l, s, j):
        return mod[:, l, s, j].reshape(bsz, 1, d)

    packed = pack_shards(ffn_w_gate, ffn_w_up, ffn_w_down, hyb_w_in, hyb_w_out, mla_w_down, mla_w_uq, mla_w_ukv,
                         mla_w_out, BF16)
    w = unpack_full(all_gather(packed, "gather_weights"))

    cq_tab, sg_tab = rope_tables(positions)
    lb0 = lb_fwd(hgrn_lb_logits, 0)
    aux = [dict(posc=positions.reshape(bsz, seq, 1),
                posr=jnp.pad(positions, ((0, 0), (A_BLOCK, A_BLOCK))).reshape(bsz, 1, seq + 2 * A_BLOCK),
                sink=attn_sink[0], lb=lb0.reshape(B_HEADS, 1, LANES), nw=hgrn_norm_w[0].reshape(B_HEADS, 1, LANES)),
           dict(qn=qn_full, kvn=kvn_full, cq=cq_tab, sg=sg_tab)]
    mixers = [(hybrid_fwd, hybrid_bwd), (mla_fwd, mla_bwd)]

    tape = []
    xin = x
    h = mod_fwd(x, mvec(0, 0, 1), mvec(0, 0, 0))
    for l in range(DEPTH):
        for s in range(N_SUB):
            h2 = h.reshape(t, d)
            if s == 1:
                y, saved = mixers[l][0](h2, (bsz, seq), w, aux[l])
            else:
                i = 2 * l + s // 2
                y, saved = ffn_fwd(h2, w, i)
            y = y.reshape(bsz, seq, d)
            last = l == DEPTH - 1 and s == N_SUB - 1
            ln, sn = (l, s + 1) if s + 1 < N_SUB else (l + 1, 0)
            nxt = None if last else (mvec(ln, sn, 1), mvec(ln, sn, 0))
            xn, hn = resln_fwd(xin, y, mvec(l, s, 2), ln_g_full[l, s], ln_b_full[l, s], RES_W[s], nxt)
            tape.append((l, s, xin, h2, y, saved))
            xin, h = xn, hn
    loss_part, gout = loss_fwd(xin, loss_target)
    loss = lax.psum(loss_part, ("x", "y", "c"))

    dmod = [[[None] * 3 for _ in range(N_SUB)] for _ in range(DEPTH)]
    dln_g = [[None] * N_SUB for _ in range(DEPTH)]
    dln_b = [[None] * N_SUB for _ in range(DEPTH)]
    big = dict(wgu_t=[None] * N_FFN, wd=[None] * N_FFN)
    small = {}
    for l, s, xs, h2, y, saved in reversed(tape):
        dxa, dy, dgate, dg, db = resln_bwd(gout, xs, y, mvec(l, s, 2), ln_g_full[l, s], RES_W[s])
        dy2 = dy.reshape(t, d)
        if s == 1:
            dh, dbig, dsmall = mixers[l][1](dy2, h2, (bsz, seq), w, aux[l], saved)
            big.update(dbig)
            small.update(dsmall)
        else:
            i = 2 * l + s // 2
            dh, big["wgu_t"][i], big["wd"][i] = ffn_bwd(dy2, h2, w, i, saved)
        gout, dscale, dshift = mod_bwd(dxa, dh.reshape(bsz, seq, d), xs, mvec(l, s, 1))
        dmod[l][s] = [dshift, dscale, dgate]
        dln_g[l][s], dln_b[l][s] = dg, db
    grad_x = gout
    big["wgu_t"] = jnp.stack(big["wgu_t"])
    big["wd"] = jnp.stack(big["wd"])

    dmod_mine = jnp.stack([jnp.stack([jnp.concatenate(dmod[l][s], axis=1) for s in range(N_SUB)], axis=1)
                           for l in range(DEPTH)], axis=1)
    dmod_rows = bsz * DEPTH * MOD_W // ROW_W
    misc = jnp.concatenate([small["lb"].reshape(-1), small["nw"].reshape(-1), small["qn"].reshape(-1),
                            small["kvn"].reshape(-1), small["sink"].reshape(-1)[:A_HEADS]])
    small2 = jnp.concatenate([_rows1024(dmod_mine),
                              _rows1024(jnp.stack([jnp.stack(r) for r in dln_g])),
                              _rows1024(jnp.stack([jnp.stack(r) for r in dln_b])),
                              _rows1024(misc, rows=2)], axis=0)
    g2 = all_gather(small2, "gather_small_grads")
    dmod_all = g2[:, :dmod_rows].reshape(nb_tot, DEPTH * MOD_W // ROW_W, ROW_W)
    grad_ada_b = sum_parts(dmod_all, "sum_ada_b").reshape(DEPTH, MOD_W)
    dmod_cols = lax.dynamic_slice_in_dim(dmod_all.reshape(nb_tot, DEPTH, MOD_W), me * MOD_SH, MOD_SH, axis=2)
    grad_ada_w = jnp.stack([matmul(cond, dmod_cols[:, l], "tn", F32, "ada_dw") for l in range(DEPTH)])
    rest = sum_parts(g2[:, dmod_rows:], "sum_small")
    n6 = DEPTH * N_SUB
    gl_g = lax.dynamic_slice_in_dim(rest[:n6].reshape(DEPTH, N_SUB, d), me * LANES, LANES, axis=2)
    gl_b = lax.dynamic_slice_in_dim(rest[n6:2 * n6].reshape(DEPTH, N_SUB, d), me * LANES, LANES, axis=2)
    mrow = rest[2 * n6:].reshape(-1)
    dlb0 = mrow[:512].reshape(1, 512)
    g_nw = mrow[512:1024].reshape(1, B_HEADS, LANES)
    g_qn = lax.dynamic_slice_in_dim(mrow[1024:1280], me * 32, 32).reshape(1, 32)
    g_kvn = lax.dynamic_slice_in_dim(mrow[1280:1536], me * 32, 32).reshape(1, 32)
    g_sink = mrow[1536:1536 + A_HEADS].reshape(1, A_HEADS)
    g_lb = lb_bwd(hgrn_lb_logits, dlb0, 0)

    mc = lax.axis_index("c")
    outgoing = jnp.swapaxes(pack_full(big, BF16).reshape(4, 2, PACK_ROWS, ROW_W), 0, 1)
    pair = pair_add(outgoing, swap_with_sibling(outgoing, "exchange_d2d"), mc)
    reduced = sum_parts(chip_all_to_all(pair, "exchange_ici"), "sum_grads")
    grads = unpack_shards(reduced)
    grads.update(ada_w=grad_ada_w, ada_b=grad_ada_b, ln_g=gl_g, ln_b=gl_b, attn_sink=g_sink,
                 hgrn_lb_logits=g_lb, hgrn_norm_w=g_nw, mla_q_norm=g_qn, mla_kv_norm=g_kvn)

    delta, new_m, new_v = {}, {}, {}
    for name in W_NAMES:
        if name in SMALL_NAMES:
            continue
        shp = weights[name].shape
        two_d = (-1, shp[-1])
        out = adamw(weights[name].reshape(two_d), grads[name].reshape(two_d), mom1[name].reshape(two_d),
                    mom2[name].reshape(two_d), "adamw_" + name)
        delta[name], new_m[name], new_v[name] = [o.reshape(shp) for o in out]

    def pack_small(src):
        flat = jnp.concatenate([src[n].reshape(-1) for n in SMALL_NAMES])
        return jnp.pad(flat, (0, -flat.shape[0] % (8 * LANES))).reshape(-1, LANES)

    outs = adamw(pack_small(weights), pack_small(grads), pack_small(mom1), pack_small(mom2), "adamw_small")
    off = 0
    for name in SMALL_NAMES:
        n = weights[name].size
        for dst, o in zip((delta, new_m, new_v), outs):
            dst[name] = o.reshape(-1)[off:off + n].reshape(weights[name].shape)
        off += n

    return (loss, grad_x, *[grads[n] for n in W_NAMES], *[delta[n] for n in W_NAMES],
            *[new_m[n] for n in W_NAMES], *[new_v[n] for n in W_NAMES])
```

```python
import functools
import math

import jax
import jax.numpy as jnp
import numpy as np
from jax import lax
from jax.experimental import pallas as pl
from jax.experimental.pallas import tpu as pltpu

F32 = jnp.float32
BF16 = jnp.bfloat16

D_MODEL = 1024
DEPTH = 2
D_FF = 2816
A_HEADS = 8
A_KV_HEADS = 2
A_HEAD_DIM = 64
WINDOW = 128
A_BLOCK = 128
B_HEADS = 4
B_KEY_DIM = 128
B_CHUNK = 64
C_HEADS = 16
C_Q_RANK = 256
C_KV_RANK = 256
C_NOPE = 64
C_ROPE = 32
C_V = 64
ROPE_THETA = 10000.0
LN_EPS = 1e-5
RMS_EPS = 1e-6
DEEPNORM_ALPHA = (2 * DEPTH) ** 0.25
N_SUB = 3
N_DEV = 8
LANES = 128
HYB_PAD = 3840
MO_PAD = 1536
MLA_DOWN_PAD = 640
MLA_HEAD_PAD = C_HEADS * LANES
ADAM_LR = 0.001
ADAM_B1 = 0.9
ADAM_B2 = 0.999
ADAM_EPS = 1e-08
ADAM_WD = 0.01
ADAM_STEP = 10
VMEM_LIMIT = 48 * 1024 * 1024
MESH_ID = pl.DeviceIdType.MESH
HIGHEST = lax.Precision.HIGHEST


def _cparams(sem=None):
    return pltpu.CompilerParams(dimension_semantics=sem, vmem_limit_bytes=VMEM_LIMIT)


def _tile(n, cap):
    if n <= cap:
        return n
    t = (cap // LANES) * LANES
    while t >= LANES:
        if n % t == 0:
            return t
        t -= LANES
    return n


def _rows_tile(n, cap):
    if n <= cap:
        return n
    t = cap
    while t >= 8:
        if n % t == 0:
            return t
        t -= 8
    return n


def _sigmoid(x):
    return 1.0 / (1.0 + jnp.exp(-x))


def _dot(a, b, dims, precision=None):
    return lax.dot_general(a, b, (dims, ((), ())), precision=precision,
                           preferred_element_type=F32)


NN = ((1,), (0,))
NT = ((1,), (1,))
TN = ((0,), (0,))


def matmul(a, b, form, out_dtype, name, bias=None, tm_cap=512, tn_cap=1024, tk_cap=1024):
    if form == "nn":
        (m, k), (k2, n) = a.shape, b.shape
    elif form == "nt":
        (m, k), (n, k2) = a.shape, b.shape
    else:
        (k, m), (k2, n) = a.shape, b.shape
    assert k == k2, (a.shape, b.shape, form)
    tm = _tile(m, tm_cap)
    tn = _tile(n, tn_cap)
    tk = k if k <= 2 * D_FF else _tile(k, tk_cap)
    nk = k // tk
    dims = {"nn": NN, "nt": NT, "tn": TN}[form]

    def body(*refs):
        if bias is None:
            a_ref, b_ref, o_ref, acc_ref = refs
            bias_ref = None
        else:
            a_ref, b_ref, bias_ref, o_ref, acc_ref = refs
        kk = pl.program_id(2)
        part = _dot(a_ref[...].astype(BF16), b_ref[...].astype(BF16), dims)

        def finish(total):
            if bias_ref is not None:
                total = total + bias_ref[...]
            o_ref[...] = total.astype(o_ref.dtype)

        if nk == 1:
            finish(part)
        else:
            @pl.when(kk == 0)
            def _():
                acc_ref[...] = part

            @pl.when(jnp.logical_and(kk > 0, kk < nk - 1))
            def _():
                acc_ref[...] += part

            @pl.when(kk == nk - 1)
            def _():
                finish(acc_ref[...] + part)

    if form == "nn":
        a_spec = pl.BlockSpec((tm, tk), lambda i, j, kk: (i, kk))
        b_spec = pl.BlockSpec((tk, tn), lambda i, j, kk: (kk, j))
    elif form == "nt":
        a_spec = pl.BlockSpec((tm, tk), lambda i, j, kk: (i, kk))
        b_spec = pl.BlockSpec((tn, tk), lambda i, j, kk: (j, kk))
    else:
        a_spec = pl.BlockSpec((tk, tm), lambda i, j, kk: (kk, i))
        b_spec = pl.BlockSpec((tk, tn), lambda i, j, kk: (kk, j))
    in_specs = [a_spec, b_spec]
    args = [a, b]
    if bias is not None:
        in_specs.append(pl.BlockSpec((1, tn), lambda i, j, kk: (0, j)))
        args.append(bias.reshape(1, n))
    return pl.pallas_call(
        body, name=name,
        out_shape=jax.ShapeDtypeStruct((m, n), out_dtype),
        grid=(m // tm, n // tn, nk),
        in_specs=in_specs,
        out_specs=pl.BlockSpec((tm, tn), lambda i, j, kk: (i, j)),
        scratch_shapes=[pltpu.VMEM((tm, tn), F32)],
        compiler_params=_cparams(("parallel", "parallel", "arbitrary")),
    )(*args)


TOK_TILE = 256


def _vec_spec(d):
    return pl.BlockSpec((1, 1, d), lambda b, s: (b, 0, 0))


def _tok_spec(ts, d):
    return pl.BlockSpec((1, ts, d), lambda b, s: (b, s, 0))


def mod_fwd(x, scale, shift):
    bsz, seq, d = x.shape
    ts = _rows_tile(seq, TOK_TILE)

    def body(x_ref, sc_ref, sh_ref, h_ref):
        h_ref[0] = (x_ref[0] * (1.0 + sc_ref[0]) + sh_ref[0]).astype(BF16)

    return pl.pallas_call(
        body, name="mod_fwd",
        out_shape=jax.ShapeDtypeStruct((bsz, seq, d), BF16),
        grid=(bsz, seq // ts),
        in_specs=[_tok_spec(ts, d), _vec_spec(d), _vec_spec(d)],
        out_specs=_tok_spec(ts, d),
        compiler_params=_cparams(("parallel", "parallel")),
    )(x, scale, shift)


def _ln_stats(z):
    mu = jnp.mean(z, axis=-1, keepdims=True)
    zc = z - mu
    var = jnp.mean(zc * zc, axis=-1, keepdims=True)
    return zc, lax.rsqrt(var + LN_EPS)


def resln_fwd(x, y, gate, g, b, res_w, nxt):
    bsz, seq, d = x.shape
    ts = _rows_tile(seq, TOK_TILE)

    def body(*refs):
        if nxt is None:
            x_ref, y_ref, gt_ref, g_ref, b_ref, xn_ref = refs
        else:
            x_ref, y_ref, gt_ref, g_ref, b_ref, sc_ref, sh_ref, xn_ref, hn_ref = refs
        z = DEEPNORM_ALPHA * x_ref[0] + (res_w * (1.0 + gt_ref[0])) * y_ref[0]
        zc, r = _ln_stats(z)
        xn = zc * r * g_ref[...] + b_ref[...]
        xn_ref[0] = xn
        if nxt is not None:
            hn_ref[0] = (xn * (1.0 + sc_ref[0]) + sh_ref[0]).astype(BF16)

    row = pl.BlockSpec((1, d), lambda bb, s: (0, 0))
    in_specs = [_tok_spec(ts, d), _tok_spec(ts, d), _vec_spec(d), row, row]
    args = [x, y, gate, g.reshape(1, d), b.reshape(1, d)]
    out_shape = [jax.ShapeDtypeStruct((bsz, seq, d), F32)]
    out_specs = [_tok_spec(ts, d)]
    if nxt is not None:
        in_specs += [_vec_spec(d), _vec_spec(d)]
        args += list(nxt)
        out_shape.append(jax.ShapeDtypeStruct((bsz, seq, d), BF16))
        out_specs.append(_tok_spec(ts, d))
    out = pl.pallas_call(
        body, name="resln_fwd",
        out_shape=out_shape, grid=(bsz, seq // ts),
        in_specs=in_specs, out_specs=out_specs,
        compiler_params=_cparams(("parallel", "parallel")),
    )(*args)
    return (out[0], None) if nxt is None else (out[0], out[1])


def resln_bwd(gout, x, y, gate, g, res_w):
    bsz, seq, d = x.shape
    ts = _rows_tile(seq, TOK_TILE)

    def body(go_ref, x_ref, y_ref, gt_ref, g_ref, dxa_ref, dy_ref, dgt_ref, dg_ref, db_ref):
        bb, s = pl.program_id(0), pl.program_id(1)
        yv = y_ref[0]
        rw = res_w * (1.0 + gt_ref[0])
        z = DEEPNORM_ALPHA * x_ref[0] + rw * yv
        zc, r = _ln_stats(z)
        xhat = zc * r
        go = go_ref[0]
        dxh = go * g_ref[...]
        dz = r * (dxh - jnp.mean(dxh, axis=-1, keepdims=True)
                  - xhat * jnp.mean(dxh * xhat, axis=-1, keepdims=True))
        dxa_ref[0] = DEEPNORM_ALPHA * dz
        dy_ref[0] = (rw * dz).astype(BF16)
        dgt = res_w * jnp.sum(dz * yv, axis=0, keepdims=True)
        dg = jnp.sum(go * xhat, axis=0, keepdims=True)
        db = jnp.sum(go, axis=0, keepdims=True)

        @pl.when(s == 0)
        def _():
            dgt_ref[0] = dgt

        @pl.when(s > 0)
        def _():
            dgt_ref[0] += dgt

        first = jnp.logical_and(bb == 0, s == 0)

        @pl.when(first)
        def _():
            dg_ref[...] = dg
            db_ref[...] = db

        @pl.when(jnp.logical_not(first))
        def _():
            dg_ref[...] += dg
            db_ref[...] += db

    row = pl.BlockSpec((1, d), lambda bb, s: (0, 0))
    return pl.pallas_call(
        body, name="resln_bwd",
        out_shape=[jax.ShapeDtypeStruct((bsz, seq, d), F32),
                   jax.ShapeDtypeStruct((bsz, seq, d), BF16),
                   jax.ShapeDtypeStruct((bsz, 1, d), F32),
                   jax.ShapeDtypeStruct((1, d), F32),
                   jax.ShapeDtypeStruct((1, d), F32)],
        grid=(bsz, seq // ts),
        in_specs=[_tok_spec(ts, d), _tok_spec(ts, d), _tok_spec(ts, d), _vec_spec(d), row],
        out_specs=[_tok_spec(ts, d), _tok_spec(ts, d), _vec_spec(d), row, row],
        compiler_params=_cparams(("arbitrary", "arbitrary")),
    )(gout, x, y, gate, g.reshape(1, d))


def mod_bwd(dxa, dh, x, scale):
    bsz, seq, d = x.shape
    ts = _rows_tile(seq, TOK_TILE)

    def body(dxa_ref, dh_ref, x_ref, sc_ref, dx_ref, dsc_ref, dsh_ref):
        s = pl.program_id(1)
        dh = dh_ref[0].astype(F32)
        dx_ref[0] = dxa_ref[0] + dh * (1.0 + sc_ref[0])
        dsc = jnp.sum(dh * x_ref[0], axis=0, keepdims=True)
        dsh = jnp.sum(dh, axis=0, keepdims=True)

        @pl.when(s == 0)
        def _():
            dsc_ref[0] = dsc
            dsh_ref[0] = dsh

        @pl.when(s > 0)
        def _():
            dsc_ref[0] += dsc
            dsh_ref[0] += dsh

    return pl.pallas_call(
        body, name="mod_bwd",
        out_shape=[jax.ShapeDtypeStruct((bsz, seq, d), F32),
                   jax.ShapeDtypeStruct((bsz, 1, d), F32),
                   jax.ShapeDtypeStruct((bsz, 1, d), F32)],
        grid=(bsz, seq // ts),
        in_specs=[_tok_spec(ts, d), _tok_spec(ts, d), _tok_spec(ts, d), _vec_spec(d)],
        out_specs=[_tok_spec(ts, d), _vec_spec(d), _vec_spec(d)],
        compiler_params=_cparams(("parallel", "arbitrary")),
    )(dxa, dh, x, scale)


FF_HALF = D_FF // 2
FF_CHUNKS = ((0, 384), (384, 384), (768, 384), (1152, 256))


def ffn_up_fused(h2, wgu):
    t, d = h2.shape
    tm = _rows_tile(t, 512)

    def body(h_ref, w_ref, gu_ref, a_ref):
        h = h_ref[...]
        for c0, cw in FF_CHUNKS:
            g = _dot(h, w_ref[:, c0:c0 + cw], NN)
            u = _dot(h, w_ref[:, FF_HALF + c0:FF_HALF + c0 + cw], NN)
            gu_ref[:, c0:c0 + cw] = g.astype(BF16)
            gu_ref[:, FF_HALF + c0:FF_HALF + c0 + cw] = u.astype(BF16)
            a_ref[:, c0:c0 + cw] = (g * _sigmoid(g) * u).astype(BF16)

    return pl.pallas_call(
        body, name="ffn_up_fused",
        out_shape=[jax.ShapeDtypeStruct((t, 2 * D_FF), BF16), jax.ShapeDtypeStruct((t, D_FF), BF16)],
        grid=(2, t // tm),
        in_specs=[pl.BlockSpec((tm, d), lambda j, i: (i, 0)),
                  pl.BlockSpec((d, 2 * FF_HALF), lambda j, i: (0, j))],
        out_specs=[pl.BlockSpec((tm, 2 * FF_HALF), lambda j, i: (i, j)),
                   pl.BlockSpec((tm, FF_HALF), lambda j, i: (i, j))],
        compiler_params=_cparams(("parallel", "parallel")),
    )(h2, wgu)


def ffn_down_bwd_fused(dy2, wd_t, gu):
    t, d = dy2.shape
    tm = _rows_tile(t, 512)

    def body(dy_ref, w_ref, gu_ref, o_ref):
        dy = dy_ref[...]
        for c0, cw in FF_CHUNKS:
            da = _dot(dy, w_ref[:, c0:c0 + cw], NN)
            g = gu_ref[:, c0:c0 + cw].astype(F32)
            u = gu_ref[:, FF_HALF + c0:FF_HALF + c0 + cw].astype(F32)
            sg = _sigmoid(g)
            o_ref[:, c0:c0 + cw] = (da * u * (sg * (1.0 + g * (1.0 - sg)))).astype(BF16)
            o_ref[:, FF_HALF + c0:FF_HALF + c0 + cw] = (da * (g * sg)).astype(BF16)

    return pl.pallas_call(
        body, name="ffn_down_bwd_fused",
        out_shape=jax.ShapeDtypeStruct((t, 2 * D_FF), BF16),
        grid=(2, t // tm),
        in_specs=[pl.BlockSpec((tm, d), lambda j, i: (i, 0)),
                  pl.BlockSpec((d, FF_HALF), lambda j, i: (0, j)),
                  pl.BlockSpec((tm, 2 * FF_HALF), lambda j, i: (i, j))],
        out_specs=pl.BlockSpec((tm, 2 * FF_HALF), lambda j, i: (i, j)),
        compiler_params=_cparams(("parallel", "parallel")),
    )(dy2, wd_t, gu)


def loss_fwd(y, tgt):
    bsz, seq, d = y.shape
    ts = _rows_tile(seq, TOK_TILE)

    def body(y_ref, t_ref, dy_ref, l_ref, acc_ref):
        bb, s = pl.program_id(0), pl.program_id(1)
        e = y_ref[0] - t_ref[0]
        dy_ref[0] = e * (1.0 / d)
        part = jnp.sum((e * e).reshape(ts // 8, 8, d), axis=0)
        first = jnp.logical_and(bb == 0, s == 0)

        @pl.when(first)
        def _():
            acc_ref[...] = part

        @pl.when(jnp.logical_not(first))
        def _():
            acc_ref[...] += part

        @pl.when(jnp.logical_and(bb == pl.num_programs(0) - 1, s == pl.num_programs(1) - 1))
        def _():
            tot = jnp.sum(jnp.sum(acc_ref[...], axis=1, keepdims=True), axis=0, keepdims=True)
            l_ref[...] = jnp.broadcast_to(tot * (0.5 / d), (8, LANES))

    dy, l = pl.pallas_call(
        body, name="loss_fwd",
        out_shape=[jax.ShapeDtypeStruct((bsz, seq, d), F32), jax.ShapeDtypeStruct((8, LANES), F32)],
        grid=(bsz, seq // ts),
        in_specs=[_tok_spec(ts, d), _tok_spec(ts, d)],
        out_specs=[_tok_spec(ts, d), pl.BlockSpec((8, LANES), lambda bb, s: (0, 0))],
        scratch_shapes=[pltpu.VMEM((8, d), F32)],
        compiler_params=_cparams(("arbitrary", "arbitrary")),
    )(y, tgt)
    return l[0, 0], dy


def _rot_half(x):
    lane = lax.broadcasted_iota(jnp.int32, x.shape, 1)
    swapped = jnp.where(lane < 80, pltpu.roll(x, 112, axis=1), pltpu.roll(x, 16, axis=1))
    return jnp.where((lane >= C_NOPE) & (lane < C_NOPE + C_ROPE), swapped, 0.0)


def _rms(x, w):
    r = lax.rsqrt(jnp.mean(x * x, axis=-1, keepdims=True) + RMS_EPS)
    return x * r * w, r


def _rms_bwd(dy, x, w):
    r = lax.rsqrt(jnp.mean(x * x, axis=-1, keepdims=True) + RMS_EPS)
    wd = dy * w
    dx = r * wd - x * (r * r * r) * jnp.mean(x * wd, axis=-1, keepdims=True)
    return dx, jnp.sum(dy * x * r, axis=0, keepdims=True)


def mla_mid_fwd(cqkv, qw, kw, cq_tab, sg_tab):
    bsz, seq, _ = cqkv.shape
    ts = _rows_tile(seq, TOK_TILE)
    r = C_Q_RANK

    def body(x_ref, qw_ref, kw_ref, c_ref, s_ref, nq_ref, nkv_ref, kr_ref):
        x = x_ref[0]
        nq_ref[0] = _rms(x[:, :r], qw_ref[...])[0].astype(BF16)
        nkv_ref[0] = _rms(x[:, r:2 * r], kw_ref[...])[0].astype(BF16)
        xr = x[:, 2 * r:]
        kr_ref[0] = (xr * c_ref[0] + _rot_half(xr) * s_ref[0]).astype(BF16)

    row = pl.BlockSpec((1, r), lambda b, s: (0, 0))
    return pl.pallas_call(
        body, name="mla_mid_fwd",
        out_shape=[jax.ShapeDtypeStruct((bsz, seq, r), BF16), jax.ShapeDtypeStruct((bsz, seq, r), BF16),
                   jax.ShapeDtypeStruct((bsz, seq, LANES), BF16)],
        grid=(bsz, seq // ts),
        in_specs=[_tok_spec(ts, MLA_DOWN_PAD), row, row, _tok_spec(ts, LANES), _tok_spec(ts, LANES)],
        out_specs=[_tok_spec(ts, r), _tok_spec(ts, r), _tok_spec(ts, LANES)],
        compiler_params=_cparams(("parallel", "parallel")),
    )(cqkv, qw.reshape(1, r), kw.reshape(1, r), cq_tab, sg_tab)


def mla_mid_bwd(dnq, dnkv, dkr, cqkv, qw, kw, cq_tab, sg_tab):
    bsz, seq, _ = cqkv.shape
    ts = _rows_tile(seq, TOK_TILE)
    r = C_Q_RANK

    def body(dnq_ref, dnkv_ref, dkr_ref, x_ref, qw_ref, kw_ref, c_ref, s_ref, dx_ref, dqw_ref, dkw_ref):
        bb, s = pl.program_id(0), pl.program_id(1)
        x = x_ref[0]
        dcq, dqw = _rms_bwd(dnq_ref[0].astype(F32), x[:, :r], qw_ref[...])
        dckv, dkw = _rms_bwd(dnkv_ref[0].astype(F32), x[:, r:2 * r], kw_ref[...])
        dk = dkr_ref[0]
        dxr = dk * c_ref[0] + _rot_half(dk * s_ref[0])
        dx_ref[0, :, :r] = dcq.astype(BF16)
        dx_ref[0, :, r:2 * r] = dckv.astype(BF16)
        dx_ref[0, :, 2 * r:] = dxr.astype(BF16)
        first = jnp.logical_and(bb == 0, s == 0)

        @pl.when(first)
        def _():
            dqw_ref[...] = dqw
            dkw_ref[...] = dkw

        @pl.when(jnp.logical_not(first))
        def _():
            dqw_ref[...] += dqw
            dkw_ref[...] += dkw

    row = pl.BlockSpec((1, r), lambda b, s: (0, 0))
    return pl.pallas_call(
        body, name="mla_mid_bwd",
        out_shape=[jax.ShapeDtypeStruct((bsz, seq, MLA_DOWN_PAD), BF16),
                   jax.ShapeDtypeStruct((1, r), F32), jax.ShapeDtypeStruct((1, r), F32)],
        grid=(bsz, seq // ts),
        in_specs=[_tok_spec(ts, r), _tok_spec(ts, r), _tok_spec(ts, LANES), _tok_spec(ts, MLA_DOWN_PAD),
                  row, row, _tok_spec(ts, LANES), _tok_spec(ts, LANES)],
        out_specs=[_tok_spec(ts, MLA_DOWN_PAD), row, row],
        compiler_params=_cparams(("arbitrary", "arbitrary")),
    )(dnq, dnkv, dkr, cqkv, qw.reshape(1, r), kw.reshape(1, r), cq_tab, sg_tab)


def q_rope(q, cq_tab, sg_tab, transpose_rule, name, out_scale=1.0):
    bsz, seq, w = q.shape
    ts = _rows_tile(seq, TOK_TILE)

    def body(q_ref, c_ref, s_ref, o_ref):
        c, s = c_ref[0], s_ref[0]
        for h in range(w // LANES):
            x = q_ref[0, :, h * LANES:(h + 1) * LANES].astype(F32)
            y = x * c + (_rot_half(x * s) if transpose_rule else _rot_half(x) * s)
            o_ref[0, :, h * LANES:(h + 1) * LANES] = (y * out_scale).astype(BF16)

    return pl.pallas_call(
        body, name=name,
        out_shape=jax.ShapeDtypeStruct((bsz, seq, w), BF16),
        grid=(bsz, seq // ts),
        in_specs=[_tok_spec(ts, w), _tok_spec(ts, LANES), _tok_spec(ts, LANES)],
        out_specs=_tok_spec(ts, w),
        compiler_params=_cparams(("parallel", "parallel")),
    )(q, cq_tab, sg_tab)


MLA_SCALE = (C_NOPE + C_ROPE) ** -0.5
LOG2E = math.log2(math.e)
MLA_QSCALE = MLA_SCALE * LOG2E
MLA_TILE = 1024
NEG_BIG = -1e30


def _eye_mask(n):
    return lax.broadcasted_iota(jnp.int32, (n, n), 0) == lax.broadcasted_iota(jnp.int32, (n, n), 1)


def mla_attn_fwd(qp, kv, kr):
    bsz, seq, _ = qp.shape
    tq = _rows_tile(seq, MLA_TILE)
    nt = seq // tq

    def body(q_ref, kv_ref, kr_ref, o_ref, lse_ref, kc_ref):
        lane = lax.broadcasted_iota(jnp.int32, (seq, LANES), 1)
        kc_ref[...] = jnp.where(lane < C_NOPE, kv_ref[0], kr_ref[0])
        lane_q = lax.broadcasted_iota(jnp.int32, (tq, LANES), 1)

        def q_body(i, carry):
            qrows = pl.ds(pl.multiple_of(i * tq, tq), tq)
            q_i = q_ref[0, qrows, :]

            def key_body(j, st):
                m, l, acc = st
                krows = pl.ds(pl.multiple_of(j * tq, tq), tq)
                s = _dot(q_i, kc_ref[krows, :], NT)
                m_new = jnp.maximum(m, jnp.max(s, axis=-1, keepdims=True))
                alpha = jnp.exp2(m - m_new)
                p = jnp.exp2(s - m_new)
                l = alpha * l + jnp.sum(p, axis=-1, keepdims=True)
                acc = alpha * acc + _dot(p.astype(BF16), kv_ref[0, krows, :], NN)
                return m_new, l, acc

            m, l, acc = lax.fori_loop(0, nt, key_body, (
                jnp.full((tq, 1), NEG_BIG, F32), jnp.zeros((tq, 1), F32), jnp.zeros((tq, LANES), F32)))
            o_ref[0, qrows, :] = jnp.where(lane_q >= C_NOPE, acc * (1.0 / l), 0.0).astype(BF16)
            lse = m + jnp.log2(l)
            lse_ref[0, 0, :, qrows] = jnp.sum(jnp.where(_eye_mask(tq), lse, 0.0), axis=0, keepdims=True)
            return carry

        lax.fori_loop(0, nt, q_body, 0)

    head = pl.BlockSpec((1, seq, LANES), lambda b, h: (b, 0, h))
    return pl.pallas_call(
        body, name="mla_attn_fwd",
        out_shape=[jax.ShapeDtypeStruct((bsz, seq, MLA_HEAD_PAD), BF16),
                   jax.ShapeDtypeStruct((bsz, C_HEADS, 1, seq), F32)],
        grid=(bsz, C_HEADS),
        in_specs=[head, head, pl.BlockSpec((1, seq, LANES), lambda b, h: (b, 0, 0))],
        out_specs=[head, pl.BlockSpec((1, 1, 1, seq), lambda b, h: (b, h, 0, 0))],
        scratch_shapes=[pltpu.VMEM((seq, LANES), BF16)],
        compiler_params=_cparams(("parallel", "parallel")),
    )(qp, kv, kr)


def mla_attn_bwd(qp, kv, kr, o, do, lse):
    bsz, seq, _ = qp.shape
    tq = _rows_tile(seq, MLA_TILE)
    nt = seq // tq

    def body(q_ref, kv_ref, kr_ref, o_ref, do_ref, lse_ref, dq_ref, dkv_ref, dkr_ref,
             kc_ref, drow_ref, dqa_ref, dkc_ref, dkvv_ref):
        h = pl.program_id(1)
        lane_s = lax.broadcasted_iota(jnp.int32, (seq, LANES), 1)
        lane_t = lax.broadcasted_iota(jnp.int32, (tq, LANES), 1)
        kc_ref[...] = jnp.where(lane_s < C_NOPE, kv_ref[0], kr_ref[0])
        dqa_ref[...] = jnp.zeros_like(dqa_ref)
        ones = jnp.ones((8, LANES), F32)

        def delta_body(i, carry):
            rows = pl.ds(pl.multiple_of(i * tq, tq), tq)
            prod = do_ref[0, rows, :].astype(F32) * o_ref[0, rows, :].astype(F32)
            drow_ref[:, rows] = _dot(ones, prod, NT, precision=HIGHEST)
            return carry

        lax.fori_loop(0, nt, delta_body, 0)

        def key_body(j, carry):
            krows = pl.ds(pl.multiple_of(j * tq, tq), tq)
            kc_j = kc_ref[krows, :]
            kv_j = kv_ref[0, krows, :]
            dkc_ref[...] = jnp.zeros_like(dkc_ref)
            dkvv_ref[...] = jnp.zeros_like(dkvv_ref)

            def q_body(i, c2):
                qrows = pl.ds(pl.multiple_of(i * tq, tq), tq)
                q_i = q_ref[0, qrows, :]
                do_i = do_ref[0, qrows, :]
                st = _dot(kc_j, q_i, NT)
                pt = jnp.exp2(st - lse_ref[0, 0, :, qrows])
                dpt = _dot(kv_j, do_i, NT)
                dst = (pt * (dpt - drow_ref[0:1, qrows])).astype(BF16)
                dkvv_ref[...] += _dot(pt.astype(BF16), do_i, NN)
                dkc_ref[...] += _dot(dst, q_i, NN)
                dqa_ref[qrows, :] += _dot(dst, kc_j, TN)
                return c2

            lax.fori_loop(0, nt, q_body, 0)
            dkc = dkc_ref[...] * (MLA_SCALE / MLA_QSCALE)
            dkv_ref[0, krows, :] = jnp.where(lane_t < C_NOPE, dkc, dkvv_ref[...]).astype(BF16)
            dkr_j = jnp.where(lane_t >= C_NOPE, dkc, 0.0)

            @pl.when(h == 0)
            def _():
                dkr_ref[0, krows, :] = dkr_j

            @pl.when(h > 0)
            def _():
                dkr_ref[0, krows, :] += dkr_j

            return carry

        lax.fori_loop(0, nt, key_body, 0)
        dq_ref[0] = (dqa_ref[...] * MLA_SCALE).astype(BF16)

    head = pl.BlockSpec((1, seq, LANES), lambda b, h: (b, 0, h))
    shared = pl.BlockSpec((1, seq, LANES), lambda b, h: (b, 0, 0))
    return pl.pallas_call(
        body, name="mla_attn_bwd",
        out_shape=[jax.ShapeDtypeStruct((bsz, seq, MLA_HEAD_PAD), BF16),
                   jax.ShapeDtypeStruct((bsz, seq, MLA_HEAD_PAD), BF16),
                   jax.ShapeDtypeStruct((bsz, seq, LANES), F32)],
        grid=(bsz, C_HEADS),
        in_specs=[head, head, shared, head, head,
                  pl.BlockSpec((1, 1, 1, seq), lambda b, h: (b, h, 0, 0))],
        out_specs=[head, head, shared],
        scratch_shapes=[pltpu.VMEM((seq, LANES), BF16), pltpu.VMEM((8, seq), F32),
                        pltpu.VMEM((seq, LANES), F32), pltpu.VMEM((tq, LANES), F32),
                        pltpu.VMEM((tq, LANES), F32)],
        compiler_params=_cparams(("parallel", "arbitrary")),
    )(qp, kv, kr, o, do, lse)


A_SCALE = A_HEAD_DIM ** -0.5
A_GROUP = A_HEADS // A_KV_HEADS
A_BAND = 3 * A_BLOCK


def _wattn_block(i, seq, posc_ref, posr_ref):
    start = pl.multiple_of(i * A_BLOCK, A_BLOCK)
    pq = posc_ref[0]
    pk = posr_ref[0, :, pl.ds(start, A_BAND)]
    dist = jnp.abs(pq - pk).astype(F32)
    qi = lax.broadcasted_iota(jnp.int32, (A_BLOCK, A_BAND), 0) + A_BLOCK
    ki = lax.broadcasted_iota(jnp.int32, (A_BLOCK, A_BAND), 1)
    absk = i * A_BLOCK + ki - A_BLOCK
    valid = (jnp.abs(qi - ki) <= WINDOW) & (absk >= 0) & (absk < seq)
    return start, dist, valid


def _wattn_probs(qh, kb, dist, valid, slope, sink):
    s = _dot(qh, kb, NT) * A_SCALE - slope * dist
    s = jnp.where(valid, s, NEG_BIG)
    m = jnp.maximum(jnp.max(s, axis=-1, keepdims=True), sink)
    p = jnp.exp(s - m)
    es = jnp.exp(sink - m)
    inv = 1.0 / (jnp.sum(p, axis=-1, keepdims=True) + es)
    return p * inv, es * inv


def _alibi_slope(h):
    return 2.0 ** (-8.0 * (h + 1) / A_HEADS)


def wattn_fwd(proj, kp, vp, posc, posr, sink):
    bsz, seq, _ = proj.shape
    nb = seq // A_BLOCK
    qw = A_HEADS * LANES

    def body(sink_ref, q_ref, kp_ref, vp_ref, posc_ref, posr_ref, o_ref):
        i = pl.program_id(1)
        start, dist, valid = _wattn_block(i, seq, posc_ref, posr_ref)
        kb = kp_ref[0, pl.ds(start, A_BAND), :]
        vb = vp_ref[0, pl.ds(start, A_BAND), :]
        lane = lax.broadcasted_iota(jnp.int32, (A_BLOCK, LANES), 1)
        for h in range(A_HEADS):
            g = h // A_GROUP
            qh = q_ref[0, :, h * LANES:(h + 1) * LANES].astype(BF16)
            p, _ = _wattn_probs(qh, kb, dist, valid, _alibi_slope(h), sink_ref[h])
            o = _dot(p.astype(BF16), vb, NN)
            mine = (lane >= A_HEAD_DIM) if g == 1 else (lane < A_HEAD_DIM)
            o_ref[0, :, h * LANES:(h + 1) * LANES] = jnp.where(mine, o, 0.0).astype(BF16)

    return pl.pallas_call(
        body, name="wattn_fwd",
        out_shape=jax.ShapeDtypeStruct((bsz, seq, MO_PAD), BF16),
        grid=(bsz, nb),
        in_specs=[pl.BlockSpec(memory_space=pltpu.SMEM),
                  pl.BlockSpec((1, A_BLOCK, qw), lambda b, i: (b, i, 0)),
                  pl.BlockSpec((1, seq + 2 * A_BLOCK, LANES), lambda b, i: (b, 0, 0)),
                  pl.BlockSpec((1, seq + 2 * A_BLOCK, LANES), lambda b, i: (b, 0, 0)),
                  pl.BlockSpec((1, A_BLOCK, 1), lambda b, i: (b, i, 0)),
                  pl.BlockSpec((1, 1, seq + 2 * A_BLOCK), lambda b, i: (b, 0, 0))],
        out_specs=pl.BlockSpec((1, A_BLOCK, qw), lambda b, i: (b, i, 0)),
        compiler_params=_cparams(("parallel", "parallel")),
    )(sink, proj, kp, vp, posc, posr)


def wattn_bwd(proj, kp, vp, posc, posr, sink, dmo):
    bsz, seq, _ = proj.shape
    nb = seq // A_BLOCK
    qw = A_HEADS * LANES
    sp = seq + 2 * A_BLOCK

    def body(sink_ref, q_ref, kp_ref, vp_ref, posc_ref, posr_ref, do_ref,
             dq_ref, dk_ref, dv_ref, ds_ref):
        i = pl.program_id(1)

        @pl.when(i == 0)
        def _():
            dk_ref[...] = jnp.zeros_like(dk_ref)
            dv_ref[...] = jnp.zeros_like(dv_ref)

        @pl.when(jnp.logical_and(i == 0, pl.program_id(0) == 0))
        def _():
            ds_ref[...] = jnp.zeros_like(ds_ref)

        start, dist, valid = _wattn_block(i, seq, posc_ref, posr_ref)
        kb = kp_ref[0, pl.ds(start, A_BAND), :]
        vb = vp_ref[0, pl.ds(start, A_BAND), :]
        lane = lax.broadcasted_iota(jnp.int32, (A_BLOCK, LANES), 1)
        lane1 = lax.broadcasted_iota(jnp.int32, (1, LANES), 1)
        dk_acc = jnp.zeros((A_BAND, LANES), F32)
        dv_acc = jnp.zeros((A_BAND, LANES), F32)
        dsink = jnp.zeros((1, LANES), F32)
        for h in range(A_HEADS):
            g = h // A_GROUP
            qh = q_ref[0, :, h * LANES:(h + 1) * LANES].astype(BF16)
            p, psink = _wattn_probs(qh, kb, dist, valid, _alibi_slope(h), sink_ref[h])
            pb = p.astype(BF16)
            do = do_ref[0, :, h * LANES:(h + 1) * LANES]
            dob = do.astype(BF16)
            mine = (lane >= A_HEAD_DIM) if g == 1 else (lane < A_HEAD_DIM)
            o = jnp.where(mine, _dot(pb, vb, NN), 0.0)
            delta = jnp.sum(do * o, axis=-1, keepdims=True)
            dp = _dot(dob, vb, NT)
            ds = (p * (dp - delta) * A_SCALE).astype(BF16)
            dq_ref[0, :, h * LANES:(h + 1) * LANES] = _dot(ds, kb, NN).astype(BF16)
            dk_acc = dk_acc + _dot(ds, qh, TN)
            dv_acc = dv_acc + _dot(pb, dob, TN)
            dsh = -jnp.sum(psink * delta, axis=0, keepdims=True)
            dsink = dsink + jnp.where(lane1 == h, dsh, 0.0)
        dk_ref[0, pl.ds(start, A_BAND), :] += dk_acc
        dv_ref[0, pl.ds(start, A_BAND), :] += dv_acc
        ds_ref[0] += dsink

    full = pl.BlockSpec((1, sp, LANES), lambda b, i: (b, 0, 0))
    return pl.pallas_call(
        body, name="wattn_bwd",
        out_shape=[jax.ShapeDtypeStruct((bsz, seq, qw), BF16),
                   jax.ShapeDtypeStruct((bsz, sp, LANES), F32),
                   jax.ShapeDtypeStruct((bsz, sp, LANES), F32),
                   jax.ShapeDtypeStruct((1, 1, LANES), F32)],
        grid=(bsz, nb),
        in_specs=[pl.BlockSpec(memory_space=pltpu.SMEM),
                  pl.BlockSpec((1, A_BLOCK, qw), lambda b, i: (b, i, 0)),
                  full, full,
                  pl.BlockSpec((1, A_BLOCK, 1), lambda b, i: (b, i, 0)),
                  pl.BlockSpec((1, 1, sp), lambda b, i: (b, 0, 0)),
                  pl.BlockSpec((1, A_BLOCK, qw), lambda b, i: (b, i, 0))],
        out_specs=[pl.BlockSpec((1, A_BLOCK, qw), lambda b, i: (b, i, 0)), full, full,
                   pl.BlockSpec((1, 1, LANES), lambda b, i: (0, 0, 0))],
        compiler_params=_cparams(("arbitrary", "arbitrary")),
    )(sink, proj, kp, vp, posc, posr, dmo)


HG_Q, HG_FF, HG_FB, HG_I, HG_G = 10, 14, 18, 22, 26
HG_OUT = 8
CH = B_CHUNK
HG_GROUP = 4
GR = HG_GROUP * CH


def _hgrn_consts(reverse):
    r = lax.broadcasted_iota(jnp.int32, (GR, GR), 0)
    c = lax.broadcasted_iota(jnp.int32, (GR, GR), 1)
    same = jnp.right_shift(r, 6) == jnp.right_shift(c, 6)
    incl = same & ((c >= r) if reverse else (c <= r))
    row = lax.broadcasted_iota(jnp.int32, (HG_GROUP, CH, LANES), 1)
    mid = CH // 2 if reverse else CH // 2 - 1
    end = 0 if reverse else CH - 1
    return incl, row == mid, row == end


def _per_chunk(x, sel=None):
    x3 = x.reshape(HG_GROUP, CH, LANES)
    return jnp.sum(x3 if sel is None else jnp.where(sel, x3, 0.0), axis=1, keepdims=True)


def _to_rows(x3):
    return jnp.broadcast_to(x3, (HG_GROUP, CH, LANES)).reshape(GR, LANES)


def _chunk_cumsum(x, reverse):
    pos = lax.broadcasted_iota(jnp.int32, (GR, LANES), 0) & (CH - 1)
    s = 1
    while s < CH:
        if reverse:
            x = x + jnp.where(pos < CH - s, pltpu.roll(x, GR - s, axis=0), 0.0)
        else:
            x = x + jnp.where(pos >= s, pltpu.roll(x, s, axis=0), 0.0)
        s *= 2
    return x


def _block_diag(x):
    chunk = jnp.right_shift(lax.broadcasted_iota(jnp.int32, (GR, LANES), 0), 6)
    return jnp.concatenate([jnp.where(chunk == n, x, jnp.zeros_like(x)) for n in range(HG_GROUP)], axis=1)


def _hgrn_gates(hf, lb):
    sig = _sigmoid(hf)
    f = lb + (1.0 - lb) * sig
    return 1.0 - f, jnp.log(f), sig, f


def _hgrn_decays(lf, consts, reverse):
    _, is_mid, is_end = consts
    b = _chunk_cumsum(lf, reverse)
    return b, _per_chunk(b, is_mid), _per_chunk(b, is_end)


def _chunk_order(reverse):
    return range(HG_GROUP - 1, -1, -1) if reverse else range(HG_GROUP)


def _lane_block(x, n):
    return x[:, n * LANES:(n + 1) * LANES]


def _hgrn_states(vb, kd, dec3, st, reverse):
    inc = _dot(vb, _block_diag(kd), TN)
    entering = [None] * HG_GROUP
    for n in _chunk_order(reverse):
        entering[n] = st
        st = dec3[n] * st + _lane_block(inc, n)
    return entering, st


def _hgrn_group(q, k, v, lf, st, consts, reverse):
    b, bm3, be3 = _hgrn_decays(lf, consts, reverse)
    bm, be = _to_rows(bm3), _to_rows(be3)
    qs = (q * jnp.exp(b - bm)).astype(BF16)
    ks = (k * jnp.exp(bm - b)).astype(BF16)
    a = jnp.where(consts[0], _dot(qs, ks, NT), 0.0).astype(BF16)
    qe = (q * jnp.exp(b)).astype(BF16)
    vb = v.astype(BF16)
    kd = (k * jnp.exp(be - b)).astype(BF16)
    entering, st = _hgrn_states(vb, kd, jnp.exp(be3), st, reverse)
    s_all = jnp.concatenate([e.astype(BF16) for e in entering], axis=1)
    return _dot(a, vb, NN) + _dot(_block_diag(qe), s_all, NT), st


def _silu(x):
    return x * _sigmoid(x)


def hgrn_fwd(proj, mo, lb, nw):
    bsz, seq, _ = proj.shape
    n_gr = seq // GR
    ts = _rows_tile(seq, 512)

    def body(hq_ref, hff_ref, hfb_ref, hi_ref, hg_ref, mo_in_ref, lb_ref, nw_ref, y_ref, ot_ref):
        del mo_in_ref
        lb_v, nw_v = lb_ref[0], nw_ref[0]

        def run(hf_ref, reverse, first):
            consts = _hgrn_consts(reverse)

            def step(t, st):
                n = (n_gr - 1 - t) if reverse else t
                rows = pl.ds(pl.multiple_of(n * GR, GR), GR)
                k, lf, _, _ = _hgrn_gates(hf_ref[0, rows, :], lb_v)
                o, st = _hgrn_group(_silu(hq_ref[0, rows, :]), k, hi_ref[0, rows, :], lf, st, consts, reverse)
                if first:
                    ot_ref[0, rows, :] = o
                else:
                    ot_ref[0, rows, :] += o
                return st

            lax.fori_loop(0, n_gr, step, jnp.zeros((LANES, LANES), F32))

        run(hff_ref, False, True)
        run(hfb_ref, True, False)

        def finish(i, carry):
            rows = pl.ds(pl.multiple_of(i * ts, ts), ts)
            o = ot_ref[0, rows, :]
            r = lax.rsqrt(jnp.mean(o * o, axis=-1, keepdims=True) + RMS_EPS)
            y_ref[0, rows, :] = (o * r * nw_v * _silu(hg_ref[0, rows, :])).astype(BF16)
            return carry

        lax.fori_loop(0, seq // ts, finish, 0)

    def col(g):
        return pl.BlockSpec((1, seq, LANES), lambda b, h: (b, 0, g + h))

    par = pl.BlockSpec((1, 1, LANES), lambda b, h: (h, 0, 0))
    return pl.pallas_call(
        body, name="hgrn_fwd",
        out_shape=[jax.ShapeDtypeStruct(mo.shape, BF16),
                   jax.ShapeDtypeStruct((bsz, seq, B_HEADS * LANES), F32)],
        grid=(bsz, B_HEADS),
        in_specs=[col(HG_Q), col(HG_FF), col(HG_FB), col(HG_I), col(HG_G),
                  pl.BlockSpec(memory_space=pl.ANY), par, par],
        out_specs=[col(HG_OUT), col(0)],
        input_output_aliases={5: 0},
        compiler_params=_cparams(("parallel", "parallel")),
    )(proj, proj, proj, proj, proj, mo, lb, nw)


def hgrn_out_bwd(proj, otot, dmo, nw):
    bsz, seq, _ = proj.shape
    ts = _rows_tile(seq, 512)

    def body(hg_ref, ot_ref, dmo_ref, nw_ref, do_ref, dhg_ref, dnw_ref):
        nw_v = nw_ref[0]
        dy = dmo_ref[0].astype(F32)
        hg = hg_ref[0]
        o = ot_ref[0]
        sg = _sigmoid(hg)
        r = lax.rsqrt(jnp.mean(o * o, axis=-1, keepdims=True) + RMS_EPS)
        dhg_ref[0] = (dy * (o * r * nw_v) * (sg * (1.0 + hg * (1.0 - sg)))).astype(BF16)
        drn = dy * (hg * sg)
        dnw = jnp.sum(drn * o * r, axis=0, keepdims=True)
        wd = drn * nw_v
        do_ref[0] = r * wd - o * (r * r * r) * jnp.mean(o * wd, axis=-1, keepdims=True)
        first = jnp.logical_and(pl.program_id(1) == 0, pl.program_id(2) == 0)

        @pl.when(first)
        def _():
            dnw_ref[0] = dnw

        @pl.when(jnp.logical_not(first))
        def _():
            dnw_ref[0] += dnw

    def col(g):
        return pl.BlockSpec((1, ts, LANES), lambda h, b, s: (b, s, g + h))

    par = pl.BlockSpec((1, 1, LANES), lambda h, b, s: (h, 0, 0))
    return pl.pallas_call(
        body, name="hgrn_out_bwd",
        out_shape=[jax.ShapeDtypeStruct((bsz, seq, B_HEADS * LANES), F32),
                   jax.ShapeDtypeStruct((bsz, seq, B_HEADS * LANES), BF16),
                   jax.ShapeDtypeStruct((B_HEADS, 1, LANES), F32)],
        grid=(B_HEADS, bsz, seq // ts),
        in_specs=[col(HG_G), col(0), col(HG_OUT), par],
        out_specs=[col(0), col(0), par],
        compiler_params=_cparams(("parallel", "arbitrary", "arbitrary")),
    )(proj, otot, dmo, nw)


def hgrn_bwd(proj, do, lb):
    bsz, seq, _ = proj.shape
    n_ch = seq // CH
    n_gr = seq // GR
    ts = _rows_tile(seq, 512)

    def body(hq_ref, hff_ref, hfb_ref, hi_ref, do_scr, lb_ref,
             dhq_ref, dhff_ref, dhfb_ref, dhi_ref, dlb_ref, st_scr, dq_scr, dv_scr):
        lb_v = lb_ref[0]
        do_scr = do_scr.at[0]

        def run(hf_ref, dhf_ref, reverse, first, dlb0):
            consts = _hgrn_consts(reverse)
            incl, is_mid, is_end = consts

            def load(n):
                rows = pl.ds(pl.multiple_of(n * GR, GR), GR)
                hf = hf_ref[0, rows, :]
                k, lf, sig, f = _hgrn_gates(hf, lb_v)
                return rows, _silu(hq_ref[0, rows, :]), k, hi_ref[0, rows, :], lf, sig, f

            def fwd_step(t, st):
                n = (n_gr - 1 - t) if reverse else t
                _, _, k, v, lf, _, _ = load(n)
                b, _, be3 = _hgrn_decays(lf, consts, reverse)
                kd = (k * jnp.exp(_to_rows(be3) - b)).astype(BF16)
                entering, st = _hgrn_states(v.astype(BF16), kd, jnp.exp(be3), st, reverse)
                for c in range(HG_GROUP):
                    st_scr[n * HG_GROUP + c] = entering[c].astype(BF16)
                return st

            lax.fori_loop(0, n_gr, fwd_step, jnp.zeros((LANES, LANES), F32))

            def bwd_step(t, carry):
                gt, dlb = carry
                n = t if reverse else (n_gr - 1 - t)
                rows, q, k, v, lf, sig, f = load(n)
                do_c = do_scr[rows, :].astype(BF16)
                b, bm3, be3 = _hgrn_decays(lf, consts, reverse)
                bm, be = _to_rows(bm3), _to_rows(be3)
                e_qs, e_ks, e_q, e_kd = jnp.exp(b - bm), jnp.exp(bm - b), jnp.exp(b), jnp.exp(be - b)
                dec3 = jnp.exp(be3)
                qs, ks, qe, kd = q * e_qs, k * e_ks, q * e_q, k * e_kd
                qs_b, ks_b, qe_b, kd_b = qs.astype(BF16), ks.astype(BF16), qe.astype(BF16), kd.astype(BF16)
                vb = v.astype(BF16)
                a = jnp.where(incl, _dot(qs_b, ks_b, NT), 0.0).astype(BF16)
                da = jnp.where(incl, _dot(do_c, vb, NT), 0.0).astype(BF16)
                dv = _dot(a, do_c, TN)
                dqs = _dot(da, ks_b, NN)
                dks = _dot(da, qs_b, TN)
                zed = _dot(do_c, _block_diag(qe_b), TN)
                sts = [st_scr[n * HG_GROUP + c] for c in range(HG_GROUP)]
                gts, ddec = [None] * HG_GROUP, [None] * HG_GROUP
                for c in reversed(list(_chunk_order(reverse))):
                    gts[c] = gt
                    ddec[c] = jnp.sum(gt * sts[c].astype(F32), axis=0, keepdims=True)
                    gt = dec3[c] * gt + _lane_block(zed, c)
                g_b = [g.astype(BF16) for g in gts]
                dqe = _dot(_block_diag(do_c), jnp.concatenate(sts, axis=0), NN)
                dkd = _dot(_block_diag(vb), jnp.concatenate(g_b, axis=0), NN)
                dv = dv + _dot(_block_diag(kd_b), jnp.concatenate(g_b, axis=1), NT)
                ddec3 = jnp.stack(ddec, axis=0)
                dq = dqs * e_qs + dqe * e_q
                dk = dks * e_ks + dkd * e_kd
                t_qs, t_ks, t_kd = dqs * qs, dks * ks, dkd * kd
                db = (t_qs - t_ks + dqe * qe - t_kd).reshape(HG_GROUP, CH, LANES)
                dbm3 = _per_chunk(t_ks - t_qs)
                dbe3 = _per_chunk(t_kd) + ddec3 * dec3
                db = (db + jnp.where(is_mid, dbm3, 0.0) + jnp.where(is_end, dbe3, 0.0)).reshape(GR, LANES)
                dlf = _chunk_cumsum(db, not reverse)
                df = dlf / f - dk
                dhf_ref[0, rows, :] = (df * (1.0 - lb_v) * sig * (1.0 - sig)).astype(BF16)
                dlb = dlb + jnp.sum(df * (1.0 - sig), axis=0, keepdims=True)
                if first:
                    dq_scr[rows, :] = dq
                    dv_scr[rows, :] = dv
                else:
                    dq_scr[rows, :] += dq
                    dv_scr[rows, :] += dv
                return gt, dlb

            return lax.fori_loop(0, n_gr, bwd_step, (jnp.zeros((LANES, LANES), F32), dlb0))[1]

        dlb = run(hff_ref, dhff_ref, False, True, jnp.zeros((1, LANES), F32))
        dlb = run(hfb_ref, dhfb_ref, True, False, dlb)

        @pl.when(pl.program_id(1) == 0)
        def _():
            dlb_ref[0] = dlb

        @pl.when(pl.program_id(1) > 0)
        def _():
            dlb_ref[0] += dlb

        def finish(i, carry):
            rows = pl.ds(pl.multiple_of(i * ts, ts), ts)
            hq = hq_ref[0, rows, :]
            sq = _sigmoid(hq)
            dhq_ref[0, rows, :] = (dq_scr[rows, :] * (sq * (1.0 + hq * (1.0 - sq)))).astype(BF16)
            dhi_ref[0, rows, :] = dv_scr[rows, :].astype(BF16)
            return carry

        lax.fori_loop(0, seq // ts, finish, 0)

    def col(g):
        return pl.BlockSpec((1, seq, LANES), lambda h, b: (b, 0, g + h))

    par = pl.BlockSpec((1, 1, LANES), lambda h, b: (h, 0, 0))
    wide = jax.ShapeDtypeStruct((bsz, seq, B_HEADS * LANES), BF16)
    small = jax.ShapeDtypeStruct((B_HEADS, 1, LANES), F32)
    return pl.pallas_call(
        body, name="hgrn_bwd",
        out_shape=[wide, wide, wide, wide, small],
        grid=(B_HEADS, bsz),
        in_specs=[col(HG_Q), col(HG_FF), col(HG_FB), col(HG_I), col(0), par],
        out_specs=[col(0), col(0), col(0), col(0), par],
        scratch_shapes=[pltpu.VMEM((n_ch, LANES, LANES), BF16),
                        pltpu.VMEM((seq, LANES), F32), pltpu.VMEM((seq, LANES), F32)],
        compiler_params=_cparams(("parallel", "arbitrary")),
    )(proj, proj, proj, proj, do, lb)


def _whole(shape):
    return pl.BlockSpec(shape, lambda: (0,) * len(shape))


def silu_small(x):
    def body(x_ref, o_ref):
        o_ref[...] = _silu(x_ref[...])

    return pl.pallas_call(body, name="silu_small", out_shape=jax.ShapeDtypeStruct(x.shape, F32),
                          in_specs=[_whole(x.shape)], out_specs=_whole(x.shape))(x)


def _softmax_rows(x):
    e = jnp.exp(x - jnp.max(x, axis=0, keepdims=True))
    return e / jnp.sum(e, axis=0, keepdims=True)


def lb_fwd(logits, layer):
    n, w = logits.shape

    def body(x_ref, o_ref):
        p = _softmax_rows(x_ref[...])
        row = lax.broadcasted_iota(jnp.int32, (n, w), 0)
        o_ref[...] = jnp.sum(jnp.where(row <= layer, p, 0.0), axis=0, keepdims=True)

    return pl.pallas_call(body, name="lb_fwd", out_shape=jax.ShapeDtypeStruct((1, w), F32),
                          in_specs=[_whole((n, w))], out_specs=_whole((1, w)))(logits)


def lb_bwd(logits, dlb, layer):
    n, w = logits.shape

    def body(x_ref, d_ref, o_ref):
        p = _softmax_rows(x_ref[...])
        row = lax.broadcasted_iota(jnp.int32, (n, w), 0)
        dp = jnp.where(row <= layer, d_ref[...], 0.0)
        o_ref[...] = p * (dp - jnp.sum(p * dp, axis=0, keepdims=True))

    return pl.pallas_call(body, name="lb_bwd", out_shape=jax.ShapeDtypeStruct((n, w), F32),
                          in_specs=[_whole((n, w)), _whole((1, w))], out_specs=_whole((n, w)))(logits, dlb)


def sum_parts(x, name):
    p, r, c = x.shape
    tr = _rows_tile(r, max(8, (1 << 23) // (4 * c * p) // 8 * 8))

    def body(x_ref, o_ref):
        acc = x_ref[0].astype(F32)
        for j in range(1, p):
            acc = acc + x_ref[j].astype(F32)
        o_ref[...] = acc

    return pl.pallas_call(
        body, name=name, out_shape=jax.ShapeDtypeStruct((r, c), F32),
        grid=(r // tr,),
        in_specs=[pl.BlockSpec((p, tr, c), lambda i: (0, i, 0))],
        out_specs=pl.BlockSpec((tr, c), lambda i: (i, 0)),
        compiler_params=_cparams(("parallel",)),
    )(x)


def adamw(w, g, m, v, name):
    r, c = w.shape
    tr = _rows_tile(r, max(8, (1 << 20) // (4 * c) // 8 * 8))
    c1 = 1.0 - ADAM_B1 ** ADAM_STEP
    c2 = 1.0 - ADAM_B2 ** ADAM_STEP

    def body(w_ref, g_ref, m_ref, v_ref, d_ref, nm_ref, nv_ref):
        g = g_ref[...]
        nm = ADAM_B1 * m_ref[...] + (1.0 - ADAM_B1) * g
        nv = ADAM_B2 * v_ref[...] + (1.0 - ADAM_B2) * (g * g)
        nm_ref[...] = nm
        nv_ref[...] = nv
        d_ref[...] = -ADAM_LR * ((nm / c1) / (jnp.sqrt(nv / c2) + ADAM_EPS) + ADAM_WD * w_ref[...])

    spec = pl.BlockSpec((tr, c), lambda i: (i, 0))
    sds = jax.ShapeDtypeStruct((r, c), F32)
    return pl.pallas_call(
        body, name=name, out_shape=[sds, sds, sds], grid=(r // tr,),
        in_specs=[spec, spec, spec, spec], out_specs=[spec, spec, spec],
        compiler_params=_cparams(("parallel",)),
    )(w, g, m, v)


ANY = pl.BlockSpec(memory_space=pl.ANY)


def _my_place():
    return lax.axis_index("x"), lax.axis_index("y"), lax.axis_index("c")


def all_gather(x, name):
    def body(x_ref, out_ref, send_sems, recv_sems, local_sem):
        mx, my, mc = _my_place()
        me, sibling = (mx, my, mc), (mx, my, 1 - mc)
        chips = [(1 - mx, my), (mx, 1 - my), (1 - mx, 1 - my)]

        def slot(px, py, pc):
            return out_ref.at[4 * px + 2 * py + pc]

        def copy(k, block, to, src=None):
            return pltpu.make_async_remote_copy(
                src_ref=slot(*block) if src is None else src, dst_ref=slot(*block),
                send_sem=send_sems.at[k], recv_sem=recv_sems.at[k],
                device_id=to, device_id_type=MESH_ID)

        mine = pltpu.make_async_copy(x_ref, slot(*me), local_sem)
        mine.start()
        first = [copy(0, me, sibling, src=x_ref)]
        first += [copy(1 + j, me, (*chip, mc), src=x_ref) for j, chip in enumerate(chips)]
        for cp in first:
            cp.start()
        passed = [copy(4 + j, (*chip, mc), sibling) for j, chip in enumerate(chips)]
        for j, chip in enumerate(chips):
            copy(1 + j, (*chip, mc), me).wait_recv()
            passed[j].start()
        copy(0, sibling, me).wait_recv()
        for j, chip in enumerate(chips):
            copy(4 + j, (*chip, 1 - mc), me).wait_recv()
        for cp in first + passed:
            cp.wait_send()
        mine.wait()

    return pl.pallas_call(
        body, name=name,
        out_shape=jax.ShapeDtypeStruct((N_DEV,) + x.shape, x.dtype),
        in_specs=[ANY], out_specs=ANY,
        scratch_shapes=[pltpu.SemaphoreType.DMA((7,)), pltpu.SemaphoreType.DMA((7,)),
                        pltpu.SemaphoreType.DMA(())],
    )(x)


def swap_with_sibling(x, name):
    def body(x_ref, out_ref, send_sem, recv_sem):
        mx, my, mc = _my_place()
        cp = pltpu.make_async_remote_copy(
            src_ref=x_ref.at[1 - mc], dst_ref=out_ref, send_sem=send_sem, recv_sem=recv_sem,
            device_id=(mx, my, 1 - mc), device_id_type=MESH_ID)
        cp.start()
        cp.wait()

    return pl.pallas_call(
        body, name=name,
        out_shape=jax.ShapeDtypeStruct(x.shape[1:], x.dtype),
        in_specs=[ANY], out_specs=ANY,
        scratch_shapes=[pltpu.SemaphoreType.DMA(()), pltpu.SemaphoreType.DMA(())],
    )(x)


def pair_add(x, r, mc):
    _, n, rows, c = x.shape
    tr = _rows_tile(rows, 512)

    def body(mc_ref, x_ref, r_ref, o_ref):
        del mc_ref
        o_ref[0] = (x_ref[0, 0].astype(F32) + r_ref[0].astype(F32)).astype(BF16)

    return pl.pallas_call(
        body, name="pair_add",
        out_shape=jax.ShapeDtypeStruct((n, rows, c), BF16),
        grid_spec=pltpu.PrefetchScalarGridSpec(
            num_scalar_prefetch=1, grid=(n, rows // tr),
            in_specs=[pl.BlockSpec((1, 1, tr, c), lambda j, i, s: (s[0], j, i, 0)),
                      pl.BlockSpec((1, tr, c), lambda j, i, s: (j, i, 0))],
            out_specs=pl.BlockSpec((1, tr, c), lambda j, i, s: (j, i, 0))),
        compiler_params=_cparams(("parallel", "parallel")),
    )(mc.reshape(1).astype(jnp.int32), x, r)


def chip_all_to_all(x, name):
    def body(x_ref, out_ref, send_sems, recv_sems, local_sem):
        mx, my, mc = _my_place()
        me = 2 * mx + my
        mine = pltpu.make_async_copy(x_ref.at[me], out_ref.at[me], local_sem)
        mine.start()
        copies = []
        for k, (fx, fy) in enumerate(((1, 0), (0, 1), (1, 1))):
            px = 1 - mx if fx else mx
            py = 1 - my if fy else my
            peer = 2 * px + py
            cp = pltpu.make_async_remote_copy(
                src_ref=x_ref.at[peer], dst_ref=out_ref.at[me],
                send_sem=send_sems.at[k], recv_sem=recv_sems.at[k],
                device_id=(px, py, mc), device_id_type=MESH_ID)
            cp.start()
            copies.append((cp, peer, (px, py, mc), k))
        for cp, peer, to, k in copies:
            pltpu.make_async_remote_copy(
                src_ref=x_ref.at[peer], dst_ref=out_ref.at[peer],
                send_sem=send_sems.at[k], recv_sem=recv_sems.at[k],
                device_id=to, device_id_type=MESH_ID).wait_recv()
        for cp, _, _, _ in copies:
            cp.wait_send()
        mine.wait()

    return pl.pallas_call(
        body, name=name,
        out_shape=jax.ShapeDtypeStruct(x.shape, x.dtype),
        in_specs=[ANY], out_specs=ANY,
        scratch_shapes=[pltpu.SemaphoreType.DMA((3,)), pltpu.SemaphoreType.DMA((3,)),
                        pltpu.SemaphoreType.DMA(())],
    )(x)


ROW_W = D_MODEL
FF_SH = D_FF // N_DEV
N_FFN = 2 * DEPTH
SEC = {}
_off = 0
for _name, _rows in (("gate", N_FFN * FF_SH), ("up", N_FFN * FF_SH), ("down", N_FFN * FF_SH),
                     ("hin", 416), ("hout", 128), ("mout", 128), ("mdown", 80), ("uq", 48), ("ukv", 64)):
    SEC[_name] = (_off, _rows)
    _off += _rows
PACK_ROWS = _off
MDOWN_ROWS = 128 * 544 // ROW_W
A_Q_W = A_HEADS * A_HEAD_DIM
_KVHEAD = (np.arange(A_HEADS) // A_GROUP).reshape(A_HEADS, 1, 1)


def pack_shards(gate, up, down, hin, hout, mdown, uq, ukv, mout, dtype):
    def rows(a):
        return a.astype(dtype).reshape(-1, ROW_W)
    parts = [rows(jnp.swapaxes(gate, -1, -2)), rows(jnp.swapaxes(up, -1, -2)), rows(down),
             rows(jnp.swapaxes(hin, -1, -2)), rows(hout), rows(mout),
             jnp.pad(rows(mdown), ((0, SEC["mdown"][1] - MDOWN_ROWS), (0, 0))),
             rows(jnp.swapaxes(uq, -1, -2)), rows(jnp.swapaxes(ukv, -1, -2))]
    return jnp.concatenate(parts, axis=0)


def unpack_shards(p):
    def sec(name):
        o, n = SEC[name]
        return p[o:o + n]

    def col(name, r, c):
        return jnp.swapaxes(sec(name).reshape(-1, c, r), -1, -2)

    return dict(
        ffn_w_gate=col("gate", 1024, FF_SH).reshape(DEPTH, 2, 1024, FF_SH),
        ffn_w_up=col("up", 1024, FF_SH).reshape(DEPTH, 2, 1024, FF_SH),
        ffn_w_down=sec("down").reshape(DEPTH, 2, FF_SH, 1024),
        hyb_w_in=col("hin", 1024, 416),
        hyb_w_out=sec("hout").reshape(1, 128, 1024),
        mla_w_out=sec("mout").reshape(1, 128, 1024),
        mla_w_down=sec("mdown")[:MDOWN_ROWS].reshape(1, 128, 544),
        mla_w_uq=col("uq", 256, 192),
        mla_w_ukv=col("ukv", 256, 256),
    )


def _pad_qheads(w):
    a = w.reshape(A_HEADS, 1, A_HEAD_DIM, -1)
    kvh = _KVHEAD[..., None]
    both = jnp.concatenate([jnp.where(kvh == 0, a, 0), jnp.where(kvh == 1, a, 0)], axis=1)
    return both.reshape(A_HEADS * LANES, -1)


def _unpad_qheads(w):
    a = w.reshape(A_HEADS, 2, A_HEAD_DIM, -1)
    return jnp.where(_KVHEAD == 0, a[:, 0], a[:, 1]).reshape(A_Q_W, -1)


def unpack_full(g):
    def sec(name):
        o, n = SEC[name]
        return g[:, o:o + n]

    def ffn(name):
        return jnp.swapaxes(sec(name).reshape(N_DEV, N_FFN, FF_SH, ROW_W), 0, 1).reshape(N_FFN, D_FF, ROW_W)

    def z(*s):
        return jnp.zeros(s, g.dtype)

    def halves(a):
        return a.reshape(N_FFN, 2, 1, FF_HALF, ROW_W)

    gate_t, up_t, down = ffn("gate"), ffn("up"), ffn("down")
    hin_t = sec("hin").reshape(-1, ROW_W)
    hout = sec("hout").reshape(-1, ROW_W)
    mout = sec("mout").reshape(C_HEADS, C_V, ROW_W)
    mdown = sec("mdown")[:, :MDOWN_ROWS].reshape(D_MODEL, 544)
    uq_t = sec("uq").reshape(C_HEADS, C_NOPE + C_ROPE, C_Q_RANK)
    w = dict(
        wgu_t=jnp.concatenate([halves(gate_t), halves(up_t)], axis=2).reshape(N_FFN, 2 * D_FF, ROW_W),
        wd=down,
        hin_t=jnp.concatenate([_pad_qheads(hin_t[:A_Q_W]), hin_t[A_Q_W:]], axis=0),
        hout=jnp.concatenate([_pad_qheads(hout[:A_Q_W]), hout[A_Q_W:]], axis=0),
        mout=jnp.concatenate([z(C_HEADS, C_NOPE, ROW_W), mout], axis=1).reshape(MLA_HEAD_PAD, ROW_W),
        mdown=jnp.concatenate([mdown[:, :512], z(D_MODEL, 64), mdown[:, 512:], z(D_MODEL, 32)], axis=1),
        uq_t=jnp.concatenate([uq_t, z(C_HEADS, 32, C_Q_RANK)], axis=1).reshape(MLA_HEAD_PAD, C_Q_RANK),
        ukv_t=sec("ukv").reshape(MLA_HEAD_PAD, C_KV_RANK),
    )
    w.update(wgu=jnp.swapaxes(w["wgu_t"], 1, 2), wd_t=jnp.swapaxes(down, 1, 2), hin=w["hin_t"].T,
             hout_t=w["hout"].T, mout_t=w["mout"].T, mdown_t=w["mdown"].T, uq=w["uq_t"].T, ukv=w["ukv_t"].T)
    return w


def pack_full(d, dtype):
    d = {k: v.astype(dtype) for k, v in d.items()}
    gu = d["wgu_t"].reshape(N_FFN, 2, 2, FF_HALF, ROW_W)
    d["wgu_t"] = jnp.concatenate([gu[:, :, 0].reshape(N_FFN, D_FF, ROW_W),
                                  gu[:, :, 1].reshape(N_FFN, D_FF, ROW_W)], axis=1)

    def split(a):
        return a.reshape(N_DEV, -1, ROW_W)

    def ffn(a):
        return jnp.swapaxes(a.reshape(N_FFN, N_DEV, FF_SH, ROW_W), 0, 1).reshape(N_DEV, N_FFN * FF_SH, ROW_W)

    hin = jnp.concatenate([_unpad_qheads(d["hin_t"][:A_HEADS * LANES]), d["hin_t"][A_HEADS * LANES:]], axis=0)
    hout = jnp.concatenate([_unpad_qheads(d["hout"][:A_HEADS * LANES]), d["hout"][A_HEADS * LANES:]], axis=0)
    mout = d["mout"].reshape(C_HEADS, LANES, ROW_W)[:, C_NOPE:].reshape(-1, ROW_W)
    mdown = jnp.concatenate([d["mdown"][:, :512], d["mdown"][:, 576:608]], axis=1)
    uq = d["uq_t"].reshape(C_HEADS, LANES, C_Q_RANK)[:, :C_NOPE + C_ROPE]
    parts = [ffn(d["wgu_t"][:, :D_FF]), ffn(d["wgu_t"][:, D_FF:]), ffn(d["wd"]),
             split(hin), split(hout), split(mout),
             jnp.pad(split(mdown), ((0, 0), (0, SEC["mdown"][1] - MDOWN_ROWS), (0, 0))),
             split(uq), split(d["ukv_t"])]
    return jnp.concatenate(parts, axis=1)


def rope_tables(positions):
    half = C_ROPE // 2
    freqs = ROPE_THETA ** (-jnp.arange(half, dtype=F32) / half)
    ang = positions.astype(F32)[..., None] * freqs
    cos, sin = jnp.cos(ang), jnp.sin(ang)
    shape = positions.shape
    cq = jnp.concatenate([jnp.ones(shape + (C_NOPE,), F32), cos, cos, jnp.zeros(shape + (32,), F32)], axis=-1)
    sg = jnp.concatenate([jnp.zeros(shape + (C_NOPE,), F32), -sin, sin, jnp.zeros(shape + (32,), F32)], axis=-1)
    return cq, sg


def ffn_fwd(h2, w, i):
    gu, a = ffn_up_fused(h2, w["wgu"][i])
    return matmul(a, w["wd"][i], "nn", F32, "ffn_down"), (gu, a)


def ffn_bwd(dy2, h2, w, i, saved):
    gu, a = saved
    dgu = ffn_down_bwd_fused(dy2, w["wd_t"][i], gu)
    dwd = matmul(a, dy2, "tn", F32, "ffn_down_dw", tm_cap=FF_HALF)
    dwgu_t = matmul(dgu, h2, "tn", F32, "ffn_up_dw", tm_cap=FF_HALF)
    dh = matmul(dgu, w["wgu_t"][i], "nn", F32, "ffn_up_dx")
    return dh, dwgu_t, dwd


def hybrid_fwd(h2, shape, w, aux):
    bsz, seq = shape
    proj = matmul(h2, w["hin"], "nn", F32, "hyb_in").reshape(bsz, seq, HYB_PAD)
    pad = ((0, 0), (A_BLOCK, A_BLOCK), (0, 0))
    kp = jnp.pad(proj[:, :, 8 * LANES:9 * LANES].astype(BF16), pad)
    vp = jnp.pad(proj[:, :, 9 * LANES:10 * LANES].astype(BF16), pad)
    mo = wattn_fwd(proj, kp, vp, aux["posc"], aux["posr"], aux["sink"])
    mo, otot = hgrn_fwd(proj, mo, aux["lb"], aux["nw"])
    y = matmul(mo.reshape(bsz * seq, MO_PAD), w["hout"], "nn", F32, "hyb_out")
    return y, (proj, kp, vp, mo, otot)


def hybrid_bwd(dy2, h2, shape, w, aux, saved):
    bsz, seq = shape
    proj, kp, vp, mo, otot = saved
    mo2 = mo.reshape(bsz * seq, MO_PAD)
    dmo = matmul(dy2, w["hout_t"], "nn", F32, "hyb_out_dx").reshape(bsz, seq, MO_PAD)
    dhout = matmul(mo2, dy2, "tn", F32, "hyb_out_dw")
    dq, dkp, dvp, dsink = wattn_bwd(proj, kp, vp, aux["posc"], aux["posr"], aux["sink"], dmo)
    do, dhg, dnw = hgrn_out_bwd(proj, otot, dmo, aux["nw"])
    dhq, dhff, dhfb, dhi, dlb = hgrn_bwd(proj, do, aux["lb"])
    dproj = jnp.concatenate([dq, dkp[:, A_BLOCK:-A_BLOCK].astype(BF16), dvp[:, A_BLOCK:-A_BLOCK].astype(BF16),
                             dhq, dhff, dhfb, dhi, dhg], axis=-1).reshape(bsz * seq, HYB_PAD)
    dhin_t = matmul(dproj, h2, "tn", F32, "hyb_in_dw")
    dh = matmul(dproj, w["hin_t"], "nn", F32, "hyb_in_dx")
    return dh, dict(hin_t=dhin_t, hout=dhout), dict(sink=dsink, lb=dlb, nw=dnw)


def mla_fwd(h2, shape, w, aux):
    bsz, seq = shape
    t = bsz * seq
    cqkv = matmul(h2, w["mdown"], "nn", F32, "mla_down").reshape(bsz, seq, MLA_DOWN_PAD)
    nq, nkv, kr = mla_mid_fwd(cqkv, aux["qn"], aux["kvn"], aux["cq"], aux["sg"])
    q = matmul(nq.reshape(t, C_Q_RANK), w["uq"], "nn", F32, "mla_uq").reshape(bsz, seq, MLA_HEAD_PAD)
    qp = q_rope(q, aux["cq"], aux["sg"], False, "q_rope_fwd", out_scale=MLA_QSCALE)
    kv = matmul(nkv.reshape(t, C_KV_RANK), w["ukv"], "nn", BF16, "mla_ukv").reshape(bsz, seq, MLA_HEAD_PAD)
    o, lse = mla_attn_fwd(qp, kv, kr)
    y = matmul(o.reshape(t, MLA_HEAD_PAD), w["mout"], "nn", F32, "mla_out")
    return y, (cqkv, nq, nkv, kr, qp, kv, o, lse)


def mla_bwd(dy2, h2, shape, w, aux, saved):
    bsz, seq = shape
    t = bsz * seq
    cqkv, nq, nkv, kr, qp, kv, o, lse = saved
    do = matmul(dy2, w["mout_t"], "nn", BF16, "mla_out_dx").reshape(bsz, seq, MLA_HEAD_PAD)
    dmout = matmul(o.reshape(t, MLA_HEAD_PAD), dy2, "tn", F32, "mla_out_dw")
    dqp, dkv, dkr = mla_attn_bwd(qp, kv, kr, o, do, lse)
    dq = q_rope(dqp, aux["cq"], aux["sg"], True, "q_rope_bwd").reshape(t, MLA_HEAD_PAD)
    dkv2 = dkv.reshape(t, MLA_HEAD_PAD)
    dnq = matmul(dq, w["uq_t"], "nn", F32, "mla_uq_dx").reshape(bsz, seq, C_Q_RANK)
    duq_t = matmul(dq, nq.reshape(t, C_Q_RANK), "tn", F32, "mla_uq_dw")
    dnkv = matmul(dkv2, w["ukv_t"], "nn", F32, "mla_ukv_dx").reshape(bsz, seq, C_KV_RANK)
    dukv_t = matmul(dkv2, nkv.reshape(t, C_KV_RANK), "tn", F32, "mla_ukv_dw")
    dcqkv, dqn, dkvn = mla_mid_bwd(dnq, dnkv, dkr, cqkv, aux["qn"], aux["kvn"], aux["cq"], aux["sg"])
    dcqkv2 = dcqkv.reshape(t, MLA_DOWN_PAD)
    dmdown = matmul(h2, dcqkv2, "tn", F32, "mla_down_dw")
    dh = matmul(dcqkv2, w["mdown_t"], "nn", F32, "mla_down_dx")
    return dh, dict(mdown=dmdown, uq_t=duq_t, ukv_t=dukv_t, mout=dmout), dict(qn=dqn, kvn=dkvn)


W_NAMES = ['ada_w', 'ada_b', 'ln_g', 'ln_b', 'ffn_w_gate', 'ffn_w_up', 'ffn_w_down', 'hyb_w_in', 'hyb_w_out',
           'attn_sink', 'hgrn_lb_logits', 'hgrn_norm_w', 'mla_w_down', 'mla_q_norm', 'mla_kv_norm', 'mla_w_uq',
           'mla_w_ukv', 'mla_w_out']
SMALL_NAMES = ['ada_b', 'ln_g', 'ln_b', 'attn_sink', 'hgrn_lb_logits', 'hgrn_norm_w', 'mla_q_norm', 'mla_kv_norm']
MOD_W = N_SUB * 3 * D_MODEL
MOD_SH = MOD_W // N_DEV
RES_W = (0.5, 1.0, 0.5)


def _rows1024(a, rows=None):
    flat = a.astype(F32).reshape(-1)
    n = flat.shape[0]
    total = (-(-n // ROW_W) if rows is None else rows) * ROW_W
    return jnp.pad(flat, (0, total - n)).reshape(-1, ROW_W)


def kernel(x, c, positions, ada_w, ada_b, ln_g, ln_b, ffn_w_gate, ffn_w_up, ffn_w_down, hyb_w_in, hyb_w_out, attn_sink, hgrn_lb_logits, hgrn_norm_w, mla_w_down, mla_q_norm, mla_kv_norm, mla_w_uq, mla_w_ukv, mla_w_out, loss_target, m_ada_w, m_ada_b, m_ln_g, m_ln_b, m_ffn_w_gate, m_ffn_w_up, m_ffn_w_down, m_hyb_w_in, m_hyb_w_out, m_attn_sink, m_hgrn_lb_logits, m_hgrn_norm_w, m_mla_w_down, m_mla_q_norm, m_mla_kv_norm, m_mla_w_uq, m_mla_w_ukv, m_mla_w_out, v_ada_w, v_ada_b, v_ln_g, v_ln_b, v_ffn_w_gate, v_ffn_w_up, v_ffn_w_down, v_hyb_w_in, v_hyb_w_out, v_attn_sink, v_hgrn_lb_logits, v_hgrn_norm_w, v_mla_w_down, v_mla_q_norm, v_mla_kv_norm, v_mla_w_uq, v_mla_w_ukv, v_mla_w_out):
    weights = dict(zip(W_NAMES, (ada_w, ada_b, ln_g, ln_b, ffn_w_gate, ffn_w_up, ffn_w_down, hyb_w_in, hyb_w_out,
                                 attn_sink, hgrn_lb_logits, hgrn_norm_w, mla_w_down, mla_q_norm, mla_kv_norm,
                                 mla_w_uq, mla_w_ukv, mla_w_out)))
    mom1 = dict(zip(W_NAMES, (m_ada_w, m_ada_b, m_ln_g, m_ln_b, m_ffn_w_gate, m_ffn_w_up, m_ffn_w_down, m_hyb_w_in,
                              m_hyb_w_out, m_attn_sink, m_hgrn_lb_logits, m_hgrn_norm_w, m_mla_w_down, m_mla_q_norm,
                              m_mla_kv_norm, m_mla_w_uq, m_mla_w_ukv, m_mla_w_out)))
    mom2 = dict(zip(W_NAMES, (v_ada_w, v_ada_b, v_ln_g, v_ln_b, v_ffn_w_gate, v_ffn_w_up, v_ffn_w_down, v_hyb_w_in,
                              v_hyb_w_out, v_attn_sink, v_hgrn_lb_logits, v_hgrn_norm_w, v_mla_w_down, v_mla_q_norm,
                              v_mla_kv_norm, v_mla_w_uq, v_mla_w_ukv, v_mla_w_out)))
    bsz, seq, d = x.shape
    t = bsz * seq
    nb_tot = N_DEV * bsz
    me = 4 * lax.axis_index("x") + 2 * lax.axis_index("y") + lax.axis_index("c")

    c_all = all_gather(c, "gather_c").reshape(nb_tot, d)
    cond = silu_small(c_all)
    ada_b_mine = lax.dynamic_slice_in_dim(ada_b, me * MOD_SH, MOD_SH, axis=1)
    modp = jnp.concatenate([matmul(cond, ada_w[l], "nn", F32, "ada_fwd", bias=ada_b_mine[l])
                            for l in range(DEPTH)], axis=1)
    mod_rows = nb_tot * DEPTH * MOD_SH // ROW_W
    small1 = jnp.concatenate([_rows1024(modp), _rows1024(jnp.concatenate(
        [ln_g.reshape(-1), ln_b.reshape(-1), mla_q_norm.reshape(-1), mla_kv_norm.reshape(-1)]), rows=4)], axis=0)
    g1 = all_gather(small1, "gather_mod")
    mod_all = g1[:, :mod_rows].reshape(N_DEV, nb_tot, DEPTH, MOD_SH)
    mod_all = jnp.transpose(mod_all, (1, 2, 0, 3)).reshape(nb_tot, DEPTH, N_SUB, 3, d)
    mod = lax.dynamic_slice_in_dim(mod_all, me * bsz, bsz, axis=0)
    tail = g1[:, mod_rows:].reshape(N_DEV, -1)
    n_ln = DEPTH * N_SUB * LANES
    ln_g_full = jnp.transpose(tail[:, :n_ln].reshape(N_DEV, DEPTH, N_SUB, LANES), (1, 2, 0, 3)).reshape(DEPTH, N_SUB, d)
    ln_b_full = jnp.transpose(tail[:, n_ln:2 * n_ln].reshape(N_DEV, DEPTH, N_SUB, LANES), (1, 2, 0, 3)).reshape(DEPTH, N_SUB, d)
    qn_full = tail[:, 2 * n_ln:2 * n_ln + 32].reshape(C_Q_RANK)
    kvn_full = tail[:, 2 * n_ln + 32:2 * n_ln + 64].reshape(C_KV_RANK)

    def mvec(l, s, j):
        return mod[:, l, s, j].reshape(bsz, 1, d)

    packed = pack_shards(ffn_w_gate, ffn_w_up, ffn_w_down, hyb_w_in, hyb_w_out, mla_w_down, mla_w_uq, mla_w_ukv,
                         mla_w_out, BF16)
    w = unpack_full(all_gather(packed, "gather_weights"))

    cq_tab, sg_tab = rope_tables(positions)
    lb0 = lb_fwd(hgrn_lb_logits, 0)
    aux = [dict(posc=positions.reshape(bsz, seq, 1),
                posr=jnp.pad(positions, ((0, 0), (A_BLOCK, A_BLOCK))).reshape(bsz, 1, seq + 2 * A_BLOCK),
                sink=attn_sink[0], lb=lb0.reshape(B_HEADS, 1, LANES), nw=hgrn_norm_w[0].reshape(B_HEADS, 1, LANES)),
           dict(qn=qn_full, kvn=kvn_full, cq=cq_tab, sg=sg_tab)]
    mixers = [(hybrid_fwd, hybrid_bwd), (mla_fwd, mla_bwd)]

    tape = []
    xin = x
    h = mod_fwd(x, mvec(0, 0, 1), mvec(0, 0, 0))
    for l in range(DEPTH):
        for s in range(N_SUB):
            h2 = h.reshape(t, d)
            if s == 1:
                y, saved = mixers[l][0](h2, (bsz, seq), w, aux[l])
            else:
                i = 2 * l + s // 2
                y, saved = ffn_fwd(h2, w, i)
            y = y.reshape(bsz, seq, d)
            last = l == DEPTH - 1 and s == N_SUB - 1
            ln, sn = (l, s + 1) if s + 1 < N_SUB else (l + 1, 0)
            nxt = None if last else (mvec(ln, sn, 1), mvec(ln, sn, 0))
            xn, hn = resln_fwd(xin, y, mvec(l, s, 2), ln_g_full[l, s], ln_b_full[l, s], RES_W[s], nxt)
            tape.append((l, s, xin, h2, y, saved))
            xin, h = xn, hn
    loss_part, gout = loss_fwd(xin, loss_target)
    loss = lax.psum(loss_part, ("x", "y", "c"))

    dmod = [[[None] * 3 for _ in range(N_SUB)] for _ in range(DEPTH)]
    dln_g = [[None] * N_SUB for _ in range(DEPTH)]
    dln_b = [[None] * N_SUB for _ in range(DEPTH)]
    big = dict(wgu_t=[None] * N_FFN, wd=[None] * N_FFN)
    small = {}
    for l, s, xs, h2, y, saved in reversed(tape):
        dxa, dy, dgate, dg, db = resln_bwd(gout, xs, y, mvec(l, s, 2), ln_g_full[l, s], RES_W[s])
        dy2 = dy.reshape(t, d)
        if s == 1:
            dh, dbig, dsmall = mixers[l][1](dy2, h2, (bsz, seq), w, aux[l], saved)
            big.update(dbig)
            small.update(dsmall)
        else:
            i = 2 * l + s // 2
            dh, big["wgu_t"][i], big["wd"][i] = ffn_bwd(dy2, h2, w, i, saved)
        gout, dscale, dshift = mod_bwd(dxa, dh.reshape(bsz, seq, d), xs, mvec(l, s, 1))
        dmod[l][s] = [dshift, dscale, dgate]
        dln_g[l][s], dln_b[l][s] = dg, db
    grad_x = gout
    big["wgu_t"] = jnp.stack(big["wgu_t"])
    big["wd"] = jnp.stack(big["wd"])

    dmod_mine = jnp.stack([jnp.stack([jnp.concatenate(dmod[l][s], axis=1) for s in range(N_SUB)], axis=1)
                           for l in range(DEPTH)], axis=1)
    dmod_rows = bsz * DEPTH * MOD_W // ROW_W
    misc = jnp.concatenate([small["lb"].reshape(-1), small["nw"].reshape(-1), small["qn"].reshape(-1),
                            small["kvn"].reshape(-1), small["sink"].reshape(-1)[:A_HEADS]])
    small2 = jnp.concatenate([_rows1024(dmod_mine),
                              _rows1024(jnp.stack([jnp.stack(r) for r in dln_g])),
                              _rows1024(jnp.stack([jnp.stack(r) for r in dln_b])),
                              _rows1024(misc, rows=2)], axis=0)
    g2 = all_gather(small2, "gather_small_grads")
    dmod_all = g2[:, :dmod_rows].reshape(nb_tot, DEPTH * MOD_W // ROW_W, ROW_W)
    grad_ada_b = sum_parts(dmod_all, "sum_ada_b").reshape(DEPTH, MOD_W)
    dmod_cols = lax.dynamic_slice_in_dim(dmod_all.reshape(nb_tot, DEPTH, MOD_W), me * MOD_SH, MOD_SH, axis=2)
    grad_ada_w = jnp.stack([matmul(cond, dmod_cols[:, l], "tn", F32, "ada_dw") for l in range(DEPTH)])
    rest = sum_parts(g2[:, dmod_rows:], "sum_small")
    n6 = DEPTH * N_SUB
    gl_g = lax.dynamic_slice_in_dim(rest[:n6].reshape(DEPTH, N_SUB, d), me * LANES, LANES, axis=2)
    gl_b = lax.dynamic_slice_in_dim(rest[n6:2 * n6].reshape(DEPTH, N_SUB, d), me * LANES, LANES, axis=2)
    mrow = rest[2 * n6:].reshape(-1)
    dlb0 = mrow[:512].reshape(1, 512)
    g_nw = mrow[512:1024].reshape(1, B_HEADS, LANES)
    g_qn = lax.dynamic_slice_in_dim(mrow[1024:1280], me * 32, 32).reshape(1, 32)
    g_kvn = lax.dynamic_slice_in_dim(mrow[1280:1536], me * 32, 32).reshape(1, 32)
    g_sink = mrow[1536:1536 + A_HEADS].reshape(1, A_HEADS)
    g_lb = lb_bwd(hgrn_lb_logits, dlb0, 0)

    mc = lax.axis_index("c")
    outgoing = jnp.swapaxes(pack_full(big, BF16).reshape(4, 2, PACK_ROWS, ROW_W), 0, 1)
    pair = pair_add(outgoing, swap_with_sibling(outgoing, "exchange_d2d"), mc)
    reduced = sum_parts(chip_all_to_all(pair, "exchange_ici"), "sum_grads")
    grads = unpack_shards(reduced)
    grads.update(ada_w=grad_ada_w, ada_b=grad_ada_b, ln_g=gl_g, ln_b=gl_b, attn_sink=g_sink,
                 hgrn_lb_logits=g_lb, hgrn_norm_w=g_nw, mla_q_norm=g_qn, mla_kv_norm=g_kvn)

    delta, new_m, new_v = {}, {}, {}
    for name in W_NAMES:
        if name in SMALL_NAMES:
            continue
        shp = weights[name].shape
        two_d = (-1, shp[-1])
        out = adamw(weights[name].reshape(two_d), grads[name].reshape(two_d), mom1[name].reshape(two_d),
                    mom2[name].reshape(two_d), "adamw_" + name)
        delta[name], new_m[name], new_v[name] = [o.reshape(shp) for o in out]

    def pack_small(src):
        flat = jnp.concatenate([src[n].reshape(-1) for n in SMALL_NAMES])
        return jnp.pad(flat, (0, -flat.shape[0] % (8 * LANES))).reshape(-1, LANES)

    outs = adamw(pack_small(weights), pack_small(grads), pack_small(mom1), pack_small(mom2), "adamw_small")
    off = 0
    for name in SMALL_NAMES:
        n = weights[name].size
        for dst, o in zip((delta, new_m, new_v), outs):
            dst[name] = o.reshape(-1)[off:off + n].reshape(weights[name].shape)
        off += n

    return (loss, grad_x, *[grads[n] for n in W_NAMES], *[delta[n] for n in W_NAMES],
            *[new_m[n] for n in W_NAMES], *[new_v[n] for n in W_NAMES])
```

```python
import functools
import math

import jax
import jax.numpy as jnp
import numpy as np
from jax import lax
from jax.experimental import pallas as pl
from jax.experimental.pallas import tpu as pltpu

F32 = jnp.float32
BF16 = jnp.bfloat16

D_MODEL = 1024
DEPTH = 2
D_FF = 2816
A_HEADS = 8
A_KV_HEADS = 2
A_HEAD_DIM = 64
WINDOW = 128
A_BLOCK = 128
B_HEADS = 4
B_KEY_DIM = 128
B_CHUNK = 64
C_HEADS = 16
C_Q_RANK = 256
C_KV_RANK = 256
C_NOPE = 64
C_ROPE = 32
C_V = 64
ROPE_THETA = 10000.0
LN_EPS = 1e-5
RMS_EPS = 1e-6
DEEPNORM_ALPHA = (2 * DEPTH) ** 0.25
N_SUB = 3
N_DEV = 8
LANES = 128
HYB_PAD = 3840
MO_PAD = 1536
MLA_DOWN_PAD = 640
MLA_HEAD_PAD = C_HEADS * LANES
ADAM_LR = 0.001
ADAM_B1 = 0.9
ADAM_B2 = 0.999
ADAM_EPS = 1e-08
ADAM_WD = 0.01
ADAM_STEP = 10
VMEM_LIMIT = 48 * 1024 * 1024
MESH_ID = pl.DeviceIdType.MESH
HIGHEST = lax.Precision.HIGHEST


def _cparams(sem=None):
    return pltpu.CompilerParams(dimension_semantics=sem, vmem_limit_bytes=VMEM_LIMIT)


def _tile(n, cap):
    if n <= cap:
        return n
    t = (cap // LANES) * LANES
    while t >= LANES:
        if n % t == 0:
            return t
        t -= LANES
    return n


def _rows_tile(n, cap):
    if n <= cap:
        return n
    t = cap
    while t >= 8:
        if n % t == 0:
            return t
        t -= 8
    return n


def _sigmoid(x):
    return 1.0 / (1.0 + jnp.exp(-x))


def _dot(a, b, dims, precision=None):
    return lax.dot_general(a, b, (dims, ((), ())), precision=precision,
                           preferred_element_type=F32)


NN = ((1,), (0,))
NT = ((1,), (1,))
TN = ((0,), (0,))


def matmul(a, b, form, out_dtype, name, bias=None, tm_cap=512, tn_cap=1024, tk_cap=1024):
    if form == "nn":
        (m, k), (k2, n) = a.shape, b.shape
    elif form == "nt":
        (m, k), (n, k2) = a.shape, b.shape
    else:
        (k, m), (k2, n) = a.shape, b.shape
    assert k == k2, (a.shape, b.shape, form)
    tm = _tile(m, tm_cap)
    tn = _tile(n, tn_cap)
    tk = k if k <= 2 * D_FF else _tile(k, tk_cap)
    nk = k // tk
    dims = {"nn": NN, "nt": NT, "tn": TN}[form]

    def body(*refs):
        if bias is None:
            a_ref, b_ref, o_ref, acc_ref = refs
            bias_ref = None
        else:
            a_ref, b_ref, bias_ref, o_ref, acc_ref = refs
        kk = pl.program_id(2)
        part = _dot(a_ref[...].astype(BF16), b_ref[...].astype(BF16), dims)

        def finish(total):
            if bias_ref is not None:
                total = total + bias_ref[...]
            o_ref[...] = total.astype(o_ref.dtype)

        if nk == 1:
            finish(part)
        else:
            @pl.when(kk == 0)
            def _():
                acc_ref[...] = part

            @pl.when(jnp.logical_and(kk > 0, kk < nk - 1))
            def _():
                acc_ref[...] += part

            @pl.when(kk == nk - 1)
            def _():
                finish(acc_ref[...] + part)

    if form == "nn":
        a_spec = pl.BlockSpec((tm, tk), lambda i, j, kk: (i, kk))
        b_spec = pl.BlockSpec((tk, tn), lambda i, j, kk: (kk, j))
    elif form == "nt":
        a_spec = pl.BlockSpec((tm, tk), lambda i, j, kk: (i, kk))
        b_spec = pl.BlockSpec((tn, tk), lambda i, j, kk: (j, kk))
    else:
        a_spec = pl.BlockSpec((tk, tm), lambda i, j, kk: (kk, i))
        b_spec = pl.BlockSpec((tk, tn), lambda i, j, kk: (kk, j))
    in_specs = [a_spec, b_spec]
    args = [a, b]
    if bias is not None:
        in_specs.append(pl.BlockSpec((1, tn), lambda i, j, kk: (0, j)))
        args.append(bias.reshape(1, n))
    return pl.pallas_call(
        body, name=name,
        out_shape=jax.ShapeDtypeStruct((m, n), out_dtype),
        grid=(m // tm, n // tn, nk),
        in_specs=in_specs,
        out_specs=pl.BlockSpec((tm, tn), lambda i, j, kk: (i, j)),
        scratch_shapes=[pltpu.VMEM((tm, tn), F32)],
        compiler_params=_cparams(("parallel", "parallel", "arbitrary")),
    )(*args)


TOK_TILE = 256


def _vec_spec(d):
    return pl.BlockSpec((1, 1, d), lambda b, s: (b, 0, 0))


def _tok_spec(ts, d):
    return pl.BlockSpec((1, ts, d), lambda b, s: (b, s, 0))


def mod_fwd(x, scale, shift):
    bsz, seq, d = x.shape
    ts = _rows_tile(seq, TOK_TILE)

    def body(x_ref, sc_ref, sh_ref, h_ref):
        h_ref[0] = (x_ref[0] * (1.0 + sc_ref[0]) + sh_ref[0]).astype(BF16)

    return pl.pallas_call(
        body, name="mod_fwd",
        out_shape=jax.ShapeDtypeStruct((bsz, seq, d), BF16),
        grid=(bsz, seq // ts),
        in_specs=[_tok_spec(ts, d), _vec_spec(d), _vec_spec(d)],
        out_specs=_tok_spec(ts, d),
        compiler_params=_cparams(("parallel", "parallel")),
    )(x, scale, shift)


def _ln_stats(z):
    mu = jnp.mean(z, axis=-1, keepdims=True)
    zc = z - mu
    var = jnp.mean(zc * zc, axis=-1, keepdims=True)
    return zc, lax.rsqrt(var + LN_EPS)


def resln_fwd(x, y, gate, g, b, res_w, nxt):
    bsz, seq, d = x.shape
    ts = _rows_tile(seq, TOK_TILE)

    def body(*refs):
        if nxt is None:
            x_ref, y_ref, gt_ref, g_ref, b_ref, xn_ref = refs
        else:
            x_ref, y_ref, gt_ref, g_ref, b_ref, sc_ref, sh_ref, xn_ref, hn_ref = refs
        z = DEEPNORM_ALPHA * x_ref[0] + (res_w * (1.0 + gt_ref[0])) * y_ref[0]
        zc, r = _ln_stats(z)
        xn = zc * r * g_ref[...] + b_ref[...]
        xn_ref[0] = xn
        if nxt is not None:
            hn_ref[0] = (xn * (1.0 + sc_ref[0]) + sh_ref[0]).astype(BF16)

    row = pl.BlockSpec((1, d), lambda bb, s: (0, 0))
    in_specs = [_tok_spec(ts, d), _tok_spec(ts, d), _vec_spec(d), row, row]
    args = [x, y, gate, g.reshape(1, d), b.reshape(1, d)]
    out_shape = [jax.ShapeDtypeStruct((bsz, seq, d), F32)]
    out_specs = [_tok_spec(ts, d)]
    if nxt is not None:
        in_specs += [_vec_spec(d), _vec_spec(d)]
        args += list(nxt)
        out_shape.append(jax.ShapeDtypeStruct((bsz, seq, d), BF16))
        out_specs.append(_tok_spec(ts, d))
    out = pl.pallas_call(
        body, name="resln_fwd",
        out_shape=out_shape, grid=(bsz, seq // ts),
        in_specs=in_specs, out_specs=out_specs,
        compiler_params=_cparams(("parallel", "parallel")),
    )(*args)
    return (out[0], None) if nxt is None else (out[0], out[1])


def resln_bwd(gout, x, y, gate, g, res_w, upstream=None):
    bsz, seq, d = x.shape
    ts = _rows_tile(seq, TOK_TILE)

    def body(*refs):
        if upstream is None:
            go_ref, x_ref, y_ref, gt_ref, g_ref, dxa_ref, dy_ref, dgt_ref, dg_ref, db_ref = refs
        else:
            (go_ref, x_ref, y_ref, gt_ref, g_ref, dh_ref, sc_ref, b_ref,
             dxa_ref, dy_ref, dgt_ref, dg_ref, db_ref, dsc_ref, dsh_ref) = refs
        bb, s = pl.program_id(0), pl.program_id(1)
        yv = y_ref[0]
        rw = res_w * (1.0 + gt_ref[0])
        z = DEEPNORM_ALPHA * x_ref[0] + rw * yv
        zc, r = _ln_stats(z)
        xhat = zc * r
        go = go_ref[0]
        if upstream is not None:
            dh = dh_ref[0].astype(F32)
            go = go + dh * (1.0 + sc_ref[0])
            dsc = jnp.sum(dh * (xhat * g_ref[...] + b_ref[...]), axis=0, keepdims=True)
            dsh = jnp.sum(dh, axis=0, keepdims=True)

            @pl.when(s == 0)
            def _():
                dsc_ref[0] = dsc
                dsh_ref[0] = dsh

            @pl.when(s > 0)
            def _():
                dsc_ref[0] += dsc
                dsh_ref[0] += dsh

        dxh = go * g_ref[...]
        dz = r * (dxh - jnp.mean(dxh, axis=-1, keepdims=True)
                  - xhat * jnp.mean(dxh * xhat, axis=-1, keepdims=True))
        dxa_ref[0] = DEEPNORM_ALPHA * dz
        dy_ref[0] = (rw * dz).astype(BF16)
        dgt = res_w * jnp.sum(dz * yv, axis=0, keepdims=True)
        dg = jnp.sum(go * xhat, axis=0, keepdims=True)
        db = jnp.sum(go, axis=0, keepdims=True)

        @pl.when(s == 0)
        def _():
            dgt_ref[0] = dgt

        @pl.when(s > 0)
        def _():
            dgt_ref[0] += dgt

        first = jnp.logical_and(bb == 0, s == 0)

        @pl.when(first)
        def _():
            dg_ref[...] = dg
            db_ref[...] = db

        @pl.when(jnp.logical_not(first))
        def _():
            dg_ref[...] += dg
            db_ref[...] += db

    row = pl.BlockSpec((1, d), lambda bb, s: (0, 0))
    vec = jax.ShapeDtypeStruct((bsz, 1, d), F32)
    in_specs = [_tok_spec(ts, d), _tok_spec(ts, d), _tok_spec(ts, d), _vec_spec(d), row]
    args = [gout, x, y, gate, g.reshape(1, d)]
    out_shape = [jax.ShapeDtypeStruct((bsz, seq, d), F32), jax.ShapeDtypeStruct((bsz, seq, d), BF16), vec,
                 jax.ShapeDtypeStruct((1, d), F32), jax.ShapeDtypeStruct((1, d), F32)]
    out_specs = [_tok_spec(ts, d), _tok_spec(ts, d), _vec_spec(d), row, row]
    if upstream is not None:
        dh, scale, b = upstream
        in_specs += [_tok_spec(ts, d), _vec_spec(d), row]
        args += [dh, scale, b.reshape(1, d)]
        out_shape += [vec, vec]
        out_specs += [_vec_spec(d), _vec_spec(d)]
    return pl.pallas_call(
        body, name="resln_bwd" if upstream is None else "resln_mod_bwd",
        out_shape=out_shape, grid=(bsz, seq // ts),
        in_specs=in_specs, out_specs=out_specs,
        compiler_params=_cparams(("arbitrary", "arbitrary")),
    )(*args)


def mod_bwd(dxa, dh, x, scale):
    bsz, seq, d = x.shape
    ts = _rows_tile(seq, TOK_TILE)

    def body(dxa_ref, dh_ref, x_ref, sc_ref, dx_ref, dsc_ref, dsh_ref):
        s = pl.program_id(1)
        dh = dh_ref[0].astype(F32)
        dx_ref[0] = dxa_ref[0] + dh * (1.0 + sc_ref[0])
        dsc = jnp.sum(dh * x_ref[0], axis=0, keepdims=True)
        dsh = jnp.sum(dh, axis=0, keepdims=True)

        @pl.when(s == 0)
        def _():
            dsc_ref[0] = dsc
            dsh_ref[0] = dsh

        @pl.when(s > 0)
        def _():
            dsc_ref[0] += dsc
            dsh_ref[0] += dsh

    return pl.pallas_call(
        body, name="mod_bwd",
        out_shape=[jax.ShapeDtypeStruct((bsz, seq, d), F32),
                   jax.ShapeDtypeStruct((bsz, 1, d), F32),
                   jax.ShapeDtypeStruct((bsz, 1, d), F32)],
        grid=(bsz, seq // ts),
        in_specs=[_tok_spec(ts, d), _tok_spec(ts, d), _tok_spec(ts, d), _vec_spec(d)],
        out_specs=[_tok_spec(ts, d), _vec_spec(d), _vec_spec(d)],
        compiler_params=_cparams(("parallel", "arbitrary")),
    )(dxa, dh, x, scale)


FF_HALF = D_FF // 2
FF_CHUNKS = ((0, 384), (384, 384), (768, 384), (1152, 256))


def ffn_up_fused(h2, wgu):
    t, d = h2.shape
    tm = _rows_tile(t, 512)

    def body(h_ref, w_ref, gu_ref, a_ref):
        h = h_ref[...]
        for c0, cw in FF_CHUNKS:
            g = _dot(h, w_ref[:, c0:c0 + cw], NN)
            u = _dot(h, w_ref[:, FF_HALF + c0:FF_HALF + c0 + cw], NN)
            gu_ref[:, c0:c0 + cw] = g.astype(BF16)
            gu_ref[:, FF_HALF + c0:FF_HALF + c0 + cw] = u.astype(BF16)
            a_ref[:, c0:c0 + cw] = (g * _sigmoid(g) * u).astype(BF16)

    return pl.pallas_call(
        body, name="ffn_up_fused",
        out_shape=[jax.ShapeDtypeStruct((t, 2 * D_FF), BF16), jax.ShapeDtypeStruct((t, D_FF), BF16)],
        grid=(2, t // tm),
        in_specs=[pl.BlockSpec((tm, d), lambda j, i: (i, 0)),
                  pl.BlockSpec((d, 2 * FF_HALF), lambda j, i: (0, j))],
        out_specs=[pl.BlockSpec((tm, 2 * FF_HALF), lambda j, i: (i, j)),
                   pl.BlockSpec((tm, FF_HALF), lambda j, i: (i, j))],
        compiler_params=_cparams(("parallel", "parallel")),
    )(h2, wgu)


def ffn_down_bwd_fused(dy2, wd_t, gu):
    t, d = dy2.shape
    tm = _rows_tile(t, 512)

    def body(dy_ref, w_ref, gu_ref, o_ref):
        dy = dy_ref[...]
        for c0, cw in FF_CHUNKS:
            da = _dot(dy, w_ref[:, c0:c0 + cw], NN)
            g = gu_ref[:, c0:c0 + cw].astype(F32)
            u = gu_ref[:, FF_HALF + c0:FF_HALF + c0 + cw].astype(F32)
            sg = _sigmoid(g)
            o_ref[:, c0:c0 + cw] = (da * u * (sg * (1.0 + g * (1.0 - sg)))).astype(BF16)
            o_ref[:, FF_HALF + c0:FF_HALF + c0 + cw] = (da * (g * sg)).astype(BF16)

    return pl.pallas_call(
        body, name="ffn_down_bwd_fused",
        out_shape=jax.ShapeDtypeStruct((t, 2 * D_FF), BF16),
        grid=(2, t // tm),
        in_specs=[pl.BlockSpec((tm, d), lambda j, i: (i, 0)),
                  pl.BlockSpec((d, FF_HALF), lambda j, i: (0, j)),
                  pl.BlockSpec((tm, 2 * FF_HALF), lambda j, i: (i, j))],
        out_specs=pl.BlockSpec((tm, 2 * FF_HALF), lambda j, i: (i, j)),
        compiler_params=_cparams(("parallel", "parallel")),
    )(dy2, wd_t, gu)


def loss_fwd(y, tgt):
    bsz, seq, d = y.shape
    ts = _rows_tile(seq, TOK_TILE)

    def body(y_ref, t_ref, dy_ref, l_ref, acc_ref):
        bb, s = pl.program_id(0), pl.program_id(1)
        e = y_ref[0] - t_ref[0]
        dy_ref[0] = e * (1.0 / d)
        part = jnp.sum((e * e).reshape(ts // 8, 8, d), axis=0)
        first = jnp.logical_and(bb == 0, s == 0)

        @pl.when(first)
        def _():
            acc_ref[...] = part

        @pl.when(jnp.logical_not(first))
        def _():
            acc_ref[...] += part

        @pl.when(jnp.logical_and(bb == pl.num_programs(0) - 1, s == pl.num_programs(1) - 1))
        def _():
            tot = jnp.sum(jnp.sum(acc_ref[...], axis=1, keepdims=True), axis=0, keepdims=True)
            l_ref[...] = jnp.broadcast_to(tot * (0.5 / d), (8, LANES))

    dy, l = pl.pallas_call(
        body, name="loss_fwd",
        out_shape=[jax.ShapeDtypeStruct((bsz, seq, d), F32), jax.ShapeDtypeStruct((8, LANES), F32)],
        grid=(bsz, seq // ts),
        in_specs=[_tok_spec(ts, d), _tok_spec(ts, d)],
        out_specs=[_tok_spec(ts, d), pl.BlockSpec((8, LANES), lambda bb, s: (0, 0))],
        scratch_shapes=[pltpu.VMEM((8, d), F32)],
        compiler_params=_cparams(("arbitrary", "arbitrary")),
    )(y, tgt)
    return l[0, 0], dy


def _rot_half(x):
    lane = lax.broadcasted_iota(jnp.int32, x.shape, 1)
    swapped = jnp.where(lane < 80, pltpu.roll(x, 112, axis=1), pltpu.roll(x, 16, axis=1))
    return jnp.where((lane >= C_NOPE) & (lane < C_NOPE + C_ROPE), swapped, 0.0)


def _rms(x, w):
    r = lax.rsqrt(jnp.mean(x * x, axis=-1, keepdims=True) + RMS_EPS)
    return x * r * w, r


def _rms_bwd(dy, x, w):
    r = lax.rsqrt(jnp.mean(x * x, axis=-1, keepdims=True) + RMS_EPS)
    wd = dy * w
    dx = r * wd - x * (r * r * r) * jnp.mean(x * wd, axis=-1, keepdims=True)
    return dx, jnp.sum(dy * x * r, axis=0, keepdims=True)


def mla_mid_fwd(cqkv, qw, kw, cq_tab, sg_tab):
    bsz, seq, _ = cqkv.shape
    ts = _rows_tile(seq, TOK_TILE)
    r = C_Q_RANK

    def body(x_ref, qw_ref, kw_ref, c_ref, s_ref, nq_ref, nkv_ref, kr_ref):
        x = x_ref[0]
        nq_ref[0] = _rms(x[:, :r], qw_ref[...])[0].astype(BF16)
        nkv_ref[0] = _rms(x[:, r:2 * r], kw_ref[...])[0].astype(BF16)
        xr = x[:, 2 * r:]
        kr_ref[0] = (xr * c_ref[0] + _rot_half(xr) * s_ref[0]).astype(BF16)

    row = pl.BlockSpec((1, r), lambda b, s: (0, 0))
    return pl.pallas_call(
        body, name="mla_mid_fwd",
        out_shape=[jax.ShapeDtypeStruct((bsz, seq, r), BF16), jax.ShapeDtypeStruct((bsz, seq, r), BF16),
                   jax.ShapeDtypeStruct((bsz, seq, LANES), BF16)],
        grid=(bsz, seq // ts),
        in_specs=[_tok_spec(ts, MLA_DOWN_PAD), row, row, _tok_spec(ts, LANES), _tok_spec(ts, LANES)],
        out_specs=[_tok_spec(ts, r), _tok_spec(ts, r), _tok_spec(ts, LANES)],
        compiler_params=_cparams(("parallel", "parallel")),
    )(cqkv, qw.reshape(1, r), kw.reshape(1, r), cq_tab, sg_tab)


def mla_mid_bwd(dnq, dnkv, dkr, cqkv, qw, kw, cq_tab, sg_tab):
    bsz, seq, _ = cqkv.shape
    ts = _rows_tile(seq, TOK_TILE)
    r = C_Q_RANK

    def body(dnq_ref, dnkv_ref, dkr_ref, x_ref, qw_ref, kw_ref, c_ref, s_ref, dx_ref, dqw_ref, dkw_ref):
        bb, s = pl.program_id(0), pl.program_id(1)
        x = x_ref[0]
        dcq, dqw = _rms_bwd(dnq_ref[0].astype(F32), x[:, :r], qw_ref[...])
        dckv, dkw = _rms_bwd(dnkv_ref[0].astype(F32), x[:, r:2 * r], kw_ref[...])
        dk = dkr_ref[0]
        dxr = dk * c_ref[0] + _rot_half(dk * s_ref[0])
        dx_ref[0, :, :r] = dcq.astype(BF16)
        dx_ref[0, :, r:2 * r] = dckv.astype(BF16)
        dx_ref[0, :, 2 * r:] = dxr.astype(BF16)
        first = jnp.logical_and(bb == 0, s == 0)

        @pl.when(first)
        def _():
            dqw_ref[...] = dqw
            dkw_ref[...] = dkw

        @pl.when(jnp.logical_not(first))
        def _():
            dqw_ref[...] += dqw
            dkw_ref[...] += dkw

    row = pl.BlockSpec((1, r), lambda b, s: (0, 0))
    return pl.pallas_call(
        body, name="mla_mid_bwd",
        out_shape=[jax.ShapeDtypeStruct((bsz, seq, MLA_DOWN_PAD), BF16),
                   jax.ShapeDtypeStruct((1, r), F32), jax.ShapeDtypeStruct((1, r), F32)],
        grid=(bsz, seq // ts),
        in_specs=[_tok_spec(ts, r), _tok_spec(ts, r), _tok_spec(ts, LANES), _tok_spec(ts, MLA_DOWN_PAD),
                  row, row, _tok_spec(ts, LANES), _tok_spec(ts, LANES)],
        out_specs=[_tok_spec(ts, MLA_DOWN_PAD), row, row],
        compiler_params=_cparams(("arbitrary", "arbitrary")),
    )(dnq, dnkv, dkr, cqkv, qw.reshape(1, r), kw.reshape(1, r), cq_tab, sg_tab)


def q_rope(q, cq_tab, sg_tab, transpose_rule, name, out_scale=1.0):
    bsz, seq, w = q.shape
    ts = _rows_tile(seq, TOK_TILE)

    def body(q_ref, c_ref, s_ref, o_ref):
        c, s = c_ref[0], s_ref[0]
        for h in range(w // LANES):
            x = q_ref[0, :, h * LANES:(h + 1) * LANES].astype(F32)
            y = x * c + (_rot_half(x * s) if transpose_rule else _rot_half(x) * s)
            o_ref[0, :, h * LANES:(h + 1) * LANES] = (y * out_scale).astype(BF16)

    return pl.pallas_call(
        body, name=name,
        out_shape=jax.ShapeDtypeStruct((bsz, seq, w), BF16),
        grid=(bsz, seq // ts),
        in_specs=[_tok_spec(ts, w), _tok_spec(ts, LANES), _tok_spec(ts, LANES)],
        out_specs=_tok_spec(ts, w),
        compiler_params=_cparams(("parallel", "parallel")),
    )(q, cq_tab, sg_tab)


MLA_SCALE = (C_NOPE + C_ROPE) ** -0.5
LOG2E = math.log2(math.e)
MLA_QSCALE = MLA_SCALE * LOG2E
MLA_TILE = 1024
NEG_BIG = -1e30


def _eye_mask(n):
    return lax.broadcasted_iota(jnp.int32, (n, n), 0) == lax.broadcasted_iota(jnp.int32, (n, n), 1)


def mla_attn_fwd(qp, kv, kr):
    bsz, seq, _ = qp.shape
    tq = _rows_tile(seq, MLA_TILE)
    nt = seq // tq

    def body(q_ref, kv_ref, kr_ref, o_ref, lse_ref, kc_ref):
        lane = lax.broadcasted_iota(jnp.int32, (seq, LANES), 1)
        kc_ref[...] = jnp.where(lane < C_NOPE, kv_ref[0], kr_ref[0])
        lane_q = lax.broadcasted_iota(jnp.int32, (tq, LANES), 1)

        def q_body(i, carry):
            qrows = pl.ds(pl.multiple_of(i * tq, tq), tq)
            q_i = q_ref[0, qrows, :]

            def key_body(j, st):
                m, l, acc = st
                krows = pl.ds(pl.multiple_of(j * tq, tq), tq)
                s = _dot(q_i, kc_ref[krows, :], NT)
                m_new = jnp.maximum(m, jnp.max(s, axis=-1, keepdims=True))
                alpha = jnp.exp2(m - m_new)
                p = jnp.exp2(s - m_new)
                l = alpha * l + jnp.sum(p, axis=-1, keepdims=True)
                acc = alpha * acc + _dot(p.astype(BF16), kv_ref[0, krows, :], NN)
                return m_new, l, acc

            m, l, acc = lax.fori_loop(0, nt, key_body, (
                jnp.full((tq, 1), NEG_BIG, F32), jnp.zeros((tq, 1), F32), jnp.zeros((tq, LANES), F32)))
            o_ref[0, qrows, :] = jnp.where(lane_q >= C_NOPE, acc * (1.0 / l), 0.0).astype(BF16)
            lse = m + jnp.log2(l)
            lse_ref[0, 0, :, qrows] = jnp.sum(jnp.where(_eye_mask(tq), lse, 0.0), axis=0, keepdims=True)
            return carry

        lax.fori_loop(0, nt, q_body, 0)

    head = pl.BlockSpec((1, seq, LANES), lambda b, h: (b, 0, h))
    return pl.pallas_call(
        body, name="mla_attn_fwd",
        out_shape=[jax.ShapeDtypeStruct((bsz, seq, MLA_HEAD_PAD), BF16),
                   jax.ShapeDtypeStruct((bsz, C_HEADS, 1, seq), F32)],
        grid=(bsz, C_HEADS),
        in_specs=[head, head, pl.BlockSpec((1, seq, LANES), lambda b, h: (b, 0, 0))],
        out_specs=[head, pl.BlockSpec((1, 1, 1, seq), lambda b, h: (b, h, 0, 0))],
        scratch_shapes=[pltpu.VMEM((seq, LANES), BF16)],
        compiler_params=_cparams(("parallel", "parallel")),
    )(qp, kv, kr)


def mla_attn_bwd(qp, kv, kr, o, do, lse):
    bsz, seq, _ = qp.shape
    tq = _rows_tile(seq, MLA_TILE)
    nt = seq // tq

    def body(q_ref, kv_ref, kr_ref, o_ref, do_ref, lse_ref, dq_ref, dkv_ref, dkr_ref,
             kc_ref, drow_ref, dqa_ref, dkc_ref, dkvv_ref):
        h = pl.program_id(1)
        lane_s = lax.broadcasted_iota(jnp.int32, (seq, LANES), 1)
        lane_t = lax.broadcasted_iota(jnp.int32, (tq, LANES), 1)
        kc_ref[...] = jnp.where(lane_s < C_NOPE, kv_ref[0], kr_ref[0])
        dqa_ref[...] = jnp.zeros_like(dqa_ref)
        ones = jnp.ones((8, LANES), F32)

        def delta_body(i, carry):
            rows = pl.ds(pl.multiple_of(i * tq, tq), tq)
            prod = do_ref[0, rows, :].astype(F32) * o_ref[0, rows, :].astype(F32)
            drow_ref[:, rows] = _dot(ones, prod, NT, precision=HIGHEST)
            return carry

        lax.fori_loop(0, nt, delta_body, 0)

        def key_body(j, carry):
            krows = pl.ds(pl.multiple_of(j * tq, tq), tq)
            kc_j = kc_ref[krows, :]
            kv_j = kv_ref[0, krows, :]
            dkc_ref[...] = jnp.zeros_like(dkc_ref)
            dkvv_ref[...] = jnp.zeros_like(dkvv_ref)

            def q_body(i, c2):
                qrows = pl.ds(pl.multiple_of(i * tq, tq), tq)
                q_i = q_ref[0, qrows, :]
                do_i = do_ref[0, qrows, :]
                st = _dot(kc_j, q_i, NT)
                pt = jnp.exp2(st - lse_ref[0, 0, :, qrows])
                dpt = _dot(kv_j, do_i, NT)
                dst = (pt * (dpt - drow_ref[0:1, qrows])).astype(BF16)
                dkvv_ref[...] += _dot(pt.astype(BF16), do_i, NN)
                dkc_ref[...] += _dot(dst, q_i, NN)
                dqa_ref[qrows, :] += _dot(dst, kc_j, TN)
                return c2

            lax.fori_loop(0, nt, q_body, 0)
            dkc = dkc_ref[...] * (MLA_SCALE / MLA_QSCALE)
            dkv_ref[0, krows, :] = jnp.where(lane_t < C_NOPE, dkc, dkvv_ref[...]).astype(BF16)
            dkr_j = jnp.where(lane_t >= C_NOPE, dkc, 0.0)

            @pl.when(h == 0)
            def _():
                dkr_ref[0, krows, :] = dkr_j

            @pl.when(h > 0)
            def _():
                dkr_ref[0, krows, :] += dkr_j

            return carry

        lax.fori_loop(0, nt, key_body, 0)
        dq_ref[0] = (dqa_ref[...] * MLA_SCALE).astype(BF16)

    head = pl.BlockSpec((1, seq, LANES), lambda b, h: (b, 0, h))
    shared = pl.BlockSpec((1, seq, LANES), lambda b, h: (b, 0, 0))
    return pl.pallas_call(
        body, name="mla_attn_bwd",
        out_shape=[jax.ShapeDtypeStruct((bsz, seq, MLA_HEAD_PAD), BF16),
                   jax.ShapeDtypeStruct((bsz, seq, MLA_HEAD_PAD), BF16),
                   jax.ShapeDtypeStruct((bsz, seq, LANES), F32)],
        grid=(bsz, C_HEADS),
        in_specs=[head, head, shared, head, head,
                  pl.BlockSpec((1, 1, 1, seq), lambda b, h: (b, h, 0, 0))],
        out_specs=[head, head, shared],
        scratch_shapes=[pltpu.VMEM((seq, LANES), BF16), pltpu.VMEM((8, seq), F32),
                        pltpu.VMEM((seq, LANES), F32), pltpu.VMEM((tq, LANES), F32),
                        pltpu.VMEM((tq, LANES), F32)],
        compiler_params=_cparams(("parallel", "arbitrary")),
    )(qp, kv, kr, o, do, lse)


A_SCALE = A_HEAD_DIM ** -0.5
A_GROUP = A_HEADS // A_KV_HEADS
A_BAND = 3 * A_BLOCK


def _wattn_block(i, seq, posc_ref, posr_ref):
    start = pl.multiple_of(i * A_BLOCK, A_BLOCK)
    pq = jnp.concatenate([posc_ref[0]] * A_GROUP, axis=0)
    pk = posr_ref[0, :, pl.ds(start, A_BAND)]
    dist = jnp.abs(pq - pk).astype(F32)
    shape = (A_GROUP * A_BLOCK, A_BAND)
    qi = (lax.broadcasted_iota(jnp.int32, shape, 0) & (A_BLOCK - 1)) + A_BLOCK
    ki = lax.broadcasted_iota(jnp.int32, shape, 1)
    absk = i * A_BLOCK + ki - A_BLOCK
    valid = (jnp.abs(qi - ki) <= WINDOW) & (absk >= 0) & (absk < seq)
    return start, dist, valid


def _wattn_probs(q4, kb, dist, valid, slope4, sink4):
    s = _dot(q4, kb, NT) * A_SCALE - slope4 * dist
    s = jnp.where(valid, s, NEG_BIG)
    m = jnp.maximum(jnp.max(s, axis=-1, keepdims=True), sink4)
    p = jnp.exp(s - m)
    es = jnp.exp(sink4 - m)
    inv = 1.0 / (jnp.sum(p, axis=-1, keepdims=True) + es)
    return p * inv, es * inv


def _alibi_slope(h):
    return 2.0 ** (-8.0 * (h + 1) / A_HEADS)


def _group_heads(g):
    return range(g * A_GROUP, (g + 1) * A_GROUP)


def _stack_heads(ref, g, dtype):
    return jnp.concatenate([ref[0, :, h * LANES:(h + 1) * LANES].astype(dtype) for h in _group_heads(g)], axis=0)


def _per_head_column(values):
    return jnp.concatenate([jnp.full((A_BLOCK, 1), v, F32) for v in values], axis=0)


def wattn_fwd(proj, kp, vp, posc, posr, sink):
    bsz, seq, _ = proj.shape
    nb = seq // A_BLOCK
    qw = A_HEADS * LANES

    def body(sink_ref, q_ref, kp_ref, vp_ref, posc_ref, posr_ref, o_ref):
        i = pl.program_id(1)
        start, dist, valid = _wattn_block(i, seq, posc_ref, posr_ref)
        kb = kp_ref[0, pl.ds(start, A_BAND), :]
        vb = vp_ref[0, pl.ds(start, A_BAND), :]
        lane = lax.broadcasted_iota(jnp.int32, (A_GROUP * A_BLOCK, LANES), 1)
        for g in range(A_KV_HEADS):
            heads = _group_heads(g)
            p, _ = _wattn_probs(_stack_heads(q_ref, g, BF16), kb, dist, valid,
                                _per_head_column([_alibi_slope(h) for h in heads]),
                                _per_head_column([sink_ref[h] for h in heads]))
            o = _dot(p.astype(BF16), vb, NN)
            mine = (lane >= A_HEAD_DIM) if g == 1 else (lane < A_HEAD_DIM)
            o = jnp.where(mine, o, 0.0).astype(BF16)
            for j, h in enumerate(heads):
                o_ref[0, :, h * LANES:(h + 1) * LANES] = o[j * A_BLOCK:(j + 1) * A_BLOCK]

    return pl.pallas_call(
        body, name="wattn_fwd",
        out_shape=jax.ShapeDtypeStruct((bsz, seq, MO_PAD), BF16),
        grid=(bsz, nb),
        in_specs=[pl.BlockSpec(memory_space=pltpu.SMEM),
                  pl.BlockSpec((1, A_BLOCK, qw), lambda b, i: (b, i, 0)),
                  pl.BlockSpec((1, seq + 2 * A_BLOCK, LANES), lambda b, i: (b, 0, 0)),
                  pl.BlockSpec((1, seq + 2 * A_BLOCK, LANES), lambda b, i: (b, 0, 0)),
                  pl.BlockSpec((1, A_BLOCK, 1), lambda b, i: (b, i, 0)),
                  pl.BlockSpec((1, 1, seq + 2 * A_BLOCK), lambda b, i: (b, 0, 0))],
        out_specs=pl.BlockSpec((1, A_BLOCK, qw), lambda b, i: (b, i, 0)),
        compiler_params=_cparams(("parallel", "parallel")),
    )(sink, proj, kp, vp, posc, posr)


def wattn_bwd(proj, kp, vp, posc, posr, sink, dmo):
    bsz, seq, _ = proj.shape
    nb = seq // A_BLOCK
    qw = A_HEADS * LANES
    sp = seq + 2 * A_BLOCK

    def body(sink_ref, q_ref, kp_ref, vp_ref, posc_ref, posr_ref, do_ref,
             dq_ref, dk_ref, dv_ref, ds_ref):
        i = pl.program_id(1)

        @pl.when(i == 0)
        def _():
            dk_ref[...] = jnp.zeros_like(dk_ref)
            dv_ref[...] = jnp.zeros_like(dv_ref)

        @pl.when(jnp.logical_and(i == 0, pl.program_id(0) == 0))
        def _():
            ds_ref[...] = jnp.zeros_like(ds_ref)

        start, dist, valid = _wattn_block(i, seq, posc_ref, posr_ref)
        kb = kp_ref[0, pl.ds(start, A_BAND), :]
        vb = vp_ref[0, pl.ds(start, A_BAND), :]
        lane = lax.broadcasted_iota(jnp.int32, (A_GROUP * A_BLOCK, LANES), 1)
        lane1 = lax.broadcasted_iota(jnp.int32, (1, LANES), 1)
        dk_acc = jnp.zeros((A_BAND, LANES), F32)
        dv_acc = jnp.zeros((A_BAND, LANES), F32)
        dsink = jnp.zeros((1, LANES), F32)
        for g in range(A_KV_HEADS):
            heads = _group_heads(g)
            q4 = _stack_heads(q_ref, g, BF16)
            p, psink = _wattn_probs(q4, kb, dist, valid,
                                    _per_head_column([_alibi_slope(h) for h in heads]),
                                    _per_head_column([sink_ref[h] for h in heads]))
            pb = p.astype(BF16)
            do = _stack_heads(do_ref, g, F32)
            dob = do.astype(BF16)
            mine = (lane >= A_HEAD_DIM) if g == 1 else (lane < A_HEAD_DIM)
            o = jnp.where(mine, _dot(pb, vb, NN), 0.0)
            delta = jnp.sum(do * o, axis=-1, keepdims=True)
            dp = _dot(dob, vb, NT)
            ds = (p * (dp - delta) * A_SCALE).astype(BF16)
            dq = _dot(ds, kb, NN).astype(BF16)
            dk_acc = dk_acc + _dot(ds, q4, TN)
            dv_acc = dv_acc + _dot(pb, dob, TN)
            sd = psink * delta
            for j, h in enumerate(heads):
                dq_ref[0, :, h * LANES:(h + 1) * LANES] = dq[j * A_BLOCK:(j + 1) * A_BLOCK]
                dsh = -jnp.sum(sd[j * A_BLOCK:(j + 1) * A_BLOCK], axis=0, keepdims=True)
                dsink = dsink + jnp.where(lane1 == h, dsh, 0.0)
        dk_ref[0, pl.ds(start, A_BAND), :] += dk_acc
        dv_ref[0, pl.ds(start, A_BAND), :] += dv_acc
        ds_ref[0] += dsink

    full = pl.BlockSpec((1, sp, LANES), lambda b, i: (b, 0, 0))
    return pl.pallas_call(
        body, name="wattn_bwd",
        out_shape=[jax.ShapeDtypeStruct((bsz, seq, qw), BF16),
                   jax.ShapeDtypeStruct((bsz, sp, LANES), F32),
                   jax.ShapeDtypeStruct((bsz, sp, LANES), F32),
                   jax.ShapeDtypeStruct((1, 1, LANES), F32)],
        grid=(bsz, nb),
        in_specs=[pl.BlockSpec(memory_space=pltpu.SMEM),
                  pl.BlockSpec((1, A_BLOCK, qw), lambda b, i: (b, i, 0)),
                  full, full,
                  pl.BlockSpec((1, A_BLOCK, 1), lambda b, i: (b, i, 0)),
                  pl.BlockSpec((1, 1, sp), lambda b, i: (b, 0, 0)),
                  pl.BlockSpec((1, A_BLOCK, qw), lambda b, i: (b, i, 0))],
        out_specs=[pl.BlockSpec((1, A_BLOCK, qw), lambda b, i: (b, i, 0)), full, full,
                   pl.BlockSpec((1, 1, LANES), lambda b, i: (0, 0, 0))],
        compiler_params=_cparams(("arbitrary", "arbitrary")),
    )(sink, proj, kp, vp, posc, posr, dmo)


HG_Q, HG_FF, HG_FB, HG_I, HG_G = 10, 14, 18, 22, 26
HG_OUT = 8
CH = B_CHUNK
HG_GROUP = 4
GR = HG_GROUP * CH


def _hgrn_consts(reverse):
    r = lax.broadcasted_iota(jnp.int32, (GR, GR), 0)
    c = lax.broadcasted_iota(jnp.int32, (GR, GR), 1)
    same = jnp.right_shift(r, 6) == jnp.right_shift(c, 6)
    incl = same & ((c >= r) if reverse else (c <= r))
    row = lax.broadcasted_iota(jnp.int32, (HG_GROUP, CH, LANES), 1)
    mid = CH // 2 if reverse else CH // 2 - 1
    end = 0 if reverse else CH - 1
    return incl, row == mid, row == end


def _per_chunk(x, sel=None):
    x3 = x.reshape(HG_GROUP, CH, LANES)
    return jnp.sum(x3 if sel is None else jnp.where(sel, x3, 0.0), axis=1, keepdims=True)


def _to_rows(x3):
    return jnp.broadcast_to(x3, (HG_GROUP, CH, LANES)).reshape(GR, LANES)


def _chunk_cumsum(x, reverse):
    pos = lax.broadcasted_iota(jnp.int32, (GR, LANES), 0) & (CH - 1)
    s = 1
    while s < CH:
        if reverse:
            x = x + jnp.where(pos < CH - s, pltpu.roll(x, GR - s, axis=0), 0.0)
        else:
            x = x + jnp.where(pos >= s, pltpu.roll(x, s, axis=0), 0.0)
        s *= 2
    return x


def _block_diag(x):
    chunk = jnp.right_shift(lax.broadcasted_iota(jnp.int32, (GR, LANES), 0), 6)
    return jnp.concatenate([jnp.where(chunk == n, x, jnp.zeros_like(x)) for n in range(HG_GROUP)], axis=1)


def _hgrn_gates(hf, lb):
    sig = _sigmoid(hf)
    f = lb + (1.0 - lb) * sig
    return 1.0 - f, jnp.log(f), sig, f


def _hgrn_decays(lf, consts, reverse):
    _, is_mid, is_end = consts
    b = _chunk_cumsum(lf, reverse)
    return b, _per_chunk(b, is_mid), _per_chunk(b, is_end)


def _chunk_order(reverse):
    return range(HG_GROUP - 1, -1, -1) if reverse else range(HG_GROUP)


def _lane_block(x, n):
    return x[:, n * LANES:(n + 1) * LANES]


def _hgrn_states(vb, kd, dec3, st, reverse):
    inc = _dot(vb, _block_diag(kd), TN)
    entering = [None] * HG_GROUP
    for n in _chunk_order(reverse):
        entering[n] = st
        st = dec3[n] * st + _lane_block(inc, n)
    return entering, st


def _hgrn_group(q, k, v, lf, st, consts, reverse):
    b, bm3, be3 = _hgrn_decays(lf, consts, reverse)
    bm, be = _to_rows(bm3), _to_rows(be3)
    qs = (q * jnp.exp(b - bm)).astype(BF16)
    ks = (k * jnp.exp(bm - b)).astype(BF16)
    a = jnp.where(consts[0], _dot(qs, ks, NT), 0.0).astype(BF16)
    qe = (q * jnp.exp(b)).astype(BF16)
    vb = v.astype(BF16)
    kd = (k * jnp.exp(be - b)).astype(BF16)
    entering, st = _hgrn_states(vb, kd, jnp.exp(be3), st, reverse)
    s_all = jnp.concatenate([e.astype(BF16) for e in entering], axis=1)
    return _dot(a, vb, NN) + _dot(_block_diag(qe), s_all, NT), st


def _silu(x):
    return x * _sigmoid(x)


def hgrn_fwd(proj, mo, lb, nw):
    bsz, seq, _ = proj.shape
    n_gr = seq // GR
    ts = _rows_tile(seq, 512)

    def body(hq_ref, hff_ref, hfb_ref, hi_ref, hg_ref, mo_in_ref, lb_ref, nw_ref, y_ref, ot_ref):
        del mo_in_ref
        lb_v, nw_v = lb_ref[0], nw_ref[0]

        def run(hf_ref, reverse, first):
            consts = _hgrn_consts(reverse)

            def step(t, st):
                n = (n_gr - 1 - t) if reverse else t
                rows = pl.ds(pl.multiple_of(n * GR, GR), GR)
                k, lf, _, _ = _hgrn_gates(hf_ref[0, rows, :], lb_v)
                o, st = _hgrn_group(_silu(hq_ref[0, rows, :]), k, hi_ref[0, rows, :], lf, st, consts, reverse)
                if first:
                    ot_ref[0, rows, :] = o
                else:
                    ot_ref[0, rows, :] += o
                return st

            lax.fori_loop(0, n_gr, step, jnp.zeros((LANES, LANES), F32))

        run(hff_ref, False, True)
        run(hfb_ref, True, False)

        def finish(i, carry):
            rows = pl.ds(pl.multiple_of(i * ts, ts), ts)
            o = ot_ref[0, rows, :]
            r = lax.rsqrt(jnp.mean(o * o, axis=-1, keepdims=True) + RMS_EPS)
            y_ref[0, rows, :] = (o * r * nw_v * _silu(hg_ref[0, rows, :])).astype(BF16)
            return carry

        lax.fori_loop(0, seq // ts, finish, 0)

    def col(g):
        return pl.BlockSpec((1, seq, LANES), lambda b, h: (b, 0, g + h))

    par = pl.BlockSpec((1, 1, LANES), lambda b, h: (h, 0, 0))
    return pl.pallas_call(
        body, name="hgrn_fwd",
        out_shape=[jax.ShapeDtypeStruct(mo.shape, BF16),
                   jax.ShapeDtypeStruct((bsz, seq, B_HEADS * LANES), F32)],
        grid=(bsz, B_HEADS),
        in_specs=[col(HG_Q), col(HG_FF), col(HG_FB), col(HG_I), col(HG_G),
                  pl.BlockSpec(memory_space=pl.ANY), par, par],
        out_specs=[col(HG_OUT), col(0)],
        input_output_aliases={5: 0},
        compiler_params=_cparams(("parallel", "parallel")),
    )(proj, proj, proj, proj, proj, mo, lb, nw)


def hgrn_out_bwd(proj, otot, dmo, nw):
    bsz, seq, _ = proj.shape
    ts = _rows_tile(seq, 512)

    def body(hg_ref, ot_ref, dmo_ref, nw_ref, do_ref, dhg_ref, dnw_ref):
        nw_v = nw_ref[0]
        dy = dmo_ref[0].astype(F32)
        hg = hg_ref[0]
        o = ot_ref[0]
        sg = _sigmoid(hg)
        r = lax.rsqrt(jnp.mean(o * o, axis=-1, keepdims=True) + RMS_EPS)
        dhg_ref[0] = (dy * (o * r * nw_v) * (sg * (1.0 + hg * (1.0 - sg)))).astype(BF16)
        drn = dy * (hg * sg)
        dnw = jnp.sum(drn * o * r, axis=0, keepdims=True)
        wd = drn * nw_v
        do_ref[0] = r * wd - o * (r * r * r) * jnp.mean(o * wd, axis=-1, keepdims=True)
        first = jnp.logical_and(pl.program_id(1) == 0, pl.program_id(2) == 0)

        @pl.when(first)
        def _():
            dnw_ref[0] = dnw

        @pl.when(jnp.logical_not(first))
        def _():
            dnw_ref[0] += dnw

    def col(g):
        return pl.BlockSpec((1, ts, LANES), lambda h, b, s: (b, s, g + h))

    par = pl.BlockSpec((1, 1, LANES), lambda h, b, s: (h, 0, 0))
    return pl.pallas_call(
        body, name="hgrn_out_bwd",
        out_shape=[jax.ShapeDtypeStruct((bsz, seq, B_HEADS * LANES), F32),
                   jax.ShapeDtypeStruct((bsz, seq, B_HEADS * LANES), BF16),
                   jax.ShapeDtypeStruct((B_HEADS, 1, LANES), F32)],
        grid=(B_HEADS, bsz, seq // ts),
        in_specs=[col(HG_G), col(0), col(HG_OUT), par],
        out_specs=[col(0), col(0), par],
        compiler_params=_cparams(("parallel", "arbitrary", "arbitrary")),
    )(proj, otot, dmo, nw)


def hgrn_bwd(proj, do, lb):
    bsz, seq, _ = proj.shape
    n_ch = seq // CH
    n_gr = seq // GR
    ts = _rows_tile(seq, 512)

    def body(hq_ref, hff_ref, hfb_ref, hi_ref, do_scr, lb_ref,
             dhq_ref, dhff_ref, dhfb_ref, dhi_ref, dlb_ref, st_scr, dq_scr, dv_scr):
        lb_v = lb_ref[0]
        do_scr = do_scr.at[0]

        def run(hf_ref, dhf_ref, reverse, first, dlb0):
            consts = _hgrn_consts(reverse)
            incl, is_mid, is_end = consts

            def load(n):
                rows = pl.ds(pl.multiple_of(n * GR, GR), GR)
                hf = hf_ref[0, rows, :]
                k, lf, sig, f = _hgrn_gates(hf, lb_v)
                return rows, _silu(hq_ref[0, rows, :]), k, hi_ref[0, rows, :], lf, sig, f

            def fwd_step(t, st):
                n = (n_gr - 1 - t) if reverse else t
                _, _, k, v, lf, _, _ = load(n)
                b, _, be3 = _hgrn_decays(lf, consts, reverse)
                kd = (k * jnp.exp(_to_rows(be3) - b)).astype(BF16)
                entering, st = _hgrn_states(v.astype(BF16), kd, jnp.exp(be3), st, reverse)
                for c in range(HG_GROUP):
                    st_scr[n * HG_GROUP + c] = entering[c].astype(BF16)
                return st

            lax.fori_loop(0, n_gr, fwd_step, jnp.zeros((LANES, LANES), F32))

            def bwd_step(t, carry):
                gt, dlb = carry
                n = t if reverse else (n_gr - 1 - t)
                rows, q, k, v, lf, sig, f = load(n)
                do_c = do_scr[rows, :].astype(BF16)
                b, bm3, be3 = _hgrn_decays(lf, consts, reverse)
                bm, be = _to_rows(bm3), _to_rows(be3)
                e_qs, e_ks, e_q, e_kd = jnp.exp(b - bm), jnp.exp(bm - b), jnp.exp(b), jnp.exp(be - b)
                dec3 = jnp.exp(be3)
                qs, ks, qe, kd = q * e_qs, k * e_ks, q * e_q, k * e_kd
                qs_b, ks_b, qe_b, kd_b = qs.astype(BF16), ks.astype(BF16), qe.astype(BF16), kd.astype(BF16)
                vb = v.astype(BF16)
                a = jnp.where(incl, _dot(qs_b, ks_b, NT), 0.0).astype(BF16)
                da = jnp.where(incl, _dot(do_c, vb, NT), 0.0).astype(BF16)
                dv = _dot(a, do_c, TN)
                dqs = _dot(da, ks_b, NN)
                dks = _dot(da, qs_b, TN)
                zed = _dot(do_c, _block_diag(qe_b), TN)
                sts = [st_scr[n * HG_GROUP + c] for c in range(HG_GROUP)]
                gts, ddec = [None] * HG_GROUP, [None] * HG_GROUP
                for c in reversed(list(_chunk_order(reverse))):
                    gts[c] = gt
                    ddec[c] = jnp.sum(gt * sts[c].astype(F32), axis=0, keepdims=True)
                    gt = dec3[c] * gt + _lane_block(zed, c)
                g_b = [g.astype(BF16) for g in gts]
                dqe = _dot(_block_diag(do_c), jnp.concatenate(sts, axis=0), NN)
                dkd = _dot(_block_diag(vb), jnp.concatenate(g_b, axis=0), NN)
                dv = dv + _dot(_block_diag(kd_b), jnp.concatenate(g_b, axis=1), NT)
                ddec3 = jnp.stack(ddec, axis=0)
                dq = dqs * e_qs + dqe * e_q
                dk = dks * e_ks + dkd * e_kd
                t_qs, t_ks, t_kd = dqs * qs, dks * ks, dkd * kd
                db = (t_qs - t_ks + dqe * qe - t_kd).reshape(HG_GROUP, CH, LANES)
                dbm3 = _per_chunk(t_ks - t_qs)
                dbe3 = _per_chunk(t_kd) + ddec3 * dec3
                db = (db + jnp.where(is_mid, dbm3, 0.0) + jnp.where(is_end, dbe3, 0.0)).reshape(GR, LANES)
                dlf = _chunk_cumsum(db, not reverse)
                df = dlf / f - dk
                dhf_ref[0, rows, :] = (df * (1.0 - lb_v) * sig * (1.0 - sig)).astype(BF16)
                dlb = dlb + jnp.sum(df * (1.0 - sig), axis=0, keepdims=True)
                if first:
                    dq_scr[rows, :] = dq
                    dv_scr[rows, :] = dv
                else:
                    dq_scr[rows, :] += dq
                    dv_scr[rows, :] += dv
                return gt, dlb

            return lax.fori_loop(0, n_gr, bwd_step, (jnp.zeros((LANES, LANES), F32), dlb0))[1]

        dlb = run(hff_ref, dhff_ref, False, True, jnp.zeros((1, LANES), F32))
        dlb = run(hfb_ref, dhfb_ref, True, False, dlb)

        @pl.when(pl.program_id(1) == 0)
        def _():
            dlb_ref[0] = dlb

        @pl.when(pl.program_id(1) > 0)
        def _():
            dlb_ref[0] += dlb

        def finish(i, carry):
            rows = pl.ds(pl.multiple_of(i * ts, ts), ts)
            hq = hq_ref[0, rows, :]
            sq = _sigmoid(hq)
            dhq_ref[0, rows, :] = (dq_scr[rows, :] * (sq * (1.0 + hq * (1.0 - sq)))).astype(BF16)
            dhi_ref[0, rows, :] = dv_scr[rows, :].astype(BF16)
            return carry

        lax.fori_loop(0, seq // ts, finish, 0)

    def col(g):
        return pl.BlockSpec((1, seq, LANES), lambda h, b: (b, 0, g + h))

    par = pl.BlockSpec((1, 1, LANES), lambda h, b: (h, 0, 0))
    wide = jax.ShapeDtypeStruct((bsz, seq, B_HEADS * LANES), BF16)
    small = jax.ShapeDtypeStruct((B_HEADS, 1, LANES), F32)
    return pl.pallas_call(
        body, name="hgrn_bwd",
        out_shape=[wide, wide, wide, wide, small],
        grid=(B_HEADS, bsz),
        in_specs=[col(HG_Q), col(HG_FF), col(HG_FB), col(HG_I), col(0), par],
        out_specs=[col(0), col(0), col(0), col(0), par],
        scratch_shapes=[pltpu.VMEM((n_ch, LANES, LANES), BF16),
                        pltpu.VMEM((seq, LANES), F32), pltpu.VMEM((seq, LANES), F32)],
        compiler_params=_cparams(("parallel", "arbitrary")),
    )(proj, proj, proj, proj, do, lb)


def _whole(shape):
    return pl.BlockSpec(shape, lambda: (0,) * len(shape))


def silu_small(x):
    def body(x_ref, o_ref):
        o_ref[...] = _silu(x_ref[...])

    return pl.pallas_call(body, name="silu_small", out_shape=jax.ShapeDtypeStruct(x.shape, F32),
                          in_specs=[_whole(x.shape)], out_specs=_whole(x.shape))(x)


def _softmax_rows(x):
    e = jnp.exp(x - jnp.max(x, axis=0, keepdims=True))
    return e / jnp.sum(e, axis=0, keepdims=True)


def lb_fwd(logits, layer):
    n, w = logits.shape

    def body(x_ref, o_ref):
        p = _softmax_rows(x_ref[...])
        row = lax.broadcasted_iota(jnp.int32, (n, w), 0)
        o_ref[...] = jnp.sum(jnp.where(row <= layer, p, 0.0), axis=0, keepdims=True)

    return pl.pallas_call(body, name="lb_fwd", out_shape=jax.ShapeDtypeStruct((1, w), F32),
                          in_specs=[_whole((n, w))], out_specs=_whole((1, w)))(logits)


def lb_bwd(logits, dlb, layer):
    n, w = logits.shape

    def body(x_ref, d_ref, o_ref):
        p = _softmax_rows(x_ref[...])
        row = lax.broadcasted_iota(jnp.int32, (n, w), 0)
        dp = jnp.where(row <= layer, d_ref[...], 0.0)
        o_ref[...] = p * (dp - jnp.sum(p * dp, axis=0, keepdims=True))

    return pl.pallas_call(body, name="lb_bwd", out_shape=jax.ShapeDtypeStruct((n, w), F32),
                          in_specs=[_whole((n, w)), _whole((1, w))], out_specs=_whole((n, w)))(logits, dlb)


def sum_parts(x, name):
    p, r, c = x.shape
    tr = _rows_tile(r, max(8, (1 << 23) // (4 * c * p) // 8 * 8))

    def body(x_ref, o_ref):
        acc = x_ref[0].astype(F32)
        for j in range(1, p):
            acc = acc + x_ref[j].astype(F32)
        o_ref[...] = acc

    return pl.pallas_call(
        body, name=name, out_shape=jax.ShapeDtypeStruct((r, c), F32),
        grid=(r // tr,),
        in_specs=[pl.BlockSpec((p, tr, c), lambda i: (0, i, 0))],
        out_specs=pl.BlockSpec((tr, c), lambda i: (i, 0)),
        compiler_params=_cparams(("parallel",)),
    )(x)


def adamw(w, g, m, v, name):
    r, c = w.shape
    tr = _rows_tile(r, max(8, (1 << 20) // (4 * c) // 8 * 8))
    c1 = 1.0 - ADAM_B1 ** ADAM_STEP
    c2 = 1.0 - ADAM_B2 ** ADAM_STEP

    def body(w_ref, g_ref, m_ref, v_ref, d_ref, nm_ref, nv_ref):
        g = g_ref[...]
        nm = ADAM_B1 * m_ref[...] + (1.0 - ADAM_B1) * g
        nv = ADAM_B2 * v_ref[...] + (1.0 - ADAM_B2) * (g * g)
        nm_ref[...] = nm
        nv_ref[...] = nv
        d_ref[...] = -ADAM_LR * ((nm / c1) / (jnp.sqrt(nv / c2) + ADAM_EPS) + ADAM_WD * w_ref[...])

    spec = pl.BlockSpec((tr, c), lambda i: (i, 0))
    sds = jax.ShapeDtypeStruct((r, c), F32)
    return pl.pallas_call(
        body, name=name, out_shape=[sds, sds, sds], grid=(r // tr,),
        in_specs=[spec, spec, spec, spec], out_specs=[spec, spec, spec],
        compiler_params=_cparams(("parallel",)),
    )(w, g, m, v)


ANY = pl.BlockSpec(memory_space=pl.ANY)


def _my_place():
    return lax.axis_index("x"), lax.axis_index("y"), lax.axis_index("c")


def all_gather(x, name):
    def body(x_ref, out_ref, send_sems, recv_sems, local_sem):
        mx, my, mc = _my_place()
        me, sibling = (mx, my, mc), (mx, my, 1 - mc)
        chips = [(1 - mx, my), (mx, 1 - my), (1 - mx, 1 - my)]

        def slot(px, py, pc):
            return out_ref.at[4 * px + 2 * py + pc]

        def copy(k, block, to, src=None):
            return pltpu.make_async_remote_copy(
                src_ref=slot(*block) if src is None else src, dst_ref=slot(*block),
                send_sem=send_sems.at[k], recv_sem=recv_sems.at[k],
                device_id=to, device_id_type=MESH_ID)

        mine = pltpu.make_async_copy(x_ref, slot(*me), local_sem)
        mine.start()
        first = [copy(0, me, sibling, src=x_ref)]
        first += [copy(1 + j, me, (*chip, mc), src=x_ref) for j, chip in enumerate(chips)]
        for cp in first:
            cp.start()
        passed = [copy(4 + j, (*chip, mc), sibling) for j, chip in enumerate(chips)]
        for j, chip in enumerate(chips):
            copy(1 + j, (*chip, mc), me).wait_recv()
            passed[j].start()
        copy(0, sibling, me).wait_recv()
        for j, chip in enumerate(chips):
            copy(4 + j, (*chip, 1 - mc), me).wait_recv()
        for cp in first + passed:
            cp.wait_send()
        mine.wait()

    return pl.pallas_call(
        body, name=name,
        out_shape=jax.ShapeDtypeStruct((N_DEV,) + x.shape, x.dtype),
        in_specs=[ANY], out_specs=ANY,
        scratch_shapes=[pltpu.SemaphoreType.DMA((7,)), pltpu.SemaphoreType.DMA((7,)),
                        pltpu.SemaphoreType.DMA(())],
    )(x)


def swap_with_sibling(x, name):
    def body(x_ref, out_ref, send_sem, recv_sem):
        mx, my, mc = _my_place()
        cp = pltpu.make_async_remote_copy(
            src_ref=x_ref.at[1 - mc], dst_ref=out_ref, send_sem=send_sem, recv_sem=recv_sem,
            device_id=(mx, my, 1 - mc), device_id_type=MESH_ID)
        cp.start()
        cp.wait()

    return pl.pallas_call(
        body, name=name,
        out_shape=jax.ShapeDtypeStruct(x.shape[1:], x.dtype),
        in_specs=[ANY], out_specs=ANY,
        scratch_shapes=[pltpu.SemaphoreType.DMA(()), pltpu.SemaphoreType.DMA(())],
    )(x)


def pair_add(x, r, mc):
    _, n, rows, c = x.shape
    tr = _rows_tile(rows, 512)

    def body(mc_ref, x_ref, r_ref, o_ref):
        del mc_ref
        o_ref[0] = (x_ref[0, 0].astype(F32) + r_ref[0].astype(F32)).astype(BF16)

    return pl.pallas_call(
        body, name="pair_add",
        out_shape=jax.ShapeDtypeStruct((n, rows, c), BF16),
        grid_spec=pltpu.PrefetchScalarGridSpec(
            num_scalar_prefetch=1, grid=(n, rows // tr),
            in_specs=[pl.BlockSpec((1, 1, tr, c), lambda j, i, s: (s[0], j, i, 0)),
                      pl.BlockSpec((1, tr, c), lambda j, i, s: (j, i, 0))],
            out_specs=pl.BlockSpec((1, tr, c), lambda j, i, s: (j, i, 0))),
        compiler_params=_cparams(("parallel", "parallel")),
    )(mc.reshape(1).astype(jnp.int32), x, r)


def chip_all_to_all(x, name):
    def body(x_ref, out_ref, send_sems, recv_sems, local_sem):
        mx, my, mc = _my_place()
        me = 2 * mx + my
        mine = pltpu.make_async_copy(x_ref.at[me], out_ref.at[me], local_sem)
        mine.start()
        copies = []
        for k, (fx, fy) in enumerate(((1, 0), (0, 1), (1, 1))):
            px = 1 - mx if fx else mx
            py = 1 - my if fy else my
            peer = 2 * px + py
            cp = pltpu.make_async_remote_copy(
                src_ref=x_ref.at[peer], dst_ref=out_ref.at[me],
                send_sem=send_sems.at[k], recv_sem=recv_sems.at[k],
                device_id=(px, py, mc), device_id_type=MESH_ID)
            cp.start()
            copies.append((cp, peer, (px, py, mc), k))
        for cp, peer, to, k in copies:
            pltpu.make_async_remote_copy(
                src_ref=x_ref.at[peer], dst_ref=out_ref.at[peer],
                send_sem=send_sems.at[k], recv_sem=recv_sems.at[k],
                device_id=to, device_id_type=MESH_ID).wait_recv()
        for cp, _, _, _ in copies:
            cp.wait_send()
        mine.wait()

    return pl.pallas_call(
        body, name=name,
        out_shape=jax.ShapeDtypeStruct(x.shape, x.dtype),
        in_specs=[ANY], out_specs=ANY,
        scratch_shapes=[pltpu.SemaphoreType.DMA((3,)), pltpu.SemaphoreType.DMA((3,)),
                        pltpu.SemaphoreType.DMA(())],
    )(x)


ROW_W = D_MODEL
FF_SH = D_FF // N_DEV
N_FFN = 2 * DEPTH
SEC = {}
_off = 0
for _name, _rows in (("gate", N_FFN * FF_SH), ("up", N_FFN * FF_SH), ("down", N_FFN * FF_SH),
                     ("hin", 416), ("hout", 128), ("mout", 128), ("mdown", 80), ("uq", 48), ("ukv", 64)):
    SEC[_name] = (_off, _rows)
    _off += _rows
PACK_ROWS = _off
MDOWN_ROWS = 128 * 544 // ROW_W
A_Q_W = A_HEADS * A_HEAD_DIM
_KVHEAD = (np.arange(A_HEADS) // A_GROUP).reshape(A_HEADS, 1, 1)


def pack_shards(gate, up, down, hin, hout, mdown, uq, ukv, mout, dtype):
    def rows(a):
        return a.astype(dtype).reshape(-1, ROW_W)
    parts = [rows(jnp.swapaxes(gate, -1, -2)), rows(jnp.swapaxes(up, -1, -2)), rows(down),
             rows(jnp.swapaxes(hin, -1, -2)), rows(hout), rows(mout),
             jnp.pad(rows(mdown), ((0, SEC["mdown"][1] - MDOWN_ROWS), (0, 0))),
             rows(jnp.swapaxes(uq, -1, -2)), rows(jnp.swapaxes(ukv, -1, -2))]
    return jnp.concatenate(parts, axis=0)


def unpack_shards(p):
    def sec(name):
        o, n = SEC[name]
        return p[o:o + n]

    def col(name, r, c):
        return jnp.swapaxes(sec(name).reshape(-1, c, r), -1, -2)

    return dict(
        ffn_w_gate=col("gate", 1024, FF_SH).reshape(DEPTH, 2, 1024, FF_SH),
        ffn_w_up=col("up", 1024, FF_SH).reshape(DEPTH, 2, 1024, FF_SH),
        ffn_w_down=sec("down").reshape(DEPTH, 2, FF_SH, 1024),
        hyb_w_in=col("hin", 1024, 416),
        hyb_w_out=sec("hout").reshape(1, 128, 1024),
        mla_w_out=sec("mout").reshape(1, 128, 1024),
        mla_w_down=sec("mdown")[:MDOWN_ROWS].reshape(1, 128, 544),
        mla_w_uq=col("uq", 256, 192),
        mla_w_ukv=col("ukv", 256, 256),
    )


def _pad_qheads(w):
    a = w.reshape(A_HEADS, 1, A_HEAD_DIM, -1)
    kvh = _KVHEAD[..., None]
    both = jnp.concatenate([jnp.where(kvh == 0, a, 0), jnp.where(kvh == 1, a, 0)], axis=1)
    return both.reshape(A_HEADS * LANES, -1)


def _unpad_qheads(w):
    a = w.reshape(A_HEADS, 2, A_HEAD_DIM, -1)
    return jnp.where(_KVHEAD == 0, a[:, 0], a[:, 1]).reshape(A_Q_W, -1)


def unpack_full(g):
    def sec(name):
        o, n = SEC[name]
        return g[:, o:o + n]

    def ffn(name):
        return jnp.swapaxes(sec(name).reshape(N_DEV, N_FFN, FF_SH, ROW_W), 0, 1).reshape(N_FFN, D_FF, ROW_W)

    def z(*s):
        return jnp.zeros(s, g.dtype)

    def halves(a):
        return a.reshape(N_FFN, 2, 1, FF_HALF, ROW_W)

    gate_t, up_t, down = ffn("gate"), ffn("up"), ffn("down")
    hin_t = sec("hin").reshape(-1, ROW_W)
    hout = sec("hout").reshape(-1, ROW_W)
    mout = sec("mout").reshape(C_HEADS, C_V, ROW_W)
    mdown = sec("mdown")[:, :MDOWN_ROWS].reshape(D_MODEL, 544)
    uq_t = sec("uq").reshape(C_HEADS, C_NOPE + C_ROPE, C_Q_RANK)
    w = dict(
        wgu_t=jnp.concatenate([halves(gate_t), halves(up_t)], axis=2).reshape(N_FFN, 2 * D_FF, ROW_W),
        wd=down,
        hin_t=jnp.concatenate([_pad_qheads(hin_t[:A_Q_W]), hin_t[A_Q_W:]], axis=0),
        hout=jnp.concatenate([_pad_qheads(hout[:A_Q_W]), hout[A_Q_W:]], axis=0),
        mout=jnp.concatenate([z(C_HEADS, C_NOPE, ROW_W), mout], axis=1).reshape(MLA_HEAD_PAD, ROW_W),
        mdown=jnp.concatenate([mdown[:, :512], z(D_MODEL, 64), mdown[:, 512:], z(D_MODEL, 32)], axis=1),
        uq_t=jnp.concatenate([uq_t, z(C_HEADS, 32, C_Q_RANK)], axis=1).reshape(MLA_HEAD_PAD, C_Q_RANK),
        ukv_t=sec("ukv").reshape(MLA_HEAD_PAD, C_KV_RANK),
    )
    w.update(wgu=jnp.swapaxes(w["wgu_t"], 1, 2), wd_t=jnp.swapaxes(down, 1, 2), hin=w["hin_t"].T,
             hout_t=w["hout"].T, mout_t=w["mout"].T, mdown_t=w["mdown"].T, uq=w["uq_t"].T, ukv=w["ukv_t"].T)
    return w


def pack_full(d, dtype):
    d = {k: v.astype(dtype) for k, v in d.items()}
    gu = d["wgu_t"].reshape(N_FFN, 2, 2, FF_HALF, ROW_W)
    d["wgu_t"] = jnp.concatenate([gu[:, :, 0].reshape(N_FFN, D_FF, ROW_W),
                                  gu[:, :, 1].reshape(N_FFN, D_FF, ROW_W)], axis=1)

    def split(a):
        return a.reshape(N_DEV, -1, ROW_W)

    def ffn(a):
        return jnp.swapaxes(a.reshape(N_FFN, N_DEV, FF_SH, ROW_W), 0, 1).reshape(N_DEV, N_FFN * FF_SH, ROW_W)

    hin = jnp.concatenate([_unpad_qheads(d["hin_t"][:A_HEADS * LANES]), d["hin_t"][A_HEADS * LANES:]], axis=0)
    hout = jnp.concatenate([_unpad_qheads(d["hout"][:A_HEADS * LANES]), d["hout"][A_HEADS * LANES:]], axis=0)
    mout = d["mout"].reshape(C_HEADS, LANES, ROW_W)[:, C_NOPE:].reshape(-1, ROW_W)
    mdown = jnp.concatenate([d["mdown"][:, :512], d["mdown"][:, 576:608]], axis=1)
    uq = d["uq_t"].reshape(C_HEADS, LANES, C_Q_RANK)[:, :C_NOPE + C_ROPE]
    parts = [ffn(d["wgu_t"][:, :D_FF]), ffn(d["wgu_t"][:, D_FF:]), ffn(d["wd"]),
             split(hin), split(hout), split(mout),
             jnp.pad(split(mdown), ((0, 0), (0, SEC["mdown"][1] - MDOWN_ROWS), (0, 0))),
             split(uq), split(d["ukv_t"])]
    return jnp.concatenate(parts, axis=1)


def rope_tables(positions):
    half = C_ROPE // 2
    freqs = ROPE_THETA ** (-jnp.arange(half, dtype=F32) / half)
    ang = positions.astype(F32)[..., None] * freqs
    cos, sin = jnp.cos(ang), jnp.sin(ang)
    shape = positions.shape
    cq = jnp.concatenate([jnp.ones(shape + (C_NOPE,), F32), cos, cos, jnp.zeros(shape + (32,), F32)], axis=-1)
    sg = jnp.concatenate([jnp.zeros(shape + (C_NOPE,), F32), -sin, sin, jnp.zeros(shape + (32,), F32)], axis=-1)
    return cq, sg


GRAD_DT = BF16
def ffn_fwd(h2, w, i):
    gu, a = ffn_up_fused(h2, w["wgu"][i])
    return matmul(a, w["wd"][i], "nn", F32, "ffn_down"), (gu, a)


def ffn_bwd(dy2, h2, w, i, saved):
    gu, a = saved
    dgu = ffn_down_bwd_fused(dy2, w["wd_t"][i], gu)
    dwd = matmul(a, dy2, "tn", GRAD_DT, "ffn_down_dw", tm_cap=FF_HALF)
    dwgu_t = matmul(dgu, h2, "tn", GRAD_DT, "ffn_up_dw", tm_cap=FF_HALF)
    dh = matmul(dgu, w["wgu_t"][i], "nn", F32, "ffn_up_dx")
    return dh, dwgu_t, dwd


def hybrid_fwd(h2, shape, w, aux):
    bsz, seq = shape
    proj = matmul(h2, w["hin"], "nn", F32, "hyb_in").reshape(bsz, seq, HYB_PAD)
    pad = ((0, 0), (A_BLOCK, A_BLOCK), (0, 0))
    kp = jnp.pad(proj[:, :, 8 * LANES:9 * LANES].astype(BF16), pad)
    vp = jnp.pad(proj[:, :, 9 * LANES:10 * LANES].astype(BF16), pad)
    mo = wattn_fwd(proj, kp, vp, aux["posc"], aux["posr"], aux["sink"])
    mo, otot = hgrn_fwd(proj, mo, aux["lb"], aux["nw"])
    y = matmul(mo.reshape(bsz * seq, MO_PAD), w["hout"], "nn", F32, "hyb_out")
    return y, (proj, kp, vp, mo, otot)


def hybrid_bwd(dy2, h2, shape, w, aux, saved):
    bsz, seq = shape
    proj, kp, vp, mo, otot = saved
    mo2 = mo.reshape(bsz * seq, MO_PAD)
    dmo = matmul(dy2, w["hout_t"], "nn", F32, "hyb_out_dx").reshape(bsz, seq, MO_PAD)
    dhout = matmul(mo2, dy2, "tn", GRAD_DT, "hyb_out_dw")
    dq, dkp, dvp, dsink = wattn_bwd(proj, kp, vp, aux["posc"], aux["posr"], aux["sink"], dmo)
    do, dhg, dnw = hgrn_out_bwd(proj, otot, dmo, aux["nw"])
    dhq, dhff, dhfb, dhi, dlb = hgrn_bwd(proj, do, aux["lb"])
    dproj = jnp.concatenate([dq, dkp[:, A_BLOCK:-A_BLOCK].astype(BF16), dvp[:, A_BLOCK:-A_BLOCK].astype(BF16),
                             dhq, dhff, dhfb, dhi, dhg], axis=-1).reshape(bsz * seq, HYB_PAD)
    dhin_t = matmul(dproj, h2, "tn", GRAD_DT, "hyb_in_dw")
    dh = matmul(dproj, w["hin_t"], "nn", F32, "hyb_in_dx")
    return dh, dict(hin_t=dhin_t, hout=dhout), dict(sink=dsink, lb=dlb, nw=dnw)


def mla_fwd(h2, shape, w, aux):
    bsz, seq = shape
    t = bsz * seq
    cqkv = matmul(h2, w["mdown"], "nn", F32, "mla_down").reshape(bsz, seq, MLA_DOWN_PAD)
    nq, nkv, kr = mla_mid_fwd(cqkv, aux["qn"], aux["kvn"], aux["cq"], aux["sg"])
    q = matmul(nq.reshape(t, C_Q_RANK), w["uq"], "nn", F32, "mla_uq").reshape(bsz, seq, MLA_HEAD_PAD)
    qp = q_rope(q, aux["cq"], aux["sg"], False, "q_rope_fwd", out_scale=MLA_QSCALE)
    kv = matmul(nkv.reshape(t, C_KV_RANK), w["ukv"], "nn", BF16, "mla_ukv").reshape(bsz, seq, MLA_HEAD_PAD)
    o, lse = mla_attn_fwd(qp, kv, kr)
    y = matmul(o.reshape(t, MLA_HEAD_PAD), w["mout"], "nn", F32, "mla_out")
    return y, (cqkv, nq, nkv, kr, qp, kv, o, lse)


def mla_bwd(dy2, h2, shape, w, aux, saved):
    bsz, seq = shape
    t = bsz * seq
    cqkv, nq, nkv, kr, qp, kv, o, lse = saved
    do = matmul(dy2, w["mout_t"], "nn", BF16, "mla_out_dx").reshape(bsz, seq, MLA_HEAD_PAD)
    dmout = matmul(o.reshape(t, MLA_HEAD_PAD), dy2, "tn", GRAD_DT, "mla_out_dw")
    dqp, dkv, dkr = mla_attn_bwd(qp, kv, kr, o, do, lse)
    dq = q_rope(dqp, aux["cq"], aux["sg"], True, "q_rope_bwd").reshape(t, MLA_HEAD_PAD)
    dkv2 = dkv.reshape(t, MLA_HEAD_PAD)
    dnq = matmul(dq, w["uq_t"], "nn", F32, "mla_uq_dx").reshape(bsz, seq, C_Q_RANK)
    duq_t = matmul(dq, nq.reshape(t, C_Q_RANK), "tn", GRAD_DT, "mla_uq_dw")
    dnkv = matmul(dkv2, w["ukv_t"], "nn", F32, "mla_ukv_dx").reshape(bsz, seq, C_KV_RANK)
    dukv_t = matmul(dkv2, nkv.reshape(t, C_KV_RANK), "tn", GRAD_DT, "mla_ukv_dw")
    dcqkv, dqn, dkvn = mla_mid_bwd(dnq, dnkv, dkr, cqkv, aux["qn"], aux["kvn"], aux["cq"], aux["sg"])
    dcqkv2 = dcqkv.reshape(t, MLA_DOWN_PAD)
    dmdown = matmul(h2, dcqkv2, "tn", GRAD_DT, "mla_down_dw")
    dh = matmul(dcqkv2, w["mdown_t"], "nn", F32, "mla_down_dx")
    return dh, dict(mdown=dmdown, uq_t=duq_t, ukv_t=dukv_t, mout=dmout), dict(qn=dqn, kvn=dkvn)


W_NAMES = ['ada_w', 'ada_b', 'ln_g', 'ln_b', 'ffn_w_gate', 'ffn_w_up', 'ffn_w_down', 'hyb_w_in', 'hyb_w_out',
           'attn_sink', 'hgrn_lb_logits', 'hgrn_norm_w', 'mla_w_down', 'mla_q_norm', 'mla_kv_norm', 'mla_w_uq',
           'mla_w_ukv', 'mla_w_out']
SMALL_NAMES = ['ada_b', 'ln_g', 'ln_b', 'attn_sink', 'hgrn_lb_logits', 'hgrn_norm_w', 'mla_q_norm', 'mla_kv_norm']
MOD_W = N_SUB * 3 * D_MODEL
MOD_SH = MOD_W // N_DEV
RES_W = (0.5, 1.0, 0.5)


def _rows1024(a, rows=None):
    flat = a.astype(F32).reshape(-1)
    n = flat.shape[0]
    total = (-(-n // ROW_W) if rows is None else rows) * ROW_W
    return jnp.pad(flat, (0, total - n)).reshape(-1, ROW_W)


def kernel(x, c, positions, ada_w, ada_b, ln_g, ln_b, ffn_w_gate, ffn_w_up, ffn_w_down, hyb_w_in, hyb_w_out, attn_sink, hgrn_lb_logits, hgrn_norm_w, mla_w_down, mla_q_norm, mla_kv_norm, mla_w_uq, mla_w_ukv, mla_w_out, loss_target, m_ada_w, m_ada_b, m_ln_g, m_ln_b, m_ffn_w_gate, m_ffn_w_up, m_ffn_w_down, m_hyb_w_in, m_hyb_w_out, m_attn_sink, m_hgrn_lb_logits, m_hgrn_norm_w, m_mla_w_down, m_mla_q_norm, m_mla_kv_norm, m_mla_w_uq, m_mla_w_ukv, m_mla_w_out, v_ada_w, v_ada_b, v_ln_g, v_ln_b, v_ffn_w_gate, v_ffn_w_up, v_ffn_w_down, v_hyb_w_in, v_hyb_w_out, v_attn_sink, v_hgrn_lb_logits, v_hgrn_norm_w, v_mla_w_down, v_mla_q_norm, v_mla_kv_norm, v_mla_w_uq, v_mla_w_ukv, v_mla_w_out):
    weights = dict(zip(W_NAMES, (ada_w, ada_b, ln_g, ln_b, ffn_w_gate, ffn_w_up, ffn_w_down, hyb_w_in, hyb_w_out,
                                 attn_sink, hgrn_lb_logits, hgrn_norm_w, mla_w_down, mla_q_norm, mla_kv_norm,
                                 mla_w_uq, mla_w_ukv, mla_w_out)))
    mom1 = dict(zip(W_NAMES, (m_ada_w, m_ada_b, m_ln_g, m_ln_b, m_ffn_w_gate, m_ffn_w_up, m_ffn_w_down, m_hyb_w_in,
                              m_hyb_w_out, m_attn_sink, m_hgrn_lb_logits, m_hgrn_norm_w, m_mla_w_down, m_mla_q_norm,
                              m_mla_kv_norm, m_mla_w_uq, m_mla_w_ukv, m_mla_w_out)))
    mom2 = dict(zip(W_NAMES, (v_ada_w, v_ada_b, v_ln_g, v_ln_b, v_ffn_w_gate, v_ffn_w_up, v_ffn_w_down, v_hyb_w_in,
                              v_hyb_w_out, v_attn_sink, v_hgrn_lb_logits, v_hgrn_norm_w, v_mla_w_down, v_mla_q_norm,
                              v_mla_kv_norm, v_mla_w_uq, v_mla_w_ukv, v_mla_w_out)))
    bsz, seq, d = x.shape
    t = bsz * seq
    nb_tot = N_DEV * bsz
    me = 4 * lax.axis_index("x") + 2 * lax.axis_index("y") + lax.axis_index("c")

    c_all = all_gather(c, "gather_c").reshape(nb_tot, d)
    cond = silu_small(c_all)
    ada_b_mine = lax.dynamic_slice_in_dim(ada_b, me * MOD_SH, MOD_SH, axis=1)
    modp = jnp.concatenate([matmul(cond, ada_w[l], "nn", F32, "ada_fwd", bias=ada_b_mine[l])
                            for l in range(DEPTH)], axis=1)
    mod_rows = nb_tot * DEPTH * MOD_SH // ROW_W
    small1 = jnp.concatenate([_rows1024(modp), _rows1024(jnp.concatenate(
        [ln_g.reshape(-1), ln_b.reshape(-1), mla_q_norm.reshape(-1), mla_kv_norm.reshape(-1)]), rows=4)], axis=0)
    g1 = all_gather(small1, "gather_mod")
    mod_all = g1[:, :mod_rows].reshape(N_DEV, nb_tot, DEPTH, MOD_SH)
    mod_all = jnp.transpose(mod_all, (1, 2, 0, 3)).reshape(nb_tot, DEPTH, N_SUB, 3, d)
    mod = lax.dynamic_slice_in_dim(mod_all, me * bsz, bsz, axis=0)
    tail = g1[:, mod_rows:].reshape(N_DEV, -1)
    n_ln = DEPTH * N_SUB * LANES
    ln_g_full = jnp.transpose(tail[:, :n_ln].reshape(N_DEV, DEPTH, N_SUB, LANES), (1, 2, 0, 3)).reshape(DEPTH, N_SUB, d)
    ln_b_full = jnp.transpose(tail[:, n_ln:2 * n_ln].reshape(N_DEV, DEPTH, N_SUB, LANES), (1, 2, 0, 3)).reshape(DEPTH, N_SUB, d)
    qn_full = tail[:, 2 * n_ln:2 * n_ln + 32].reshape(C_Q_RANK)
    kvn_full = tail[:, 2 * n_ln + 32:2 * n_ln + 64].reshape(C_KV_RANK)

    def mvec(l, s, j):
        return mod[:, l, s, j].reshape(bsz, 1, d)

    packed = pack_shards(ffn_w_gate, ffn_w_up, ffn_w_down, hyb_w_in, hyb_w_out, mla_w_down, mla_w_uq, mla_w_ukv,
                         mla_w_out, BF16)
    w = unpack_full(all_gather(packed, "gather_weights"))

    cq_tab, sg_tab = rope_tables(positions)
    lb0 = lb_fwd(hgrn_lb_logits, 0)
    aux = [dict(posc=positions.reshape(bsz, seq, 1),
                posr=jnp.pad(positions, ((0, 0), (A_BLOCK, A_BLOCK))).reshape(bsz, 1, seq + 2 * A_BLOCK),
                sink=attn_sink[0], lb=lb0.reshape(B_HEADS, 1, LANES), nw=hgrn_norm_w[0].reshape(B_HEADS, 1, LANES)),
           dict(qn=qn_full, kvn=kvn_full, cq=cq_tab, sg=sg_tab)]
    mixers = [(hybrid_fwd, hybrid_bwd), (mla_fwd, mla_bwd)]

    tape = []
    xin = x
    h = mod_fwd(x, mvec(0, 0, 1), mvec(0, 0, 0))
    for l in range(DEPTH):
        for s in range(N_SUB):
            h2 = h.reshape(t, d)
            if s == 1:
                y, saved = mixers[l][0](h2, (bsz, seq), w, aux[l])
            else:
                i = 2 * l + s // 2
                y, saved = ffn_fwd(h2, w, i)
            y = y.reshape(bsz, seq, d)
            last = l == DEPTH - 1 and s == N_SUB - 1
            ln, sn = (l, s + 1) if s + 1 < N_SUB else (l + 1, 0)
            nxt = None if last else (mvec(ln, sn, 1), mvec(ln, sn, 0))
            xn, hn = resln_fwd(xin, y, mvec(l, s, 2), ln_g_full[l, s], ln_b_full[l, s], RES_W[s], nxt)
            tape.append((l, s, xin, h2, y, saved))
            xin, h = xn, hn
    loss_part, gout = loss_fwd(xin, loss_target)
    loss = lax.psum(loss_part, ("x", "y", "c"))

    dmod = [[[None] * 3 for _ in range(N_SUB)] for _ in range(DEPTH)]
    dln_g = [[None] * N_SUB for _ in range(DEPTH)]
    dln_b = [[None] * N_SUB for _ in range(DEPTH)]
    big = dict(wgu_t=[None] * N_FFN, wd=[None] * N_FFN)
    small = {}
    later = None
    for l, s, xs, h2, y, saved in reversed(tape):
        if later is None:
            dxa, dy, dgate, dg, db = resln_bwd(gout, xs, y, mvec(l, s, 2), ln_g_full[l, s], RES_W[s])
        else:
            ndxa, ndh, nl, ns = later
            dxa, dy, dgate, dg, db, dscale, dshift = resln_bwd(
                ndxa, xs, y, mvec(l, s, 2), ln_g_full[l, s], RES_W[s],
                upstream=(ndh, mvec(nl, ns, 1), ln_b_full[l, s]))
            dmod[nl][ns][0], dmod[nl][ns][1] = dshift, dscale
        dmod[l][s][2] = dgate
        dln_g[l][s], dln_b[l][s] = dg, db
        dy2 = dy.reshape(t, d)
        if s == 1:
            dh, dbig, dsmall = mixers[l][1](dy2, h2, (bsz, seq), w, aux[l], saved)
            big.update(dbig)
            small.update(dsmall)
        else:
            i = 2 * l + s // 2
            dh, big["wgu_t"][i], big["wd"][i] = ffn_bwd(dy2, h2, w, i, saved)
        later = (dxa, dh.reshape(bsz, seq, d), l, s)
    grad_x, dmod[0][0][1], dmod[0][0][0] = mod_bwd(later[0], later[1], x, mvec(0, 0, 1))
    big["wgu_t"] = jnp.stack(big["wgu_t"])
    big["wd"] = jnp.stack(big["wd"])

    dmod_mine = jnp.stack([jnp.stack([jnp.concatenate(dmod[l][s], axis=1) for s in range(N_SUB)], axis=1)
                           for l in range(DEPTH)], axis=1)
    dmod_rows = bsz * DEPTH * MOD_W // ROW_W
    misc = jnp.concatenate([small["lb"].reshape(-1), small["nw"].reshape(-1), small["qn"].reshape(-1),
                            small["kvn"].reshape(-1), small["sink"].reshape(-1)[:A_HEADS]])
    small2 = jnp.concatenate([_rows1024(dmod_mine),
                              _rows1024(jnp.stack([jnp.stack(r) for r in dln_g])),
                              _rows1024(jnp.stack([jnp.stack(r) for r in dln_b])),
                              _rows1024(misc, rows=2)], axis=0)
    g2 = all_gather(small2, "gather_small_grads")
    dmod_all = g2[:, :dmod_rows].reshape(nb_tot, DEPTH * MOD_W // ROW_W, ROW_W)
    grad_ada_b = sum_parts(dmod_all, "sum_ada_b").reshape(DEPTH, MOD_W)
    dmod_cols = lax.dynamic_slice_in_dim(dmod_all.reshape(nb_tot, DEPTH, MOD_W), me * MOD_SH, MOD_SH, axis=2)
    grad_ada_w = jnp.stack([matmul(cond, dmod_cols[:, l], "tn", F32, "ada_dw") for l in range(DEPTH)])
    rest = sum_parts(g2[:, dmod_rows:], "sum_small")
    n6 = DEPTH * N_SUB
    gl_g = lax.dynamic_slice_in_dim(rest[:n6].reshape(DEPTH, N_SUB, d), me * LANES, LANES, axis=2)
    gl_b = lax.dynamic_slice_in_dim(rest[n6:2 * n6].reshape(DEPTH, N_SUB, d), me * LANES, LANES, axis=2)
    mrow = rest[2 * n6:].reshape(-1)
    dlb0 = mrow[:512].reshape(1, 512)
    g_nw = mrow[512:1024].reshape(1, B_HEADS, LANES)
    g_qn = lax.dynamic_slice_in_dim(mrow[1024:1280], me * 32, 32).reshape(1, 32)
    g_kvn = lax.dynamic_slice_in_dim(mrow[1280:1536], me * 32, 32).reshape(1, 32)
    g_sink = mrow[1536:1536 + A_HEADS].reshape(1, A_HEADS)
    g_lb = lb_bwd(hgrn_lb_logits, dlb0, 0)

    mc = lax.axis_index("c")
    outgoing = jnp.swapaxes(pack_full(big, BF16).reshape(4, 2, PACK_ROWS, ROW_W), 0, 1)
    pair = pair_add(outgoing, swap_with_sibling(outgoing, "exchange_d2d"), mc)
    reduced = sum_parts(chip_all_to_all(pair, "exchange_ici"), "sum_grads")
    grads = unpack_shards(reduced)
    grads.update(ada_w=grad_ada_w, ada_b=grad_ada_b, ln_g=gl_g, ln_b=gl_b, attn_sink=g_sink,
                 hgrn_lb_logits=g_lb, hgrn_norm_w=g_nw, mla_q_norm=g_qn, mla_kv_norm=g_kvn)

    delta, new_m, new_v = {}, {}, {}
    for name in W_NAMES:
        if name in SMALL_NAMES:
            continue
        shp = weights[name].shape
        two_d = (-1, shp[-1])
        out = adamw(weights[name].reshape(two_d), grads[name].reshape(two_d), mom1[name].reshape(two_d),
                    mom2[name].reshape(two_d), "adamw_" + name)
        delta[name], new_m[name], new_v[name] = [o.reshape(shp) for o in out]

    def pack_small(src):
        flat = jnp.concatenate([src[n].reshape(-1) for n in SMALL_NAMES])
        return jnp.pad(flat, (0, -flat.shape[0] % (8 * LANES))).reshape(-1, LANES)

    outs = adamw(pack_small(weights), pack_small(grads), pack_small(mom1), pack_small(mom2), "adamw_small")
    off = 0
    for name in SMALL_NAMES:
        n = weights[name].size
        for dst, o in zip((delta, new_m, new_v), outs):
            dst[name] = o.reshape(-1)[off:off + n].reshape(weights[name].shape)
        off += n

    return (loss, grad_x, *[grads[n] for n in W_NAMES], *[delta[n] for n in W_NAMES],
            *[new_m[n] for n in W_NAMES], *[new_v[n] for n in W_NAMES])
```

```python
import functools
import math

import jax
import jax.numpy as jnp
import numpy as np
from jax import lax
from jax.experimental import pallas as pl
from jax.experimental.pallas import tpu as pltpu

F32 = jnp.float32
BF16 = jnp.bfloat16

D_MODEL = 1024
DEPTH = 2
D_FF = 2816
A_HEADS = 8
A_KV_HEADS = 2
A_HEAD_DIM = 64
WINDOW = 128
A_BLOCK = 128
B_HEADS = 4
B_KEY_DIM = 128
B_CHUNK = 64
C_HEADS = 16
C_Q_RANK = 256
C_KV_RANK = 256
C_NOPE = 64
C_ROPE = 32
C_V = 64
ROPE_THETA = 10000.0
LN_EPS = 1e-5
RMS_EPS = 1e-6
DEEPNORM_ALPHA = (2 * DEPTH) ** 0.25
N_SUB = 3
N_DEV = 8
LANES = 128
HYB_PAD = 3840
MO_PAD = 1536
MLA_DOWN_PAD = 640
MLA_HEAD_PAD = C_HEADS * LANES
ADAM_LR = 0.001
ADAM_B1 = 0.9
ADAM_B2 = 0.999
ADAM_EPS = 1e-08
ADAM_WD = 0.01
ADAM_STEP = 10
VMEM_LIMIT = 48 * 1024 * 1024
MESH_ID = pl.DeviceIdType.MESH
HIGHEST = lax.Precision.HIGHEST


def _cparams(sem=None):
    return pltpu.CompilerParams(dimension_semantics=sem, vmem_limit_bytes=VMEM_LIMIT)


def _tile(n, cap):
    if n <= cap:
        return n
    t = (cap // LANES) * LANES
    while t >= LANES:
        if n % t == 0:
            return t
        t -= LANES
    return n


def _rows_tile(n, cap):
    if n <= cap:
        return n
    t = cap
    while t >= 8:
        if n % t == 0:
            return t
        t -= 8
    return n


def _sigmoid(x):
    return 1.0 / (1.0 + jnp.exp(-x))


def _dot(a, b, dims, precision=None):
    return lax.dot_general(a, b, (dims, ((), ())), precision=precision,
                           preferred_element_type=F32)


NN = ((1,), (0,))
NT = ((1,), (1,))
TN = ((0,), (0,))


class Rider:
    def __init__(self, ins, outs, copies, n_remote, aliases=None):
        self.ins, self.outs, self.copies, self.n_remote = list(ins), list(outs), copies, n_remote
        self.aliases = aliases or {}

    def wrap(self, body, n_in, n_out, first, last):
        nci, nco = len(self.ins), len(self.outs)

        def wrapped(*refs):
            ins, cins = refs[:n_in], refs[n_in:n_in + nci]
            outs = refs[n_in + nci:n_in + nci + n_out]
            couts = refs[n_in + nci + n_out:n_in + nci + n_out + nco]
            scratch = refs[n_in + nci + n_out + nco:-3]
            sems = refs[-3:]

            @pl.when(first())
            def _():
                for started, _ in self.copies(cins, couts, *sems):
                    started.start()

            body(*ins, *outs, *scratch)

            @pl.when(last())
            def _():
                for _, awaited in self.copies(cins, couts, *sems):
                    awaited.wait()

        return wrapped

    def extend(self, n_in, n_out, args, in_specs, out_shape, out_specs, scratch):
        any_spec = pl.BlockSpec(memory_space=pl.ANY)
        aliases = {n_in + i: n_out + o for i, o in self.aliases.items()}
        return (list(args) + self.ins, list(in_specs) + [any_spec] * len(self.ins),
                list(out_shape) + self.outs, list(out_specs) + [any_spec] * len(self.outs),
                list(scratch) + [pltpu.SemaphoreType.DMA((self.n_remote,)), pltpu.SemaphoreType.DMA((self.n_remote,)),
                                 pltpu.SemaphoreType.DMA(())], aliases)


def _dev_slot(ref, px, py, pc):
    return ref.at[4 * px + 2 * py + pc]


def _other_chips(mx, my):
    return [(1 - mx, my), (mx, 1 - my), (1 - mx, 1 - my)]


def ride_gather_level1(x):
    def copies(cins, couts, send_sems, recv_sems, local_sem):
        (x_ref,), (out_ref,) = cins, couts
        mx, my, mc = _my_place()
        targets = [(mx, my, 1 - mc)] + [(*chip, mc) for chip in _other_chips(mx, my)]
        mine = _dev_slot(out_ref, mx, my, mc)

        def remote(k, to, dst):
            return pltpu.make_async_remote_copy(src_ref=x_ref, dst_ref=dst, send_sem=send_sems.at[k],
                                                recv_sem=recv_sems.at[k], device_id=to, device_id_type=MESH_ID)

        local = pltpu.make_async_copy(x_ref, mine, local_sem)
        return [(local, local)] + [(remote(k, to, mine), remote(k, to, _dev_slot(out_ref, *to)))
                                   for k, to in enumerate(targets)]

    return Rider([x], [jax.ShapeDtypeStruct((N_DEV,) + x.shape, x.dtype)], copies, 4)


def ride_gather_level2(g):
    def copies(cins, couts, send_sems, recv_sems, local_sem):
        (g_ref,), (out_ref,) = cins, couts
        mx, my, mc = _my_place()

        def remote(k, chip, pc):
            return pltpu.make_async_remote_copy(
                src_ref=_dev_slot(g_ref, *chip, mc), dst_ref=_dev_slot(out_ref, *chip, pc),
                send_sem=send_sems.at[k], recv_sem=recv_sems.at[k],
                device_id=(mx, my, 1 - mc), device_id_type=MESH_ID)

        return [(remote(k, chip, mc), remote(k, chip, 1 - mc)) for k, chip in enumerate(_other_chips(mx, my))]

    return Rider([g], [jax.ShapeDtypeStruct(g.shape, g.dtype)], copies, 3, aliases={0: 0})


def ride_swap_with_sibling(x):
    def copies(cins, couts, send_sems, recv_sems, local_sem):
        (x_ref,), (out_ref,) = cins, couts
        mx, my, mc = _my_place()
        cp = pltpu.make_async_remote_copy(
            src_ref=x_ref.at[1 - mc], dst_ref=out_ref, send_sem=send_sems.at[0], recv_sem=recv_sems.at[0],
            device_id=(mx, my, 1 - mc), device_id_type=MESH_ID)
        return [(cp, cp)]

    return Rider([x], [jax.ShapeDtypeStruct(x.shape[1:], x.dtype)], copies, 1)


def ride_chip_all_to_all(x):
    def copies(cins, couts, send_sems, recv_sems, local_sem):
        (x_ref,), (out_ref,) = cins, couts
        mx, my, mc = _my_place()
        me = 2 * mx + my

        def remote(k, px, py, dst):
            return pltpu.make_async_remote_copy(src_ref=x_ref.at[2 * px + py], dst_ref=dst,
                                                send_sem=send_sems.at[k], recv_sem=recv_sems.at[k],
                                                device_id=(px, py, mc), device_id_type=MESH_ID)

        local = pltpu.make_async_copy(x_ref.at[me], out_ref.at[me], local_sem)
        return [(local, local)] + [(remote(k, px, py, out_ref.at[me]), remote(k, px, py, out_ref.at[2 * px + py]))
                                   for k, (px, py) in enumerate(_other_chips(mx, my))]

    return Rider([x], [jax.ShapeDtypeStruct(x.shape, x.dtype)], copies, 3)


def matmul(a, b, form, out_dtype, name, bias=None, tm_cap=512, tn_cap=1024, tk_cap=1024, rider=None):
    if form == "nn":
        (m, k), (k2, n) = a.shape, b.shape
    elif form == "nt":
        (m, k), (n, k2) = a.shape, b.shape
    else:
        (k, m), (k2, n) = a.shape, b.shape
    assert k == k2, (a.shape, b.shape, form)
    tm = _tile(m, tm_cap)
    tn = _tile(n, tn_cap)
    tk = k if k <= 2 * D_FF else _tile(k, tk_cap)
    nk = k // tk
    dims = {"nn": NN, "nt": NT, "tn": TN}[form]

    def body(*refs):
        if bias is None:
            a_ref, b_ref, o_ref, acc_ref = refs
            bias_ref = None
        else:
            a_ref, b_ref, bias_ref, o_ref, acc_ref = refs
        kk = pl.program_id(2)
        part = _dot(a_ref[...].astype(BF16), b_ref[...].astype(BF16), dims)

        def finish(total):
            if bias_ref is not None:
                total = total + bias_ref[...]
            o_ref[...] = total.astype(o_ref.dtype)

        if nk == 1:
            finish(part)
        else:
            @pl.when(kk == 0)
            def _():
                acc_ref[...] = part

            @pl.when(jnp.logical_and(kk > 0, kk < nk - 1))
            def _():
                acc_ref[...] += part

            @pl.when(kk == nk - 1)
            def _():
                finish(acc_ref[...] + part)

    if form == "nn":
        a_spec = pl.BlockSpec((tm, tk), lambda i, j, kk: (i, kk))
        b_spec = pl.BlockSpec((tk, tn), lambda i, j, kk: (kk, j))
    elif form == "nt":
        a_spec = pl.BlockSpec((tm, tk), lambda i, j, kk: (i, kk))
        b_spec = pl.BlockSpec((tn, tk), lambda i, j, kk: (j, kk))
    else:
        a_spec = pl.BlockSpec((tk, tm), lambda i, j, kk: (kk, i))
        b_spec = pl.BlockSpec((tk, tn), lambda i, j, kk: (kk, j))
    in_specs = [a_spec, b_spec]
    args = [a, b]
    if bias is not None:
        in_specs.append(pl.BlockSpec((1, tn), lambda i, j, kk: (0, j)))
        args.append(bias.reshape(1, n))
    out_shape = jax.ShapeDtypeStruct((m, n), out_dtype)
    out_spec = pl.BlockSpec((tm, tn), lambda i, j, kk: (i, j))
    grid = (m // tm, n // tn, nk)
    if rider is None:
        return pl.pallas_call(
            body, name=name, out_shape=out_shape, grid=grid, in_specs=in_specs, out_specs=out_spec,
            scratch_shapes=[pltpu.VMEM((tm, tn), F32)],
            compiler_params=_cparams(("parallel", "parallel", "arbitrary")),
        )(*args)

    def at_step(which):
        def test():
            ids = [pl.program_id(ax) for ax in range(3)]
            want = [0, 0, 0] if which == "first" else [g - 1 for g in grid]
            return jnp.logical_and(jnp.logical_and(ids[0] == want[0], ids[1] == want[1]), ids[2] == want[2])
        return test

    n_in = len(args)
    args, in_specs, out_shape, out_specs, scratch, aliases = rider.extend(
        n_in, 1, args, in_specs, [out_shape], [out_spec], [pltpu.VMEM((tm, tn), F32)])
    return pl.pallas_call(
        rider.wrap(body, n_in, 1, at_step("first"), at_step("last")), name=name,
        out_shape=out_shape, grid=grid, in_specs=in_specs, out_specs=out_specs, scratch_shapes=scratch,
        input_output_aliases=aliases,
        compiler_params=_cparams(("arbitrary", "arbitrary", "arbitrary")),
    )(*args)


TOK_TILE = 256


def _vec_spec(d):
    return pl.BlockSpec((1, 1, d), lambda b, s: (b, 0, 0))


def _tok_spec(ts, d):
    return pl.BlockSpec((1, ts, d), lambda b, s: (b, s, 0))


def mod_fwd(x, scale, shift):
    bsz, seq, d = x.shape
    ts = _rows_tile(seq, TOK_TILE)

    def body(x_ref, sc_ref, sh_ref, h_ref):
        h_ref[0] = (x_ref[0] * (1.0 + sc_ref[0]) + sh_ref[0]).astype(BF16)

    return pl.pallas_call(
        body, name="mod_fwd",
        out_shape=jax.ShapeDtypeStruct((bsz, seq, d), BF16),
        grid=(bsz, seq // ts),
        in_specs=[_tok_spec(ts, d), _vec_spec(d), _vec_spec(d)],
        out_specs=_tok_spec(ts, d),
        compiler_params=_cparams(("parallel", "parallel")),
    )(x, scale, shift)


def _ln_stats(z):
    mu = jnp.mean(z, axis=-1, keepdims=True)
    zc = z - mu
    var = jnp.mean(zc * zc, axis=-1, keepdims=True)
    return zc, lax.rsqrt(var + LN_EPS)


def resln_fwd(x, y, gate, g, b, res_w, nxt):
    bsz, seq, d = x.shape
    ts = _rows_tile(seq, TOK_TILE)

    def body(*refs):
        if nxt is None:
            x_ref, y_ref, gt_ref, g_ref, b_ref, xn_ref = refs
        else:
            x_ref, y_ref, gt_ref, g_ref, b_ref, sc_ref, sh_ref, xn_ref, hn_ref = refs
        z = DEEPNORM_ALPHA * x_ref[0] + (res_w * (1.0 + gt_ref[0])) * y_ref[0]
        zc, r = _ln_stats(z)
        xn = zc * r * g_ref[...] + b_ref[...]
        xn_ref[0] = xn
        if nxt is not None:
            hn_ref[0] = (xn * (1.0 + sc_ref[0]) + sh_ref[0]).astype(BF16)

    row = pl.BlockSpec((1, d), lambda bb, s: (0, 0))
    in_specs = [_tok_spec(ts, d), _tok_spec(ts, d), _vec_spec(d), row, row]
    args = [x, y, gate, g.reshape(1, d), b.reshape(1, d)]
    out_shape = [jax.ShapeDtypeStruct((bsz, seq, d), F32)]
    out_specs = [_tok_spec(ts, d)]
    if nxt is not None:
        in_specs += [_vec_spec(d), _vec_spec(d)]
        args += list(nxt)
        out_shape.append(jax.ShapeDtypeStruct((bsz, seq, d), BF16))
        out_specs.append(_tok_spec(ts, d))
    out = pl.pallas_call(
        body, name="resln_fwd",
        out_shape=out_shape, grid=(bsz, seq // ts),
        in_specs=in_specs, out_specs=out_specs,
        compiler_params=_cparams(("parallel", "parallel")),
    )(*args)
    return (out[0], None) if nxt is None else (out[0], out[1])


def resln_bwd(gout, x, y, gate, g, res_w, upstream=None):
    bsz, seq, d = x.shape
    ts = _rows_tile(seq, TOK_TILE)

    def body(*refs):
        if upstream is None:
            go_ref, x_ref, y_ref, gt_ref, g_ref, dxa_ref, dy_ref, dgt_ref, dg_ref, db_ref = refs
        else:
            (go_ref, x_ref, y_ref, gt_ref, g_ref, dh_ref, sc_ref, b_ref,
             dxa_ref, dy_ref, dgt_ref, dg_ref, db_ref, dsc_ref, dsh_ref) = refs
        bb, s = pl.program_id(0), pl.program_id(1)
        yv = y_ref[0]
        rw = res_w * (1.0 + gt_ref[0])
        z = DEEPNORM_ALPHA * x_ref[0] + rw * yv
        zc, r = _ln_stats(z)
        xhat = zc * r
        go = go_ref[0]
        if upstream is not None:
            dh = dh_ref[0].astype(F32)
            go = go + dh * (1.0 + sc_ref[0])
            dsc = jnp.sum(dh * (xhat * g_ref[...] + b_ref[...]), axis=0, keepdims=True)
            dsh = jnp.sum(dh, axis=0, keepdims=True)

            @pl.when(s == 0)
            def _():
                dsc_ref[0] = dsc
                dsh_ref[0] = dsh

            @pl.when(s > 0)
            def _():
                dsc_ref[0] += dsc
                dsh_ref[0] += dsh

        dxh = go * g_ref[...]
        dz = r * (dxh - jnp.mean(dxh, axis=-1, keepdims=True)
                  - xhat * jnp.mean(dxh * xhat, axis=-1, keepdims=True))
        dxa_ref[0] = DEEPNORM_ALPHA * dz
        dy_ref[0] = (rw * dz).astype(BF16)
        dgt = res_w * jnp.sum(dz * yv, axis=0, keepdims=True)
        dg = jnp.sum(go * xhat, axis=0, keepdims=True)
        db = jnp.sum(go, axis=0, keepdims=True)

        @pl.when(s == 0)
        def _():
            dgt_ref[0] = dgt

        @pl.when(s > 0)
        def _():
            dgt_ref[0] += dgt

        first = jnp.logical_and(bb == 0, s == 0)

        @pl.when(first)
        def _():
            dg_ref[...] = dg
            db_ref[...] = db

        @pl.when(jnp.logical_not(first))
        def _():
            dg_ref[...] += dg
            db_ref[...] += db

    row = pl.BlockSpec((1, d), lambda bb, s: (0, 0))
    vec = jax.ShapeDtypeStruct((bsz, 1, d), F32)
    in_specs = [_tok_spec(ts, d), _tok_spec(ts, d), _tok_spec(ts, d), _vec_spec(d), row]
    args = [gout, x, y, gate, g.reshape(1, d)]
    out_shape = [jax.ShapeDtypeStruct((bsz, seq, d), F32), jax.ShapeDtypeStruct((bsz, seq, d), BF16), vec,
                 jax.ShapeDtypeStruct((1, d), F32), jax.ShapeDtypeStruct((1, d), F32)]
    out_specs = [_tok_spec(ts, d), _tok_spec(ts, d), _vec_spec(d), row, row]
    if upstream is not None:
        dh, scale, b = upstream
        in_specs += [_tok_spec(ts, d), _vec_spec(d), row]
        args += [dh, scale, b.reshape(1, d)]
        out_shape += [vec, vec]
        out_specs += [_vec_spec(d), _vec_spec(d)]
    return pl.pallas_call(
        body, name="resln_bwd" if upstream is None else "resln_mod_bwd",
        out_shape=out_shape, grid=(bsz, seq // ts),
        in_specs=in_specs, out_specs=out_specs,
        compiler_params=_cparams(("arbitrary", "arbitrary")),
    )(*args)


def mod_bwd(dxa, dh, x, scale):
    bsz, seq, d = x.shape
    ts = _rows_tile(seq, TOK_TILE)

    def body(dxa_ref, dh_ref, x_ref, sc_ref, dx_ref, dsc_ref, dsh_ref):
        s = pl.program_id(1)
        dh = dh_ref[0].astype(F32)
        dx_ref[0] = dxa_ref[0] + dh * (1.0 + sc_ref[0])
        dsc = jnp.sum(dh * x_ref[0], axis=0, keepdims=True)
        dsh = jnp.sum(dh, axis=0, keepdims=True)

        @pl.when(s == 0)
        def _():
            dsc_ref[0] = dsc
            dsh_ref[0] = dsh

        @pl.when(s > 0)
        def _():
            dsc_ref[0] += dsc
            dsh_ref[0] += dsh

    return pl.pallas_call(
        body, name="mod_bwd",
        out_shape=[jax.ShapeDtypeStruct((bsz, seq, d), F32),
                   jax.ShapeDtypeStruct((bsz, 1, d), F32),
                   jax.ShapeDtypeStruct((bsz, 1, d), F32)],
        grid=(bsz, seq // ts),
        in_specs=[_tok_spec(ts, d), _tok_spec(ts, d), _tok_spec(ts, d), _vec_spec(d)],
        out_specs=[_tok_spec(ts, d), _vec_spec(d), _vec_spec(d)],
        compiler_params=_cparams(("parallel", "arbitrary")),
    )(dxa, dh, x, scale)


FF_HALF = D_FF // 2
FF_CHUNKS = ((0, 384), (384, 384), (768, 384), (1152, 256))


def ffn_up_fused(h2, wgu):
    t, d = h2.shape
    tm = _rows_tile(t, 512)

    def body(h_ref, w_ref, gu_ref, a_ref):
        h = h_ref[...]
        for c0, cw in FF_CHUNKS:
            g = _dot(h, w_ref[:, c0:c0 + cw], NN)
            u = _dot(h, w_ref[:, FF_HALF + c0:FF_HALF + c0 + cw], NN)
            gu_ref[:, c0:c0 + cw] = g.astype(BF16)
            gu_ref[:, FF_HALF + c0:FF_HALF + c0 + cw] = u.astype(BF16)
            a_ref[:, c0:c0 + cw] = (g * _sigmoid(g) * u).astype(BF16)

    return pl.pallas_call(
        body, name="ffn_up_fused",
        out_shape=[jax.ShapeDtypeStruct((t, 2 * D_FF), BF16), jax.ShapeDtypeStruct((t, D_FF), BF16)],
        grid=(2, t // tm),
        in_specs=[pl.BlockSpec((tm, d), lambda j, i: (i, 0)),
                  pl.BlockSpec((d, 2 * FF_HALF), lambda j, i: (0, j))],
        out_specs=[pl.BlockSpec((tm, 2 * FF_HALF), lambda j, i: (i, j)),
                   pl.BlockSpec((tm, FF_HALF), lambda j, i: (i, j))],
        compiler_params=_cparams(("parallel", "parallel")),
    )(h2, wgu)


def ffn_down_bwd_fused(dy2, wd_t, gu):
    t, d = dy2.shape
    tm = _rows_tile(t, 512)

    def body(dy_ref, w_ref, gu_ref, o_ref):
        dy = dy_ref[...]
        for c0, cw in FF_CHUNKS:
            da = _dot(dy, w_ref[:, c0:c0 + cw], NN)
            g = gu_ref[:, c0:c0 + cw].astype(F32)
            u = gu_ref[:, FF_HALF + c0:FF_HALF + c0 + cw].astype(F32)
            sg = _sigmoid(g)
            o_ref[:, c0:c0 + cw] = (da * u * (sg * (1.0 + g * (1.0 - sg)))).astype(BF16)
            o_ref[:, FF_HALF + c0:FF_HALF + c0 + cw] = (da * (g * sg)).astype(BF16)

    return pl.pallas_call(
        body, name="ffn_down_bwd_fused",
        out_shape=jax.ShapeDtypeStruct((t, 2 * D_FF), BF16),
        grid=(2, t // tm),
        in_specs=[pl.BlockSpec((tm, d), lambda j, i: (i, 0)),
                  pl.BlockSpec((d, FF_HALF), lambda j, i: (0, j)),
                  pl.BlockSpec((tm, 2 * FF_HALF), lambda j, i: (i, j))],
        out_specs=pl.BlockSpec((tm, 2 * FF_HALF), lambda j, i: (i, j)),
        compiler_params=_cparams(("parallel", "parallel")),
    )(dy2, wd_t, gu)


def loss_fwd(y, tgt):
    bsz, seq, d = y.shape
    ts = _rows_tile(seq, TOK_TILE)

    def body(y_ref, t_ref, dy_ref, l_ref, acc_ref):
        bb, s = pl.program_id(0), pl.program_id(1)
        e = y_ref[0] - t_ref[0]
        dy_ref[0] = e * (1.0 / d)
        part = jnp.sum((e * e).reshape(ts // 8, 8, d), axis=0)
        first = jnp.logical_and(bb == 0, s == 0)

        @pl.when(first)
        def _():
            acc_ref[...] = part

        @pl.when(jnp.logical_not(first))
        def _():
            acc_ref[...] += part

        @pl.when(jnp.logical_and(bb == pl.num_programs(0) - 1, s == pl.num_programs(1) - 1))
        def _():
            tot = jnp.sum(jnp.sum(acc_ref[...], axis=1, keepdims=True), axis=0, keepdims=True)
            l_ref[...] = jnp.broadcast_to(tot * (0.5 / d), (8, LANES))

    dy, l = pl.pallas_call(
        body, name="loss_fwd",
        out_shape=[jax.ShapeDtypeStruct((bsz, seq, d), F32), jax.ShapeDtypeStruct((8, LANES), F32)],
        grid=(bsz, seq // ts),
        in_specs=[_tok_spec(ts, d), _tok_spec(ts, d)],
        out_specs=[_tok_spec(ts, d), pl.BlockSpec((8, LANES), lambda bb, s: (0, 0))],
        scratch_shapes=[pltpu.VMEM((8, d), F32)],
        compiler_params=_cparams(("arbitrary", "arbitrary")),
    )(y, tgt)
    return l[0, 0], dy


def _rot_half(x):
    lane = lax.broadcasted_iota(jnp.int32, x.shape, 1)
    swapped = jnp.where(lane < 80, pltpu.roll(x, 112, axis=1), pltpu.roll(x, 16, axis=1))
    return jnp.where((lane >= C_NOPE) & (lane < C_NOPE + C_ROPE), swapped, 0.0)


def _rms(x, w):
    r = lax.rsqrt(jnp.mean(x * x, axis=-1, keepdims=True) + RMS_EPS)
    return x * r * w, r


def _rms_bwd(dy, x, w):
    r = lax.rsqrt(jnp.mean(x * x, axis=-1, keepdims=True) + RMS_EPS)
    wd = dy * w
    dx = r * wd - x * (r * r * r) * jnp.mean(x * wd, axis=-1, keepdims=True)
    return dx, jnp.sum(dy * x * r, axis=0, keepdims=True)


def mla_mid_fwd(cqkv, qw, kw, cq_tab, sg_tab):
    bsz, seq, _ = cqkv.shape
    ts = _rows_tile(seq, TOK_TILE)
    r = C_Q_RANK

    def body(x_ref, qw_ref, kw_ref, c_ref, s_ref, nq_ref, nkv_ref, kr_ref):
        x = x_ref[0]
        nq_ref[0] = _rms(x[:, :r], qw_ref[...])[0].astype(BF16)
        nkv_ref[0] = _rms(x[:, r:2 * r], kw_ref[...])[0].astype(BF16)
        xr = x[:, 2 * r:]
        kr_ref[0] = (xr * c_ref[0] + _rot_half(xr) * s_ref[0]).astype(BF16)

    row = pl.BlockSpec((1, r), lambda b, s: (0, 0))
    return pl.pallas_call(
        body, name="mla_mid_fwd",
        out_shape=[jax.ShapeDtypeStruct((bsz, seq, r), BF16), jax.ShapeDtypeStruct((bsz, seq, r), BF16),
                   jax.ShapeDtypeStruct((bsz, seq, LANES), BF16)],
        grid=(bsz, seq // ts),
        in_specs=[_tok_spec(ts, MLA_DOWN_PAD), row, row, _tok_spec(ts, LANES), _tok_spec(ts, LANES)],
        out_specs=[_tok_spec(ts, r), _tok_spec(ts, r), _tok_spec(ts, LANES)],
        compiler_params=_cparams(("parallel", "parallel")),
    )(cqkv, qw.reshape(1, r), kw.reshape(1, r), cq_tab, sg_tab)


def mla_mid_bwd(dnq, dnkv, dkr, cqkv, qw, kw, cq_tab, sg_tab):
    bsz, seq, _ = cqkv.shape
    ts = _rows_tile(seq, TOK_TILE)
    r = C_Q_RANK

    def body(dnq_ref, dnkv_ref, dkr_ref, x_ref, qw_ref, kw_ref, c_ref, s_ref, dx_ref, dqw_ref, dkw_ref):
        bb, s = pl.program_id(0), pl.program_id(1)
        x = x_ref[0]
        dcq, dqw = _rms_bwd(dnq_ref[0].astype(F32), x[:, :r], qw_ref[...])
        dckv, dkw = _rms_bwd(dnkv_ref[0].astype(F32), x[:, r:2 * r], kw_ref[...])
        dk = dkr_ref[0]
        dxr = dk * c_ref[0] + _rot_half(dk * s_ref[0])
        dx_ref[0, :, :r] = dcq.astype(BF16)
        dx_ref[0, :, r:2 * r] = dckv.astype(BF16)
        dx_ref[0, :, 2 * r:] = dxr.astype(BF16)
        first = jnp.logical_and(bb == 0, s == 0)

        @pl.when(first)
        def _():
            dqw_ref[...] = dqw
            dkw_ref[...] = dkw

        @pl.when(jnp.logical_not(first))
        def _():
            dqw_ref[...] += dqw
            dkw_ref[...] += dkw

    row = pl.BlockSpec((1, r), lambda b, s: (0, 0))
    return pl.pallas_call(
        body, name="mla_mid_bwd",
        out_shape=[jax.ShapeDtypeStruct((bsz, seq, MLA_DOWN_PAD), BF16),
                   jax.ShapeDtypeStruct((1, r), F32), jax.ShapeDtypeStruct((1, r), F32)],
        grid=(bsz, seq // ts),
        in_specs=[_tok_spec(ts, r), _tok_spec(ts, r), _tok_spec(ts, LANES), _tok_spec(ts, MLA_DOWN_PAD),
                  row, row, _tok_spec(ts, LANES), _tok_spec(ts, LANES)],
        out_specs=[_tok_spec(ts, MLA_DOWN_PAD), row, row],
        compiler_params=_cparams(("arbitrary", "arbitrary")),
    )(dnq, dnkv, dkr, cqkv, qw.reshape(1, r), kw.reshape(1, r), cq_tab, sg_tab)


def q_rope(q, cq_tab, sg_tab, transpose_rule, name, out_scale=1.0):
    bsz, seq, w = q.shape
    ts = _rows_tile(seq, TOK_TILE)

    def body(q_ref, c_ref, s_ref, o_ref):
        c, s = c_ref[0], s_ref[0]
        for h in range(w // LANES):
            x = q_ref[0, :, h * LANES:(h + 1) * LANES].astype(F32)
            y = x * c + (_rot_half(x * s) if transpose_rule else _rot_half(x) * s)
            o_ref[0, :, h * LANES:(h + 1) * LANES] = (y * out_scale).astype(BF16)

    return pl.pallas_call(
        body, name=name,
        out_shape=jax.ShapeDtypeStruct((bsz, seq, w), BF16),
        grid=(bsz, seq // ts),
        in_specs=[_tok_spec(ts, w), _tok_spec(ts, LANES), _tok_spec(ts, LANES)],
        out_specs=_tok_spec(ts, w),
        compiler_params=_cparams(("parallel", "parallel")),
    )(q, cq_tab, sg_tab)


MLA_SCALE = (C_NOPE + C_ROPE) ** -0.5
LOG2E = math.log2(math.e)
MLA_QSCALE = MLA_SCALE * LOG2E
MLA_TILE = 1024
MLA_HEADS_PER_STEP = 2
NEG_BIG = -1e30


def _eye_mask(n):
    return lax.broadcasted_iota(jnp.int32, (n, n), 0) == lax.broadcasted_iota(jnp.int32, (n, n), 1)


def mla_attn_fwd(qp, kv, kr):
    bsz, seq, _ = qp.shape
    tq = _rows_tile(seq, MLA_TILE)
    nt = seq // tq

    hps = MLA_HEADS_PER_STEP

    def lanes_of(hh):
        return slice(hh * LANES, (hh + 1) * LANES)

    def body(q_ref, kv_ref, kr_ref, o_ref, lse_ref, kc_ref):
        lane = lax.broadcasted_iota(jnp.int32, (seq, LANES), 1)
        for hh in range(hps):
            kc_ref[hh] = jnp.where(lane < C_NOPE, kv_ref[0, :, lanes_of(hh)], kr_ref[0])
        lane_q = lax.broadcasted_iota(jnp.int32, (tq, LANES), 1)

        def q_body(i, carry):
            qrows = pl.ds(pl.multiple_of(i * tq, tq), tq)
            q_i = [q_ref[0, qrows, lanes_of(hh)] for hh in range(hps)]

            def key_body(j, st):
                krows = pl.ds(pl.multiple_of(j * tq, tq), tq)
                out = []
                for hh in range(hps):
                    m, l, acc = st[hh]
                    s = _dot(q_i[hh], kc_ref[hh, krows, :], NT)
                    m_new = jnp.maximum(m, jnp.max(s, axis=-1, keepdims=True))
                    alpha = jnp.exp2(m - m_new)
                    p = jnp.exp2(s - m_new)
                    l = alpha * l + jnp.sum(p, axis=-1, keepdims=True)
                    acc = alpha * acc + _dot(p.astype(BF16), kv_ref[0, krows, lanes_of(hh)], NN)
                    out.append((m_new, l, acc))
                return tuple(out)

            init = tuple((jnp.full((tq, 1), NEG_BIG, F32), jnp.zeros((tq, 1), F32), jnp.zeros((tq, LANES), F32))
                         for _ in range(hps))
            res = lax.fori_loop(0, nt, key_body, init)
            for hh in range(hps):
                m, l, acc = res[hh]
                o_ref[0, qrows, lanes_of(hh)] = jnp.where(lane_q >= C_NOPE, acc * (1.0 / l), 0.0).astype(BF16)
                lse = m + jnp.log2(l)
                lse_ref[0, hh, :, qrows] = jnp.sum(jnp.where(_eye_mask(tq), lse, 0.0), axis=0, keepdims=True)
            return carry

        lax.fori_loop(0, nt, q_body, 0)

    heads = pl.BlockSpec((1, seq, hps * LANES), lambda b, h: (b, 0, h))
    return pl.pallas_call(
        body, name="mla_attn_fwd",
        out_shape=[jax.ShapeDtypeStruct((bsz, seq, MLA_HEAD_PAD), BF16),
                   jax.ShapeDtypeStruct((bsz, C_HEADS, 1, seq), F32)],
        grid=(bsz, C_HEADS // hps),
        in_specs=[heads, heads, pl.BlockSpec((1, seq, LANES), lambda b, h: (b, 0, 0))],
        out_specs=[heads, pl.BlockSpec((1, hps, 1, seq), lambda b, h: (b, h, 0, 0))],
        scratch_shapes=[pltpu.VMEM((hps, seq, LANES), BF16)],
        compiler_params=_cparams(("parallel", "parallel")),
    )(qp, kv, kr)


def mla_attn_bwd(qp, kv, kr, o, do, lse):
    bsz, seq, _ = qp.shape
    tq = _rows_tile(seq, MLA_TILE)
    nt = seq // tq

    def body(q_ref, kv_ref, kr_ref, o_ref, do_ref, lse_ref, dq_ref, dkv_ref, dkr_ref,
             kc_ref, drow_ref, dqa_ref, dkc_ref, dkvv_ref):
        h = pl.program_id(1)
        lane_s = lax.broadcasted_iota(jnp.int32, (seq, LANES), 1)
        lane_t = lax.broadcasted_iota(jnp.int32, (tq, LANES), 1)
        kc_ref[...] = jnp.where(lane_s < C_NOPE, kv_ref[0], kr_ref[0])
        dqa_ref[...] = jnp.zeros_like(dqa_ref)
        ones = jnp.ones((8, LANES), F32)

        def delta_body(i, carry):
            rows = pl.ds(pl.multiple_of(i * tq, tq), tq)
            prod = do_ref[0, rows, :].astype(F32) * o_ref[0, rows, :].astype(F32)
            drow_ref[:, rows] = _dot(ones, prod, NT, precision=HIGHEST)
            return carry

        lax.fori_loop(0, nt, delta_body, 0)

        def key_body(j, carry):
            krows = pl.ds(pl.multiple_of(j * tq, tq), tq)
            kc_j = kc_ref[krows, :]
            kv_j = kv_ref[0, krows, :]
            dkc_ref[...] = jnp.zeros_like(dkc_ref)
            dkvv_ref[...] = jnp.zeros_like(dkvv_ref)

            def q_body(i, c2):
                qrows = pl.ds(pl.multiple_of(i * tq, tq), tq)
                q_i = q_ref[0, qrows, :]
                do_i = do_ref[0, qrows, :]
                st = _dot(kc_j, q_i, NT)
                pt = jnp.exp2(st - lse_ref[0, 0, :, qrows])
                dpt = _dot(kv_j, do_i, NT)
                dst = (pt * (dpt - drow_ref[0:1, qrows])).astype(BF16)
                dkvv_ref[...] += _dot(pt.astype(BF16), do_i, NN)
                dkc_ref[...] += _dot(dst, q_i, NN)
                dqa_ref[qrows, :] += _dot(dst, kc_j, TN)
                return c2

            lax.fori_loop(0, nt, q_body, 0)
            dkc = dkc_ref[...] * (MLA_SCALE / MLA_QSCALE)
            dkv_ref[0, krows, :] = jnp.where(lane_t < C_NOPE, dkc, dkvv_ref[...]).astype(BF16)
            dkr_j = jnp.where(lane_t >= C_NOPE, dkc, 0.0)

            @pl.when(h == 0)
            def _():
                dkr_ref[0, krows, :] = dkr_j

            @pl.when(h > 0)
            def _():
                dkr_ref[0, krows, :] += dkr_j

            return carry

        lax.fori_loop(0, nt, key_body, 0)
        dq_ref[0] = (dqa_ref[...] * MLA_SCALE).astype(BF16)

    head = pl.BlockSpec((1, seq, LANES), lambda b, h: (b, 0, h))
    shared = pl.BlockSpec((1, seq, LANES), lambda b, h: (b, 0, 0))
    return pl.pallas_call(
        body, name="mla_attn_bwd",
        out_shape=[jax.ShapeDtypeStruct((bsz, seq, MLA_HEAD_PAD), BF16),
                   jax.ShapeDtypeStruct((bsz, seq, MLA_HEAD_PAD), BF16),
                   jax.ShapeDtypeStruct((bsz, seq, LANES), F32)],
        grid=(bsz, C_HEADS),
        in_specs=[head, head, shared, head, head,
                  pl.BlockSpec((1, 1, 1, seq), lambda b, h: (b, h, 0, 0))],
        out_specs=[head, head, shared],
        scratch_shapes=[pltpu.VMEM((seq, LANES), BF16), pltpu.VMEM((8, seq), F32),
                        pltpu.VMEM((seq, LANES), F32), pltpu.VMEM((tq, LANES), F32),
                        pltpu.VMEM((tq, LANES), F32)],
        compiler_params=_cparams(("parallel", "arbitrary")),
    )(qp, kv, kr, o, do, lse)


A_SCALE = A_HEAD_DIM ** -0.5
A_GROUP = A_HEADS // A_KV_HEADS
A_BAND = 3 * A_BLOCK


def _wattn_block(i, seq, posc_ref, posr_ref):
    start = pl.multiple_of(i * A_BLOCK, A_BLOCK)
    pq = jnp.concatenate([posc_ref[0]] * A_GROUP, axis=0)
    pk = posr_ref[0, :, pl.ds(start, A_BAND)]
    dist = jnp.abs(pq - pk).astype(F32)
    shape = (A_GROUP * A_BLOCK, A_BAND)
    qi = (lax.broadcasted_iota(jnp.int32, shape, 0) & (A_BLOCK - 1)) + A_BLOCK
    ki = lax.broadcasted_iota(jnp.int32, shape, 1)
    absk = i * A_BLOCK + ki - A_BLOCK
    valid = (jnp.abs(qi - ki) <= WINDOW) & (absk >= 0) & (absk < seq)
    return start, dist, valid


def _wattn_probs(q4, kb, dist, valid, slope4, sink4):
    s = _dot(q4, kb, NT) * A_SCALE - slope4 * dist
    s = jnp.where(valid, s, NEG_BIG)
    m = jnp.maximum(jnp.max(s, axis=-1, keepdims=True), sink4)
    p = jnp.exp(s - m)
    es = jnp.exp(sink4 - m)
    inv = 1.0 / (jnp.sum(p, axis=-1, keepdims=True) + es)
    return p * inv, es * inv


def _alibi_slope(h):
    return 2.0 ** (-8.0 * (h + 1) / A_HEADS)


def _group_heads(g):
    return range(g * A_GROUP, (g + 1) * A_GROUP)


def _stack_heads(ref, g, dtype):
    return jnp.concatenate([ref[0, :, h * LANES:(h + 1) * LANES].astype(dtype) for h in _group_heads(g)], axis=0)


def _per_head_column(values):
    return jnp.concatenate([jnp.full((A_BLOCK, 1), v, F32) for v in values], axis=0)


def wattn_fwd(proj, kp, vp, posc, posr, sink):
    bsz, seq, _ = proj.shape
    nb = seq // A_BLOCK
    qw = A_HEADS * LANES

    def body(sink_ref, q_ref, kp_ref, vp_ref, posc_ref, posr_ref, o_ref):
        i = pl.program_id(1)
        start, dist, valid = _wattn_block(i, seq, posc_ref, posr_ref)
        kb = kp_ref[0, pl.ds(start, A_BAND), :]
        vb = vp_ref[0, pl.ds(start, A_BAND), :]
        lane = lax.broadcasted_iota(jnp.int32, (A_GROUP * A_BLOCK, LANES), 1)
        for g in range(A_KV_HEADS):
            heads = _group_heads(g)
            p, _ = _wattn_probs(_stack_heads(q_ref, g, BF16), kb, dist, valid,
                                _per_head_column([_alibi_slope(h) for h in heads]),
                                _per_head_column([sink_ref[h] for h in heads]))
            o = _dot(p.astype(BF16), vb, NN)
            mine = (lane >= A_HEAD_DIM) if g == 1 else (lane < A_HEAD_DIM)
            o = jnp.where(mine, o, 0.0).astype(BF16)
            for j, h in enumerate(heads):
                o_ref[0, :, h * LANES:(h + 1) * LANES] = o[j * A_BLOCK:(j + 1) * A_BLOCK]

    return pl.pallas_call(
        body, name="wattn_fwd",
        out_shape=jax.ShapeDtypeStruct((bsz, seq, MO_PAD), BF16),
        grid=(bsz, nb),
        in_specs=[pl.BlockSpec(memory_space=pltpu.SMEM),
                  pl.BlockSpec((1, A_BLOCK, qw), lambda b, i: (b, i, 0)),
                  pl.BlockSpec((1, seq + 2 * A_BLOCK, LANES), lambda b, i: (b, 0, 0)),
                  pl.BlockSpec((1, seq + 2 * A_BLOCK, LANES), lambda b, i: (b, 0, 0)),
                  pl.BlockSpec((1, A_BLOCK, 1), lambda b, i: (b, i, 0)),
                  pl.BlockSpec((1, 1, seq + 2 * A_BLOCK), lambda b, i: (b, 0, 0))],
        out_specs=pl.BlockSpec((1, A_BLOCK, qw), lambda b, i: (b, i, 0)),
        compiler_params=_cparams(("parallel", "parallel")),
    )(sink, proj, kp, vp, posc, posr)


def wattn_bwd(proj, kp, vp, posc, posr, sink, dmo):
    bsz, seq, _ = proj.shape
    nb = seq // A_BLOCK
    qw = A_HEADS * LANES
    sp = seq + 2 * A_BLOCK

    def body(sink_ref, q_ref, kp_ref, vp_ref, posc_ref, posr_ref, do_ref,
             dq_ref, dk_ref, dv_ref, ds_ref):
        i = pl.program_id(1)

        @pl.when(i == 0)
        def _():
            dk_ref[...] = jnp.zeros_like(dk_ref)
            dv_ref[...] = jnp.zeros_like(dv_ref)

        @pl.when(jnp.logical_and(i == 0, pl.program_id(0) == 0))
        def _():
            ds_ref[...] = jnp.zeros_like(ds_ref)

        start, dist, valid = _wattn_block(i, seq, posc_ref, posr_ref)
        kb = kp_ref[0, pl.ds(start, A_BAND), :]
        vb = vp_ref[0, pl.ds(start, A_BAND), :]
        lane = lax.broadcasted_iota(jnp.int32, (A_GROUP * A_BLOCK, LANES), 1)
        lane1 = lax.broadcasted_iota(jnp.int32, (1, LANES), 1)
        dk_acc = jnp.zeros((A_BAND, LANES), F32)
        dv_acc = jnp.zeros((A_BAND, LANES), F32)
        dsink = jnp.zeros((1, LANES), F32)
        for g in range(A_KV_HEADS):
            heads = _group_heads(g)
            q4 = _stack_heads(q_ref, g, BF16)
            p, psink = _wattn_probs(q4, kb, dist, valid,
                                    _per_head_column([_alibi_slope(h) for h in heads]),
                                    _per_head_column([sink_ref[h] for h in heads]))
            pb = p.astype(BF16)
            do = _stack_heads(do_ref, g, F32)
            dob = do.astype(BF16)
            mine = (lane >= A_HEAD_DIM) if g == 1 else (lane < A_HEAD_DIM)
            o = jnp.where(mine, _dot(pb, vb, NN), 0.0)
            delta = jnp.sum(do * o, axis=-1, keepdims=True)
            dp = _dot(dob, vb, NT)
            ds = (p * (dp - delta) * A_SCALE).astype(BF16)
            dq = _dot(ds, kb, NN).astype(BF16)
            dk_acc = dk_acc + _dot(ds, q4, TN)
            dv_acc = dv_acc + _dot(pb, dob, TN)
            sd = psink * delta
            for j, h in enumerate(heads):
                dq_ref[0, :, h * LANES:(h + 1) * LANES] = dq[j * A_BLOCK:(j + 1) * A_BLOCK]
                dsh = -jnp.sum(sd[j * A_BLOCK:(j + 1) * A_BLOCK], axis=0, keepdims=True)
                dsink = dsink + jnp.where(lane1 == h, dsh, 0.0)
        dk_ref[0, pl.ds(start, A_BAND), :] += dk_acc
        dv_ref[0, pl.ds(start, A_BAND), :] += dv_acc
        ds_ref[0] += dsink

    full = pl.BlockSpec((1, sp, LANES), lambda b, i: (b, 0, 0))
    return pl.pallas_call(
        body, name="wattn_bwd",
        out_shape=[jax.ShapeDtypeStruct((bsz, seq, qw), BF16),
                   jax.ShapeDtypeStruct((bsz, sp, LANES), F32),
                   jax.ShapeDtypeStruct((bsz, sp, LANES), F32),
                   jax.ShapeDtypeStruct((1, 1, LANES), F32)],
        grid=(bsz, nb),
        in_specs=[pl.BlockSpec(memory_space=pltpu.SMEM),
                  pl.BlockSpec((1, A_BLOCK, qw), lambda b, i: (b, i, 0)),
                  full, full,
                  pl.BlockSpec((1, A_BLOCK, 1), lambda b, i: (b, i, 0)),
                  pl.BlockSpec((1, 1, sp), lambda b, i: (b, 0, 0)),
                  pl.BlockSpec((1, A_BLOCK, qw), lambda b, i: (b, i, 0))],
        out_specs=[pl.BlockSpec((1, A_BLOCK, qw), lambda b, i: (b, i, 0)), full, full,
                   pl.BlockSpec((1, 1, LANES), lambda b, i: (0, 0, 0))],
        compiler_params=_cparams(("arbitrary", "arbitrary")),
    )(sink, proj, kp, vp, posc, posr, dmo)


HG_Q, HG_FF, HG_FB, HG_I, HG_G = 10, 14, 18, 22, 26
HG_OUT = 8
CH = B_CHUNK
HG_GROUP = 4
GR = HG_GROUP * CH


def _hgrn_consts(reverse):
    r = lax.broadcasted_iota(jnp.int32, (GR, GR), 0)
    c = lax.broadcasted_iota(jnp.int32, (GR, GR), 1)
    same = jnp.right_shift(r, 6) == jnp.right_shift(c, 6)
    incl = same & ((c >= r) if reverse else (c <= r))
    row = lax.broadcasted_iota(jnp.int32, (HG_GROUP, CH, LANES), 1)
    mid = CH // 2 if reverse else CH // 2 - 1
    end = 0 if reverse else CH - 1
    return incl, row == mid, row == end


def _per_chunk(x, sel=None):
    x3 = x.reshape(HG_GROUP, CH, LANES)
    return jnp.sum(x3 if sel is None else jnp.where(sel, x3, 0.0), axis=1, keepdims=True)


def _to_rows(x3):
    return jnp.broadcast_to(x3, (HG_GROUP, CH, LANES)).reshape(GR, LANES)


def _chunk_cumsum(x, reverse):
    pos = lax.broadcasted_iota(jnp.int32, (GR, LANES), 0) & (CH - 1)
    s = 1
    while s < CH:
        if reverse:
            x = x + jnp.where(pos < CH - s, pltpu.roll(x, GR - s, axis=0), 0.0)
        else:
            x = x + jnp.where(pos >= s, pltpu.roll(x, s, axis=0), 0.0)
        s *= 2
    return x


def _block_diag(x):
    chunk = jnp.right_shift(lax.broadcasted_iota(jnp.int32, (GR, LANES), 0), 6)
    return jnp.concatenate([jnp.where(chunk == n, x, jnp.zeros_like(x)) for n in range(HG_GROUP)], axis=1)


def _hgrn_gates(hf, lb):
    sig = _sigmoid(hf)
    f = lb + (1.0 - lb) * sig
    return 1.0 - f, jnp.log(f), sig, f


def _hgrn_decays(lf, consts, reverse):
    _, is_mid, is_end = consts
    b = _chunk_cumsum(lf, reverse)
    return b, _per_chunk(b, is_mid), _per_chunk(b, is_end)


def _chunk_order(reverse):
    return range(HG_GROUP - 1, -1, -1) if reverse else range(HG_GROUP)


def _lane_block(x, n):
    return x[:, n * LANES:(n + 1) * LANES]


def _hgrn_states(vb, kd, dec3, st, reverse):
    inc = _dot(vb, _block_diag(kd), TN)
    entering = [None] * HG_GROUP
    for n in _chunk_order(reverse):
        entering[n] = st
        st = dec3[n] * st + _lane_block(inc, n)
    return entering, st


def _hgrn_group(q, k, v, lf, st, consts, reverse):
    b, bm3, be3 = _hgrn_decays(lf, consts, reverse)
    bm, be = _to_rows(bm3), _to_rows(be3)
    qs = (q * jnp.exp(b - bm)).astype(BF16)
    ks = (k * jnp.exp(bm - b)).astype(BF16)
    a = jnp.where(consts[0], _dot(qs, ks, NT), 0.0).astype(BF16)
    qe = (q * jnp.exp(b)).astype(BF16)
    vb = v.astype(BF16)
    kd = (k * jnp.exp(be - b)).astype(BF16)
    entering, st = _hgrn_states(vb, kd, jnp.exp(be3), st, reverse)
    s_all = jnp.concatenate([e.astype(BF16) for e in entering], axis=1)
    return _dot(a, vb, NN) + _dot(_block_diag(qe), s_all, NT), st


def _silu(x):
    return x * _sigmoid(x)


def hgrn_fwd(proj, mo, lb, nw):
    bsz, seq, _ = proj.shape
    n_gr = seq // GR
    ts = _rows_tile(seq, 512)

    def body(hq_ref, hff_ref, hfb_ref, hi_ref, hg_ref, mo_in_ref, lb_ref, nw_ref, y_ref, ot_ref):
        del mo_in_ref
        lb_v, nw_v = lb_ref[0], nw_ref[0]

        def run(hf_ref, reverse, first):
            consts = _hgrn_consts(reverse)

            def step(t, st):
                n = (n_gr - 1 - t) if reverse else t
                rows = pl.ds(pl.multiple_of(n * GR, GR), GR)
                k, lf, _, _ = _hgrn_gates(hf_ref[0, rows, :], lb_v)
                o, st = _hgrn_group(_silu(hq_ref[0, rows, :]), k, hi_ref[0, rows, :], lf, st, consts, reverse)
                if first:
                    ot_ref[0, rows, :] = o
                else:
                    ot_ref[0, rows, :] += o
                return st

            lax.fori_loop(0, n_gr, step, jnp.zeros((LANES, LANES), F32))

        run(hff_ref, False, True)
        run(hfb_ref, True, False)

        def finish(i, carry):
            rows = pl.ds(pl.multiple_of(i * ts, ts), ts)
            o = ot_ref[0, rows, :]
            r = lax.rsqrt(jnp.mean(o * o, axis=-1, keepdims=True) + RMS_EPS)
            y_ref[0, rows, :] = (o * r * nw_v * _silu(hg_ref[0, rows, :])).astype(BF16)
            return carry

        lax.fori_loop(0, seq // ts, finish, 0)

    def col(g):
        return pl.BlockSpec((1, seq, LANES), lambda b, h: (b, 0, g + h))

    par = pl.BlockSpec((1, 1, LANES), lambda b, h: (h, 0, 0))
    return pl.pallas_call(
        body, name="hgrn_fwd",
        out_shape=[jax.ShapeDtypeStruct(mo.shape, BF16),
                   jax.ShapeDtypeStruct((bsz, seq, B_HEADS * LANES), F32)],
        grid=(bsz, B_HEADS),
        in_specs=[col(HG_Q), col(HG_FF), col(HG_FB), col(HG_I), col(HG_G),
                  pl.BlockSpec(memory_space=pl.ANY), par, par],
        out_specs=[col(HG_OUT), col(0)],
        input_output_aliases={5: 0},
        compiler_params=_cparams(("parallel", "parallel")),
    )(proj, proj, proj, proj, proj, mo, lb, nw)


def hgrn_out_bwd(proj, otot, dmo, nw):
    bsz, seq, _ = proj.shape
    ts = _rows_tile(seq, 512)

    def body(hg_ref, ot_ref, dmo_ref, nw_ref, do_ref, dhg_ref, dnw_ref):
        nw_v = nw_ref[0]
        dy = dmo_ref[0].astype(F32)
        hg = hg_ref[0]
        o = ot_ref[0]
        sg = _sigmoid(hg)
        r = lax.rsqrt(jnp.mean(o * o, axis=-1, keepdims=True) + RMS_EPS)
        dhg_ref[0] = (dy * (o * r * nw_v) * (sg * (1.0 + hg * (1.0 - sg)))).astype(BF16)
        drn = dy * (hg * sg)
        dnw = jnp.sum(drn * o * r, axis=0, keepdims=True)
        wd = drn * nw_v
        do_ref[0] = r * wd - o * (r * r * r) * jnp.mean(o * wd, axis=-1, keepdims=True)
        first = jnp.logical_and(pl.program_id(1) == 0, pl.program_id(2) == 0)

        @pl.when(first)
        def _():
            dnw_ref[0] = dnw

        @pl.when(jnp.logical_not(first))
        def _():
            dnw_ref[0] += dnw

    def col(g):
        return pl.BlockSpec((1, ts, LANES), lambda h, b, s: (b, s, g + h))

    par = pl.BlockSpec((1, 1, LANES), lambda h, b, s: (h, 0, 0))
    return pl.pallas_call(
        body, name="hgrn_out_bwd",
        out_shape=[jax.ShapeDtypeStruct((bsz, seq, B_HEADS * LANES), F32),
                   jax.ShapeDtypeStruct((bsz, seq, B_HEADS * LANES), BF16),
                   jax.ShapeDtypeStruct((B_HEADS, 1, LANES), F32)],
        grid=(B_HEADS, bsz, seq // ts),
        in_specs=[col(HG_G), col(0), col(HG_OUT), par],
        out_specs=[col(0), col(0), par],
        compiler_params=_cparams(("parallel", "arbitrary", "arbitrary")),
    )(proj, otot, dmo, nw)


def hgrn_bwd(proj, do, lb, rider=None):
    bsz, seq, _ = proj.shape
    n_ch = seq // CH
    n_gr = seq // GR
    ts = _rows_tile(seq, 512)

    def body(hq_ref, hff_ref, hfb_ref, hi_ref, do_scr, lb_ref,
             dhq_ref, dhff_ref, dhfb_ref, dhi_ref, dlb_ref, st_scr, dq_scr, dv_scr):
        lb_v = lb_ref[0]
        do_scr = do_scr.at[0]

        def run(hf_ref, dhf_ref, reverse, first, dlb0):
            consts = _hgrn_consts(reverse)
            incl, is_mid, is_end = consts

            def load(n):
                rows = pl.ds(pl.multiple_of(n * GR, GR), GR)
                hf = hf_ref[0, rows, :]
                k, lf, sig, f = _hgrn_gates(hf, lb_v)
                return rows, _silu(hq_ref[0, rows, :]), k, hi_ref[0, rows, :], lf, sig, f

            def fwd_step(t, st):
                n = (n_gr - 1 - t) if reverse else t
                _, _, k, v, lf, _, _ = load(n)
                b, _, be3 = _hgrn_decays(lf, consts, reverse)
                kd = (k * jnp.exp(_to_rows(be3) - b)).astype(BF16)
                entering, st = _hgrn_states(v.astype(BF16), kd, jnp.exp(be3), st, reverse)
                for c in range(HG_GROUP):
                    st_scr[n * HG_GROUP + c] = entering[c].astype(BF16)
                return st

            lax.fori_loop(0, n_gr, fwd_step, jnp.zeros((LANES, LANES), F32))

            def bwd_step(t, carry):
                gt, dlb = carry
                n = t if reverse else (n_gr - 1 - t)
                rows, q, k, v, lf, sig, f = load(n)
                do_c = do_scr[rows, :].astype(BF16)
                b, bm3, be3 = _hgrn_decays(lf, consts, reverse)
                bm, be = _to_rows(bm3), _to_rows(be3)
                e_qs, e_ks, e_q, e_kd = jnp.exp(b - bm), jnp.exp(bm - b), jnp.exp(b), jnp.exp(be - b)
                dec3 = jnp.exp(be3)
                qs, ks, qe, kd = q * e_qs, k * e_ks, q * e_q, k * e_kd
                qs_b, ks_b, qe_b, kd_b = qs.astype(BF16), ks.astype(BF16), qe.astype(BF16), kd.astype(BF16)
                vb = v.astype(BF16)
                a = jnp.where(incl, _dot(qs_b, ks_b, NT), 0.0).astype(BF16)
                da = jnp.where(incl, _dot(do_c, vb, NT), 0.0).astype(BF16)
                dv = _dot(a, do_c, TN)
                dqs = _dot(da, ks_b, NN)
                dks = _dot(da, qs_b, TN)
                zed = _dot(do_c, _block_diag(qe_b), TN)
                sts = [st_scr[n * HG_GROUP + c] for c in range(HG_GROUP)]
                gts, ddec = [None] * HG_GROUP, [None] * HG_GROUP
                for c in reversed(list(_chunk_order(reverse))):
                    gts[c] = gt
                    ddec[c] = jnp.sum(gt * sts[c].astype(F32), axis=0, keepdims=True)
                    gt = dec3[c] * gt + _lane_block(zed, c)
                g_b = [g.astype(BF16) for g in gts]
                dqe = _dot(_block_diag(do_c), jnp.concatenate(sts, axis=0), NN)
                dkd = _dot(_block_diag(vb), jnp.concatenate(g_b, axis=0), NN)
                dv = dv + _dot(_block_diag(kd_b), jnp.concatenate(g_b, axis=1), NT)
                ddec3 = jnp.stack(ddec, axis=0)
                dq = dqs * e_qs + dqe * e_q
                dk = dks * e_ks + dkd * e_kd
                t_qs, t_ks, t_kd = dqs * qs, dks * ks, dkd * kd
                db = (t_qs - t_ks + dqe * qe - t_kd).reshape(HG_GROUP, CH, LANES)
                dbm3 = _per_chunk(t_ks - t_qs)
                dbe3 = _per_chunk(t_kd) + ddec3 * dec3
                db = (db + jnp.where(is_mid, dbm3, 0.0) + jnp.where(is_end, dbe3, 0.0)).reshape(GR, LANES)
                dlf = _chunk_cumsum(db, not reverse)
                df = dlf / f - dk
                dhf_ref[0, rows, :] = (df * (1.0 - lb_v) * sig * (1.0 - sig)).astype(BF16)
                dlb = dlb + jnp.sum(df * (1.0 - sig), axis=0, keepdims=True)
                if first:
                    dq_scr[rows, :] = dq
                    dv_scr[rows, :] = dv
                else:
                    dq_scr[rows, :] += dq
                    dv_scr[rows, :] += dv
                return gt, dlb

            return lax.fori_loop(0, n_gr, bwd_step, (jnp.zeros((LANES, LANES), F32), dlb0))[1]

        dlb = run(hff_ref, dhff_ref, False, True, jnp.zeros((1, LANES), F32))
        dlb = run(hfb_ref, dhfb_ref, True, False, dlb)

        @pl.when(pl.program_id(1) == 0)
        def _():
            dlb_ref[0] = dlb

        @pl.when(pl.program_id(1) > 0)
        def _():
            dlb_ref[0] += dlb

        def finish(i, carry):
            rows = pl.ds(pl.multiple_of(i * ts, ts), ts)
            hq = hq_ref[0, rows, :]
            sq = _sigmoid(hq)
            dhq_ref[0, rows, :] = (dq_scr[rows, :] * (sq * (1.0 + hq * (1.0 - sq)))).astype(BF16)
            dhi_ref[0, rows, :] = dv_scr[rows, :].astype(BF16)
            return carry

        lax.fori_loop(0, seq // ts, finish, 0)

    def col(g):
        return pl.BlockSpec((1, seq, LANES), lambda h, b: (b, 0, g + h))

    par = pl.BlockSpec((1, 1, LANES), lambda h, b: (h, 0, 0))
    wide = jax.ShapeDtypeStruct((bsz, seq, B_HEADS * LANES), BF16)
    small = jax.ShapeDtypeStruct((B_HEADS, 1, LANES), F32)
    args = [proj, proj, proj, proj, do, lb]
    in_specs = [col(HG_Q), col(HG_FF), col(HG_FB), col(HG_I), col(0), par]
    out_shape = [wide, wide, wide, wide, small]
    out_specs = [col(0), col(0), col(0), col(0), par]
    scratch = [pltpu.VMEM((n_ch, LANES, LANES), BF16), pltpu.VMEM((seq, LANES), F32), pltpu.VMEM((seq, LANES), F32)]
    aliases = {}
    if rider is not None:
        def first():
            return jnp.logical_and(pl.program_id(0) == 0, pl.program_id(1) == 0)

        def last():
            return jnp.logical_and(pl.program_id(0) == B_HEADS - 1, pl.program_id(1) == bsz - 1)

        body = rider.wrap(body, len(args), len(out_shape), first, last)
        args, in_specs, out_shape, out_specs, scratch, aliases = rider.extend(
            len(args), len(out_shape), args, in_specs, out_shape, out_specs, scratch)
    return pl.pallas_call(
        body, name="hgrn_bwd", out_shape=out_shape, grid=(B_HEADS, bsz),
        in_specs=in_specs, out_specs=out_specs, scratch_shapes=scratch, input_output_aliases=aliases,
        compiler_params=_cparams(("arbitrary", "arbitrary")),
    )(*args)


def _whole(shape):
    return pl.BlockSpec(shape, lambda: (0,) * len(shape))


def silu_small(x):
    def body(x_ref, o_ref):
        o_ref[...] = _silu(x_ref[...])

    return pl.pallas_call(body, name="silu_small", out_shape=jax.ShapeDtypeStruct(x.shape, F32),
                          in_specs=[_whole(x.shape)], out_specs=_whole(x.shape))(x)


def _softmax_rows(x):
    e = jnp.exp(x - jnp.max(x, axis=0, keepdims=True))
    return e / jnp.sum(e, axis=0, keepdims=True)


def lb_fwd(logits, layer):
    n, w = logits.shape

    def body(x_ref, o_ref):
        p = _softmax_rows(x_ref[...])
        row = lax.broadcasted_iota(jnp.int32, (n, w), 0)
        o_ref[...] = jnp.sum(jnp.where(row <= layer, p, 0.0), axis=0, keepdims=True)

    return pl.pallas_call(body, name="lb_fwd", out_shape=jax.ShapeDtypeStruct((1, w), F32),
                          in_specs=[_whole((n, w))], out_specs=_whole((1, w)))(logits)


def lb_bwd(logits, dlb, layer):
    n, w = logits.shape

    def body(x_ref, d_ref, o_ref):
        p = _softmax_rows(x_ref[...])
        row = lax.broadcasted_iota(jnp.int32, (n, w), 0)
        dp = jnp.where(row <= layer, d_ref[...], 0.0)
        o_ref[...] = p * (dp - jnp.sum(p * dp, axis=0, keepdims=True))

    return pl.pallas_call(body, name="lb_bwd", out_shape=jax.ShapeDtypeStruct((n, w), F32),
                          in_specs=[_whole((n, w)), _whole((1, w))], out_specs=_whole((n, w)))(logits, dlb)


def sum_parts(x, name):
    p, r, c = x.shape
    tr = _rows_tile(r, max(8, (1 << 23) // (4 * c * p) // 8 * 8))

    def body(x_ref, o_ref):
        acc = x_ref[0].astype(F32)
        for j in range(1, p):
            acc = acc + x_ref[j].astype(F32)
        o_ref[...] = acc

    return pl.pallas_call(
        body, name=name, out_shape=jax.ShapeDtypeStruct((r, c), F32),
        grid=(r // tr,),
        in_specs=[pl.BlockSpec((p, tr, c), lambda i: (0, i, 0))],
        out_specs=pl.BlockSpec((tr, c), lambda i: (i, 0)),
        compiler_params=_cparams(("parallel",)),
    )(x)


def adamw(w, g, m, v, name):
    r, c = w.shape
    tr = _rows_tile(r, max(8, (1 << 20) // (4 * c) // 8 * 8))
    c1 = 1.0 - ADAM_B1 ** ADAM_STEP
    c2 = 1.0 - ADAM_B2 ** ADAM_STEP

    def body(w_ref, g_ref, m_ref, v_ref, d_ref, nm_ref, nv_ref):
        g = g_ref[...]
        nm = ADAM_B1 * m_ref[...] + (1.0 - ADAM_B1) * g
        nv = ADAM_B2 * v_ref[...] + (1.0 - ADAM_B2) * (g * g)
        nm_ref[...] = nm
        nv_ref[...] = nv
        d_ref[...] = -ADAM_LR * ((nm / c1) / (jnp.sqrt(nv / c2) + ADAM_EPS) + ADAM_WD * w_ref[...])

    spec = pl.BlockSpec((tr, c), lambda i: (i, 0))
    sds = jax.ShapeDtypeStruct((r, c), F32)
    return pl.pallas_call(
        body, name=name, out_shape=[sds, sds, sds], grid=(r // tr,),
        in_specs=[spec, spec, spec, spec], out_specs=[spec, spec, spec],
        compiler_params=_cparams(("parallel",)),
    )(w, g, m, v)


ANY = pl.BlockSpec(memory_space=pl.ANY)


def _my_place():
    return lax.axis_index("x"), lax.axis_index("y"), lax.axis_index("c")


def all_gather(x, name):
    def body(x_ref, out_ref, send_sems, recv_sems, local_sem):
        mx, my, mc = _my_place()
        me, sibling = (mx, my, mc), (mx, my, 1 - mc)
        chips = [(1 - mx, my), (mx, 1 - my), (1 - mx, 1 - my)]

        def slot(px, py, pc):
            return out_ref.at[4 * px + 2 * py + pc]

        def copy(k, block, to, src=None):
            return pltpu.make_async_remote_copy(
                src_ref=slot(*block) if src is None else src, dst_ref=slot(*block),
                send_sem=send_sems.at[k], recv_sem=recv_sems.at[k],
                device_id=to, device_id_type=MESH_ID)

        mine = pltpu.make_async_copy(x_ref, slot(*me), local_sem)
        mine.start()
        first = [copy(0, me, sibling, src=x_ref)]
        first += [copy(1 + j, me, (*chip, mc), src=x_ref) for j, chip in enumerate(chips)]
        for cp in first:
            cp.start()
        passed = [copy(4 + j, (*chip, mc), sibling) for j, chip in enumerate(chips)]
        for j, chip in enumerate(chips):
            copy(1 + j, (*chip, mc), me).wait_recv()
            passed[j].start()
        copy(0, sibling, me).wait_recv()
        for j, chip in enumerate(chips):
            copy(4 + j, (*chip, 1 - mc), me).wait_recv()
        for cp in first + passed:
            cp.wait_send()
        mine.wait()

    return pl.pallas_call(
        body, name=name,
        out_shape=jax.ShapeDtypeStruct((N_DEV,) + x.shape, x.dtype),
        in_specs=[ANY], out_specs=ANY,
        scratch_shapes=[pltpu.SemaphoreType.DMA((7,)), pltpu.SemaphoreType.DMA((7,)),
                        pltpu.SemaphoreType.DMA(())],
    )(x)


def swap_with_sibling(x, name):
    def body(x_ref, out_ref, send_sem, recv_sem):
        mx, my, mc = _my_place()
        cp = pltpu.make_async_remote_copy(
            src_ref=x_ref.at[1 - mc], dst_ref=out_ref, send_sem=send_sem, recv_sem=recv_sem,
            device_id=(mx, my, 1 - mc), device_id_type=MESH_ID)
        cp.start()
        cp.wait()

    return pl.pallas_call(
        body, name=name,
        out_shape=jax.ShapeDtypeStruct(x.shape[1:], x.dtype),
        in_specs=[ANY], out_specs=ANY,
        scratch_shapes=[pltpu.SemaphoreType.DMA(()), pltpu.SemaphoreType.DMA(())],
    )(x)


def pair_add(x, r, mc):
    _, n, rows, c = x.shape
    tr = _rows_tile(rows, 512)

    def body(mc_ref, x_ref, r_ref, o_ref):
        del mc_ref
        o_ref[0] = (x_ref[0, 0].astype(F32) + r_ref[0].astype(F32)).astype(BF16)

    return pl.pallas_call(
        body, name="pair_add",
        out_shape=jax.ShapeDtypeStruct((n, rows, c), BF16),
        grid_spec=pltpu.PrefetchScalarGridSpec(
            num_scalar_prefetch=1, grid=(n, rows // tr),
            in_specs=[pl.BlockSpec((1, 1, tr, c), lambda j, i, s: (s[0], j, i, 0)),
                      pl.BlockSpec((1, tr, c), lambda j, i, s: (j, i, 0))],
            out_specs=pl.BlockSpec((1, tr, c), lambda j, i, s: (j, i, 0))),
        compiler_params=_cparams(("parallel", "parallel")),
    )(mc.reshape(1).astype(jnp.int32), x, r)


def chip_all_to_all(x, name):
    def body(x_ref, out_ref, send_sems, recv_sems, local_sem):
        mx, my, mc = _my_place()
        me = 2 * mx + my
        mine = pltpu.make_async_copy(x_ref.at[me], out_ref.at[me], local_sem)
        mine.start()
        copies = []
        for k, (fx, fy) in enumerate(((1, 0), (0, 1), (1, 1))):
            px = 1 - mx if fx else mx
            py = 1 - my if fy else my
            peer = 2 * px + py
            cp = pltpu.make_async_remote_copy(
                src_ref=x_ref.at[peer], dst_ref=out_ref.at[me],
                send_sem=send_sems.at[k], recv_sem=recv_sems.at[k],
                device_id=(px, py, mc), device_id_type=MESH_ID)
            cp.start()
            copies.append((cp, peer, (px, py, mc), k))
        for cp, peer, to, k in copies:
            pltpu.make_async_remote_copy(
                src_ref=x_ref.at[peer], dst_ref=out_ref.at[peer],
                send_sem=send_sems.at[k], recv_sem=recv_sems.at[k],
                device_id=to, device_id_type=MESH_ID).wait_recv()
        for cp, _, _, _ in copies:
            cp.wait_send()
        mine.wait()

    return pl.pallas_call(
        body, name=name,
        out_shape=jax.ShapeDtypeStruct(x.shape, x.dtype),
        in_specs=[ANY], out_specs=ANY,
        scratch_shapes=[pltpu.SemaphoreType.DMA((3,)), pltpu.SemaphoreType.DMA((3,)),
                        pltpu.SemaphoreType.DMA(())],
    )(x)


ROW_W = D_MODEL
FF_SH = D_FF // N_DEV
FFN_PER_LAYER = 2
MDOWN_SEC_ROWS = 80
MDOWN_ROWS = 128 * 544 // ROW_W
PART_SECS = (
    (("gate", FFN_PER_LAYER * FF_SH), ("up", FFN_PER_LAYER * FF_SH), ("down", FFN_PER_LAYER * FF_SH),
     ("hin", 416), ("hout", 128)),
    (("gate", FFN_PER_LAYER * FF_SH), ("up", FFN_PER_LAYER * FF_SH), ("down", FFN_PER_LAYER * FF_SH),
     ("mout", 128), ("mdown", MDOWN_SEC_ROWS), ("uq", 48), ("ukv", 64)),
)
SEC, PART_ROWS = [], []
for _secs in PART_SECS:
    _table, _off = {}, 0
    for _name, _rows in _secs:
        _table[_name] = (_off, _rows)
        _off += _rows
    SEC.append(_table)
    PART_ROWS.append(_off)
A_Q_W = A_HEADS * A_HEAD_DIM
_KVHEAD = (np.arange(A_HEADS) // A_GROUP).reshape(A_HEADS, 1, 1)


def pack_shards(layer, sh, dtype):
    def rows(a):
        return a.astype(dtype).reshape(-1, ROW_W)

    def t(a):
        return rows(jnp.swapaxes(a, -1, -2))

    parts = [t(sh["ffn_w_gate"][layer]), t(sh["ffn_w_up"][layer]), rows(sh["ffn_w_down"][layer])]
    if layer == 0:
        parts += [t(sh["hyb_w_in"]), rows(sh["hyb_w_out"])]
    else:
        parts += [rows(sh["mla_w_out"]),
                  jnp.pad(rows(sh["mla_w_down"]), ((0, MDOWN_SEC_ROWS - MDOWN_ROWS), (0, 0))),
                  t(sh["mla_w_uq"]), t(sh["mla_w_ukv"])]
    return jnp.concatenate(parts, axis=0)


def unpack_shards(parts):
    def sec(layer, name):
        o, n = SEC[layer][name]
        return parts[layer][o:o + n]

    def col(layer, name, r, c):
        return jnp.swapaxes(sec(layer, name).reshape(-1, c, r), -1, -2)

    def both(f):
        return jnp.stack([f(0), f(1)])

    return dict(
        ffn_w_gate=both(lambda l: col(l, "gate", 1024, FF_SH)),
        ffn_w_up=both(lambda l: col(l, "up", 1024, FF_SH)),
        ffn_w_down=both(lambda l: sec(l, "down").reshape(FFN_PER_LAYER, FF_SH, 1024)),
        hyb_w_in=col(0, "hin", 1024, 416),
        hyb_w_out=sec(0, "hout").reshape(1, 128, 1024),
        mla_w_out=sec(1, "mout").reshape(1, 128, 1024),
        mla_w_down=sec(1, "mdown")[:MDOWN_ROWS].reshape(1, 128, 544),
        mla_w_uq=col(1, "uq", 256, 192),
        mla_w_ukv=col(1, "ukv", 256, 256),
    )


def _pad_qheads(w):
    a = w.reshape(A_HEADS, 1, A_HEAD_DIM, -1)
    kvh = _KVHEAD[..., None]
    both = jnp.concatenate([jnp.where(kvh == 0, a, 0), jnp.where(kvh == 1, a, 0)], axis=1)
    return both.reshape(A_HEADS * LANES, -1)


def _unpad_qheads(w):
    a = w.reshape(A_HEADS, 2, A_HEAD_DIM, -1)
    return jnp.where(_KVHEAD == 0, a[:, 0], a[:, 1]).reshape(A_Q_W, -1)


def unpack_full(layer, g):
    def sec(name):
        o, n = SEC[layer][name]
        return g[:, o:o + n]

    def ffn(name):
        return jnp.swapaxes(sec(name).reshape(N_DEV, FFN_PER_LAYER, FF_SH, ROW_W), 0, 1).reshape(
            FFN_PER_LAYER, D_FF, ROW_W)

    def z(*s):
        return jnp.zeros(s, g.dtype)

    def halves(a):
        return a.reshape(FFN_PER_LAYER, 2, 1, FF_HALF, ROW_W)

    gate_t, up_t, down = ffn("gate"), ffn("up"), ffn("down")
    w = dict(wgu_t=jnp.concatenate([halves(gate_t), halves(up_t)], axis=2).reshape(FFN_PER_LAYER, 2 * D_FF, ROW_W),
             wd=down)
    w.update(wgu=jnp.swapaxes(w["wgu_t"], 1, 2), wd_t=jnp.swapaxes(down, 1, 2))
    if layer == 0:
        hin_t = sec("hin").reshape(-1, ROW_W)
        hout = sec("hout").reshape(-1, ROW_W)
        w.update(hin_t=jnp.concatenate([_pad_qheads(hin_t[:A_Q_W]), hin_t[A_Q_W:]], axis=0),
                 hout=jnp.concatenate([_pad_qheads(hout[:A_Q_W]), hout[A_Q_W:]], axis=0))
        w.update(hin=w["hin_t"].T, hout_t=w["hout"].T)
    else:
        mout = sec("mout").reshape(C_HEADS, C_V, ROW_W)
        mdown = sec("mdown")[:, :MDOWN_ROWS].reshape(D_MODEL, 544)
        uq_t = sec("uq").reshape(C_HEADS, C_NOPE + C_ROPE, C_Q_RANK)
        w.update(mout=jnp.concatenate([z(C_HEADS, C_NOPE, ROW_W), mout], axis=1).reshape(MLA_HEAD_PAD, ROW_W),
                 mdown=jnp.concatenate([mdown[:, :512], z(D_MODEL, 64), mdown[:, 512:], z(D_MODEL, 32)], axis=1),
                 uq_t=jnp.concatenate([uq_t, z(C_HEADS, 32, C_Q_RANK)], axis=1).reshape(MLA_HEAD_PAD, C_Q_RANK),
                 ukv_t=sec("ukv").reshape(MLA_HEAD_PAD, C_KV_RANK))
        w.update(mout_t=w["mout"].T, mdown_t=w["mdown"].T, uq=w["uq_t"].T, ukv=w["ukv_t"].T)
    return w


def pack_full(layer, d):
    gu = jnp.stack(d["wgu_t"]).reshape(FFN_PER_LAYER, 2, 2, FF_HALF, ROW_W)

    def split(a):
        return a.reshape(N_DEV, -1, ROW_W)

    def ffn(a):
        return jnp.swapaxes(a.reshape(FFN_PER_LAYER, N_DEV, FF_SH, ROW_W), 0, 1).reshape(
            N_DEV, FFN_PER_LAYER * FF_SH, ROW_W)

    parts = [ffn(gu[:, :, 0]), ffn(gu[:, :, 1]), ffn(jnp.stack(d["wd"]))]
    if layer == 0:
        hin = jnp.concatenate([_unpad_qheads(d["hin_t"][:A_HEADS * LANES]), d["hin_t"][A_HEADS * LANES:]], axis=0)
        hout = jnp.concatenate([_unpad_qheads(d["hout"][:A_HEADS * LANES]), d["hout"][A_HEADS * LANES:]], axis=0)
        parts += [split(hin), split(hout)]
    else:
        mout = d["mout"].reshape(C_HEADS, LANES, ROW_W)[:, C_NOPE:].reshape(-1, ROW_W)
        mdown = jnp.concatenate([d["mdown"][:, :512], d["mdown"][:, 576:608]], axis=1)
        uq = d["uq_t"].reshape(C_HEADS, LANES, C_Q_RANK)[:, :C_NOPE + C_ROPE]
        parts += [split(mout), jnp.pad(split(mdown), ((0, 0), (0, MDOWN_SEC_ROWS - MDOWN_ROWS), (0, 0))),
                  split(uq), split(d["ukv_t"])]
    return jnp.concatenate(parts, axis=1)


def rope_tables(positions):
    half = C_ROPE // 2
    freqs = ROPE_THETA ** (-jnp.arange(half, dtype=F32) / half)
    ang = positions.astype(F32)[..., None] * freqs
    cos, sin = jnp.cos(ang), jnp.sin(ang)
    shape = positions.shape
    cq = jnp.concatenate([jnp.ones(shape + (C_NOPE,), F32), cos, cos, jnp.zeros(shape + (32,), F32)], axis=-1)
    sg = jnp.concatenate([jnp.zeros(shape + (C_NOPE,), F32), -sin, sin, jnp.zeros(shape + (32,), F32)], axis=-1)
    return cq, sg


GRAD_DT = BF16
def ffn_fwd(h2, w, i):
    gu, a = ffn_up_fused(h2, w["wgu"][i])
    return matmul(a, w["wd"][i], "nn", F32, "ffn_down"), (gu, a)


def ffn_bwd(dy2, h2, w, i, saved, swap=None):
    gu, a = saved
    dgu = ffn_down_bwd_fused(dy2, w["wd_t"][i], gu)
    if swap is None:
        dwd, swapped = matmul(a, dy2, "tn", GRAD_DT, "ffn_down_dw", tm_cap=FF_HALF), None
    else:
        dwd, swapped = matmul(a, dy2, "tn", GRAD_DT, "ffn_down_dw_swap", tm_cap=FF_HALF,
                              rider=ride_swap_with_sibling(swap))
    dwgu_t = matmul(dgu, h2, "tn", GRAD_DT, "ffn_up_dw", tm_cap=FF_HALF)
    dh = matmul(dgu, w["wgu_t"][i], "nn", F32, "ffn_up_dx")
    return dh, dwgu_t, dwd, swapped


def hybrid_fwd(h2, shape, w, aux):
    bsz, seq = shape
    proj, gathered = matmul(h2, w["hin"], "nn", F32, "hyb_in_gather",
                            rider=ride_gather_level1(aux["next_packed"]))
    proj = proj.reshape(bsz, seq, HYB_PAD)
    pad = ((0, 0), (A_BLOCK, A_BLOCK), (0, 0))
    kp = jnp.pad(proj[:, :, 8 * LANES:9 * LANES].astype(BF16), pad)
    vp = jnp.pad(proj[:, :, 9 * LANES:10 * LANES].astype(BF16), pad)
    mo = wattn_fwd(proj, kp, vp, aux["posc"], aux["posr"], aux["sink"])
    mo, otot = hgrn_fwd(proj, mo, aux["lb"], aux["nw"])
    y, gathered = matmul(mo.reshape(bsz * seq, MO_PAD), w["hout"], "nn", F32, "hyb_out_gather",
                         rider=ride_gather_level2(gathered))
    return y, (proj, kp, vp, mo, otot), gathered


def hybrid_bwd(dy2, h2, shape, w, aux, saved):
    bsz, seq = shape
    proj, kp, vp, mo, otot = saved
    mo2 = mo.reshape(bsz * seq, MO_PAD)
    dmo = matmul(dy2, w["hout_t"], "nn", F32, "hyb_out_dx").reshape(bsz, seq, MO_PAD)
    dhout = matmul(mo2, dy2, "tn", GRAD_DT, "hyb_out_dw")
    dq, dkp, dvp, dsink = wattn_bwd(proj, kp, vp, aux["posc"], aux["posr"], aux["sink"], dmo)
    do, dhg, dnw = hgrn_out_bwd(proj, otot, dmo, aux["nw"])
    dhq, dhff, dhfb, dhi, dlb, aux["exchanged"] = hgrn_bwd(proj, do, aux["lb"],
                                                            rider=ride_chip_all_to_all(aux["pair_sums"]))
    dproj = jnp.concatenate([dq, dkp[:, A_BLOCK:-A_BLOCK].astype(BF16), dvp[:, A_BLOCK:-A_BLOCK].astype(BF16),
                             dhq, dhff, dhfb, dhi, dhg], axis=-1).reshape(bsz * seq, HYB_PAD)
    dhin_t = matmul(dproj, h2, "tn", GRAD_DT, "hyb_in_dw")
    dh = matmul(dproj, w["hin_t"], "nn", F32, "hyb_in_dx")
    return dh, dict(hin_t=dhin_t, hout=dhout), dict(sink=dsink, lb=dlb, nw=dnw)


def mla_fwd(h2, shape, w, aux):
    bsz, seq = shape
    t = bsz * seq
    cqkv = matmul(h2, w["mdown"], "nn", F32, "mla_down").reshape(bsz, seq, MLA_DOWN_PAD)
    nq, nkv, kr = mla_mid_fwd(cqkv, aux["qn"], aux["kvn"], aux["cq"], aux["sg"])
    q = matmul(nq.reshape(t, C_Q_RANK), w["uq"], "nn", F32, "mla_uq").reshape(bsz, seq, MLA_HEAD_PAD)
    qp = q_rope(q, aux["cq"], aux["sg"], False, "q_rope_fwd", out_scale=MLA_QSCALE)
    kv = matmul(nkv.reshape(t, C_KV_RANK), w["ukv"], "nn", BF16, "mla_ukv").reshape(bsz, seq, MLA_HEAD_PAD)
    o, lse = mla_attn_fwd(qp, kv, kr)
    y = matmul(o.reshape(t, MLA_HEAD_PAD), w["mout"], "nn", F32, "mla_out")
    return y, (cqkv, nq, nkv, kr, qp, kv, o, lse)


def mla_bwd(dy2, h2, shape, w, aux, saved):
    bsz, seq = shape
    t = bsz * seq
    cqkv, nq, nkv, kr, qp, kv, o, lse = saved
    do = matmul(dy2, w["mout_t"], "nn", BF16, "mla_out_dx").reshape(bsz, seq, MLA_HEAD_PAD)
    dmout = matmul(o.reshape(t, MLA_HEAD_PAD), dy2, "tn", GRAD_DT, "mla_out_dw")
    dqp, dkv, dkr = mla_attn_bwd(qp, kv, kr, o, do, lse)
    dq = q_rope(dqp, aux["cq"], aux["sg"], True, "q_rope_bwd").reshape(t, MLA_HEAD_PAD)
    dkv2 = dkv.reshape(t, MLA_HEAD_PAD)
    dnq = matmul(dq, w["uq_t"], "nn", F32, "mla_uq_dx").reshape(bsz, seq, C_Q_RANK)
    duq_t = matmul(dq, nq.reshape(t, C_Q_RANK), "tn", GRAD_DT, "mla_uq_dw")
    dnkv = matmul(dkv2, w["ukv_t"], "nn", F32, "mla_ukv_dx").reshape(bsz, seq, C_KV_RANK)
    dukv_t = matmul(dkv2, nkv.reshape(t, C_KV_RANK), "tn", GRAD_DT, "mla_ukv_dw")
    dcqkv, dqn, dkvn = mla_mid_bwd(dnq, dnkv, dkr, cqkv, aux["qn"], aux["kvn"], aux["cq"], aux["sg"])
    dcqkv2 = dcqkv.reshape(t, MLA_DOWN_PAD)
    dmdown = matmul(h2, dcqkv2, "tn", GRAD_DT, "mla_down_dw")
    dh = matmul(dcqkv2, w["mdown_t"], "nn", F32, "mla_down_dx")
    return dh, dict(mdown=dmdown, uq_t=duq_t, ukv_t=dukv_t, mout=dmout), dict(qn=dqn, kvn=dkvn)


W_NAMES = ['ada_w', 'ada_b', 'ln_g', 'ln_b', 'ffn_w_gate', 'ffn_w_up', 'ffn_w_down', 'hyb_w_in', 'hyb_w_out',
           'attn_sink', 'hgrn_lb_logits', 'hgrn_norm_w', 'mla_w_down', 'mla_q_norm', 'mla_kv_norm', 'mla_w_uq',
           'mla_w_ukv', 'mla_w_out']
SMALL_NAMES = ['ada_b', 'ln_g', 'ln_b', 'attn_sink', 'hgrn_lb_logits', 'hgrn_norm_w', 'mla_q_norm', 'mla_kv_norm']
MOD_W = N_SUB * 3 * D_MODEL
MOD_SH = MOD_W // N_DEV
RES_W = (0.5, 1.0, 0.5)


def _rows1024(a, rows=None):
    flat = a.astype(F32).reshape(-1)
    n = flat.shape[0]
    total = (-(-n // ROW_W) if rows is None else rows) * ROW_W
    return jnp.pad(flat, (0, total - n)).reshape(-1, ROW_W)


def kernel(x, c, positions, ada_w, ada_b, ln_g, ln_b, ffn_w_gate, ffn_w_up, ffn_w_down, hyb_w_in, hyb_w_out, attn_sink, hgrn_lb_logits, hgrn_norm_w, mla_w_down, mla_q_norm, mla_kv_norm, mla_w_uq, mla_w_ukv, mla_w_out, loss_target, m_ada_w, m_ada_b, m_ln_g, m_ln_b, m_ffn_w_gate, m_ffn_w_up, m_ffn_w_down, m_hyb_w_in, m_hyb_w_out, m_attn_sink, m_hgrn_lb_logits, m_hgrn_norm_w, m_mla_w_down, m_mla_q_norm, m_mla_kv_norm, m_mla_w_uq, m_mla_w_ukv, m_mla_w_out, v_ada_w, v_ada_b, v_ln_g, v_ln_b, v_ffn_w_gate, v_ffn_w_up, v_ffn_w_down, v_hyb_w_in, v_hyb_w_out, v_attn_sink, v_hgrn_lb_logits, v_hgrn_norm_w, v_mla_w_down, v_mla_q_norm, v_mla_kv_norm, v_mla_w_uq, v_mla_w_ukv, v_mla_w_out):
    weights = dict(zip(W_NAMES, (ada_w, ada_b, ln_g, ln_b, ffn_w_gate, ffn_w_up, ffn_w_down, hyb_w_in, hyb_w_out,
                                 attn_sink, hgrn_lb_logits, hgrn_norm_w, mla_w_down, mla_q_norm, mla_kv_norm,
                                 mla_w_uq, mla_w_ukv, mla_w_out)))
    mom1 = dict(zip(W_NAMES, (m_ada_w, m_ada_b, m_ln_g, m_ln_b, m_ffn_w_gate, m_ffn_w_up, m_ffn_w_down, m_hyb_w_in,
                              m_hyb_w_out, m_attn_sink, m_hgrn_lb_logits, m_hgrn_norm_w, m_mla_w_down, m_mla_q_norm,
                              m_mla_kv_norm, m_mla_w_uq, m_mla_w_ukv, m_mla_w_out)))
    mom2 = dict(zip(W_NAMES, (v_ada_w, v_ada_b, v_ln_g, v_ln_b, v_ffn_w_gate, v_ffn_w_up, v_ffn_w_down, v_hyb_w_in,
                              v_hyb_w_out, v_attn_sink, v_hgrn_lb_logits, v_hgrn_norm_w, v_mla_w_down, v_mla_q_norm,
                              v_mla_kv_norm, v_mla_w_uq, v_mla_w_ukv, v_mla_w_out)))
    bsz, seq, d = x.shape
    t = bsz * seq
    nb_tot = N_DEV * bsz
    me = 4 * lax.axis_index("x") + 2 * lax.axis_index("y") + lax.axis_index("c")

    c_all = all_gather(c, "gather_c").reshape(nb_tot, d)
    cond = silu_small(c_all)
    ada_b_mine = lax.dynamic_slice_in_dim(ada_b, me * MOD_SH, MOD_SH, axis=1)
    modp = jnp.concatenate([matmul(cond, ada_w[l], "nn", F32, "ada_fwd", bias=ada_b_mine[l])
                            for l in range(DEPTH)], axis=1)
    mod_rows = nb_tot * DEPTH * MOD_SH // ROW_W
    small1 = jnp.concatenate([_rows1024(modp), _rows1024(jnp.concatenate(
        [ln_g.reshape(-1), ln_b.reshape(-1), mla_q_norm.reshape(-1), mla_kv_norm.reshape(-1)]), rows=4)], axis=0)
    g1 = all_gather(small1, "gather_mod")
    mod_all = g1[:, :mod_rows].reshape(N_DEV, nb_tot, DEPTH, MOD_SH)
    mod_all = jnp.transpose(mod_all, (1, 2, 0, 3)).reshape(nb_tot, DEPTH, N_SUB, 3, d)
    mod = lax.dynamic_slice_in_dim(mod_all, me * bsz, bsz, axis=0)
    tail = g1[:, mod_rows:].reshape(N_DEV, -1)
    n_ln = DEPTH * N_SUB * LANES
    ln_g_full = jnp.transpose(tail[:, :n_ln].reshape(N_DEV, DEPTH, N_SUB, LANES), (1, 2, 0, 3)).reshape(DEPTH, N_SUB, d)
    ln_b_full = jnp.transpose(tail[:, n_ln:2 * n_ln].reshape(N_DEV, DEPTH, N_SUB, LANES), (1, 2, 0, 3)).reshape(DEPTH, N_SUB, d)
    qn_full = tail[:, 2 * n_ln:2 * n_ln + 32].reshape(C_Q_RANK)
    kvn_full = tail[:, 2 * n_ln + 32:2 * n_ln + 64].reshape(C_KV_RANK)

    def mvec(l, s, j):
        return mod[:, l, s, j].reshape(bsz, 1, d)

    packed = [pack_shards(layer, weights, BF16) for layer in range(DEPTH)]
    w = [unpack_full(0, all_gather(packed[0], "gather_weights_l0")), None]

    cq_tab, sg_tab = rope_tables(positions)
    lb0 = lb_fwd(hgrn_lb_logits, 0)
    aux = [dict(posc=positions.reshape(bsz, seq, 1),
                posr=jnp.pad(positions, ((0, 0), (A_BLOCK, A_BLOCK))).reshape(bsz, 1, seq + 2 * A_BLOCK),
                sink=attn_sink[0], lb=lb0.reshape(B_HEADS, 1, LANES), nw=hgrn_norm_w[0].reshape(B_HEADS, 1, LANES),
                next_packed=packed[1]),
           dict(qn=qn_full, kvn=kvn_full, cq=cq_tab, sg=sg_tab)]

    tape = []
    xin = x
    h = mod_fwd(x, mvec(0, 0, 1), mvec(0, 0, 0))
    for l in range(DEPTH):
        for s in range(N_SUB):
            h2 = h.reshape(t, d)
            if s != 1:
                y, saved = ffn_fwd(h2, w[l], s // 2)
            elif l == 0:
                y, saved, gathered = hybrid_fwd(h2, (bsz, seq), w[0], aux[0])
                w[1] = unpack_full(1, gathered)
            else:
                y, saved = mla_fwd(h2, (bsz, seq), w[1], aux[1])
            y = y.reshape(bsz, seq, d)
            last = l == DEPTH - 1 and s == N_SUB - 1
            ln, sn = (l, s + 1) if s + 1 < N_SUB else (l + 1, 0)
            nxt = None if last else (mvec(ln, sn, 1), mvec(ln, sn, 0))
            xn, hn = resln_fwd(xin, y, mvec(l, s, 2), ln_g_full[l, s], ln_b_full[l, s], RES_W[s], nxt)
            tape.append((l, s, xin, h2, y, saved))
            xin, h = xn, hn
    loss_part, gout = loss_fwd(xin, loss_target)
    loss = lax.psum(loss_part, ("x", "y", "c"))

    dmod = [[[None] * 3 for _ in range(N_SUB)] for _ in range(DEPTH)]
    dln_g = [[None] * N_SUB for _ in range(DEPTH)]
    dln_b = [[None] * N_SUB for _ in range(DEPTH)]
    big = [dict(wgu_t=[None] * FFN_PER_LAYER, wd=[None] * FFN_PER_LAYER) for _ in range(DEPTH)]
    small = {}
    mc = lax.axis_index("c")

    def outgoing(layer):
        return jnp.swapaxes(pack_full(layer, big[layer]).reshape(4, 2, PART_ROWS[layer], ROW_W), 0, 1)

    later = None
    for l, s, xs, h2, y, saved in reversed(tape):
        if later is None:
            dxa, dy, dgate, dg, db = resln_bwd(gout, xs, y, mvec(l, s, 2), ln_g_full[l, s], RES_W[s])
        else:
            ndxa, ndh, nl, ns = later
            dxa, dy, dgate, dg, db, dscale, dshift = resln_bwd(
                ndxa, xs, y, mvec(l, s, 2), ln_g_full[l, s], RES_W[s],
                upstream=(ndh, mvec(nl, ns, 1), ln_b_full[l, s]))
            dmod[nl][ns][0], dmod[nl][ns][1] = dshift, dscale
        dmod[l][s][2] = dgate
        dln_g[l][s], dln_b[l][s] = dg, db
        dy2 = dy.reshape(t, d)
        i = s // 2
        if s == 1 and l == 1:
            dh, dbig, dsmall = mla_bwd(dy2, h2, (bsz, seq), w[1], aux[1], saved)
        elif s == 1:
            dh, dbig, dsmall = hybrid_bwd(dy2, h2, (bsz, seq), w[0], aux[0], saved)
        elif l == 0 and s == 2:
            out1 = outgoing(1)
            dh, big[l]["wgu_t"][i], big[l]["wd"][i], swapped = ffn_bwd(dy2, h2, w[l], i, saved, swap=out1)
            aux[0]["pair_sums"] = pair_add(out1, swapped, mc)
        else:
            dh, big[l]["wgu_t"][i], big[l]["wd"][i], _ = ffn_bwd(dy2, h2, w[l], i, saved)
        if s == 1:
            big[l].update(dbig)
            small.update(dsmall)
        later = (dxa, dh.reshape(bsz, seq, d), l, s)
    grad_x, dmod[0][0][1], dmod[0][0][0] = mod_bwd(later[0], later[1], x, mvec(0, 0, 1))

    dmod_mine = jnp.stack([jnp.stack([jnp.concatenate(dmod[l][s], axis=1) for s in range(N_SUB)], axis=1)
                           for l in range(DEPTH)], axis=1)
    dmod_rows = bsz * DEPTH * MOD_W // ROW_W
    misc = jnp.concatenate([small["lb"].reshape(-1), small["nw"].reshape(-1), small["qn"].reshape(-1),
                            small["kvn"].reshape(-1), small["sink"].reshape(-1)[:A_HEADS]])
    small2 = jnp.concatenate([_rows1024(dmod_mine),
                              _rows1024(jnp.stack([jnp.stack(r) for r in dln_g])),
                              _rows1024(jnp.stack([jnp.stack(r) for r in dln_b])),
                              _rows1024(misc, rows=2)], axis=0)
    g2 = all_gather(small2, "gather_small_grads")
    dmod_all = g2[:, :dmod_rows].reshape(nb_tot, DEPTH * MOD_W // ROW_W, ROW_W)
    grad_ada_b = sum_parts(dmod_all, "sum_ada_b").reshape(DEPTH, MOD_W)
    dmod_cols = lax.dynamic_slice_in_dim(dmod_all.reshape(nb_tot, DEPTH, MOD_W), me * MOD_SH, MOD_SH, axis=2)
    grad_ada_w = jnp.stack([matmul(cond, dmod_cols[:, l], "tn", F32, "ada_dw") for l in range(DEPTH)])
    rest = sum_parts(g2[:, dmod_rows:], "sum_small")
    n6 = DEPTH * N_SUB
    gl_g = lax.dynamic_slice_in_dim(rest[:n6].reshape(DEPTH, N_SUB, d), me * LANES, LANES, axis=2)
    gl_b = lax.dynamic_slice_in_dim(rest[n6:2 * n6].reshape(DEPTH, N_SUB, d), me * LANES, LANES, axis=2)
    mrow = rest[2 * n6:].reshape(-1)
    dlb0 = mrow[:512].reshape(1, 512)
    g_nw = mrow[512:1024].reshape(1, B_HEADS, LANES)
    g_qn = lax.dynamic_slice_in_dim(mrow[1024:1280], me * 32, 32).reshape(1, 32)
    g_kvn = lax.dynamic_slice_in_dim(mrow[1280:1536], me * 32, 32).reshape(1, 32)
    g_sink = mrow[1536:1536 + A_HEADS].reshape(1, A_HEADS)
    g_lb = lb_bwd(hgrn_lb_logits, dlb0, 0)

    out0 = outgoing(0)
    pair0 = pair_add(out0, swap_with_sibling(out0, "exchange_d2d"), mc)
    reduced = [sum_parts(chip_all_to_all(pair0, "exchange_ici"), "sum_grads_l0"),
               sum_parts(aux[0]["exchanged"], "sum_grads_l1")]
    grads = unpack_shards(reduced)
    grads.update(ada_w=grad_ada_w, ada_b=grad_ada_b, ln_g=gl_g, ln_b=gl_b, attn_sink=g_sink,
                 hgrn_lb_logits=g_lb, hgrn_norm_w=g_nw, mla_q_norm=g_qn, mla_kv_norm=g_kvn)

    delta, new_m, new_v = {}, {}, {}
    for name in W_NAMES:
        if name in SMALL_NAMES:
            continue
        shp = weights[name].shape
        two_d = (-1, shp[-1])
        out = adamw(weights[name].reshape(two_d), grads[name].reshape(two_d), mom1[name].reshape(two_d),
                    mom2[name].reshape(two_d), "adamw_" + name)
        delta[name], new_m[name], new_v[name] = [o.reshape(shp) for o in out]

    def pack_small(src):
        flat = jnp.concatenate([src[n].reshape(-1) for n in SMALL_NAMES])
        return jnp.pad(flat, (0, -flat.shape[0] % (8 * LANES))).reshape(-1, LANES)

    outs = adamw(pack_small(weights), pack_small(grads), pack_small(mom1), pack_small(mom2), "adamw_small")
    off = 0
    for name in SMALL_NAMES:
        n = weights[name].size
        for dst, o in zip((delta, new_m, new_v), outs):
            dst[name] = o.reshape(-1)[off:off + n].reshape(weights[name].shape)
        off += n

    return (loss, grad_x, *[grads[n] for n in W_NAMES], *[delta[n] for n in W_NAMES],
            *[new_m[n] for n in W_NAMES], *[new_v[n] for n in W_NAMES])
```

```python
import functools
import math

import jax
import jax.numpy as jnp
import numpy as np
from jax import lax
from jax.experimental import pallas as pl
from jax.experimental.pallas import tpu as pltpu

F32 = jnp.float32
BF16 = jnp.bfloat16

D_MODEL = 1024
DEPTH = 2
D_FF = 2816
A_HEADS = 8
A_KV_HEADS = 2
A_HEAD_DIM = 64
WINDOW = 128
A_BLOCK = 128
B_HEADS = 4
B_KEY_DIM = 128
B_CHUNK = 64
C_HEADS = 16
C_Q_RANK = 256
C_KV_RANK = 256
C_NOPE = 64
C_ROPE = 32
C_V = 64
ROPE_THETA = 10000.0
LN_EPS = 1e-5
RMS_EPS = 1e-6
DEEPNORM_ALPHA = (2 * DEPTH) ** 0.25
N_SUB = 3
N_DEV = 8
LANES = 128
HYB_PAD = 3840
MO_PAD = 1536
MLA_DOWN_PAD = 640
MLA_HEAD_PAD = C_HEADS * LANES
ADAM_LR = 0.001
ADAM_B1 = 0.9
ADAM_B2 = 0.999
ADAM_EPS = 1e-08
ADAM_WD = 0.01
ADAM_STEP = 10
VMEM_LIMIT = 48 * 1024 * 1024
MESH_ID = pl.DeviceIdType.MESH
HIGHEST = lax.Precision.HIGHEST


def _cparams(sem=None):
    return pltpu.CompilerParams(dimension_semantics=sem, vmem_limit_bytes=VMEM_LIMIT)


def _tile(n, cap):
    if n <= cap:
        return n
    t = (cap // LANES) * LANES
    while t >= LANES:
        if n % t == 0:
            return t
        t -= LANES
    return n


def _rows_tile(n, cap):
    if n <= cap:
        return n
    t = cap
    while t >= 8:
        if n % t == 0:
            return t
        t -= 8
    return n


def _sigmoid(x):
    return 1.0 / (1.0 + jnp.exp(-x))


def _dot(a, b, dims, precision=None):
    return lax.dot_general(a, b, (dims, ((), ())), precision=precision,
                           preferred_element_type=F32)


NN = ((1,), (0,))
NT = ((1,), (1,))
TN = ((0,), (0,))


class Rider:
    def __init__(self, ins, outs, copies, n_remote, aliases=None):
        self.ins, self.outs, self.copies, self.n_remote = list(ins), list(outs), copies, n_remote
        self.aliases = aliases or {}

    def wrap(self, body, n_in, n_out, first, last):
        nci, nco = len(self.ins), len(self.outs)

        def wrapped(*refs):
            ins, cins = refs[:n_in], refs[n_in:n_in + nci]
            outs = refs[n_in + nci:n_in + nci + n_out]
            couts = refs[n_in + nci + n_out:n_in + nci + n_out + nco]
            scratch = refs[n_in + nci + n_out + nco:-3]
            sems = refs[-3:]

            @pl.when(first())
            def _():
                for started, _ in self.copies(cins, couts, *sems):
                    started.start()

            body(*ins, *outs, *scratch)

            @pl.when(last())
            def _():
                for _, awaited in self.copies(cins, couts, *sems):
                    awaited.wait()

        return wrapped

    def extend(self, n_in, n_out, args, in_specs, out_shape, out_specs, scratch):
        any_spec = pl.BlockSpec(memory_space=pl.ANY)
        aliases = {n_in + i: n_out + o for i, o in self.aliases.items()}
        return (list(args) + self.ins, list(in_specs) + [any_spec] * len(self.ins),
                list(out_shape) + self.outs, list(out_specs) + [any_spec] * len(self.outs),
                list(scratch) + [pltpu.SemaphoreType.DMA((self.n_remote,)), pltpu.SemaphoreType.DMA((self.n_remote,)),
                                 pltpu.SemaphoreType.DMA(())], aliases)


def _dev_slot(ref, px, py, pc):
    return ref.at[4 * px + 2 * py + pc]


def _other_chips(mx, my):
    return [(1 - mx, my), (mx, 1 - my), (1 - mx, 1 - my)]


def ride_gather_level1(x):
    def copies(cins, couts, send_sems, recv_sems, local_sem):
        (x_ref,), (out_ref,) = cins, couts
        mx, my, mc = _my_place()
        targets = [(mx, my, 1 - mc)] + [(*chip, mc) for chip in _other_chips(mx, my)]
        mine = _dev_slot(out_ref, mx, my, mc)

        def remote(k, to, dst):
            return pltpu.make_async_remote_copy(src_ref=x_ref, dst_ref=dst, send_sem=send_sems.at[k],
                                                recv_sem=recv_sems.at[k], device_id=to, device_id_type=MESH_ID)

        local = pltpu.make_async_copy(x_ref, mine, local_sem)
        return [(local, local)] + [(remote(k, to, mine), remote(k, to, _dev_slot(out_ref, *to)))
                                   for k, to in enumerate(targets)]

    return Rider([x], [jax.ShapeDtypeStruct((N_DEV,) + x.shape, x.dtype)], copies, 4)


def ride_gather_level2(g):
    def copies(cins, couts, send_sems, recv_sems, local_sem):
        (g_ref,), (out_ref,) = cins, couts
        mx, my, mc = _my_place()

        def remote(k, chip, pc):
            return pltpu.make_async_remote_copy(
                src_ref=_dev_slot(g_ref, *chip, mc), dst_ref=_dev_slot(out_ref, *chip, pc),
                send_sem=send_sems.at[k], recv_sem=recv_sems.at[k],
                device_id=(mx, my, 1 - mc), device_id_type=MESH_ID)

        return [(remote(k, chip, mc), remote(k, chip, 1 - mc)) for k, chip in enumerate(_other_chips(mx, my))]

    return Rider([g], [jax.ShapeDtypeStruct(g.shape, g.dtype)], copies, 3, aliases={0: 0})


def _sibling_swap_copies(x_ref, out_ref, send_sems, recv_sems):
    mx, my, mc = _my_place()
    cps = [pltpu.make_async_remote_copy(
        src_ref=x_ref.at[j, 1 - mc], dst_ref=out_ref.at[j], send_sem=send_sems.at[j], recv_sem=recv_sems.at[j],
        device_id=(mx, my, 1 - mc), device_id_type=MESH_ID) for j in range(4)]
    return [(cp, cp) for cp in cps]


def ride_swap_with_sibling(x):
    def copies(cins, couts, send_sems, recv_sems, local_sem):
        return _sibling_swap_copies(cins[0], couts[0], send_sems, recv_sems)

    return Rider([x], [jax.ShapeDtypeStruct((x.shape[0],) + x.shape[2:], x.dtype)], copies, 4)


def ride_chip_all_to_all(x):
    def copies(cins, couts, send_sems, recv_sems, local_sem):
        (x_ref,), (out_ref,) = cins, couts
        mx, my, mc = _my_place()
        me = 2 * mx + my

        def remote(k, px, py, dst):
            return pltpu.make_async_remote_copy(src_ref=x_ref.at[2 * px + py], dst_ref=dst,
                                                send_sem=send_sems.at[k], recv_sem=recv_sems.at[k],
                                                device_id=(px, py, mc), device_id_type=MESH_ID)

        local = pltpu.make_async_copy(x_ref.at[me], out_ref.at[me], local_sem)
        return [(local, local)] + [(remote(k, px, py, out_ref.at[me]), remote(k, px, py, out_ref.at[2 * px + py]))
                                   for k, (px, py) in enumerate(_other_chips(mx, my))]

    return Rider([x], [jax.ShapeDtypeStruct(x.shape, x.dtype)], copies, 3)


def matmul(a, b, form, out_dtype, name, bias=None, tm_cap=512, tn_cap=1024, tk_cap=1024, rider=None):
    if form == "nn":
        (m, k), (k2, n) = a.shape, b.shape
    elif form == "nt":
        (m, k), (n, k2) = a.shape, b.shape
    else:
        (k, m), (k2, n) = a.shape, b.shape
    assert k == k2, (a.shape, b.shape, form)
    tm = _tile(m, tm_cap)
    tn = _tile(n, tn_cap)
    tk = k if k <= 2 * D_FF else _tile(k, tk_cap)
    nk = k // tk
    dims = {"nn": NN, "nt": NT, "tn": TN}[form]

    def body(*refs):
        if bias is None:
            a_ref, b_ref, o_ref, acc_ref = refs
            bias_ref = None
        else:
            a_ref, b_ref, bias_ref, o_ref, acc_ref = refs
        kk = pl.program_id(2)
        part = _dot(a_ref[...].astype(BF16), b_ref[...].astype(BF16), dims)

        def finish(total):
            if bias_ref is not None:
                total = total + bias_ref[...]
            o_ref[...] = total.astype(o_ref.dtype)

        if nk == 1:
            finish(part)
        else:
            @pl.when(kk == 0)
            def _():
                acc_ref[...] = part

            @pl.when(jnp.logical_and(kk > 0, kk < nk - 1))
            def _():
                acc_ref[...] += part

            @pl.when(kk == nk - 1)
            def _():
                finish(acc_ref[...] + part)

    if form == "nn":
        a_spec = pl.BlockSpec((tm, tk), lambda i, j, kk: (i, kk))
        b_spec = pl.BlockSpec((tk, tn), lambda i, j, kk: (kk, j))
    elif form == "nt":
        a_spec = pl.BlockSpec((tm, tk), lambda i, j, kk: (i, kk))
        b_spec = pl.BlockSpec((tn, tk), lambda i, j, kk: (j, kk))
    else:
        a_spec = pl.BlockSpec((tk, tm), lambda i, j, kk: (kk, i))
        b_spec = pl.BlockSpec((tk, tn), lambda i, j, kk: (kk, j))
    in_specs = [a_spec, b_spec]
    args = [a, b]
    if bias is not None:
        in_specs.append(pl.BlockSpec((1, tn), lambda i, j, kk: (0, j)))
        args.append(bias.reshape(1, n))
    out_shape = jax.ShapeDtypeStruct((m, n), out_dtype)
    out_spec = pl.BlockSpec((tm, tn), lambda i, j, kk: (i, j))
    grid = (m // tm, n // tn, nk)
    if rider is None:
        return pl.pallas_call(
            body, name=name, out_shape=out_shape, grid=grid, in_specs=in_specs, out_specs=out_spec,
            scratch_shapes=[pltpu.VMEM((tm, tn), F32)],
            compiler_params=_cparams(("parallel", "parallel", "arbitrary")),
        )(*args)

    def at_step(which):
        def test():
            ids = [pl.program_id(ax) for ax in range(3)]
            want = [0, 0, 0] if which == "first" else [g - 1 for g in grid]
            return jnp.logical_and(jnp.logical_and(ids[0] == want[0], ids[1] == want[1]), ids[2] == want[2])
        return test

    n_in = len(args)
    args, in_specs, out_shape, out_specs, scratch, aliases = rider.extend(
        n_in, 1, args, in_specs, [out_shape], [out_spec], [pltpu.VMEM((tm, tn), F32)])
    return pl.pallas_call(
        rider.wrap(body, n_in, 1, at_step("first"), at_step("last")), name=name,
        out_shape=out_shape, grid=grid, in_specs=in_specs, out_specs=out_specs, scratch_shapes=scratch,
        input_output_aliases=aliases,
        compiler_params=_cparams(("arbitrary", "arbitrary", "arbitrary")),
    )(*args)


TOK_TILE = 512


def _vec_spec(d):
    return pl.BlockSpec((1, 1, d), lambda b, s: (b, 0, 0))


def _tok_spec(ts, d):
    return pl.BlockSpec((1, ts, d), lambda b, s: (b, s, 0))


def mod_fwd(x, scale, shift):
    bsz, seq, d = x.shape
    ts = _rows_tile(seq, TOK_TILE)

    def body(x_ref, sc_ref, sh_ref, h_ref):
        h_ref[0] = (x_ref[0] * (1.0 + sc_ref[0]) + sh_ref[0]).astype(BF16)

    return pl.pallas_call(
        body, name="mod_fwd",
        out_shape=jax.ShapeDtypeStruct((bsz, seq, d), BF16),
        grid=(bsz, seq // ts),
        in_specs=[_tok_spec(ts, d), _vec_spec(d), _vec_spec(d)],
        out_specs=_tok_spec(ts, d),
        compiler_params=_cparams(("parallel", "parallel")),
    )(x, scale, shift)


def _ln_stats(z):
    mu = jnp.mean(z, axis=-1, keepdims=True)
    zc = z - mu
    var = jnp.mean(zc * zc, axis=-1, keepdims=True)
    return zc, lax.rsqrt(var + LN_EPS)


def resln_fwd(x, y, gate, g, b, res_w, nxt):
    bsz, seq, d = x.shape
    ts = _rows_tile(seq, TOK_TILE)

    def body(*refs):
        if nxt is None:
            x_ref, y_ref, gt_ref, g_ref, b_ref, xn_ref = refs
        else:
            x_ref, y_ref, gt_ref, g_ref, b_ref, sc_ref, sh_ref, xn_ref, hn_ref = refs
        z = DEEPNORM_ALPHA * x_ref[0] + (res_w * (1.0 + gt_ref[0])) * y_ref[0]
        zc, r = _ln_stats(z)
        xn = zc * r * g_ref[...] + b_ref[...]
        xn_ref[0] = xn
        if nxt is not None:
            hn_ref[0] = (xn * (1.0 + sc_ref[0]) + sh_ref[0]).astype(BF16)

    row = pl.BlockSpec((1, d), lambda bb, s: (0, 0))
    in_specs = [_tok_spec(ts, d), _tok_spec(ts, d), _vec_spec(d), row, row]
    args = [x, y, gate, g.reshape(1, d), b.reshape(1, d)]
    out_shape = [jax.ShapeDtypeStruct((bsz, seq, d), F32)]
    out_specs = [_tok_spec(ts, d)]
    if nxt is not None:
        in_specs += [_vec_spec(d), _vec_spec(d)]
        args += list(nxt)
        out_shape.append(jax.ShapeDtypeStruct((bsz, seq, d), BF16))
        out_specs.append(_tok_spec(ts, d))
    out = pl.pallas_call(
        body, name="resln_fwd",
        out_shape=out_shape, grid=(bsz, seq // ts),
        in_specs=in_specs, out_specs=out_specs,
        compiler_params=_cparams(("parallel", "parallel")),
    )(*args)
    return (out[0], None) if nxt is None else (out[0], out[1])


def resln_bwd(gout, x, y, gate, g, res_w, upstream=None):
    bsz, seq, d = x.shape
    ts = _rows_tile(seq, TOK_TILE)

    def body(*refs):
        if upstream is None:
            go_ref, x_ref, y_ref, gt_ref, g_ref, dxa_ref, dy_ref, dgt_ref, dg_ref, db_ref = refs
        else:
            (go_ref, x_ref, y_ref, gt_ref, g_ref, dh_ref, sc_ref, b_ref,
             dxa_ref, dy_ref, dgt_ref, dg_ref, db_ref, dsc_ref, dsh_ref) = refs
        bb, s = pl.program_id(0), pl.program_id(1)
        yv = y_ref[0]
        rw = res_w * (1.0 + gt_ref[0])
        z = DEEPNORM_ALPHA * x_ref[0] + rw * yv
        zc, r = _ln_stats(z)
        xhat = zc * r
        go = go_ref[0]
        if upstream is not None:
            dh = dh_ref[0].astype(F32)
            go = go + dh * (1.0 + sc_ref[0])
            dsc = jnp.sum(dh * (xhat * g_ref[...] + b_ref[...]), axis=0, keepdims=True)
            dsh = jnp.sum(dh, axis=0, keepdims=True)

            @pl.when(s == 0)
            def _():
                dsc_ref[0] = dsc
                dsh_ref[0] = dsh

            @pl.when(s > 0)
            def _():
                dsc_ref[0] += dsc
                dsh_ref[0] += dsh

        dxh = go * g_ref[...]
        dz = r * (dxh - jnp.mean(dxh, axis=-1, keepdims=True)
                  - xhat * jnp.mean(dxh * xhat, axis=-1, keepdims=True))
        dxa_ref[0] = DEEPNORM_ALPHA * dz
        dy_ref[0] = (rw * dz).astype(BF16)
        dgt = res_w * jnp.sum(dz * yv, axis=0, keepdims=True)
        dg = jnp.sum(go * xhat, axis=0, keepdims=True)
        db = jnp.sum(go, axis=0, keepdims=True)

        @pl.when(s == 0)
        def _():
            dgt_ref[0] = dgt

        @pl.when(s > 0)
        def _():
            dgt_ref[0] += dgt

        first = jnp.logical_and(bb == 0, s == 0)

        @pl.when(first)
        def _():
            dg_ref[...] = dg
            db_ref[...] = db

        @pl.when(jnp.logical_not(first))
        def _():
            dg_ref[...] += dg
            db_ref[...] += db

    row = pl.BlockSpec((1, d), lambda bb, s: (0, 0))
    vec = jax.ShapeDtypeStruct((bsz, 1, d), F32)
    in_specs = [_tok_spec(ts, d), _tok_spec(ts, d), _tok_spec(ts, d), _vec_spec(d), row]
    args = [gout, x, y, gate, g.reshape(1, d)]
    out_shape = [jax.ShapeDtypeStruct((bsz, seq, d), F32), jax.ShapeDtypeStruct((bsz, seq, d), BF16), vec,
                 jax.ShapeDtypeStruct((1, d), F32), jax.ShapeDtypeStruct((1, d), F32)]
    out_specs = [_tok_spec(ts, d), _tok_spec(ts, d), _vec_spec(d), row, row]
    if upstream is not None:
        dh, scale, b = upstream
        in_specs += [_tok_spec(ts, d), _vec_spec(d), row]
        args += [dh, scale, b.reshape(1, d)]
        out_shape += [vec, vec]
        out_specs += [_vec_spec(d), _vec_spec(d)]
    return pl.pallas_call(
        body, name="resln_bwd" if upstream is None else "resln_mod_bwd",
        out_shape=out_shape, grid=(bsz, seq // ts),
        in_specs=in_specs, out_specs=out_specs,
        compiler_params=_cparams(("arbitrary", "arbitrary")),
    )(*args)


def mod_bwd(dxa, dh, x, scale):
    bsz, seq, d = x.shape
    ts = _rows_tile(seq, TOK_TILE)

    def body(dxa_ref, dh_ref, x_ref, sc_ref, dx_ref, dsc_ref, dsh_ref):
        s = pl.program_id(1)
        dh = dh_ref[0].astype(F32)
        dx_ref[0] = dxa_ref[0] + dh * (1.0 + sc_ref[0])
        dsc = jnp.sum(dh * x_ref[0], axis=0, keepdims=True)
        dsh = jnp.sum(dh, axis=0, keepdims=True)

        @pl.when(s == 0)
        def _():
            dsc_ref[0] = dsc
            dsh_ref[0] = dsh

        @pl.when(s > 0)
        def _():
            dsc_ref[0] += dsc
            dsh_ref[0] += dsh

    return pl.pallas_call(
        body, name="mod_bwd",
        out_shape=[jax.ShapeDtypeStruct((bsz, seq, d), F32),
                   jax.ShapeDtypeStruct((bsz, 1, d), F32),
                   jax.ShapeDtypeStruct((bsz, 1, d), F32)],
        grid=(bsz, seq // ts),
        in_specs=[_tok_spec(ts, d), _tok_spec(ts, d), _tok_spec(ts, d), _vec_spec(d)],
        out_specs=[_tok_spec(ts, d), _vec_spec(d), _vec_spec(d)],
        compiler_params=_cparams(("parallel", "arbitrary")),
    )(dxa, dh, x, scale)


FF_HALF = D_FF // 2
FF_CHUNKS = ((0, 384), (384, 384), (768, 384), (1152, 256))


def ffn_up_fused(h2, wgu):
    t, d = h2.shape
    tm = _rows_tile(t, 1024)

    def body(h_ref, w_ref, gu_ref, a_ref):
        h = h_ref[...]
        for c0, cw in FF_CHUNKS:
            g = _dot(h, w_ref[:, c0:c0 + cw], NN)
            u = _dot(h, w_ref[:, FF_HALF + c0:FF_HALF + c0 + cw], NN)
            gu_ref[:, c0:c0 + cw] = g.astype(BF16)
            gu_ref[:, FF_HALF + c0:FF_HALF + c0 + cw] = u.astype(BF16)
            a_ref[:, c0:c0 + cw] = (g * _sigmoid(g) * u).astype(BF16)

    return pl.pallas_call(
        body, name="ffn_up_fused",
        out_shape=[jax.ShapeDtypeStruct((t, 2 * D_FF), BF16), jax.ShapeDtypeStruct((t, D_FF), BF16)],
        grid=(2, t // tm),
        in_specs=[pl.BlockSpec((tm, d), lambda j, i: (i, 0)),
                  pl.BlockSpec((d, 2 * FF_HALF), lambda j, i: (0, j))],
        out_specs=[pl.BlockSpec((tm, 2 * FF_HALF), lambda j, i: (i, j)),
                   pl.BlockSpec((tm, FF_HALF), lambda j, i: (i, j))],
        compiler_params=_cparams(("parallel", "parallel")),
    )(h2, wgu)


def ffn_down_bwd_fused(dy2, wd_t, gu):
    t, d = dy2.shape
    tm = _rows_tile(t, 512)

    def body(dy_ref, w_ref, gu_ref, o_ref):
        dy = dy_ref[...]
        for c0, cw in FF_CHUNKS:
            da = _dot(dy, w_ref[:, c0:c0 + cw], NN)
            g = gu_ref[:, c0:c0 + cw].astype(F32)
            u = gu_ref[:, FF_HALF + c0:FF_HALF + c0 + cw].astype(F32)
            sg = _sigmoid(g)
            o_ref[:, c0:c0 + cw] = (da * u * (sg * (1.0 + g * (1.0 - sg)))).astype(BF16)
            o_ref[:, FF_HALF + c0:FF_HALF + c0 + cw] = (da * (g * sg)).astype(BF16)

    return pl.pallas_call(
        body, name="ffn_down_bwd_fused",
        out_shape=jax.ShapeDtypeStruct((t, 2 * D_FF), BF16),
        grid=(2, t // tm),
        in_specs=[pl.BlockSpec((tm, d), lambda j, i: (i, 0)),
                  pl.BlockSpec((d, FF_HALF), lambda j, i: (0, j)),
                  pl.BlockSpec((tm, 2 * FF_HALF), lambda j, i: (i, j))],
        out_specs=pl.BlockSpec((tm, 2 * FF_HALF), lambda j, i: (i, j)),
        compiler_params=_cparams(("parallel", "parallel")),
    )(dy2, wd_t, gu)


def loss_fwd(y, tgt):
    bsz, seq, d = y.shape
    ts = _rows_tile(seq, TOK_TILE)

    def body(y_ref, t_ref, dy_ref, l_ref, acc_ref):
        bb, s = pl.program_id(0), pl.program_id(1)
        e = y_ref[0] - t_ref[0]
        dy_ref[0] = e * (1.0 / d)
        part = jnp.sum((e * e).reshape(ts // 8, 8, d), axis=0)
        first = jnp.logical_and(bb == 0, s == 0)

        @pl.when(first)
        def _():
            acc_ref[...] = part

        @pl.when(jnp.logical_not(first))
        def _():
            acc_ref[...] += part

        @pl.when(jnp.logical_and(bb == pl.num_programs(0) - 1, s == pl.num_programs(1) - 1))
        def _():
            tot = jnp.sum(jnp.sum(acc_ref[...], axis=1, keepdims=True), axis=0, keepdims=True)
            l_ref[...] = jnp.broadcast_to(tot * (0.5 / d), (8, LANES))

    dy, l = pl.pallas_call(
        body, name="loss_fwd",
        out_shape=[jax.ShapeDtypeStruct((bsz, seq, d), F32), jax.ShapeDtypeStruct((8, LANES), F32)],
        grid=(bsz, seq // ts),
        in_specs=[_tok_spec(ts, d), _tok_spec(ts, d)],
        out_specs=[_tok_spec(ts, d), pl.BlockSpec((8, LANES), lambda bb, s: (0, 0))],
        scratch_shapes=[pltpu.VMEM((8, d), F32)],
        compiler_params=_cparams(("arbitrary", "arbitrary")),
    )(y, tgt)
    return l[0, 0], dy


def _rot_half(x):
    lane = lax.broadcasted_iota(jnp.int32, x.shape, 1)
    swapped = jnp.where(lane < 80, pltpu.roll(x, 112, axis=1), pltpu.roll(x, 16, axis=1))
    return jnp.where((lane >= C_NOPE) & (lane < C_NOPE + C_ROPE), swapped, 0.0)


def _rms(x, w):
    r = lax.rsqrt(jnp.mean(x * x, axis=-1, keepdims=True) + RMS_EPS)
    return x * r * w, r


def _rms_bwd(dy, x, w):
    r = lax.rsqrt(jnp.mean(x * x, axis=-1, keepdims=True) + RMS_EPS)
    wd = dy * w
    dx = r * wd - x * (r * r * r) * jnp.mean(x * wd, axis=-1, keepdims=True)
    return dx, jnp.sum(dy * x * r, axis=0, keepdims=True)


def mla_mid_fwd(cqkv, qw, kw, cq_tab, sg_tab):
    bsz, seq, _ = cqkv.shape
    ts = _rows_tile(seq, TOK_TILE)
    r = C_Q_RANK

    def body(x_ref, qw_ref, kw_ref, c_ref, s_ref, nq_ref, nkv_ref, kr_ref):
        x = x_ref[0]
        nq_ref[0] = _rms(x[:, :r], qw_ref[...])[0].astype(BF16)
        nkv_ref[0] = _rms(x[:, r:2 * r], kw_ref[...])[0].astype(BF16)
        xr = x[:, 2 * r:]
        kr_ref[0] = (xr * c_ref[0] + _rot_half(xr) * s_ref[0]).astype(BF16)

    row = pl.BlockSpec((1, r), lambda b, s: (0, 0))
    return pl.pallas_call(
        body, name="mla_mid_fwd",
        out_shape=[jax.ShapeDtypeStruct((bsz, seq, r), BF16), jax.ShapeDtypeStruct((bsz, seq, r), BF16),
                   jax.ShapeDtypeStruct((bsz, seq, LANES), BF16)],
        grid=(bsz, seq // ts),
        in_specs=[_tok_spec(ts, MLA_DOWN_PAD), row, row, _tok_spec(ts, LANES), _tok_spec(ts, LANES)],
        out_specs=[_tok_spec(ts, r), _tok_spec(ts, r), _tok_spec(ts, LANES)],
        compiler_params=_cparams(("parallel", "parallel")),
    )(cqkv, qw.reshape(1, r), kw.reshape(1, r), cq_tab, sg_tab)


def mla_mid_bwd(dnq, dnkv, dkr, cqkv, qw, kw, cq_tab, sg_tab):
    bsz, seq, _ = cqkv.shape
    ts = _rows_tile(seq, TOK_TILE)
    r = C_Q_RANK

    def body(dnq_ref, dnkv_ref, dkr_ref, x_ref, qw_ref, kw_ref, c_ref, s_ref, dx_ref, dqw_ref, dkw_ref):
        bb, s = pl.program_id(0), pl.program_id(1)
        x = x_ref[0]
        dcq, dqw = _rms_bwd(dnq_ref[0].astype(F32), x[:, :r], qw_ref[...])
        dckv, dkw = _rms_bwd(dnkv_ref[0].astype(F32), x[:, r:2 * r], kw_ref[...])
        dk = dkr_ref[0]
        dxr = dk * c_ref[0] + _rot_half(dk * s_ref[0])
        dx_ref[0, :, :r] = dcq.astype(BF16)
        dx_ref[0, :, r:2 * r] = dckv.astype(BF16)
        dx_ref[0, :, 2 * r:] = dxr.astype(BF16)
        first = jnp.logical_and(bb == 0, s == 0)

        @pl.when(first)
        def _():
            dqw_ref[...] = dqw
            dkw_ref[...] = dkw

        @pl.when(jnp.logical_not(first))
        def _():
            dqw_ref[...] += dqw
            dkw_ref[...] += dkw

    row = pl.BlockSpec((1, r), lambda b, s: (0, 0))
    return pl.pallas_call(
        body, name="mla_mid_bwd",
        out_shape=[jax.ShapeDtypeStruct((bsz, seq, MLA_DOWN_PAD), BF16),
                   jax.ShapeDtypeStruct((1, r), F32), jax.ShapeDtypeStruct((1, r), F32)],
        grid=(bsz, seq // ts),
        in_specs=[_tok_spec(ts, r), _tok_spec(ts, r), _tok_spec(ts, LANES), _tok_spec(ts, MLA_DOWN_PAD),
                  row, row, _tok_spec(ts, LANES), _tok_spec(ts, LANES)],
        out_specs=[_tok_spec(ts, MLA_DOWN_PAD), row, row],
        compiler_params=_cparams(("arbitrary", "arbitrary")),
    )(dnq, dnkv, dkr, cqkv, qw.reshape(1, r), kw.reshape(1, r), cq_tab, sg_tab)


def q_rope(q, cq_tab, sg_tab, transpose_rule, name, out_scale=1.0):
    bsz, seq, w = q.shape
    ts = _rows_tile(seq, TOK_TILE)

    def body(q_ref, c_ref, s_ref, o_ref):
        c, s = c_ref[0], s_ref[0]
        for h in range(w // LANES):
            x = q_ref[0, :, h * LANES:(h + 1) * LANES].astype(F32)
            y = x * c + (_rot_half(x * s) if transpose_rule else _rot_half(x) * s)
            o_ref[0, :, h * LANES:(h + 1) * LANES] = (y * out_scale).astype(BF16)

    return pl.pallas_call(
        body, name=name,
        out_shape=jax.ShapeDtypeStruct((bsz, seq, w), BF16),
        grid=(bsz, seq // ts),
        in_specs=[_tok_spec(ts, w), _tok_spec(ts, LANES), _tok_spec(ts, LANES)],
        out_specs=_tok_spec(ts, w),
        compiler_params=_cparams(("parallel", "parallel")),
    )(q, cq_tab, sg_tab)


MLA_SCALE = (C_NOPE + C_ROPE) ** -0.5
LOG2E = math.log2(math.e)
MLA_QSCALE = MLA_SCALE * LOG2E
MLA_TILE = 1024
MLA_HEADS_PER_STEP = 2
NEG_BIG = -1e30


def _eye_mask(n):
    return lax.broadcasted_iota(jnp.int32, (n, n), 0) == lax.broadcasted_iota(jnp.int32, (n, n), 1)


def mla_attn_fwd(qp, kv, kr):
    bsz, seq, _ = qp.shape
    tq = _rows_tile(seq, MLA_TILE)
    nt = seq // tq

    hps = MLA_HEADS_PER_STEP

    def lanes_of(hh):
        return slice(hh * LANES, (hh + 1) * LANES)

    def body(q_ref, kv_ref, kr_ref, o_ref, lse_ref, kc_ref):
        lane = lax.broadcasted_iota(jnp.int32, (seq, LANES), 1)
        for hh in range(hps):
            kc_ref[hh] = jnp.where(lane < C_NOPE, kv_ref[0, :, lanes_of(hh)], kr_ref[0])
        lane_q = lax.broadcasted_iota(jnp.int32, (tq, LANES), 1)

        def q_body(i, carry):
            qrows = pl.ds(pl.multiple_of(i * tq, tq), tq)
            q_i = [q_ref[0, qrows, lanes_of(hh)] for hh in range(hps)]

            def key_body(j, st):
                krows = pl.ds(pl.multiple_of(j * tq, tq), tq)
                out = []
                for hh in range(hps):
                    m, l, acc = st[hh]
                    s = _dot(q_i[hh], kc_ref[hh, krows, :], NT)
                    m_new = jnp.maximum(m, jnp.max(s, axis=-1, keepdims=True))
                    alpha = jnp.exp2(m - m_new)
                    p = jnp.exp2(s - m_new)
                    l = alpha * l + jnp.sum(p, axis=-1, keepdims=True)
                    acc = alpha * acc + _dot(p.astype(BF16), kv_ref[0, krows, lanes_of(hh)], NN)
                    out.append((m_new, l, acc))
                return tuple(out)

            init = tuple((jnp.full((tq, 1), NEG_BIG, F32), jnp.zeros((tq, 1), F32), jnp.zeros((tq, LANES), F32))
                         for _ in range(hps))
            res = lax.fori_loop(0, nt, key_body, init)
            for hh in range(hps):
                m, l, acc = res[hh]
                o_ref[0, qrows, lanes_of(hh)] = jnp.where(lane_q >= C_NOPE, acc * (1.0 / l), 0.0).astype(BF16)
                lse = m + jnp.log2(l)
                lse_ref[0, hh, :, qrows] = jnp.sum(jnp.where(_eye_mask(tq), lse, 0.0), axis=0, keepdims=True)
            return carry

        lax.fori_loop(0, nt, q_body, 0)

    heads = pl.BlockSpec((1, seq, hps * LANES), lambda b, h: (b, 0, h))
    return pl.pallas_call(
        body, name="mla_attn_fwd",
        out_shape=[jax.ShapeDtypeStruct((bsz, seq, MLA_HEAD_PAD), BF16),
                   jax.ShapeDtypeStruct((bsz, C_HEADS, 1, seq), F32)],
        grid=(bsz, C_HEADS // hps),
        in_specs=[heads, heads, pl.BlockSpec((1, seq, LANES), lambda b, h: (b, 0, 0))],
        out_specs=[heads, pl.BlockSpec((1, hps, 1, seq), lambda b, h: (b, h, 0, 0))],
        scratch_shapes=[pltpu.VMEM((hps, seq, LANES), BF16)],
        compiler_params=_cparams(("parallel", "parallel")),
    )(qp, kv, kr)


def mla_attn_bwd(qp, kv, kr, o, do, lse):
    bsz, seq, _ = qp.shape
    tq = _rows_tile(seq, MLA_TILE)
    nt = seq // tq

    def body(q_ref, kv_ref, kr_ref, o_ref, do_ref, lse_ref, dq_ref, dkv_ref, dkr_ref,
             kc_ref, drow_ref, dqa_ref, dkc_ref, dkvv_ref):
        h = pl.program_id(1)
        lane_s = lax.broadcasted_iota(jnp.int32, (seq, LANES), 1)
        lane_t = lax.broadcasted_iota(jnp.int32, (tq, LANES), 1)
        kc_ref[...] = jnp.where(lane_s < C_NOPE, kv_ref[0], kr_ref[0])
        dqa_ref[...] = jnp.zeros_like(dqa_ref)
        ones = jnp.ones((8, LANES), F32)

        def delta_body(i, carry):
            rows = pl.ds(pl.multiple_of(i * tq, tq), tq)
            prod = do_ref[0, rows, :].astype(F32) * o_ref[0, rows, :].astype(F32)
            drow_ref[:, rows] = _dot(ones, prod, NT, precision=HIGHEST)
            return carry

        lax.fori_loop(0, nt, delta_body, 0)

        def key_body(j, carry):
            krows = pl.ds(pl.multiple_of(j * tq, tq), tq)
            kc_j = kc_ref[krows, :]
            kv_j = kv_ref[0, krows, :]
            dkc_ref[...] = jnp.zeros_like(dkc_ref)
            dkvv_ref[...] = jnp.zeros_like(dkvv_ref)

            def q_body(i, c2):
                qrows = pl.ds(pl.multiple_of(i * tq, tq), tq)
                q_i = q_ref[0, qrows, :]
                do_i = do_ref[0, qrows, :]
                st = _dot(kc_j, q_i, NT)
                pt = jnp.exp2(st - lse_ref[0, 0, :, qrows])
                dpt = _dot(kv_j, do_i, NT)
                dst = (pt * (dpt - drow_ref[0:1, qrows])).astype(BF16)
                dkvv_ref[...] += _dot(pt.astype(BF16), do_i, NN)
                dkc_ref[...] += _dot(dst, q_i, NN)
                dqa_ref[qrows, :] += _dot(dst, kc_j, TN)
                return c2

            lax.fori_loop(0, nt, q_body, 0)
            dkc = dkc_ref[...] * (MLA_SCALE / MLA_QSCALE)
            dkv_ref[0, krows, :] = jnp.where(lane_t < C_NOPE, dkc, dkvv_ref[...]).astype(BF16)
            dkr_j = jnp.where(lane_t >= C_NOPE, dkc, 0.0)

            @pl.when(h == 0)
            def _():
                dkr_ref[0, krows, :] = dkr_j

            @pl.when(h > 0)
            def _():
                dkr_ref[0, krows, :] += dkr_j

            return carry

        lax.fori_loop(0, nt, key_body, 0)
        dq_ref[0] = (dqa_ref[...] * MLA_SCALE).astype(BF16)

    head = pl.BlockSpec((1, seq, LANES), lambda b, h: (b, 0, h))
    shared = pl.BlockSpec((1, seq, LANES), lambda b, h: (b, 0, 0))
    return pl.pallas_call(
        body, name="mla_attn_bwd",
        out_shape=[jax.ShapeDtypeStruct((bsz, seq, MLA_HEAD_PAD), BF16),
                   jax.ShapeDtypeStruct((bsz, seq, MLA_HEAD_PAD), BF16),
                   jax.ShapeDtypeStruct((bsz, seq, LANES), F32)],
        grid=(bsz, C_HEADS),
        in_specs=[head, head, shared, head, head,
                  pl.BlockSpec((1, 1, 1, seq), lambda b, h: (b, h, 0, 0))],
        out_specs=[head, head, shared],
        scratch_shapes=[pltpu.VMEM((seq, LANES), BF16), pltpu.VMEM((8, seq), F32),
                        pltpu.VMEM((seq, LANES), F32), pltpu.VMEM((tq, LANES), F32),
                        pltpu.VMEM((tq, LANES), F32)],
        compiler_params=_cparams(("parallel", "arbitrary")),
    )(qp, kv, kr, o, do, lse)


A_SCALE = A_HEAD_DIM ** -0.5
A_GROUP = A_HEADS // A_KV_HEADS
A_BAND = 3 * A_BLOCK


def _wattn_block(i, seq, posc_ref, posr_ref):
    start = pl.multiple_of(i * A_BLOCK, A_BLOCK)
    pq = jnp.concatenate([posc_ref[0]] * A_GROUP, axis=0)
    pk = posr_ref[0, :, pl.ds(start, A_BAND)]
    dist = jnp.abs(pq - pk).astype(F32)
    shape = (A_GROUP * A_BLOCK, A_BAND)
    qi = (lax.broadcasted_iota(jnp.int32, shape, 0) & (A_BLOCK - 1)) + A_BLOCK
    ki = lax.broadcasted_iota(jnp.int32, shape, 1)
    absk = i * A_BLOCK + ki - A_BLOCK
    valid = (jnp.abs(qi - ki) <= WINDOW) & (absk >= 0) & (absk < seq)
    return start, dist, valid


def _wattn_probs(q4, kb, dist, valid, slope4, sink4):
    s = _dot(q4, kb, NT) * A_SCALE - slope4 * dist
    s = jnp.where(valid, s, NEG_BIG)
    m = jnp.maximum(jnp.max(s, axis=-1, keepdims=True), sink4)
    p = jnp.exp(s - m)
    es = jnp.exp(sink4 - m)
    inv = 1.0 / (jnp.sum(p, axis=-1, keepdims=True) + es)
    return p * inv, es * inv


def _alibi_slope(h):
    return 2.0 ** (-8.0 * (h + 1) / A_HEADS)


def _group_heads(g):
    return range(g * A_GROUP, (g + 1) * A_GROUP)


def _stack_heads(ref, g, dtype):
    return jnp.concatenate([ref[0, :, h * LANES:(h + 1) * LANES].astype(dtype) for h in _group_heads(g)], axis=0)


def _per_head_column(values):
    return jnp.concatenate([jnp.full((A_BLOCK, 1), v, F32) for v in values], axis=0)


def wattn_fwd(proj, kp, vp, posc, posr, sink):
    bsz, seq, _ = proj.shape
    nb = seq // A_BLOCK
    qw = A_HEADS * LANES

    def body(sink_ref, q_ref, kp_ref, vp_ref, posc_ref, posr_ref, o_ref):
        i = pl.program_id(1)
        start, dist, valid = _wattn_block(i, seq, posc_ref, posr_ref)
        kb = kp_ref[0, pl.ds(start, A_BAND), :]
        vb = vp_ref[0, pl.ds(start, A_BAND), :]
        lane = lax.broadcasted_iota(jnp.int32, (A_GROUP * A_BLOCK, LANES), 1)
        for g in range(A_KV_HEADS):
            heads = _group_heads(g)
            p, _ = _wattn_probs(_stack_heads(q_ref, g, BF16), kb, dist, valid,
                                _per_head_column([_alibi_slope(h) for h in heads]),
                                _per_head_column([sink_ref[h] for h in heads]))
            o = _dot(p.astype(BF16), vb, NN)
            mine = (lane >= A_HEAD_DIM) if g == 1 else (lane < A_HEAD_DIM)
            o = jnp.where(mine, o, 0.0).astype(BF16)
            for j, h in enumerate(heads):
                o_ref[0, :, h * LANES:(h + 1) * LANES] = o[j * A_BLOCK:(j + 1) * A_BLOCK]

    return pl.pallas_call(
        body, name="wattn_fwd",
        out_shape=jax.ShapeDtypeStruct((bsz, seq, MO_PAD), BF16),
        grid=(bsz, nb),
        in_specs=[pl.BlockSpec(memory_space=pltpu.SMEM),
                  pl.BlockSpec((1, A_BLOCK, qw), lambda b, i: (b, i, 0)),
                  pl.BlockSpec((1, seq + 2 * A_BLOCK, LANES), lambda b, i: (b, 0, 0)),
                  pl.BlockSpec((1, seq + 2 * A_BLOCK, LANES), lambda b, i: (b, 0, 0)),
                  pl.BlockSpec((1, A_BLOCK, 1), lambda b, i: (b, i, 0)),
                  pl.BlockSpec((1, 1, seq + 2 * A_BLOCK), lambda b, i: (b, 0, 0))],
        out_specs=pl.BlockSpec((1, A_BLOCK, qw), lambda b, i: (b, i, 0)),
        compiler_params=_cparams(("parallel", "parallel")),
    )(sink, proj, kp, vp, posc, posr)


def wattn_bwd(proj, kp, vp, posc, posr, sink, dmo):
    bsz, seq, _ = proj.shape
    nb = seq // A_BLOCK
    qw = A_HEADS * LANES
    sp = seq + 2 * A_BLOCK

    def body(sink_ref, q_ref, kp_ref, vp_ref, posc_ref, posr_ref, do_ref,
             dq_ref, dk_ref, dv_ref, ds_ref):
        i = pl.program_id(1)

        @pl.when(i == 0)
        def _():
            dk_ref[...] = jnp.zeros_like(dk_ref)
            dv_ref[...] = jnp.zeros_like(dv_ref)

        @pl.when(jnp.logical_and(i == 0, pl.program_id(0) == 0))
        def _():
            ds_ref[...] = jnp.zeros_like(ds_ref)

        start, dist, valid = _wattn_block(i, seq, posc_ref, posr_ref)
        kb = kp_ref[0, pl.ds(start, A_BAND), :]
        vb = vp_ref[0, pl.ds(start, A_BAND), :]
        lane = lax.broadcasted_iota(jnp.int32, (A_GROUP * A_BLOCK, LANES), 1)
        lane1 = lax.broadcasted_iota(jnp.int32, (1, LANES), 1)
        dk_acc = jnp.zeros((A_BAND, LANES), F32)
        dv_acc = jnp.zeros((A_BAND, LANES), F32)
        dsink = jnp.zeros((1, LANES), F32)
        for g in range(A_KV_HEADS):
            heads = _group_heads(g)
            q4 = _stack_heads(q_ref, g, BF16)
            p, psink = _wattn_probs(q4, kb, dist, valid,
                                    _per_head_column([_alibi_slope(h) for h in heads]),
                                    _per_head_column([sink_ref[h] for h in heads]))
            pb = p.astype(BF16)
            do = _stack_heads(do_ref, g, F32)
            dob = do.astype(BF16)
            mine = (lane >= A_HEAD_DIM) if g == 1 else (lane < A_HEAD_DIM)
            o = jnp.where(mine, _dot(pb, vb, NN), 0.0)
            delta = jnp.sum(do * o, axis=-1, keepdims=True)
            dp = _dot(dob, vb, NT)
            ds = (p * (dp - delta) * A_SCALE).astype(BF16)
            dq = _dot(ds, kb, NN).astype(BF16)
            dk_acc = dk_acc + _dot(ds, q4, TN)
            dv_acc = dv_acc + _dot(pb, dob, TN)
            sd = psink * delta
            for j, h in enumerate(heads):
                dq_ref[0, :, h * LANES:(h + 1) * LANES] = dq[j * A_BLOCK:(j + 1) * A_BLOCK]
                dsh = -jnp.sum(sd[j * A_BLOCK:(j + 1) * A_BLOCK], axis=0, keepdims=True)
                dsink = dsink + jnp.where(lane1 == h, dsh, 0.0)
        dk_ref[0, pl.ds(start, A_BAND), :] += dk_acc
        dv_ref[0, pl.ds(start, A_BAND), :] += dv_acc
        ds_ref[0] += dsink

    full = pl.BlockSpec((1, sp, LANES), lambda b, i: (b, 0, 0))
    return pl.pallas_call(
        body, name="wattn_bwd",
        out_shape=[jax.ShapeDtypeStruct((bsz, seq, qw), BF16),
                   jax.ShapeDtypeStruct((bsz, sp, LANES), F32),
                   jax.ShapeDtypeStruct((bsz, sp, LANES), F32),
                   jax.ShapeDtypeStruct((1, 1, LANES), F32)],
        grid=(bsz, nb),
        in_specs=[pl.BlockSpec(memory_space=pltpu.SMEM),
                  pl.BlockSpec((1, A_BLOCK, qw), lambda b, i: (b, i, 0)),
                  full, full,
                  pl.BlockSpec((1, A_BLOCK, 1), lambda b, i: (b, i, 0)),
                  pl.BlockSpec((1, 1, sp), lambda b, i: (b, 0, 0)),
                  pl.BlockSpec((1, A_BLOCK, qw), lambda b, i: (b, i, 0))],
        out_specs=[pl.BlockSpec((1, A_BLOCK, qw), lambda b, i: (b, i, 0)), full, full,
                   pl.BlockSpec((1, 1, LANES), lambda b, i: (0, 0, 0))],
        compiler_params=_cparams(("arbitrary", "arbitrary")),
    )(sink, proj, kp, vp, posc, posr, dmo)


HG_Q, HG_FF, HG_FB, HG_I, HG_G = 10, 14, 18, 22, 26
HG_OUT = 8
CH = B_CHUNK
HG_GROUP = 4
GR = HG_GROUP * CH


def _hgrn_consts(reverse):
    r = lax.broadcasted_iota(jnp.int32, (GR, GR), 0)
    c = lax.broadcasted_iota(jnp.int32, (GR, GR), 1)
    same = jnp.right_shift(r, 6) == jnp.right_shift(c, 6)
    incl = same & ((c >= r) if reverse else (c <= r))
    row = lax.broadcasted_iota(jnp.int32, (HG_GROUP, CH, LANES), 1)
    mid = CH // 2 if reverse else CH // 2 - 1
    end = 0 if reverse else CH - 1
    return incl, row == mid, row == end


def _per_chunk(x, sel=None):
    x3 = x.reshape(HG_GROUP, CH, LANES)
    return jnp.sum(x3 if sel is None else jnp.where(sel, x3, 0.0), axis=1, keepdims=True)


def _to_rows(x3):
    return jnp.broadcast_to(x3, (HG_GROUP, CH, LANES)).reshape(GR, LANES)


def _chunk_cumsum(x, reverse):
    pos = lax.broadcasted_iota(jnp.int32, (GR, LANES), 0) & (CH - 1)
    s = 1
    while s < CH:
        if reverse:
            x = x + jnp.where(pos < CH - s, pltpu.roll(x, GR - s, axis=0), 0.0)
        else:
            x = x + jnp.where(pos >= s, pltpu.roll(x, s, axis=0), 0.0)
        s *= 2
    return x


def _block_diag(x):
    chunk = jnp.right_shift(lax.broadcasted_iota(jnp.int32, (GR, LANES), 0), 6)
    return jnp.concatenate([jnp.where(chunk == n, x, jnp.zeros_like(x)) for n in range(HG_GROUP)], axis=1)


def _hgrn_gates(hf, lb):
    sig = _sigmoid(hf)
    f = lb + (1.0 - lb) * sig
    return 1.0 - f, jnp.log(f), sig, f


def _hgrn_decays(lf, consts, reverse):
    _, is_mid, is_end = consts
    b = _chunk_cumsum(lf, reverse)
    return b, _per_chunk(b, is_mid), _per_chunk(b, is_end)


def _chunk_order(reverse):
    return range(HG_GROUP - 1, -1, -1) if reverse else range(HG_GROUP)


def _lane_block(x, n):
    return x[:, n * LANES:(n + 1) * LANES]


def _hgrn_states(vb, kd, dec3, st, reverse):
    inc = _dot(vb, _block_diag(kd), TN)
    entering = [None] * HG_GROUP
    for n in _chunk_order(reverse):
        entering[n] = st
        st = dec3[n] * st + _lane_block(inc, n)
    return entering, st


def _hgrn_group(q, k, v, lf, st, consts, reverse):
    b, bm3, be3 = _hgrn_decays(lf, consts, reverse)
    bm, be = _to_rows(bm3), _to_rows(be3)
    qs = (q * jnp.exp(b - bm)).astype(BF16)
    ks = (k * jnp.exp(bm - b)).astype(BF16)
    a = jnp.where(consts[0], _dot(qs, ks, NT), 0.0).astype(BF16)
    qe = (q * jnp.exp(b)).astype(BF16)
    vb = v.astype(BF16)
    kd = (k * jnp.exp(be - b)).astype(BF16)
    entering, st = _hgrn_states(vb, kd, jnp.exp(be3), st, reverse)
    s_all = jnp.concatenate([e.astype(BF16) for e in entering], axis=1)
    return _dot(a, vb, NN) + _dot(_block_diag(qe), s_all, NT), st


def _silu(x):
    return x * _sigmoid(x)


def hgrn_fwd(proj, mo, lb, nw, rider=None):
    bsz, seq, _ = proj.shape
    n_gr = seq // GR
    ts = _rows_tile(seq, 512)

    def body(hq_ref, hff_ref, hfb_ref, hi_ref, hg_ref, mo_in_ref, lb_ref, nw_ref, y_ref, ot_ref):
        del mo_in_ref
        lb_v, nw_v = lb_ref[0], nw_ref[0]

        def run(hf_ref, reverse, first):
            consts = _hgrn_consts(reverse)

            def step(t, st):
                n = (n_gr - 1 - t) if reverse else t
                rows = pl.ds(pl.multiple_of(n * GR, GR), GR)
                k, lf, _, _ = _hgrn_gates(hf_ref[0, rows, :], lb_v)
                o, st = _hgrn_group(_silu(hq_ref[0, rows, :]), k, hi_ref[0, rows, :], lf, st, consts, reverse)
                if first:
                    ot_ref[0, rows, :] = o
                else:
                    ot_ref[0, rows, :] += o
                return st

            lax.fori_loop(0, n_gr, step, jnp.zeros((LANES, LANES), F32))

        run(hff_ref, False, True)
        run(hfb_ref, True, False)

        def finish(i, carry):
            rows = pl.ds(pl.multiple_of(i * ts, ts), ts)
            o = ot_ref[0, rows, :]
            r = lax.rsqrt(jnp.mean(o * o, axis=-1, keepdims=True) + RMS_EPS)
            y_ref[0, rows, :] = (o * r * nw_v * _silu(hg_ref[0, rows, :])).astype(BF16)
            return carry

        lax.fori_loop(0, seq // ts, finish, 0)

    def col(g):
        return pl.BlockSpec((1, seq, LANES), lambda b, h: (b, 0, g + h))

    par = pl.BlockSpec((1, 1, LANES), lambda b, h: (h, 0, 0))
    args = [proj, proj, proj, proj, proj, mo, lb, nw]
    in_specs = [col(HG_Q), col(HG_FF), col(HG_FB), col(HG_I), col(HG_G), pl.BlockSpec(memory_space=pl.ANY), par, par]
    out_shape = [jax.ShapeDtypeStruct(mo.shape, BF16), jax.ShapeDtypeStruct((bsz, seq, B_HEADS * LANES), F32)]
    out_specs = [col(HG_OUT), col(0)]
    scratch, aliases = [], {5: 0}
    if rider is not None:
        def first():
            return jnp.logical_and(pl.program_id(0) == 0, pl.program_id(1) == 0)

        def last():
            return jnp.logical_and(pl.program_id(0) == bsz - 1, pl.program_id(1) == B_HEADS - 1)

        body = rider.wrap(body, len(args), len(out_shape), first, last)
        args, in_specs, out_shape, out_specs, scratch, more = rider.extend(
            len(args), len(out_shape), args, in_specs, out_shape, out_specs, scratch)
        aliases.update(more)
    return pl.pallas_call(
        body, name="hgrn_fwd", out_shape=out_shape, grid=(bsz, B_HEADS),
        in_specs=in_specs, out_specs=out_specs, scratch_shapes=scratch, input_output_aliases=aliases,
        compiler_params=_cparams(("arbitrary", "arbitrary") if rider is not None else ("parallel", "parallel")),
    )(*args)


def hgrn_out_bwd(proj, otot, dmo, nw):
    bsz, seq, _ = proj.shape
    ts = _rows_tile(seq, 512)

    def body(hg_ref, ot_ref, dmo_ref, nw_ref, do_ref, dhg_ref, dnw_ref):
        nw_v = nw_ref[0]
        dy = dmo_ref[0].astype(F32)
        hg = hg_ref[0]
        o = ot_ref[0]
        sg = _sigmoid(hg)
        r = lax.rsqrt(jnp.mean(o * o, axis=-1, keepdims=True) + RMS_EPS)
        dhg_ref[0] = (dy * (o * r * nw_v) * (sg * (1.0 + hg * (1.0 - sg)))).astype(BF16)
        drn = dy * (hg * sg)
        dnw = jnp.sum(drn * o * r, axis=0, keepdims=True)
        wd = drn * nw_v
        do_ref[0] = r * wd - o * (r * r * r) * jnp.mean(o * wd, axis=-1, keepdims=True)
        first = jnp.logical_and(pl.program_id(1) == 0, pl.program_id(2) == 0)

        @pl.when(first)
        def _():
            dnw_ref[0] = dnw

        @pl.when(jnp.logical_not(first))
        def _():
            dnw_ref[0] += dnw

    def col(g):
        return pl.BlockSpec((1, ts, LANES), lambda h, b, s: (b, s, g + h))

    par = pl.BlockSpec((1, 1, LANES), lambda h, b, s: (h, 0, 0))
    return pl.pallas_call(
        body, name="hgrn_out_bwd",
        out_shape=[jax.ShapeDtypeStruct((bsz, seq, B_HEADS * LANES), F32),
                   jax.ShapeDtypeStruct((bsz, seq, B_HEADS * LANES), BF16),
                   jax.ShapeDtypeStruct((B_HEADS, 1, LANES), F32)],
        grid=(B_HEADS, bsz, seq // ts),
        in_specs=[col(HG_G), col(0), col(HG_OUT), par],
        out_specs=[col(0), col(0), par],
        compiler_params=_cparams(("parallel", "arbitrary", "arbitrary")),
    )(proj, otot, dmo, nw)


def hgrn_bwd(proj, do, lb, rider=None):
    bsz, seq, _ = proj.shape
    n_ch = seq // CH
    n_gr = seq // GR
    ts = _rows_tile(seq, 512)

    def body(hq_ref, hff_ref, hfb_ref, hi_ref, do_scr, lb_ref,
             dhq_ref, dhff_ref, dhfb_ref, dhi_ref, dlb_ref, st_scr, dq_scr, dv_scr):
        lb_v = lb_ref[0]
        do_scr = do_scr.at[0]

        def run(hf_ref, dhf_ref, reverse, first, dlb0):
            consts = _hgrn_consts(reverse)
            incl, is_mid, is_end = consts

            def load(n):
                rows = pl.ds(pl.multiple_of(n * GR, GR), GR)
                hf = hf_ref[0, rows, :]
                k, lf, sig, f = _hgrn_gates(hf, lb_v)
                return rows, _silu(hq_ref[0, rows, :]), k, hi_ref[0, rows, :], lf, sig, f

            def fwd_step(t, st):
                n = (n_gr - 1 - t) if reverse else t
                _, _, k, v, lf, _, _ = load(n)
                b, _, be3 = _hgrn_decays(lf, consts, reverse)
                kd = (k * jnp.exp(_to_rows(be3) - b)).astype(BF16)
                entering, st = _hgrn_states(v.astype(BF16), kd, jnp.exp(be3), st, reverse)
                for c in range(HG_GROUP):
                    st_scr[n * HG_GROUP + c] = entering[c].astype(BF16)
                return st

            lax.fori_loop(0, n_gr, fwd_step, jnp.zeros((LANES, LANES), F32))

            def bwd_step(t, carry):
                gt, dlb = carry
                n = t if reverse else (n_gr - 1 - t)
                rows, q, k, v, lf, sig, f = load(n)
                do_c = do_scr[rows, :].astype(BF16)
                b, bm3, be3 = _hgrn_decays(lf, consts, reverse)
                bm, be = _to_rows(bm3), _to_rows(be3)
                e_qs, e_ks, e_q, e_kd = jnp.exp(b - bm), jnp.exp(bm - b), jnp.exp(b), jnp.exp(be - b)
                dec3 = jnp.exp(be3)
                qs, ks, qe, kd = q * e_qs, k * e_ks, q * e_q, k * e_kd
                qs_b, ks_b, qe_b, kd_b = qs.astype(BF16), ks.astype(BF16), qe.astype(BF16), kd.astype(BF16)
                vb = v.astype(BF16)
                a = jnp.where(incl, _dot(qs_b, ks_b, NT), 0.0).astype(BF16)
                da = jnp.where(incl, _dot(do_c, vb, NT), 0.0).astype(BF16)
                dv = _dot(a, do_c, TN)
                dqs = _dot(da, ks_b, NN)
                dks = _dot(da, qs_b, TN)
                zed = _dot(do_c, _block_diag(qe_b), TN)
                sts = [st_scr[n * HG_GROUP + c] for c in range(HG_GROUP)]
                gts, ddec = [None] * HG_GROUP, [None] * HG_GROUP
                for c in reversed(list(_chunk_order(reverse))):
                    gts[c] = gt
                    ddec[c] = jnp.sum(gt * sts[c].astype(F32), axis=0, keepdims=True)
                    gt = dec3[c] * gt + _lane_block(zed, c)
                g_b = [g.astype(BF16) for g in gts]
                dqe = _dot(_block_diag(do_c), jnp.concatenate(sts, axis=0), NN)
                dkd = _dot(_block_diag(vb), jnp.concatenate(g_b, axis=0), NN)
                dv = dv + _dot(_block_diag(kd_b), jnp.concatenate(g_b, axis=1), NT)
                ddec3 = jnp.stack(ddec, axis=0)
                dq = dqs * e_qs + dqe * e_q
                dk = dks * e_ks + dkd * e_kd
                t_qs, t_ks, t_kd = dqs * qs, dks * ks, dkd * kd
                db = (t_qs - t_ks + dqe * qe - t_kd).reshape(HG_GROUP, CH, LANES)
                dbm3 = _per_chunk(t_ks - t_qs)
                dbe3 = _per_chunk(t_kd) + ddec3 * dec3
                db = (db + jnp.where(is_mid, dbm3, 0.0) + jnp.where(is_end, dbe3, 0.0)).reshape(GR, LANES)
                dlf = _chunk_cumsum(db, not reverse)
                df = dlf / f - dk
                dhf_ref[0, rows, :] = (df * (1.0 - lb_v) * sig * (1.0 - sig)).astype(BF16)
                dlb = dlb + jnp.sum(df * (1.0 - sig), axis=0, keepdims=True)
                if first:
                    dq_scr[rows, :] = dq
                    dv_scr[rows, :] = dv
                else:
                    dq_scr[rows, :] += dq
                    dv_scr[rows, :] += dv
                return gt, dlb

            return lax.fori_loop(0, n_gr, bwd_step, (jnp.zeros((LANES, LANES), F32), dlb0))[1]

        dlb = run(hff_ref, dhff_ref, False, True, jnp.zeros((1, LANES), F32))
        dlb = run(hfb_ref, dhfb_ref, True, False, dlb)

        @pl.when(pl.program_id(1) == 0)
        def _():
            dlb_ref[0] = dlb

        @pl.when(pl.program_id(1) > 0)
        def _():
            dlb_ref[0] += dlb

        def finish(i, carry):
            rows = pl.ds(pl.multiple_of(i * ts, ts), ts)
            hq = hq_ref[0, rows, :]
            sq = _sigmoid(hq)
            dhq_ref[0, rows, :] = (dq_scr[rows, :] * (sq * (1.0 + hq * (1.0 - sq)))).astype(BF16)
            dhi_ref[0, rows, :] = dv_scr[rows, :].astype(BF16)
            return carry

        lax.fori_loop(0, seq // ts, finish, 0)

    def col(g):
        return pl.BlockSpec((1, seq, LANES), lambda h, b: (b, 0, g + h))

    par = pl.BlockSpec((1, 1, LANES), lambda h, b: (h, 0, 0))
    wide = jax.ShapeDtypeStruct((bsz, seq, B_HEADS * LANES), BF16)
    small = jax.ShapeDtypeStruct((B_HEADS, 1, LANES), F32)
    args = [proj, proj, proj, proj, do, lb]
    in_specs = [col(HG_Q), col(HG_FF), col(HG_FB), col(HG_I), col(0), par]
    out_shape = [wide, wide, wide, wide, small]
    out_specs = [col(0), col(0), col(0), col(0), par]
    scratch = [pltpu.VMEM((n_ch, LANES, LANES), BF16), pltpu.VMEM((seq, LANES), F32), pltpu.VMEM((seq, LANES), F32)]
    aliases = {}
    if rider is not None:
        def first():
            return jnp.logical_and(pl.program_id(0) == 0, pl.program_id(1) == 0)

        def last():
            return jnp.logical_and(pl.program_id(0) == B_HEADS - 1, pl.program_id(1) == bsz - 1)

        body = rider.wrap(body, len(args), len(out_shape), first, last)
        args, in_specs, out_shape, out_specs, scratch, aliases = rider.extend(
            len(args), len(out_shape), args, in_specs, out_shape, out_specs, scratch)
    return pl.pallas_call(
        body, name="hgrn_bwd", out_shape=out_shape, grid=(B_HEADS, bsz),
        in_specs=in_specs, out_specs=out_specs, scratch_shapes=scratch, input_output_aliases=aliases,
        compiler_params=_cparams(("arbitrary", "arbitrary")),
    )(*args)


def _whole(shape):
    return pl.BlockSpec(shape, lambda: (0,) * len(shape))


def silu_small(x):
    def body(x_ref, o_ref):
        o_ref[...] = _silu(x_ref[...])

    return pl.pallas_call(body, name="silu_small", out_shape=jax.ShapeDtypeStruct(x.shape, F32),
                          in_specs=[_whole(x.shape)], out_specs=_whole(x.shape))(x)


def _softmax_rows(x):
    e = jnp.exp(x - jnp.max(x, axis=0, keepdims=True))
    return e / jnp.sum(e, axis=0, keepdims=True)


def lb_fwd(logits, layer):
    n, w = logits.shape

    def body(x_ref, o_ref):
        p = _softmax_rows(x_ref[...])
        row = lax.broadcasted_iota(jnp.int32, (n, w), 0)
        o_ref[...] = jnp.sum(jnp.where(row <= layer, p, 0.0), axis=0, keepdims=True)

    return pl.pallas_call(body, name="lb_fwd", out_shape=jax.ShapeDtypeStruct((1, w), F32),
                          in_specs=[_whole((n, w))], out_specs=_whole((1, w)))(logits)


def lb_bwd(logits, dlb, layer):
    n, w = logits.shape

    def body(x_ref, d_ref, o_ref):
        p = _softmax_rows(x_ref[...])
        row = lax.broadcasted_iota(jnp.int32, (n, w), 0)
        dp = jnp.where(row <= layer, d_ref[...], 0.0)
        o_ref[...] = p * (dp - jnp.sum(p * dp, axis=0, keepdims=True))

    return pl.pallas_call(body, name="lb_bwd", out_shape=jax.ShapeDtypeStruct((n, w), F32),
                          in_specs=[_whole((n, w)), _whole((1, w))], out_specs=_whole((n, w)))(logits, dlb)


def sum_parts(x, name):
    p, r, c = x.shape
    tr = _rows_tile(r, max(8, (1 << 23) // (4 * c * p) // 8 * 8))

    def body(x_ref, o_ref):
        acc = x_ref[0].astype(F32)
        for j in range(1, p):
            acc = acc + x_ref[j].astype(F32)
        o_ref[...] = acc

    return pl.pallas_call(
        body, name=name, out_shape=jax.ShapeDtypeStruct((r, c), F32),
        grid=(r // tr,),
        in_specs=[pl.BlockSpec((p, tr, c), lambda i: (0, i, 0))],
        out_specs=pl.BlockSpec((tr, c), lambda i: (i, 0)),
        compiler_params=_cparams(("parallel",)),
    )(x)


def adamw(w, g, m, v, name):
    r, c = w.shape
    tr = _rows_tile(r, max(8, (1 << 20) // (4 * c) // 8 * 8))
    c1 = 1.0 - ADAM_B1 ** ADAM_STEP
    c2 = 1.0 - ADAM_B2 ** ADAM_STEP

    def body(w_ref, g_ref, m_ref, v_ref, d_ref, nm_ref, nv_ref):
        g = g_ref[...]
        nm = ADAM_B1 * m_ref[...] + (1.0 - ADAM_B1) * g
        nv = ADAM_B2 * v_ref[...] + (1.0 - ADAM_B2) * (g * g)
        nm_ref[...] = nm
        nv_ref[...] = nv
        d_ref[...] = -ADAM_LR * ((nm / c1) / (jnp.sqrt(nv / c2) + ADAM_EPS) + ADAM_WD * w_ref[...])

    spec = pl.BlockSpec((tr, c), lambda i: (i, 0))
    sds = jax.ShapeDtypeStruct((r, c), F32)
    return pl.pallas_call(
        body, name=name, out_shape=[sds, sds, sds], grid=(r // tr,),
        in_specs=[spec, spec, spec, spec], out_specs=[spec, spec, spec],
        compiler_params=_cparams(("parallel",)),
    )(w, g, m, v)


ANY = pl.BlockSpec(memory_space=pl.ANY)


def _my_place():
    return lax.axis_index("x"), lax.axis_index("y"), lax.axis_index("c")


def all_gather(x, name):
    def body(x_ref, out_ref, send_sems, recv_sems, local_sem):
        mx, my, mc = _my_place()
        me, sibling = (mx, my, mc), (mx, my, 1 - mc)
        chips = [(1 - mx, my), (mx, 1 - my), (1 - mx, 1 - my)]

        def slot(px, py, pc):
            return out_ref.at[4 * px + 2 * py + pc]

        def copy(k, block, to, src=None):
            return pltpu.make_async_remote_copy(
                src_ref=slot(*block) if src is None else src, dst_ref=slot(*block),
                send_sem=send_sems.at[k], recv_sem=recv_sems.at[k],
                device_id=to, device_id_type=MESH_ID)

        mine = pltpu.make_async_copy(x_ref, slot(*me), local_sem)
        mine.start()
        first = [copy(0, me, sibling, src=x_ref)]
        first += [copy(1 + j, me, (*chip, mc), src=x_ref) for j, chip in enumerate(chips)]
        for cp in first:
            cp.start()
        passed = [copy(4 + j, (*chip, mc), sibling) for j, chip in enumerate(chips)]
        for j, chip in enumerate(chips):
            copy(1 + j, (*chip, mc), me).wait_recv()
            passed[j].start()
        copy(0, sibling, me).wait_recv()
        for j, chip in enumerate(chips):
            copy(4 + j, (*chip, 1 - mc), me).wait_recv()
        for cp in first + passed:
            cp.wait_send()
        mine.wait()

    return pl.pallas_call(
        body, name=name,
        out_shape=jax.ShapeDtypeStruct((N_DEV,) + x.shape, x.dtype),
        in_specs=[ANY], out_specs=ANY,
        scratch_shapes=[pltpu.SemaphoreType.DMA((7,)), pltpu.SemaphoreType.DMA((7,)),
                        pltpu.SemaphoreType.DMA(())],
    )(x)


def swap_with_sibling(x, name):
    def body(x_ref, out_ref, send_sems, recv_sems):
        copies = _sibling_swap_copies(x_ref, out_ref, send_sems, recv_sems)
        for cp, _ in copies:
            cp.start()
        for _, cp in copies:
            cp.wait()

    return pl.pallas_call(
        body, name=name,
        out_shape=jax.ShapeDtypeStruct((x.shape[0],) + x.shape[2:], x.dtype),
        in_specs=[ANY], out_specs=ANY,
        scratch_shapes=[pltpu.SemaphoreType.DMA((4,)), pltpu.SemaphoreType.DMA((4,))],
    )(x)


def pair_add(x, r, mc):
    n, _, rows, c = x.shape
    tr = _rows_tile(rows, 512)

    def body(mc_ref, x_ref, r_ref, o_ref):
        del mc_ref
        o_ref[0] = (x_ref[0, 0].astype(F32) + r_ref[0].astype(F32)).astype(BF16)

    return pl.pallas_call(
        body, name="pair_add",
        out_shape=jax.ShapeDtypeStruct((n, rows, c), BF16),
        grid_spec=pltpu.PrefetchScalarGridSpec(
            num_scalar_prefetch=1, grid=(n, rows // tr),
            in_specs=[pl.BlockSpec((1, 1, tr, c), lambda j, i, s: (j, s[0], i, 0)),
                      pl.BlockSpec((1, tr, c), lambda j, i, s: (j, i, 0))],
            out_specs=pl.BlockSpec((1, tr, c), lambda j, i, s: (j, i, 0))),
        compiler_params=_cparams(("parallel", "parallel")),
    )(mc.reshape(1).astype(jnp.int32), x, r)


def chip_all_to_all(x, name):
    def body(x_ref, out_ref, send_sems, recv_sems, local_sem):
        mx, my, mc = _my_place()
        me = 2 * mx + my
        mine = pltpu.make_async_copy(x_ref.at[me], out_ref.at[me], local_sem)
        mine.start()
        copies = []
        for k, (fx, fy) in enumerate(((1, 0), (0, 1), (1, 1))):
            px = 1 - mx if fx else mx
            py = 1 - my if fy else my
            peer = 2 * px + py
            cp = pltpu.make_async_remote_copy(
                src_ref=x_ref.at[peer], dst_ref=out_ref.at[me],
                send_sem=send_sems.at[k], recv_sem=recv_sems.at[k],
                device_id=(px, py, mc), device_id_type=MESH_ID)
            cp.start()
            copies.append((cp, peer, (px, py, mc), k))
        for cp, peer, to, k in copies:
            pltpu.make_async_remote_copy(
                src_ref=x_ref.at[peer], dst_ref=out_ref.at[peer],
                send_sem=send_sems.at[k], recv_sem=recv_sems.at[k],
                device_id=to, device_id_type=MESH_ID).wait_recv()
        for cp, _, _, _ in copies:
            cp.wait_send()
        mine.wait()

    return pl.pallas_call(
        body, name=name,
        out_shape=jax.ShapeDtypeStruct(x.shape, x.dtype),
        in_specs=[ANY], out_specs=ANY,
        scratch_shapes=[pltpu.SemaphoreType.DMA((3,)), pltpu.SemaphoreType.DMA((3,)),
                        pltpu.SemaphoreType.DMA(())],
    )(x)


ROW_W = D_MODEL
FF_SH = D_FF // N_DEV
FFN_PER_LAYER = 2
MDOWN_SEC_ROWS = 80
MDOWN_ROWS = 128 * 544 // ROW_W
PART_SECS = (
    (("gate", FFN_PER_LAYER * FF_SH), ("up", FFN_PER_LAYER * FF_SH), ("down", FFN_PER_LAYER * FF_SH),
     ("hin", 416), ("hout", 128)),
    (("gate", FFN_PER_LAYER * FF_SH), ("up", FFN_PER_LAYER * FF_SH), ("down", FFN_PER_LAYER * FF_SH),
     ("mout", 128), ("mdown", MDOWN_SEC_ROWS), ("uq", 48), ("ukv", 64)),
)
SEC, PART_ROWS = [], []
for _secs in PART_SECS:
    _table, _off = {}, 0
    for _name, _rows in _secs:
        _table[_name] = (_off, _rows)
        _off += _rows
    SEC.append(_table)
    PART_ROWS.append(_off)
A_Q_W = A_HEADS * A_HEAD_DIM
_KVHEAD = (np.arange(A_HEADS) // A_GROUP).reshape(A_HEADS, 1, 1)


def pack_shards(layer, sh, dtype):
    def rows(a):
        return a.astype(dtype).reshape(-1, ROW_W)

    def t(a):
        return rows(jnp.swapaxes(a, -1, -2))

    parts = [t(sh["ffn_w_gate"][layer]), t(sh["ffn_w_up"][layer]), rows(sh["ffn_w_down"][layer])]
    if layer == 0:
        parts += [t(sh["hyb_w_in"]), rows(sh["hyb_w_out"])]
    else:
        parts += [rows(sh["mla_w_out"]),
                  jnp.pad(rows(sh["mla_w_down"]), ((0, MDOWN_SEC_ROWS - MDOWN_ROWS), (0, 0))),
                  t(sh["mla_w_uq"]), t(sh["mla_w_ukv"])]
    return jnp.concatenate(parts, axis=0)


def unpack_shards(parts):
    def sec(layer, name):
        o, n = SEC[layer][name]
        return parts[layer][o:o + n]

    def col(layer, name, r, c):
        return jnp.swapaxes(sec(layer, name).reshape(-1, c, r), -1, -2)

    def both(f):
        return jnp.stack([f(0), f(1)])

    return dict(
        ffn_w_gate=both(lambda l: col(l, "gate", 1024, FF_SH)),
        ffn_w_up=both(lambda l: col(l, "up", 1024, FF_SH)),
        ffn_w_down=both(lambda l: sec(l, "down").reshape(FFN_PER_LAYER, FF_SH, 1024)),
        hyb_w_in=col(0, "hin", 1024, 416),
        hyb_w_out=sec(0, "hout").reshape(1, 128, 1024),
        mla_w_out=sec(1, "mout").reshape(1, 128, 1024),
        mla_w_down=sec(1, "mdown")[:MDOWN_ROWS].reshape(1, 128, 544),
        mla_w_uq=col(1, "uq", 256, 192),
        mla_w_ukv=col(1, "ukv", 256, 256),
    )


def _pad_qheads(w):
    a = w.reshape(A_HEADS, 1, A_HEAD_DIM, -1)
    kvh = _KVHEAD[..., None]
    both = jnp.concatenate([jnp.where(kvh == 0, a, 0), jnp.where(kvh == 1, a, 0)], axis=1)
    return both.reshape(A_HEADS * LANES, -1)


def _unpad_qheads(w):
    a = w.reshape(A_HEADS, 2, A_HEAD_DIM, -1)
    return jnp.where(_KVHEAD == 0, a[:, 0], a[:, 1]).reshape(A_Q_W, -1)


def unpack_full(layer, g):
    def sec(name):
        o, n = SEC[layer][name]
        return g[:, o:o + n]

    def ffn(name):
        return jnp.swapaxes(sec(name).reshape(N_DEV, FFN_PER_LAYER, FF_SH, ROW_W), 0, 1).reshape(
            FFN_PER_LAYER, D_FF, ROW_W)

    def z(*s):
        return jnp.zeros(s, g.dtype)

    def halves(a):
        return a.reshape(FFN_PER_LAYER, 2, 1, FF_HALF, ROW_W)

    gate_t, up_t, down = ffn("gate"), ffn("up"), ffn("down")
    w = dict(wgu_t=jnp.concatenate([halves(gate_t), halves(up_t)], axis=2).reshape(FFN_PER_LAYER, 2 * D_FF, ROW_W),
             wd=down)
    w.update(wgu=jnp.swapaxes(w["wgu_t"], 1, 2), wd_t=jnp.swapaxes(down, 1, 2))
    if layer == 0:
        hin_t = sec("hin").reshape(-1, ROW_W)
        hout = sec("hout").reshape(-1, ROW_W)
        w.update(hin_t=jnp.concatenate([_pad_qheads(hin_t[:A_Q_W]), hin_t[A_Q_W:]], axis=0),
                 hout=jnp.concatenate([_pad_qheads(hout[:A_Q_W]), hout[A_Q_W:]], axis=0))
        w.update(hin=w["hin_t"].T, hout_t=w["hout"].T)
    else:
        mout = sec("mout").reshape(C_HEADS, C_V, ROW_W)
        mdown = sec("mdown")[:, :MDOWN_ROWS].reshape(D_MODEL, 544)
        uq_t = sec("uq").reshape(C_HEADS, C_NOPE + C_ROPE, C_Q_RANK)
        w.update(mout=jnp.concatenate([z(C_HEADS, C_NOPE, ROW_W), mout], axis=1).reshape(MLA_HEAD_PAD, ROW_W),
                 mdown=jnp.concatenate([mdown[:, :512], z(D_MODEL, 64), mdown[:, 512:], z(D_MODEL, 32)], axis=1),
                 uq_t=jnp.concatenate([uq_t, z(C_HEADS, 32, C_Q_RANK)], axis=1).reshape(MLA_HEAD_PAD, C_Q_RANK),
                 ukv_t=sec("ukv").reshape(MLA_HEAD_PAD, C_KV_RANK))
        w.update(mout_t=w["mout"].T, mdown_t=w["mdown"].T, uq=w["uq_t"].T, ukv=w["ukv_t"].T)
    return w


def pack_full(layer, d):
    gu = jnp.stack(d["wgu_t"]).reshape(FFN_PER_LAYER, 2, 2, FF_HALF, ROW_W)

    def split(a):
        return a.reshape(N_DEV, -1, ROW_W)

    def ffn(a):
        return jnp.swapaxes(a.reshape(FFN_PER_LAYER, N_DEV, FF_SH, ROW_W), 0, 1).reshape(
            N_DEV, FFN_PER_LAYER * FF_SH, ROW_W)

    parts = [ffn(gu[:, :, 0]), ffn(gu[:, :, 1]), ffn(jnp.stack(d["wd"]))]
    if layer == 0:
        hin = jnp.concatenate([_unpad_qheads(d["hin_t"][:A_HEADS * LANES]), d["hin_t"][A_HEADS * LANES:]], axis=0)
        hout = jnp.concatenate([_unpad_qheads(d["hout"][:A_HEADS * LANES]), d["hout"][A_HEADS * LANES:]], axis=0)
        parts += [split(hin), split(hout)]
    else:
        mout = d["mout"].reshape(C_HEADS, LANES, ROW_W)[:, C_NOPE:].reshape(-1, ROW_W)
        mdown = jnp.concatenate([d["mdown"][:, :512], d["mdown"][:, 576:608]], axis=1)
        uq = d["uq_t"].reshape(C_HEADS, LANES, C_Q_RANK)[:, :C_NOPE + C_ROPE]
        parts += [split(mout), jnp.pad(split(mdown), ((0, 0), (0, MDOWN_SEC_ROWS - MDOWN_ROWS), (0, 0))),
                  split(uq), split(d["ukv_t"])]
    return jnp.concatenate(parts, axis=1)


def rope_tables(positions):
    half = C_ROPE // 2
    freqs = ROPE_THETA ** (-jnp.arange(half, dtype=F32) / half)
    ang = positions.astype(F32)[..., None] * freqs
    cos, sin = jnp.cos(ang), jnp.sin(ang)
    shape = positions.shape
    cq = jnp.concatenate([jnp.ones(shape + (C_NOPE,), F32), cos, cos, jnp.zeros(shape + (32,), F32)], axis=-1)
    sg = jnp.concatenate([jnp.zeros(shape + (C_NOPE,), F32), -sin, sin, jnp.zeros(shape + (32,), F32)], axis=-1)
    return cq, sg


GRAD_DT = BF16
def ffn_fwd(h2, w, i):
    gu, a = ffn_up_fused(h2, w["wgu"][i])
    return matmul(a, w["wd"][i], "nn", F32, "ffn_down"), (gu, a)


def ffn_bwd(dy2, h2, w, i, saved, swap=None):
    gu, a = saved
    dgu = ffn_down_bwd_fused(dy2, w["wd_t"][i], gu)
    if swap is None:
        dwd, swapped = matmul(a, dy2, "tn", GRAD_DT, "ffn_down_dw", tm_cap=FF_HALF), None
    else:
        dwd, swapped = matmul(a, dy2, "tn", GRAD_DT, "ffn_down_dw_swap", tm_cap=FF_HALF,
                              rider=ride_swap_with_sibling(swap))
    dwgu_t = matmul(dgu, h2, "tn", GRAD_DT, "ffn_up_dw", tm_cap=FF_HALF)
    dh = matmul(dgu, w["wgu_t"][i], "nn", F32, "ffn_up_dx")
    return dh, dwgu_t, dwd, swapped


def hybrid_fwd(h2, shape, w, aux):
    bsz, seq = shape
    proj = matmul(h2, w["hin"], "nn", F32, "hyb_in").reshape(bsz, seq, HYB_PAD)
    pad = ((0, 0), (A_BLOCK, A_BLOCK), (0, 0))
    kp = jnp.pad(proj[:, :, 8 * LANES:9 * LANES].astype(BF16), pad)
    vp = jnp.pad(proj[:, :, 9 * LANES:10 * LANES].astype(BF16), pad)
    mo = wattn_fwd(proj, kp, vp, aux["posc"], aux["posr"], aux["sink"])
    mo, otot, gathered = hgrn_fwd(proj, mo, aux["lb"], aux["nw"], rider=ride_gather_level1(aux["next_packed"]))
    y, gathered = matmul(mo.reshape(bsz * seq, MO_PAD), w["hout"], "nn", F32, "hyb_out_gather",
                         rider=ride_gather_level2(gathered))
    return y, (proj, kp, vp, mo, otot), gathered


def hybrid_bwd(dy2, h2, shape, w, aux, saved):
    bsz, seq = shape
    proj, kp, vp, mo, otot = saved
    mo2 = mo.reshape(bsz * seq, MO_PAD)
    dmo = matmul(dy2, w["hout_t"], "nn", F32, "hyb_out_dx").reshape(bsz, seq, MO_PAD)
    dhout = matmul(mo2, dy2, "tn", GRAD_DT, "hyb_out_dw")
    dq, dkp, dvp, dsink = wattn_bwd(proj, kp, vp, aux["posc"], aux["posr"], aux["sink"], dmo)
    do, dhg, dnw = hgrn_out_bwd(proj, otot, dmo, aux["nw"])
    dhq, dhff, dhfb, dhi, dlb, aux["exchanged"] = hgrn_bwd(proj, do, aux["lb"],
                                                            rider=ride_chip_all_to_all(aux["pair_sums"]))
    dproj = jnp.concatenate([dq, dkp[:, A_BLOCK:-A_BLOCK].astype(BF16), dvp[:, A_BLOCK:-A_BLOCK].astype(BF16),
                             dhq, dhff, dhfb, dhi, dhg], axis=-1).reshape(bsz * seq, HYB_PAD)
    dhin_t = matmul(dproj, h2, "tn", GRAD_DT, "hyb_in_dw")
    dh = matmul(dproj, w["hin_t"], "nn", F32, "hyb_in_dx")
    return dh, dict(hin_t=dhin_t, hout=dhout), dict(sink=dsink, lb=dlb, nw=dnw)


def mla_fwd(h2, shape, w, aux):
    bsz, seq = shape
    t = bsz * seq
    cqkv = matmul(h2, w["mdown"], "nn", F32, "mla_down").reshape(bsz, seq, MLA_DOWN_PAD)
    nq, nkv, kr = mla_mid_fwd(cqkv, aux["qn"], aux["kvn"], aux["cq"], aux["sg"])
    q = matmul(nq.reshape(t, C_Q_RANK), w["uq"], "nn", F32, "mla_uq").reshape(bsz, seq, MLA_HEAD_PAD)
    qp = q_rope(q, aux["cq"], aux["sg"], False, "q_rope_fwd", out_scale=MLA_QSCALE)
    kv = matmul(nkv.reshape(t, C_KV_RANK), w["ukv"], "nn", BF16, "mla_ukv").reshape(bsz, seq, MLA_HEAD_PAD)
    o, lse = mla_attn_fwd(qp, kv, kr)
    y = matmul(o.reshape(t, MLA_HEAD_PAD), w["mout"], "nn", F32, "mla_out")
    return y, (cqkv, nq, nkv, kr, qp, kv, o, lse)


def mla_bwd(dy2, h2, shape, w, aux, saved):
    bsz, seq = shape
    t = bsz * seq
    cqkv, nq, nkv, kr, qp, kv, o, lse = saved
    do = matmul(dy2, w["mout_t"], "nn", BF16, "mla_out_dx").reshape(bsz, seq, MLA_HEAD_PAD)
    dmout = matmul(o.reshape(t, MLA_HEAD_PAD), dy2, "tn", GRAD_DT, "mla_out_dw")
    dqp, dkv, dkr = mla_attn_bwd(qp, kv, kr, o, do, lse)
    dq = q_rope(dqp, aux["cq"], aux["sg"], True, "q_rope_bwd").reshape(t, MLA_HEAD_PAD)
    dkv2 = dkv.reshape(t, MLA_HEAD_PAD)
    dnq = matmul(dq, w["uq_t"], "nn", F32, "mla_uq_dx").reshape(bsz, seq, C_Q_RANK)
    duq_t = matmul(dq, nq.reshape(t, C_Q_RANK), "tn", GRAD_DT, "mla_uq_dw")
    dnkv = matmul(dkv2, w["ukv_t"], "nn", F32, "mla_ukv_dx").reshape(bsz, seq, C_KV_RANK)
    dukv_t = matmul(dkv2, nkv.reshape(t, C_KV_RANK), "tn", GRAD_DT, "mla_ukv_dw")
    dcqkv, dqn, dkvn = mla_mid_bwd(dnq, dnkv, dkr, cqkv, aux["qn"], aux["kvn"], aux["cq"], aux["sg"])
    dcqkv2 = dcqkv.reshape(t, MLA_DOWN_PAD)
    dmdown = matmul(h2, dcqkv2, "tn", GRAD_DT, "mla_down_dw")
    dh = matmul(dcqkv2, w["mdown_t"], "nn", F32, "mla_down_dx")
    return dh, dict(mdown=dmdown, uq_t=duq_t, ukv_t=dukv_t, mout=dmout), dict(qn=dqn, kvn=dkvn)


W_NAMES = ['ada_w', 'ada_b', 'ln_g', 'ln_b', 'ffn_w_gate', 'ffn_w_up', 'ffn_w_down', 'hyb_w_in', 'hyb_w_out',
           'attn_sink', 'hgrn_lb_logits', 'hgrn_norm_w', 'mla_w_down', 'mla_q_norm', 'mla_kv_norm', 'mla_w_uq',
           'mla_w_ukv', 'mla_w_out']
SMALL_NAMES = ['ada_b', 'ln_g', 'ln_b', 'attn_sink', 'hgrn_lb_logits', 'hgrn_norm_w', 'mla_q_norm', 'mla_kv_norm']
MOD_W = N_SUB * 3 * D_MODEL
MOD_SH = MOD_W // N_DEV
RES_W = (0.5, 1.0, 0.5)


def _rows1024(a, rows=None):
    flat = a.astype(F32).reshape(-1)
    n = flat.shape[0]
    total = (-(-n // ROW_W) if rows is None else rows) * ROW_W
    return jnp.pad(flat, (0, total - n)).reshape(-1, ROW_W)


def kernel(x, c, positions, ada_w, ada_b, ln_g, ln_b, ffn_w_gate, ffn_w_up, ffn_w_down, hyb_w_in, hyb_w_out, attn_sink, hgrn_lb_logits, hgrn_norm_w, mla_w_down, mla_q_norm, mla_kv_norm, mla_w_uq, mla_w_ukv, mla_w_out, loss_target, m_ada_w, m_ada_b, m_ln_g, m_ln_b, m_ffn_w_gate, m_ffn_w_up, m_ffn_w_down, m_hyb_w_in, m_hyb_w_out, m_attn_sink, m_hgrn_lb_logits, m_hgrn_norm_w, m_mla_w_down, m_mla_q_norm, m_mla_kv_norm, m_mla_w_uq, m_mla_w_ukv, m_mla_w_out, v_ada_w, v_ada_b, v_ln_g, v_ln_b, v_ffn_w_gate, v_ffn_w_up, v_ffn_w_down, v_hyb_w_in, v_hyb_w_out, v_attn_sink, v_hgrn_lb_logits, v_hgrn_norm_w, v_mla_w_down, v_mla_q_norm, v_mla_kv_norm, v_mla_w_uq, v_mla_w_ukv, v_mla_w_out):
    weights = dict(zip(W_NAMES, (ada_w, ada_b, ln_g, ln_b, ffn_w_gate, ffn_w_up, ffn_w_down, hyb_w_in, hyb_w_out,
                                 attn_sink, hgrn_lb_logits, hgrn_norm_w, mla_w_down, mla_q_norm, mla_kv_norm,
                                 mla_w_uq, mla_w_ukv, mla_w_out)))
    mom1 = dict(zip(W_NAMES, (m_ada_w, m_ada_b, m_ln_g, m_ln_b, m_ffn_w_gate, m_ffn_w_up, m_ffn_w_down, m_hyb_w_in,
                              m_hyb_w_out, m_attn_sink, m_hgrn_lb_logits, m_hgrn_norm_w, m_mla_w_down, m_mla_q_norm,
                              m_mla_kv_norm, m_mla_w_uq, m_mla_w_ukv, m_mla_w_out)))
    mom2 = dict(zip(W_NAMES, (v_ada_w, v_ada_b, v_ln_g, v_ln_b, v_ffn_w_gate, v_ffn_w_up, v_ffn_w_down, v_hyb_w_in,
                              v_hyb_w_out, v_attn_sink, v_hgrn_lb_logits, v_hgrn_norm_w, v_mla_w_down, v_mla_q_norm,
                              v_mla_kv_norm, v_mla_w_uq, v_mla_w_ukv, v_mla_w_out)))
    bsz, seq, d = x.shape
    t = bsz * seq
    nb_tot = N_DEV * bsz
    me = 4 * lax.axis_index("x") + 2 * lax.axis_index("y") + lax.axis_index("c")

    c_all = all_gather(c, "gather_c").reshape(nb_tot, d)
    cond = silu_small(c_all)
    ada_b_mine = lax.dynamic_slice_in_dim(ada_b, me * MOD_SH, MOD_SH, axis=1)
    modp = jnp.concatenate([matmul(cond, ada_w[l], "nn", F32, "ada_fwd", bias=ada_b_mine[l])
                            for l in range(DEPTH)], axis=1)
    mod_rows = nb_tot * DEPTH * MOD_SH // ROW_W
    small1 = jnp.concatenate([_rows1024(modp), _rows1024(jnp.concatenate(
        [ln_g.reshape(-1), ln_b.reshape(-1), mla_q_norm.reshape(-1), mla_kv_norm.reshape(-1)]), rows=4)], axis=0)
    g1 = all_gather(small1, "gather_mod")
    mod_all = g1[:, :mod_rows].reshape(N_DEV, nb_tot, DEPTH, MOD_SH)
    mod_all = jnp.transpose(mod_all, (1, 2, 0, 3)).reshape(nb_tot, DEPTH, N_SUB, 3, d)
    mod = lax.dynamic_slice_in_dim(mod_all, me * bsz, bsz, axis=0)
    tail = g1[:, mod_rows:].reshape(N_DEV, -1)
    n_ln = DEPTH * N_SUB * LANES
    ln_g_full = jnp.transpose(tail[:, :n_ln].reshape(N_DEV, DEPTH, N_SUB, LANES), (1, 2, 0, 3)).reshape(DEPTH, N_SUB, d)
    ln_b_full = jnp.transpose(tail[:, n_ln:2 * n_ln].reshape(N_DEV, DEPTH, N_SUB, LANES), (1, 2, 0, 3)).reshape(DEPTH, N_SUB, d)
    qn_full = tail[:, 2 * n_ln:2 * n_ln + 32].reshape(C_Q_RANK)
    kvn_full = tail[:, 2 * n_ln + 32:2 * n_ln + 64].reshape(C_KV_RANK)

    def mvec(l, s, j):
        return mod[:, l, s, j].reshape(bsz, 1, d)

    packed = [pack_shards(layer, weights, BF16) for layer in range(DEPTH)]
    w = [unpack_full(0, all_gather(packed[0], "gather_weights_l0")), None]

    cq_tab, sg_tab = rope_tables(positions)
    lb0 = lb_fwd(hgrn_lb_logits, 0)
    aux = [dict(posc=positions.reshape(bsz, seq, 1),
                posr=jnp.pad(positions, ((0, 0), (A_BLOCK, A_BLOCK))).reshape(bsz, 1, seq + 2 * A_BLOCK),
                sink=attn_sink[0], lb=lb0.reshape(B_HEADS, 1, LANES), nw=hgrn_norm_w[0].reshape(B_HEADS, 1, LANES),
                next_packed=packed[1]),
           dict(qn=qn_full, kvn=kvn_full, cq=cq_tab, sg=sg_tab)]

    tape = []
    xin = x
    h = mod_fwd(x, mvec(0, 0, 1), mvec(0, 0, 0))
    for l in range(DEPTH):
        for s in range(N_SUB):
            h2 = h.reshape(t, d)
            if s != 1:
                y, saved = ffn_fwd(h2, w[l], s // 2)
            elif l == 0:
                y, saved, gathered = hybrid_fwd(h2, (bsz, seq), w[0], aux[0])
                w[1] = unpack_full(1, gathered)
            else:
                y, saved = mla_fwd(h2, (bsz, seq), w[1], aux[1])
            y = y.reshape(bsz, seq, d)
            last = l == DEPTH - 1 and s == N_SUB - 1
            ln, sn = (l, s + 1) if s + 1 < N_SUB else (l + 1, 0)
            nxt = None if last else (mvec(ln, sn, 1), mvec(ln, sn, 0))
            xn, hn = resln_fwd(xin, y, mvec(l, s, 2), ln_g_full[l, s], ln_b_full[l, s], RES_W[s], nxt)
            tape.append((l, s, xin, h2, y, saved))
            xin, h = xn, hn
    loss_part, gout = loss_fwd(xin, loss_target)
    loss = lax.psum(loss_part, ("x", "y", "c"))

    dmod = [[[None] * 3 for _ in range(N_SUB)] for _ in range(DEPTH)]
    dln_g = [[None] * N_SUB for _ in range(DEPTH)]
    dln_b = [[None] * N_SUB for _ in range(DEPTH)]
    big = [dict(wgu_t=[None] * FFN_PER_LAYER, wd=[None] * FFN_PER_LAYER) for _ in range(DEPTH)]
    small = {}
    mc = lax.axis_index("c")

    def outgoing(layer):
        return pack_full(layer, big[layer]).reshape(4, 2, PART_ROWS[layer], ROW_W)

    later = None
    for l, s, xs, h2, y, saved in reversed(tape):
        if later is None:
            dxa, dy, dgate, dg, db = resln_bwd(gout, xs, y, mvec(l, s, 2), ln_g_full[l, s], RES_W[s])
        else:
            ndxa, ndh, nl, ns = later
            dxa, dy, dgate, dg, db, dscale, dshift = resln_bwd(
                ndxa, xs, y, mvec(l, s, 2), ln_g_full[l, s], RES_W[s],
                upstream=(ndh, mvec(nl, ns, 1), ln_b_full[l, s]))
            dmod[nl][ns][0], dmod[nl][ns][1] = dshift, dscale
        dmod[l][s][2] = dgate
        dln_g[l][s], dln_b[l][s] = dg, db
        dy2 = dy.reshape(t, d)
        i = s // 2
        if s == 1 and l == 1:
            dh, dbig, dsmall = mla_bwd(dy2, h2, (bsz, seq), w[1], aux[1], saved)
        elif s == 1:
            dh, dbig, dsmall = hybrid_bwd(dy2, h2, (bsz, seq), w[0], aux[0], saved)
        elif l == 0 and s == 2:
            out1 = outgoing(1)
            dh, big[l]["wgu_t"][i], big[l]["wd"][i], swapped = ffn_bwd(dy2, h2, w[l], i, saved, swap=out1)
            aux[0]["pair_sums"] = pair_add(out1, swapped, mc)
        else:
            dh, big[l]["wgu_t"][i], big[l]["wd"][i], _ = ffn_bwd(dy2, h2, w[l], i, saved)
        if s == 1:
            big[l].update(dbig)
            small.update(dsmall)
        later = (dxa, dh.reshape(bsz, seq, d), l, s)
    grad_x, dmod[0][0][1], dmod[0][0][0] = mod_bwd(later[0], later[1], x, mvec(0, 0, 1))

    dmod_mine = jnp.stack([jnp.stack([jnp.concatenate(dmod[l][s], axis=1) for s in range(N_SUB)], axis=1)
                           for l in range(DEPTH)], axis=1)
    dmod_rows = bsz * DEPTH * MOD_W // ROW_W
    misc = jnp.concatenate([small["lb"].reshape(-1), small["nw"].reshape(-1), small["qn"].reshape(-1),
                            small["kvn"].reshape(-1), small["sink"].reshape(-1)[:A_HEADS]])
    small2 = jnp.concatenate([_rows1024(dmod_mine),
                              _rows1024(jnp.stack([jnp.stack(r) for r in dln_g])),
                              _rows1024(jnp.stack([jnp.stack(r) for r in dln_b])),
                              _rows1024(misc, rows=2)], axis=0)
    g2 = all_gather(small2, "gather_small_grads")
    dmod_all = g2[:, :dmod_rows].reshape(nb_tot, DEPTH * MOD_W // ROW_W, ROW_W)
    grad_ada_b = sum_parts(dmod_all, "sum_ada_b").reshape(DEPTH, MOD_W)
    dmod_cols = lax.dynamic_slice_in_dim(dmod_all.reshape(nb_tot, DEPTH, MOD_W), me * MOD_SH, MOD_SH, axis=2)
    grad_ada_w = jnp.stack([matmul(cond, dmod_cols[:, l], "tn", F32, "ada_dw") for l in range(DEPTH)])
    rest = sum_parts(g2[:, dmod_rows:], "sum_small")
    n6 = DEPTH * N_SUB
    gl_g = lax.dynamic_slice_in_dim(rest[:n6].reshape(DEPTH, N_SUB, d), me * LANES, LANES, axis=2)
    gl_b = lax.dynamic_slice_in_dim(rest[n6:2 * n6].reshape(DEPTH, N_SUB, d), me * LANES, LANES, axis=2)
    mrow = rest[2 * n6:].reshape(-1)
    dlb0 = mrow[:512].reshape(1, 512)
    g_nw = mrow[512:1024].reshape(1, B_HEADS, LANES)
    g_qn = lax.dynamic_slice_in_dim(mrow[1024:1280], me * 32, 32).reshape(1, 32)
    g_kvn = lax.dynamic_slice_in_dim(mrow[1280:1536], me * 32, 32).reshape(1, 32)
    g_sink = mrow[1536:1536 + A_HEADS].reshape(1, A_HEADS)
    g_lb = lb_bwd(hgrn_lb_logits, dlb0, 0)

    out0 = outgoing(0)
    pair0 = pair_add(out0, swap_with_sibling(out0, "exchange_d2d"), mc)
    reduced = [sum_parts(chip_all_to_all(pair0, "exchange_ici"), "sum_grads_l0"),
               sum_parts(aux[0]["exchanged"], "sum_grads_l1")]
    grads = unpack_shards(reduced)
    grads.update(ada_w=grad_ada_w, ada_b=grad_ada_b, ln_g=gl_g, ln_b=gl_b, attn_sink=g_sink,
                 hgrn_lb_logits=g_lb, hgrn_norm_w=g_nw, mla_q_norm=g_qn, mla_kv_norm=g_kvn)

    delta, new_m, new_v = {}, {}, {}
    for name in W_NAMES:
        if name in SMALL_NAMES:
            continue
        shp = weights[name].shape
        two_d = (-1, shp[-1])
        out = adamw(weights[name].reshape(two_d), grads[name].reshape(two_d), mom1[name].reshape(two_d),
                    mom2[name].reshape(two_d), "adamw_" + name)
        delta[name], new_m[name], new_v[name] = [o.reshape(shp) for o in out]

    def pack_small(src):
        flat = jnp.concatenate([src[n].reshape(-1) for n in SMALL_NAMES])
        return jnp.pad(flat, (0, -flat.shape[0] % (8 * LANES))).reshape(-1, LANES)

    outs = adamw(pack_small(weights), pack_small(grads), pack_small(mom1), pack_small(mom2), "adamw_small")
    off = 0
    for name in SMALL_NAMES:
        n = weights[name].size
        for dst, o in zip((delta, new_m, new_v), outs):
            dst[name] = o.reshape(-1)[off:off + n].reshape(weights[name].shape)
        off += n

    return (loss, grad_x, *[grads[n] for n in W_NAMES], *[delta[n] for n in W_NAMES],
            *[new_m[n] for n in W_NAMES], *[new_v[n] for n in W_NAMES])
```

```python
import functools
import math

import jax
import jax.numpy as jnp
import numpy as np
from jax import lax
from jax.experimental import pallas as pl
from jax.experimental.pallas import tpu as pltpu

F32 = jnp.float32
BF16 = jnp.bfloat16

D_MODEL = 1024
DEPTH = 2
D_FF = 2816
A_HEADS = 8
A_KV_HEADS = 2
A_HEAD_DIM = 64
WINDOW = 128
A_BLOCK = 128
B_HEADS = 4
B_KEY_DIM = 128
B_CHUNK = 64
C_HEADS = 16
C_Q_RANK = 256
C_KV_RANK = 256
C_NOPE = 64
C_ROPE = 32
C_V = 64
ROPE_THETA = 10000.0
LN_EPS = 1e-5
RMS_EPS = 1e-6
DEEPNORM_ALPHA = (2 * DEPTH) ** 0.25
N_SUB = 3
N_DEV = 8
LANES = 128
HYB_PAD = 3840
MO_PAD = 1536
MLA_DOWN_PAD = 640
MLA_HEAD_PAD = C_HEADS * LANES
ADAM_LR = 0.001
ADAM_B1 = 0.9
ADAM_B2 = 0.999
ADAM_EPS = 1e-08
ADAM_WD = 0.01
ADAM_STEP = 10
VMEM_LIMIT = 48 * 1024 * 1024
MESH_ID = pl.DeviceIdType.MESH
HIGHEST = lax.Precision.HIGHEST


def _cparams(sem=None):
    return pltpu.CompilerParams(dimension_semantics=sem, vmem_limit_bytes=VMEM_LIMIT)


def _tile(n, cap):
    if n <= cap:
        return n
    t = (cap // LANES) * LANES
    while t >= LANES:
        if n % t == 0:
            return t
        t -= LANES
    return n


def _rows_tile(n, cap):
    if n <= cap:
        return n
    t = cap
    while t >= 8:
        if n % t == 0:
            return t
        t -= 8
    return n


def _sigmoid(x):
    return 1.0 / (1.0 + jnp.exp(-x))


def _dot(a, b, dims, precision=None):
    return lax.dot_general(a, b, (dims, ((), ())), precision=precision,
                           preferred_element_type=F32)


NN = ((1,), (0,))
NT = ((1,), (1,))
TN = ((0,), (0,))


class Rider:
    def __init__(self, ins, outs, copies, n_remote, aliases=None):
        self.ins, self.outs, self.copies, self.n_remote = list(ins), list(outs), copies, n_remote
        self.aliases = aliases or {}

    def wrap(self, body, n_in, n_out, first, last):
        nci, nco = len(self.ins), len(self.outs)

        def wrapped(*refs):
            ins, cins = refs[:n_in], refs[n_in:n_in + nci]
            outs = refs[n_in + nci:n_in + nci + n_out]
            couts = refs[n_in + nci + n_out:n_in + nci + n_out + nco]
            scratch = refs[n_in + nci + n_out + nco:-3]
            sems = refs[-3:]

            @pl.when(first())
            def _():
                for started, _ in self.copies(cins, couts, *sems):
                    started.start()

            body(*ins, *outs, *scratch)

            @pl.when(last())
            def _():
                for _, awaited in self.copies(cins, couts, *sems):
                    awaited.wait()

        return wrapped

    def extend(self, n_in, n_out, args, in_specs, out_shape, out_specs, scratch):
        any_spec = pl.BlockSpec(memory_space=pl.ANY)
        aliases = {n_in + i: n_out + o for i, o in self.aliases.items()}
        return (list(args) + self.ins, list(in_specs) + [any_spec] * len(self.ins),
                list(out_shape) + self.outs, list(out_specs) + [any_spec] * len(self.outs),
                list(scratch) + [pltpu.SemaphoreType.DMA((self.n_remote,)), pltpu.SemaphoreType.DMA((self.n_remote,)),
                                 pltpu.SemaphoreType.DMA(())], aliases)


def _dev_slot(ref, px, py, pc):
    return ref.at[4 * px + 2 * py + pc]


def _other_chips(mx, my):
    return [(1 - mx, my), (mx, 1 - my), (1 - mx, 1 - my)]


def ride_gather_level1(x):
    def copies(cins, couts, send_sems, recv_sems, local_sem):
        (x_ref,), (out_ref,) = cins, couts
        mx, my, mc = _my_place()
        targets = [(mx, my, 1 - mc)] + [(*chip, mc) for chip in _other_chips(mx, my)]
        mine = _dev_slot(out_ref, mx, my, mc)

        def remote(k, to, dst):
            return pltpu.make_async_remote_copy(src_ref=x_ref, dst_ref=dst, send_sem=send_sems.at[k],
                                                recv_sem=recv_sems.at[k], device_id=to, device_id_type=MESH_ID)

        local = pltpu.make_async_copy(x_ref, mine, local_sem)
        return [(local, local)] + [(remote(k, to, mine), remote(k, to, _dev_slot(out_ref, *to)))
                                   for k, to in enumerate(targets)]

    return Rider([x], [jax.ShapeDtypeStruct((N_DEV,) + x.shape, x.dtype)], copies, 4)


def ride_gather_level2(g):
    def copies(cins, couts, send_sems, recv_sems, local_sem):
        (g_ref,), (out_ref,) = cins, couts
        mx, my, mc = _my_place()

        def remote(k, chip, pc):
            return pltpu.make_async_remote_copy(
                src_ref=_dev_slot(g_ref, *chip, mc), dst_ref=_dev_slot(out_ref, *chip, pc),
                send_sem=send_sems.at[k], recv_sem=recv_sems.at[k],
                device_id=(mx, my, 1 - mc), device_id_type=MESH_ID)

        return [(remote(k, chip, mc), remote(k, chip, 1 - mc)) for k, chip in enumerate(_other_chips(mx, my))]

    return Rider([g], [jax.ShapeDtypeStruct(g.shape, g.dtype)], copies, 3, aliases={0: 0})


def _sibling_swap_copies(x_ref, out_ref, send_sems, recv_sems):
    mx, my, mc = _my_place()
    cps = [pltpu.make_async_remote_copy(
        src_ref=x_ref.at[j, 1 - mc], dst_ref=out_ref.at[j], send_sem=send_sems.at[j], recv_sem=recv_sems.at[j],
        device_id=(mx, my, 1 - mc), device_id_type=MESH_ID) for j in range(4)]
    return [(cp, cp) for cp in cps]


def ride_swap_with_sibling(x):
    def copies(cins, couts, send_sems, recv_sems, local_sem):
        return _sibling_swap_copies(cins[0], couts[0], send_sems, recv_sems)

    return Rider([x], [jax.ShapeDtypeStruct((x.shape[0],) + x.shape[2:], x.dtype)], copies, 4)


def ride_chip_all_to_all(x):
    def copies(cins, couts, send_sems, recv_sems, local_sem):
        (x_ref,), (out_ref,) = cins, couts
        mx, my, mc = _my_place()
        me = 2 * mx + my

        def remote(k, px, py, dst):
            return pltpu.make_async_remote_copy(src_ref=x_ref.at[2 * px + py], dst_ref=dst,
                                                send_sem=send_sems.at[k], recv_sem=recv_sems.at[k],
                                                device_id=(px, py, mc), device_id_type=MESH_ID)

        local = pltpu.make_async_copy(x_ref.at[me], out_ref.at[me], local_sem)
        return [(local, local)] + [(remote(k, px, py, out_ref.at[me]), remote(k, px, py, out_ref.at[2 * px + py]))
                                   for k, (px, py) in enumerate(_other_chips(mx, my))]

    return Rider([x], [jax.ShapeDtypeStruct(x.shape, x.dtype)], copies, 3)


def matmul(a, b, form, out_dtype, name, bias=None, tm_cap=512, tn_cap=1024, tk_cap=1024, rider=None):
    if form == "nn":
        (m, k), (k2, n) = a.shape, b.shape
    elif form == "nt":
        (m, k), (n, k2) = a.shape, b.shape
    else:
        (k, m), (k2, n) = a.shape, b.shape
    assert k == k2, (a.shape, b.shape, form)
    tm = _tile(m, tm_cap)
    tn = _tile(n, tn_cap)
    tk = k if k <= 2 * D_FF else _tile(k, tk_cap)
    nk = k // tk
    dims = {"nn": NN, "nt": NT, "tn": TN}[form]

    def body(*refs):
        if bias is None:
            a_ref, b_ref, o_ref, acc_ref = refs
            bias_ref = None
        else:
            a_ref, b_ref, bias_ref, o_ref, acc_ref = refs
        kk = pl.program_id(2)
        part = _dot(a_ref[...].astype(BF16), b_ref[...].astype(BF16), dims)

        def finish(total):
            if bias_ref is not None:
                total = total + bias_ref[...]
            o_ref[...] = total.astype(o_ref.dtype)

        if nk == 1:
            finish(part)
        else:
            @pl.when(kk == 0)
            def _():
                acc_ref[...] = part

            @pl.when(jnp.logical_and(kk > 0, kk < nk - 1))
            def _():
                acc_ref[...] += part

            @pl.when(kk == nk - 1)
            def _():
                finish(acc_ref[...] + part)

    if form == "nn":
        a_spec = pl.BlockSpec((tm, tk), lambda i, j, kk: (i, kk))
        b_spec = pl.BlockSpec((tk, tn), lambda i, j, kk: (kk, j))
    elif form == "nt":
        a_spec = pl.BlockSpec((tm, tk), lambda i, j, kk: (i, kk))
        b_spec = pl.BlockSpec((tn, tk), lambda i, j, kk: (j, kk))
    else:
        a_spec = pl.BlockSpec((tk, tm), lambda i, j, kk: (kk, i))
        b_spec = pl.BlockSpec((tk, tn), lambda i, j, kk: (kk, j))
    in_specs = [a_spec, b_spec]
    args = [a, b]
    if bias is not None:
        in_specs.append(pl.BlockSpec((1, tn), lambda i, j, kk: (0, j)))
        args.append(bias.reshape(1, n))
    out_shape = jax.ShapeDtypeStruct((m, n), out_dtype)
    out_spec = pl.BlockSpec((tm, tn), lambda i, j, kk: (i, j))
    grid = (m // tm, n // tn, nk)
    if rider is None:
        return pl.pallas_call(
            body, name=name, out_shape=out_shape, grid=grid, in_specs=in_specs, out_specs=out_spec,
            scratch_shapes=[pltpu.VMEM((tm, tn), F32)],
            compiler_params=_cparams(("parallel", "parallel", "arbitrary")),
        )(*args)

    def at_step(which):
        def test():
            ids = [pl.program_id(ax) for ax in range(3)]
            want = [0, 0, 0] if which == "first" else [g - 1 for g in grid]
            return jnp.logical_and(jnp.logical_and(ids[0] == want[0], ids[1] == want[1]), ids[2] == want[2])
        return test

    n_in = len(args)
    args, in_specs, out_shape, out_specs, scratch, aliases = rider.extend(
        n_in, 1, args, in_specs, [out_shape], [out_spec], [pltpu.VMEM((tm, tn), F32)])
    return pl.pallas_call(
        rider.wrap(body, n_in, 1, at_step("first"), at_step("last")), name=name,
        out_shape=out_shape, grid=grid, in_specs=in_specs, out_specs=out_specs, scratch_shapes=scratch,
        input_output_aliases=aliases,
        compiler_params=_cparams(("arbitrary", "arbitrary", "arbitrary")),
    )(*args)


TOK_TILE = 512


def _vec_spec(d):
    return pl.BlockSpec((1, 1, d), lambda b, s: (b, 0, 0))


def _tok_in(ts, d):
    return pl.BlockSpec((1, ts, d), lambda b, s: (b, s, 0))


def _tok_spec(ts, d):
    return pl.BlockSpec((1, ts, d), lambda b, s: (b, s, 0))


def mod_fwd(x, scale, shift):
    bsz, seq, d = x.shape
    ts = _rows_tile(seq, TOK_TILE)

    def body(x_ref, sc_ref, sh_ref, h_ref):
        h_ref[0] = (x_ref[0] * (1.0 + sc_ref[0]) + sh_ref[0]).astype(BF16)

    return pl.pallas_call(
        body, name="mod_fwd",
        out_shape=jax.ShapeDtypeStruct((bsz, seq, d), BF16),
        grid=(bsz, seq // ts),
        in_specs=[_tok_in(ts, d), _vec_spec(d), _vec_spec(d)],
        out_specs=_tok_spec(ts, d),
        compiler_params=_cparams(("parallel", "parallel")),
    )(x, scale, shift)


def _ln_stats(z):
    mu = jnp.mean(z, axis=-1, keepdims=True)
    zc = z - mu
    var = jnp.mean(zc * zc, axis=-1, keepdims=True)
    return zc, lax.rsqrt(var + LN_EPS)


def resln_fwd(x, y, gate, g, b, res_w, nxt):
    bsz, seq, d = x.shape
    ts = _rows_tile(seq, TOK_TILE)

    def body(*refs):
        if nxt is None:
            x_ref, y_ref, gt_ref, g_ref, b_ref, xn_ref = refs
        else:
            x_ref, y_ref, gt_ref, g_ref, b_ref, sc_ref, sh_ref, xn_ref, hn_ref = refs
        z = DEEPNORM_ALPHA * x_ref[0] + (res_w * (1.0 + gt_ref[0])) * y_ref[0].astype(F32)
        zc, r = _ln_stats(z)
        xn = zc * r * g_ref[...] + b_ref[...]
        xn_ref[0] = xn
        if nxt is not None:
            hn_ref[0] = (xn * (1.0 + sc_ref[0]) + sh_ref[0]).astype(BF16)

    row = pl.BlockSpec((1, d), lambda bb, s: (0, 0))
    in_specs = [_tok_in(ts, d), _tok_in(ts, d), _vec_spec(d), row, row]
    args = [x, y, gate, g.reshape(1, d), b.reshape(1, d)]
    out_shape = [jax.ShapeDtypeStruct((bsz, seq, d), F32)]
    out_specs = [_tok_spec(ts, d)]
    if nxt is not None:
        in_specs += [_vec_spec(d), _vec_spec(d)]
        args += list(nxt)
        out_shape.append(jax.ShapeDtypeStruct((bsz, seq, d), BF16))
        out_specs.append(_tok_spec(ts, d))
    out = pl.pallas_call(
        body, name="resln_fwd",
        out_shape=out_shape, grid=(bsz, seq // ts),
        in_specs=in_specs, out_specs=out_specs,
        compiler_params=_cparams(("parallel", "parallel")),
    )(*args)
    return (out[0], None) if nxt is None else (out[0], out[1])


def resln_bwd(gout, x, y, gate, g, res_w, upstream=None):
    bsz, seq, d = x.shape
    ts = _rows_tile(seq, TOK_TILE)

    def body(*refs):
        if upstream is None:
            go_ref, x_ref, y_ref, gt_ref, g_ref, dxa_ref, dy_ref, dgt_ref, dg_ref, db_ref = refs
        else:
            (go_ref, x_ref, y_ref, gt_ref, g_ref, dh_ref, sc_ref, b_ref,
             dxa_ref, dy_ref, dgt_ref, dg_ref, db_ref, dsc_ref, dsh_ref) = refs
        bb, s = pl.program_id(0), pl.program_id(1)
        yv = y_ref[0].astype(F32)
        rw = res_w * (1.0 + gt_ref[0])
        z = DEEPNORM_ALPHA * x_ref[0] + rw * yv
        zc, r = _ln_stats(z)
        xhat = zc * r
        go = go_ref[0]
        if upstream is not None:
            dh = dh_ref[0].astype(F32)
            go = go + dh * (1.0 + sc_ref[0])
            dsc = jnp.sum(dh * (xhat * g_ref[...] + b_ref[...]), axis=0, keepdims=True)
            dsh = jnp.sum(dh, axis=0, keepdims=True)

            @pl.when(s == 0)
            def _():
                dsc_ref[0] = dsc
                dsh_ref[0] = dsh

            @pl.when(s > 0)
            def _():
                dsc_ref[0] += dsc
                dsh_ref[0] += dsh

        dxh = go * g_ref[...]
        dz = r * (dxh - jnp.mean(dxh, axis=-1, keepdims=True)
                  - xhat * jnp.mean(dxh * xhat, axis=-1, keepdims=True))
        dxa_ref[0] = DEEPNORM_ALPHA * dz
        dy_ref[0] = (rw * dz).astype(BF16)
        dgt = res_w * jnp.sum(dz * yv, axis=0, keepdims=True)
        dg = jnp.sum(go * xhat, axis=0, keepdims=True)
        db = jnp.sum(go, axis=0, keepdims=True)

        @pl.when(s == 0)
        def _():
            dgt_ref[0] = dgt

        @pl.when(s > 0)
        def _():
            dgt_ref[0] += dgt

        first = jnp.logical_and(bb == 0, s == 0)

        @pl.when(first)
        def _():
            dg_ref[...] = dg
            db_ref[...] = db

        @pl.when(jnp.logical_not(first))
        def _():
            dg_ref[...] += dg
            db_ref[...] += db

    row = pl.BlockSpec((1, d), lambda bb, s: (0, 0))
    vec = jax.ShapeDtypeStruct((bsz, 1, d), F32)
    in_specs = [_tok_in(ts, d), _tok_in(ts, d), _tok_in(ts, d), _vec_spec(d), row]
    args = [gout, x, y, gate, g.reshape(1, d)]
    out_shape = [jax.ShapeDtypeStruct((bsz, seq, d), F32), jax.ShapeDtypeStruct((bsz, seq, d), BF16), vec,
                 jax.ShapeDtypeStruct((1, d), F32), jax.ShapeDtypeStruct((1, d), F32)]
    out_specs = [_tok_spec(ts, d), _tok_spec(ts, d), _vec_spec(d), row, row]
    if upstream is not None:
        dh, scale, b = upstream
        in_specs += [_tok_in(ts, d), _vec_spec(d), row]
        args += [dh, scale, b.reshape(1, d)]
        out_shape += [vec, vec]
        out_specs += [_vec_spec(d), _vec_spec(d)]
    return pl.pallas_call(
        body, name="resln_bwd" if upstream is None else "resln_mod_bwd",
        out_shape=out_shape, grid=(bsz, seq // ts),
        in_specs=in_specs, out_specs=out_specs,
        compiler_params=_cparams(("arbitrary", "arbitrary")),
    )(*args)


def mod_bwd(dxa, dh, x, scale):
    bsz, seq, d = x.shape
    ts = _rows_tile(seq, TOK_TILE)

    def body(dxa_ref, dh_ref, x_ref, sc_ref, dx_ref, dsc_ref, dsh_ref):
        s = pl.program_id(1)
        dh = dh_ref[0].astype(F32)
        dx_ref[0] = dxa_ref[0] + dh * (1.0 + sc_ref[0])
        dsc = jnp.sum(dh * x_ref[0], axis=0, keepdims=True)
        dsh = jnp.sum(dh, axis=0, keepdims=True)

        @pl.when(s == 0)
        def _():
            dsc_ref[0] = dsc
            dsh_ref[0] = dsh

        @pl.when(s > 0)
        def _():
            dsc_ref[0] += dsc
            dsh_ref[0] += dsh

    return pl.pallas_call(
        body, name="mod_bwd",
        out_shape=[jax.ShapeDtypeStruct((bsz, seq, d), F32),
                   jax.ShapeDtypeStruct((bsz, 1, d), F32),
                   jax.ShapeDtypeStruct((bsz, 1, d), F32)],
        grid=(bsz, seq // ts),
        in_specs=[_tok_in(ts, d), _tok_in(ts, d), _tok_in(ts, d), _vec_spec(d)],
        out_specs=[_tok_spec(ts, d), _vec_spec(d), _vec_spec(d)],
        compiler_params=_cparams(("parallel", "arbitrary")),
    )(dxa, dh, x, scale)


FF_HALF = D_FF // 2
FF_CHUNKS = ((0, 384), (384, 384), (768, 384), (1152, 256))


def ffn_up_fused(h2, wgu):
    t, d = h2.shape
    tm = _rows_tile(t, 1024)

    def body(h_ref, w_ref, gu_ref, a_ref):
        h = h_ref[...]
        for c0, cw in FF_CHUNKS:
            g = _dot(h, w_ref[:, c0:c0 + cw], NN)
            u = _dot(h, w_ref[:, FF_HALF + c0:FF_HALF + c0 + cw], NN)
            gu_ref[:, c0:c0 + cw] = g.astype(BF16)
            gu_ref[:, FF_HALF + c0:FF_HALF + c0 + cw] = u.astype(BF16)
            a_ref[:, c0:c0 + cw] = (g * _sigmoid(g) * u).astype(BF16)

    return pl.pallas_call(
        body, name="ffn_up_fused",
        out_shape=[jax.ShapeDtypeStruct((t, 2 * D_FF), BF16), jax.ShapeDtypeStruct((t, D_FF), BF16)],
        grid=(2, t // tm),
        in_specs=[pl.BlockSpec((tm, d), lambda j, i: (i, 0)),
                  pl.BlockSpec((d, 2 * FF_HALF), lambda j, i: (0, j))],
        out_specs=[pl.BlockSpec((tm, 2 * FF_HALF), lambda j, i: (i, j)),
                   pl.BlockSpec((tm, FF_HALF), lambda j, i: (i, j))],
        compiler_params=_cparams(("parallel", "parallel")),
    )(h2, wgu)


def ffn_down_bwd_fused(dy2, wd_t, gu):
    t, d = dy2.shape
    tm = _rows_tile(t, 512)

    def body(dy_ref, w_ref, gu_ref, o_ref):
        dy = dy_ref[...]
        for c0, cw in FF_CHUNKS:
            da = _dot(dy, w_ref[:, c0:c0 + cw], NN)
            g = gu_ref[:, c0:c0 + cw].astype(F32)
            u = gu_ref[:, FF_HALF + c0:FF_HALF + c0 + cw].astype(F32)
            sg = _sigmoid(g)
            o_ref[:, c0:c0 + cw] = (da * u * (sg * (1.0 + g * (1.0 - sg)))).astype(BF16)
            o_ref[:, FF_HALF + c0:FF_HALF + c0 + cw] = (da * (g * sg)).astype(BF16)

    return pl.pallas_call(
        body, name="ffn_down_bwd_fused",
        out_shape=jax.ShapeDtypeStruct((t, 2 * D_FF), BF16),
        grid=(2, t // tm),
        in_specs=[pl.BlockSpec((tm, d), lambda j, i: (i, 0)),
                  pl.BlockSpec((d, FF_HALF), lambda j, i: (0, j)),
                  pl.BlockSpec((tm, 2 * FF_HALF), lambda j, i: (i, j))],
        out_specs=pl.BlockSpec((tm, 2 * FF_HALF), lambda j, i: (i, j)),
        compiler_params=_cparams(("parallel", "parallel")),
    )(dy2, wd_t, gu)


def loss_fwd(y, tgt):
    bsz, seq, d = y.shape
    ts = _rows_tile(seq, TOK_TILE)

    def body(y_ref, t_ref, dy_ref, l_ref, acc_ref):
        bb, s = pl.program_id(0), pl.program_id(1)
        e = y_ref[0] - t_ref[0]
        dy_ref[0] = e * (1.0 / d)
        part = jnp.sum((e * e).reshape(ts // 8, 8, d), axis=0)
        first = jnp.logical_and(bb == 0, s == 0)

        @pl.when(first)
        def _():
            acc_ref[...] = part

        @pl.when(jnp.logical_not(first))
        def _():
            acc_ref[...] += part

        @pl.when(jnp.logical_and(bb == pl.num_programs(0) - 1, s == pl.num_programs(1) - 1))
        def _():
            tot = jnp.sum(jnp.sum(acc_ref[...], axis=1, keepdims=True), axis=0, keepdims=True)
            l_ref[...] = jnp.broadcast_to(tot * (0.5 / d), (8, LANES))

    dy, l = pl.pallas_call(
        body, name="loss_fwd",
        out_shape=[jax.ShapeDtypeStruct((bsz, seq, d), F32), jax.ShapeDtypeStruct((8, LANES), F32)],
        grid=(bsz, seq // ts),
        in_specs=[_tok_in(ts, d), _tok_in(ts, d)],
        out_specs=[_tok_spec(ts, d), pl.BlockSpec((8, LANES), lambda bb, s: (0, 0))],
        scratch_shapes=[pltpu.VMEM((8, d), F32)],
        compiler_params=_cparams(("arbitrary", "arbitrary")),
    )(y, tgt)
    return l[0, 0], dy


def _rot_half(x):
    lane = lax.broadcasted_iota(jnp.int32, x.shape, 1)
    swapped = jnp.where(lane < 80, pltpu.roll(x, 112, axis=1), pltpu.roll(x, 16, axis=1))
    return jnp.where((lane >= C_NOPE) & (lane < C_NOPE + C_ROPE), swapped, 0.0)


def _rms(x, w):
    r = lax.rsqrt(jnp.mean(x * x, axis=-1, keepdims=True) + RMS_EPS)
    return x * r * w, r


def _rms_bwd(dy, x, w):
    r = lax.rsqrt(jnp.mean(x * x, axis=-1, keepdims=True) + RMS_EPS)
    wd = dy * w
    dx = r * wd - x * (r * r * r) * jnp.mean(x * wd, axis=-1, keepdims=True)
    return dx, jnp.sum(dy * x * r, axis=0, keepdims=True)


def mla_mid_fwd(cqkv, qw, kw, cq_tab, sg_tab):
    bsz, seq, _ = cqkv.shape
    ts = _rows_tile(seq, TOK_TILE)
    r = C_Q_RANK

    def body(x_ref, qw_ref, kw_ref, c_ref, s_ref, nq_ref, nkv_ref, kr_ref):
        x = x_ref[0]
        nq_ref[0] = _rms(x[:, :r], qw_ref[...])[0].astype(BF16)
        nkv_ref[0] = _rms(x[:, r:2 * r], kw_ref[...])[0].astype(BF16)
        xr = x[:, 2 * r:]
        kr_ref[0] = (xr * c_ref[0] + _rot_half(xr) * s_ref[0]).astype(BF16)

    row = pl.BlockSpec((1, r), lambda b, s: (0, 0))
    return pl.pallas_call(
        body, name="mla_mid_fwd",
        out_shape=[jax.ShapeDtypeStruct((bsz, seq, r), BF16), jax.ShapeDtypeStruct((bsz, seq, r), BF16),
                   jax.ShapeDtypeStruct((bsz, seq, LANES), BF16)],
        grid=(bsz, seq // ts),
        in_specs=[_tok_spec(ts, MLA_DOWN_PAD), row, row, _tok_spec(ts, LANES), _tok_spec(ts, LANES)],
        out_specs=[_tok_spec(ts, r), _tok_spec(ts, r), _tok_spec(ts, LANES)],
        compiler_params=_cparams(("parallel", "parallel")),
    )(cqkv, qw.reshape(1, r), kw.reshape(1, r), cq_tab, sg_tab)


def mla_mid_bwd(dnq, dnkv, dkr, cqkv, qw, kw, cq_tab, sg_tab):
    bsz, seq, _ = cqkv.shape
    ts = _rows_tile(seq, TOK_TILE)
    r = C_Q_RANK

    def body(dnq_ref, dnkv_ref, dkr_ref, x_ref, qw_ref, kw_ref, c_ref, s_ref, dx_ref, dqw_ref, dkw_ref):
        bb, s = pl.program_id(0), pl.program_id(1)
        x = x_ref[0]
        dcq, dqw = _rms_bwd(dnq_ref[0].astype(F32), x[:, :r], qw_ref[...])
        dckv, dkw = _rms_bwd(dnkv_ref[0].astype(F32), x[:, r:2 * r], kw_ref[...])
        dk = dkr_ref[0]
        dxr = dk * c_ref[0] + _rot_half(dk * s_ref[0])
        dx_ref[0, :, :r] = dcq.astype(BF16)
        dx_ref[0, :, r:2 * r] = dckv.astype(BF16)
        dx_ref[0, :, 2 * r:] = dxr.astype(BF16)
        first = jnp.logical_and(bb == 0, s == 0)

        @pl.when(first)
        def _():
            dqw_ref[...] = dqw
            dkw_ref[...] = dkw

        @pl.when(jnp.logical_not(first))
        def _():
            dqw_ref[...] += dqw
            dkw_ref[...] += dkw

    row = pl.BlockSpec((1, r), lambda b, s: (0, 0))
    return pl.pallas_call(
        body, name="mla_mid_bwd",
        out_shape=[jax.ShapeDtypeStruct((bsz, seq, MLA_DOWN_PAD), BF16),
                   jax.ShapeDtypeStruct((1, r), F32), jax.ShapeDtypeStruct((1, r), F32)],
        grid=(bsz, seq // ts),
        in_specs=[_tok_spec(ts, r), _tok_spec(ts, r), _tok_spec(ts, LANES), _tok_spec(ts, MLA_DOWN_PAD),
                  row, row, _tok_spec(ts, LANES), _tok_spec(ts, LANES)],
        out_specs=[_tok_spec(ts, MLA_DOWN_PAD), row, row],
        compiler_params=_cparams(("arbitrary", "arbitrary")),
    )(dnq, dnkv, dkr, cqkv, qw.reshape(1, r), kw.reshape(1, r), cq_tab, sg_tab)


def q_rope(q, cq_tab, sg_tab, transpose_rule, name, out_scale=1.0):
    bsz, seq, w = q.shape
    ts = _rows_tile(seq, TOK_TILE)

    def body(q_ref, c_ref, s_ref, o_ref):
        c, s = c_ref[0], s_ref[0]
        for h in range(w // LANES):
            x = q_ref[0, :, h * LANES:(h + 1) * LANES].astype(F32)
            y = x * c + (_rot_half(x * s) if transpose_rule else _rot_half(x) * s)
            o_ref[0, :, h * LANES:(h + 1) * LANES] = (y * out_scale).astype(BF16)

    return pl.pallas_call(
        body, name=name,
        out_shape=jax.ShapeDtypeStruct((bsz, seq, w), BF16),
        grid=(bsz, seq // ts),
        in_specs=[_tok_spec(ts, w), _tok_spec(ts, LANES), _tok_spec(ts, LANES)],
        out_specs=_tok_spec(ts, w),
        compiler_params=_cparams(("parallel", "parallel")),
    )(q, cq_tab, sg_tab)


MLA_SCALE = (C_NOPE + C_ROPE) ** -0.5
LOG2E = math.log2(math.e)
MLA_QSCALE = MLA_SCALE * LOG2E
MLA_TILE = 1024
MLA_HEADS_PER_STEP = 2
NEG_BIG = -1e30


def _eye_mask(n):
    return lax.broadcasted_iota(jnp.int32, (n, n), 0) == lax.broadcasted_iota(jnp.int32, (n, n), 1)


def mla_attn_fwd(qp, kv, kr):
    bsz, seq, _ = qp.shape
    tq = _rows_tile(seq, MLA_TILE)
    nt = seq // tq

    hps = MLA_HEADS_PER_STEP

    def lanes_of(hh):
        return slice(hh * LANES, (hh + 1) * LANES)

    def body(q_ref, kv_ref, kr_ref, o_ref, lse_ref, kc_ref):
        lane = lax.broadcasted_iota(jnp.int32, (seq, LANES), 1)
        for hh in range(hps):
            kc_ref[hh] = jnp.where(lane < C_NOPE, kv_ref[0, :, lanes_of(hh)], kr_ref[0])
        lane_q = lax.broadcasted_iota(jnp.int32, (tq, LANES), 1)

        def q_body(i, carry):
            qrows = pl.ds(pl.multiple_of(i * tq, tq), tq)
            q_i = [q_ref[0, qrows, lanes_of(hh)] for hh in range(hps)]

            def key_body(j, st):
                krows = pl.ds(pl.multiple_of(j * tq, tq), tq)
                out = []
                for hh in range(hps):
                    m, l, acc = st[hh]
                    s = _dot(q_i[hh], kc_ref[hh, krows, :], NT)
                    m_new = jnp.maximum(m, jnp.max(s, axis=-1, keepdims=True))
                    alpha = jnp.exp2(m - m_new)
                    p = jnp.exp2(s - m_new)
                    l = alpha * l + jnp.sum(p, axis=-1, keepdims=True)
                    acc = alpha * acc + _dot(p.astype(BF16), kv_ref[0, krows, lanes_of(hh)], NN)
                    out.append((m_new, l, acc))
                return tuple(out)

            init = tuple((jnp.full((tq, 1), NEG_BIG, F32), jnp.zeros((tq, 1), F32), jnp.zeros((tq, LANES), F32))
                         for _ in range(hps))
            res = lax.fori_loop(0, nt, key_body, init)
            for hh in range(hps):
                m, l, acc = res[hh]
                o_ref[0, qrows, lanes_of(hh)] = jnp.where(lane_q >= C_NOPE, acc * (1.0 / l), 0.0).astype(BF16)
                lse = m + jnp.log2(l)
                lse_ref[0, hh, :, qrows] = jnp.sum(jnp.where(_eye_mask(tq), lse, 0.0), axis=0, keepdims=True)
            return carry

        lax.fori_loop(0, nt, q_body, 0)

    heads = pl.BlockSpec((1, seq, hps * LANES), lambda b, h: (b, 0, h))
    return pl.pallas_call(
        body, name="mla_attn_fwd",
        out_shape=[jax.ShapeDtypeStruct((bsz, seq, MLA_HEAD_PAD), BF16),
                   jax.ShapeDtypeStruct((bsz, C_HEADS, 1, seq), F32)],
        grid=(bsz, C_HEADS // hps),
        in_specs=[heads, heads, pl.BlockSpec((1, seq, LANES), lambda b, h: (b, 0, 0))],
        out_specs=[heads, pl.BlockSpec((1, hps, 1, seq), lambda b, h: (b, h, 0, 0))],
        scratch_shapes=[pltpu.VMEM((hps, seq, LANES), BF16)],
        compiler_params=_cparams(("parallel", "parallel")),
    )(qp, kv, kr)


def mla_attn_bwd(qp, kv, kr, o, do, lse):
    bsz, seq, _ = qp.shape
    tq = _rows_tile(seq, MLA_TILE)
    nt = seq // tq

    def body(q_ref, kv_ref, kr_ref, o_ref, do_ref, lse_ref, dq_ref, dkv_ref, dkr_ref,
             kc_ref, drow_ref, dqa_ref, dkc_ref, dkvv_ref):
        h = pl.program_id(1)
        lane_s = lax.broadcasted_iota(jnp.int32, (seq, LANES), 1)
        lane_t = lax.broadcasted_iota(jnp.int32, (tq, LANES), 1)
        kc_ref[...] = jnp.where(lane_s < C_NOPE, kv_ref[0], kr_ref[0])
        dqa_ref[...] = jnp.zeros_like(dqa_ref)
        ones = jnp.ones((8, LANES), F32)

        def delta_body(i, carry):
            rows = pl.ds(pl.multiple_of(i * tq, tq), tq)
            prod = do_ref[0, rows, :].astype(F32) * o_ref[0, rows, :].astype(F32)
            drow_ref[:, rows] = _dot(ones, prod, NT, precision=HIGHEST)
            return carry

        lax.fori_loop(0, nt, delta_body, 0)

        def key_body(j, carry):
            krows = pl.ds(pl.multiple_of(j * tq, tq), tq)
            kc_j = kc_ref[krows, :]
            kv_j = kv_ref[0, krows, :]
            dkc_ref[...] = jnp.zeros_like(dkc_ref)
            dkvv_ref[...] = jnp.zeros_like(dkvv_ref)

            def q_body(i, c2):
                qrows = pl.ds(pl.multiple_of(i * tq, tq), tq)
                q_i = q_ref[0, qrows, :]
                do_i = do_ref[0, qrows, :]
                st = _dot(kc_j, q_i, NT)
                pt = jnp.exp2(st - lse_ref[0, 0, :, qrows])
                dpt = _dot(kv_j, do_i, NT)
                dst = (pt * (dpt - drow_ref[0:1, qrows])).astype(BF16)
                dkvv_ref[...] += _dot(pt.astype(BF16), do_i, NN)
                dkc_ref[...] += _dot(dst, q_i, NN)
                dqa_ref[qrows, :] += _dot(dst, kc_j, TN)
                return c2

            lax.fori_loop(0, nt, q_body, 0)
            dkc = dkc_ref[...] * (MLA_SCALE / MLA_QSCALE)
            dkv_ref[0, krows, :] = jnp.where(lane_t < C_NOPE, dkc, dkvv_ref[...]).astype(BF16)
            dkr_j = jnp.where(lane_t >= C_NOPE, dkc, 0.0)

            @pl.when(h == 0)
            def _():
                dkr_ref[0, krows, :] = dkr_j

            @pl.when(h > 0)
            def _():
                dkr_ref[0, krows, :] += dkr_j

            return carry

        lax.fori_loop(0, nt, key_body, 0)
        dq_ref[0] = (dqa_ref[...] * MLA_SCALE).astype(BF16)

    head = pl.BlockSpec((1, seq, LANES), lambda b, h: (b, 0, h))
    shared = pl.BlockSpec((1, seq, LANES), lambda b, h: (b, 0, 0))
    return pl.pallas_call(
        body, name="mla_attn_bwd",
        out_shape=[jax.ShapeDtypeStruct((bsz, seq, MLA_HEAD_PAD), BF16),
                   jax.ShapeDtypeStruct((bsz, seq, MLA_HEAD_PAD), BF16),
                   jax.ShapeDtypeStruct((bsz, seq, LANES), F32)],
        grid=(bsz, C_HEADS),
        in_specs=[head, head, shared, head, head,
                  pl.BlockSpec((1, 1, 1, seq), lambda b, h: (b, h, 0, 0))],
        out_specs=[head, head, shared],
        scratch_shapes=[pltpu.VMEM((seq, LANES), BF16), pltpu.VMEM((8, seq), F32),
                        pltpu.VMEM((seq, LANES), F32), pltpu.VMEM((tq, LANES), F32),
                        pltpu.VMEM((tq, LANES), F32)],
        compiler_params=_cparams(("parallel", "arbitrary")),
    )(qp, kv, kr, o, do, lse)


A_SCALE = A_HEAD_DIM ** -0.5
A_GROUP = A_HEADS // A_KV_HEADS
A_BAND = 3 * A_BLOCK


def _wattn_block(i, seq, posc_ref, posr_ref):
    start = pl.multiple_of(i * A_BLOCK, A_BLOCK)
    pq = jnp.concatenate([posc_ref[0]] * A_GROUP, axis=0)
    pk = posr_ref[0, :, pl.ds(start, A_BAND)]
    dist = jnp.abs(pq - pk).astype(F32)
    shape = (A_GROUP * A_BLOCK, A_BAND)
    qi = (lax.broadcasted_iota(jnp.int32, shape, 0) & (A_BLOCK - 1)) + A_BLOCK
    ki = lax.broadcasted_iota(jnp.int32, shape, 1)
    absk = i * A_BLOCK + ki - A_BLOCK
    valid = (jnp.abs(qi - ki) <= WINDOW) & (absk >= 0) & (absk < seq)
    return start, dist, valid


def _wattn_probs(q4, kb, dist, valid, slope4, sink4):
    s = _dot(q4, kb, NT) * A_SCALE - slope4 * dist
    s = jnp.where(valid, s, NEG_BIG)
    m = jnp.maximum(jnp.max(s, axis=-1, keepdims=True), sink4)
    p = jnp.exp(s - m)
    es = jnp.exp(sink4 - m)
    inv = 1.0 / (jnp.sum(p, axis=-1, keepdims=True) + es)
    return p * inv, es * inv


def _alibi_slope(h):
    return 2.0 ** (-8.0 * (h + 1) / A_HEADS)


def _group_heads(g):
    return range(g * A_GROUP, (g + 1) * A_GROUP)


def _stack_heads(ref, g, dtype):
    return jnp.concatenate([ref[0, :, h * LANES:(h + 1) * LANES].astype(dtype) for h in _group_heads(g)], axis=0)


def _per_head_column(values):
    return jnp.concatenate([jnp.full((A_BLOCK, 1), v, F32) for v in values], axis=0)


def wattn_fwd(proj, kp, vp, posc, posr, sink):
    bsz, seq, _ = proj.shape
    nb = seq // A_BLOCK
    qw = A_HEADS * LANES

    def body(sink_ref, q_ref, kp_ref, vp_ref, posc_ref, posr_ref, o_ref):
        i = pl.program_id(1)
        start, dist, valid = _wattn_block(i, seq, posc_ref, posr_ref)
        kb = kp_ref[0, pl.ds(start, A_BAND), :]
        vb = vp_ref[0, pl.ds(start, A_BAND), :]
        lane = lax.broadcasted_iota(jnp.int32, (A_GROUP * A_BLOCK, LANES), 1)
        for g in range(A_KV_HEADS):
            heads = _group_heads(g)
            p, _ = _wattn_probs(_stack_heads(q_ref, g, BF16), kb, dist, valid,
                                _per_head_column([_alibi_slope(h) for h in heads]),
                                _per_head_column([sink_ref[h] for h in heads]))
            o = _dot(p.astype(BF16), vb, NN)
            mine = (lane >= A_HEAD_DIM) if g == 1 else (lane < A_HEAD_DIM)
            o = jnp.where(mine, o, 0.0).astype(BF16)
            for j, h in enumerate(heads):
                o_ref[0, :, h * LANES:(h + 1) * LANES] = o[j * A_BLOCK:(j + 1) * A_BLOCK]

    return pl.pallas_call(
        body, name="wattn_fwd",
        out_shape=jax.ShapeDtypeStruct((bsz, seq, MO_PAD), BF16),
        grid=(bsz, nb),
        in_specs=[pl.BlockSpec(memory_space=pltpu.SMEM),
                  pl.BlockSpec((1, A_BLOCK, qw), lambda b, i: (b, i, 0)),
                  pl.BlockSpec((1, seq + 2 * A_BLOCK, LANES), lambda b, i: (b, 0, 0)),
                  pl.BlockSpec((1, seq + 2 * A_BLOCK, LANES), lambda b, i: (b, 0, 0)),
                  pl.BlockSpec((1, A_BLOCK, 1), lambda b, i: (b, i, 0)),
                  pl.BlockSpec((1, 1, seq + 2 * A_BLOCK), lambda b, i: (b, 0, 0))],
        out_specs=pl.BlockSpec((1, A_BLOCK, qw), lambda b, i: (b, i, 0)),
        compiler_params=_cparams(("parallel", "parallel")),
    )(sink, proj, kp, vp, posc, posr)


def wattn_bwd(proj, kp, vp, posc, posr, sink, dmo):
    bsz, seq, _ = proj.shape
    nb = seq // A_BLOCK
    qw = A_HEADS * LANES
    sp = seq + 2 * A_BLOCK

    def body(sink_ref, q_ref, kp_ref, vp_ref, posc_ref, posr_ref, do_ref,
             dq_ref, dk_ref, dv_ref, ds_ref):
        i = pl.program_id(1)

        @pl.when(i == 0)
        def _():
            dk_ref[...] = jnp.zeros_like(dk_ref)
            dv_ref[...] = jnp.zeros_like(dv_ref)

        @pl.when(jnp.logical_and(i == 0, pl.program_id(0) == 0))
        def _():
            ds_ref[...] = jnp.zeros_like(ds_ref)

        start, dist, valid = _wattn_block(i, seq, posc_ref, posr_ref)
        kb = kp_ref[0, pl.ds(start, A_BAND), :]
        vb = vp_ref[0, pl.ds(start, A_BAND), :]
        lane = lax.broadcasted_iota(jnp.int32, (A_GROUP * A_BLOCK, LANES), 1)
        lane1 = lax.broadcasted_iota(jnp.int32, (1, LANES), 1)
        dk_acc = jnp.zeros((A_BAND, LANES), F32)
        dv_acc = jnp.zeros((A_BAND, LANES), F32)
        dsink = jnp.zeros((1, LANES), F32)
        for g in range(A_KV_HEADS):
            heads = _group_heads(g)
            q4 = _stack_heads(q_ref, g, BF16)
            p, psink = _wattn_probs(q4, kb, dist, valid,
                                    _per_head_column([_alibi_slope(h) for h in heads]),
                                    _per_head_column([sink_ref[h] for h in heads]))
            pb = p.astype(BF16)
            do = _stack_heads(do_ref, g, F32)
            dob = do.astype(BF16)
            mine = (lane >= A_HEAD_DIM) if g == 1 else (lane < A_HEAD_DIM)
            o = jnp.where(mine, _dot(pb, vb, NN), 0.0)
            delta = jnp.sum(do * o, axis=-1, keepdims=True)
            dp = _dot(dob, vb, NT)
            ds = (p * (dp - delta) * A_SCALE).astype(BF16)
            dq = _dot(ds, kb, NN).astype(BF16)
            dk_acc = dk_acc + _dot(ds, q4, TN)
            dv_acc = dv_acc + _dot(pb, dob, TN)
            sd = psink * delta
            for j, h in enumerate(heads):
                dq_ref[0, :, h * LANES:(h + 1) * LANES] = dq[j * A_BLOCK:(j + 1) * A_BLOCK]
                dsh = -jnp.sum(sd[j * A_BLOCK:(j + 1) * A_BLOCK], axis=0, keepdims=True)
                dsink = dsink + jnp.where(lane1 == h, dsh, 0.0)
        dk_ref[0, pl.ds(start, A_BAND), :] += dk_acc
        dv_ref[0, pl.ds(start, A_BAND), :] += dv_acc
        ds_ref[0] += dsink

    full = pl.BlockSpec((1, sp, LANES), lambda b, i: (b, 0, 0))
    return pl.pallas_call(
        body, name="wattn_bwd",
        out_shape=[jax.ShapeDtypeStruct((bsz, seq, qw), BF16),
                   jax.ShapeDtypeStruct((bsz, sp, LANES), F32),
                   jax.ShapeDtypeStruct((bsz, sp, LANES), F32),
                   jax.ShapeDtypeStruct((1, 1, LANES), F32)],
        grid=(bsz, nb),
        in_specs=[pl.BlockSpec(memory_space=pltpu.SMEM),
                  pl.BlockSpec((1, A_BLOCK, qw), lambda b, i: (b, i, 0)),
                  full, full,
                  pl.BlockSpec((1, A_BLOCK, 1), lambda b, i: (b, i, 0)),
                  pl.BlockSpec((1, 1, sp), lambda b, i: (b, 0, 0)),
                  pl.BlockSpec((1, A_BLOCK, qw), lambda b, i: (b, i, 0))],
        out_specs=[pl.BlockSpec((1, A_BLOCK, qw), lambda b, i: (b, i, 0)), full, full,
                   pl.BlockSpec((1, 1, LANES), lambda b, i: (0, 0, 0))],
        compiler_params=_cparams(("arbitrary", "arbitrary")),
    )(sink, proj, kp, vp, posc, posr, dmo)


HG_Q, HG_FF, HG_FB, HG_I, HG_G = 10, 14, 18, 22, 26
HG_OUT = 8
CH = B_CHUNK
HG_GROUP = 4
GR = HG_GROUP * CH


def _hgrn_consts(reverse):
    r = lax.broadcasted_iota(jnp.int32, (GR, GR), 0)
    c = lax.broadcasted_iota(jnp.int32, (GR, GR), 1)
    same = jnp.right_shift(r, 6) == jnp.right_shift(c, 6)
    incl = same & ((c >= r) if reverse else (c <= r))
    row = lax.broadcasted_iota(jnp.int32, (HG_GROUP, CH, LANES), 1)
    mid = CH // 2 if reverse else CH // 2 - 1
    end = 0 if reverse else CH - 1
    return incl, row == mid, row == end


def _per_chunk(x, sel=None):
    x3 = x.reshape(HG_GROUP, CH, LANES)
    return jnp.sum(x3 if sel is None else jnp.where(sel, x3, 0.0), axis=1, keepdims=True)


def _to_rows(x3):
    return jnp.broadcast_to(x3, (HG_GROUP, CH, LANES)).reshape(GR, LANES)


def _chunk_cumsum(x, reverse):
    pos = lax.broadcasted_iota(jnp.int32, (GR, LANES), 0) & (CH - 1)
    s = 1
    while s < CH:
        if reverse:
            x = x + jnp.where(pos < CH - s, pltpu.roll(x, GR - s, axis=0), 0.0)
        else:
            x = x + jnp.where(pos >= s, pltpu.roll(x, s, axis=0), 0.0)
        s *= 2
    return x


def _block_diag(x):
    chunk = jnp.right_shift(lax.broadcasted_iota(jnp.int32, (GR, LANES), 0), 6)
    return jnp.concatenate([jnp.where(chunk == n, x, jnp.zeros_like(x)) for n in range(HG_GROUP)], axis=1)


def _hgrn_gates(hf, lb):
    sig = _sigmoid(hf)
    f = lb + (1.0 - lb) * sig
    return 1.0 - f, jnp.log(f), sig, f


def _hgrn_decays(lf, consts, reverse):
    _, is_mid, is_end = consts
    b = _chunk_cumsum(lf, reverse)
    return b, _per_chunk(b, is_mid), _per_chunk(b, is_end)


def _chunk_order(reverse):
    return range(HG_GROUP - 1, -1, -1) if reverse else range(HG_GROUP)


def _lane_block(x, n):
    return x[:, n * LANES:(n + 1) * LANES]


def _hgrn_states(vb, kd, dec3, st, reverse):
    inc = _dot(vb, _block_diag(kd), TN)
    entering = [None] * HG_GROUP
    for n in _chunk_order(reverse):
        entering[n] = st
        st = dec3[n] * st + _lane_block(inc, n)
    return entering, st


def _hgrn_group(q, k, v, lf, st, consts, reverse):
    b, bm3, be3 = _hgrn_decays(lf, consts, reverse)
    bm, be = _to_rows(bm3), _to_rows(be3)
    qs = (q * jnp.exp(b - bm)).astype(BF16)
    ks = (k * jnp.exp(bm - b)).astype(BF16)
    a = jnp.where(consts[0], _dot(qs, ks, NT), 0.0).astype(BF16)
    qe = (q * jnp.exp(b)).astype(BF16)
    vb = v.astype(BF16)
    kd = (k * jnp.exp(be - b)).astype(BF16)
    entering, st = _hgrn_states(vb, kd, jnp.exp(be3), st, reverse)
    s_all = jnp.concatenate([e.astype(BF16) for e in entering], axis=1)
    return _dot(a, vb, NN) + _dot(_block_diag(qe), s_all, NT), st


def _silu(x):
    return x * _sigmoid(x)


def hgrn_fwd(proj, mo, lb, nw, rider=None):
    bsz, seq, _ = proj.shape
    n_gr = seq // GR
    ts = _rows_tile(seq, 512)

    def body(hq_ref, hff_ref, hfb_ref, hi_ref, hg_ref, mo_in_ref, lb_ref, nw_ref, y_ref, ot_ref):
        del mo_in_ref
        lb_v, nw_v = lb_ref[0], nw_ref[0]

        def run(hf_ref, reverse, first):
            consts = _hgrn_consts(reverse)

            def step(t, st):
                n = (n_gr - 1 - t) if reverse else t
                rows = pl.ds(pl.multiple_of(n * GR, GR), GR)
                k, lf, _, _ = _hgrn_gates(hf_ref[0, rows, :], lb_v)
                o, st = _hgrn_group(_silu(hq_ref[0, rows, :]), k, hi_ref[0, rows, :], lf, st, consts, reverse)
                if first:
                    ot_ref[0, rows, :] = o
                else:
                    ot_ref[0, rows, :] += o
                return st

            lax.fori_loop(0, n_gr, step, jnp.zeros((LANES, LANES), F32))

        run(hff_ref, False, True)
        run(hfb_ref, True, False)

        def finish(i, carry):
            rows = pl.ds(pl.multiple_of(i * ts, ts), ts)
            o = ot_ref[0, rows, :]
            r = lax.rsqrt(jnp.mean(o * o, axis=-1, keepdims=True) + RMS_EPS)
            y_ref[0, rows, :] = (o * r * nw_v * _silu(hg_ref[0, rows, :])).astype(BF16)
            return carry

        lax.fori_loop(0, seq // ts, finish, 0)

    def col(g):
        return pl.BlockSpec((1, seq, LANES), lambda b, h: (b, 0, g + h))

    par = pl.BlockSpec((1, 1, LANES), lambda b, h: (h, 0, 0))
    args = [proj, proj, proj, proj, proj, mo, lb, nw]
    in_specs = [col(HG_Q), col(HG_FF), col(HG_FB), col(HG_I), col(HG_G), pl.BlockSpec(memory_space=pl.ANY), par, par]
    out_shape = [jax.ShapeDtypeStruct(mo.shape, BF16), jax.ShapeDtypeStruct((bsz, seq, B_HEADS * LANES), F32)]
    out_specs = [col(HG_OUT), col(0)]
    scratch, aliases = [], {5: 0}
    if rider is not None:
        def first():
            return jnp.logical_and(pl.program_id(0) == 0, pl.program_id(1) == 0)

        def last():
            return jnp.logical_and(pl.program_id(0) == bsz - 1, pl.program_id(1) == B_HEADS - 1)

        body = rider.wrap(body, len(args), len(out_shape), first, last)
        args, in_specs, out_shape, out_specs, scratch, more = rider.extend(
            len(args), len(out_shape), args, in_specs, out_shape, out_specs, scratch)
        aliases.update(more)
    return pl.pallas_call(
        body, name="hgrn_fwd", out_shape=out_shape, grid=(bsz, B_HEADS),
        in_specs=in_specs, out_specs=out_specs, scratch_shapes=scratch, input_output_aliases=aliases,
        compiler_params=_cparams(("arbitrary", "arbitrary") if rider is not None else ("parallel", "parallel")),
    )(*args)


def hgrn_out_bwd(proj, otot, dmo, nw):
    bsz, seq, _ = proj.shape
    ts = _rows_tile(seq, 512)

    def body(hg_ref, ot_ref, dmo_ref, nw_ref, do_ref, dhg_ref, dnw_ref):
        nw_v = nw_ref[0]
        dy = dmo_ref[0].astype(F32)
        hg = hg_ref[0]
        o = ot_ref[0]
        sg = _sigmoid(hg)
        r = lax.rsqrt(jnp.mean(o * o, axis=-1, keepdims=True) + RMS_EPS)
        dhg_ref[0] = (dy * (o * r * nw_v) * (sg * (1.0 + hg * (1.0 - sg)))).astype(BF16)
        drn = dy * (hg * sg)
        dnw = jnp.sum(drn * o * r, axis=0, keepdims=True)
        wd = drn * nw_v
        do_ref[0] = r * wd - o * (r * r * r) * jnp.mean(o * wd, axis=-1, keepdims=True)
        first = jnp.logical_and(pl.program_id(1) == 0, pl.program_id(2) == 0)

        @pl.when(first)
        def _():
            dnw_ref[0] = dnw

        @pl.when(jnp.logical_not(first))
        def _():
            dnw_ref[0] += dnw

    def col(g):
        return pl.BlockSpec((1, ts, LANES), lambda h, b, s: (b, s, g + h))

    par = pl.BlockSpec((1, 1, LANES), lambda h, b, s: (h, 0, 0))
    return pl.pallas_call(
        body, name="hgrn_out_bwd",
        out_shape=[jax.ShapeDtypeStruct((bsz, seq, B_HEADS * LANES), F32),
                   jax.ShapeDtypeStruct((bsz, seq, B_HEADS * LANES), BF16),
                   jax.ShapeDtypeStruct((B_HEADS, 1, LANES), F32)],
        grid=(B_HEADS, bsz, seq // ts),
        in_specs=[col(HG_G), col(0), col(HG_OUT), par],
        out_specs=[col(0), col(0), par],
        compiler_params=_cparams(("parallel", "arbitrary", "arbitrary")),
    )(proj, otot, dmo, nw)


def hgrn_bwd(proj, do, lb, rider=None):
    bsz, seq, _ = proj.shape
    n_ch = seq // CH
    n_gr = seq // GR
    ts = _rows_tile(seq, 512)

    def body(hq_ref, hff_ref, hfb_ref, hi_ref, do_scr, lb_ref,
             dhq_ref, dhff_ref, dhfb_ref, dhi_ref, dlb_ref, st_scr, dq_scr, dv_scr):
        lb_v = lb_ref[0]
        do_scr = do_scr.at[0]

        def run(hf_ref, dhf_ref, reverse, first, dlb0):
            consts = _hgrn_consts(reverse)
            incl, is_mid, is_end = consts

            def load(n):
                rows = pl.ds(pl.multiple_of(n * GR, GR), GR)
                hf = hf_ref[0, rows, :]
                k, lf, sig, f = _hgrn_gates(hf, lb_v)
                return rows, _silu(hq_ref[0, rows, :]), k, hi_ref[0, rows, :], lf, sig, f

            def fwd_step(t, st):
                n = (n_gr - 1 - t) if reverse else t
                _, _, k, v, lf, _, _ = load(n)
                b, _, be3 = _hgrn_decays(lf, consts, reverse)
                kd = (k * jnp.exp(_to_rows(be3) - b)).astype(BF16)
                entering, st = _hgrn_states(v.astype(BF16), kd, jnp.exp(be3), st, reverse)
                for c in range(HG_GROUP):
                    st_scr[n * HG_GROUP + c] = entering[c].astype(BF16)
                return st

            lax.fori_loop(0, n_gr, fwd_step, jnp.zeros((LANES, LANES), F32))

            def bwd_step(t, carry):
                gt, dlb = carry
                n = t if reverse else (n_gr - 1 - t)
                rows, q, k, v, lf, sig, f = load(n)
                do_c = do_scr[rows, :].astype(BF16)
                b, bm3, be3 = _hgrn_decays(lf, consts, reverse)
                bm, be = _to_rows(bm3), _to_rows(be3)
                e_qs, e_ks, e_q, e_kd = jnp.exp(b - bm), jnp.exp(bm - b), jnp.exp(b), jnp.exp(be - b)
                dec3 = jnp.exp(be3)
                qs, ks, qe, kd = q * e_qs, k * e_ks, q * e_q, k * e_kd
                qs_b, ks_b, qe_b, kd_b = qs.astype(BF16), ks.astype(BF16), qe.astype(BF16), kd.astype(BF16)
                vb = v.astype(BF16)
                a = jnp.where(incl, _dot(qs_b, ks_b, NT), 0.0).astype(BF16)
                da = jnp.where(incl, _dot(do_c, vb, NT), 0.0).astype(BF16)
                dv = _dot(a, do_c, TN)
                dqs = _dot(da, ks_b, NN)
                dks = _dot(da, qs_b, TN)
                zed = _dot(do_c, _block_diag(qe_b), TN)
                sts = [st_scr[n * HG_GROUP + c] for c in range(HG_GROUP)]
                gts, ddec = [None] * HG_GROUP, [None] * HG_GROUP
                for c in reversed(list(_chunk_order(reverse))):
                    gts[c] = gt
                    ddec[c] = jnp.sum(gt * sts[c].astype(F32), axis=0, keepdims=True)
                    gt = dec3[c] * gt + _lane_block(zed, c)
                g_b = [g.astype(BF16) for g in gts]
                dqe = _dot(_block_diag(do_c), jnp.concatenate(sts, axis=0), NN)
                dkd = _dot(_block_diag(vb), jnp.concatenate(g_b, axis=0), NN)
                dv = dv + _dot(_block_diag(kd_b), jnp.concatenate(g_b, axis=1), NT)
                ddec3 = jnp.stack(ddec, axis=0)
                dq = dqs * e_qs + dqe * e_q
                dk = dks * e_ks + dkd * e_kd
                t_qs, t_ks, t_kd = dqs * qs, dks * ks, dkd * kd
                db = (t_qs - t_ks + dqe * qe - t_kd).reshape(HG_GROUP, CH, LANES)
                dbm3 = _per_chunk(t_ks - t_qs)
                dbe3 = _per_chunk(t_kd) + ddec3 * dec3
                db = (db + jnp.where(is_mid, dbm3, 0.0) + jnp.where(is_end, dbe3, 0.0)).reshape(GR, LANES)
                dlf = _chunk_cumsum(db, not reverse)
                df = dlf / f - dk
                dhf_ref[0, rows, :] = (df * (1.0 - lb_v) * sig * (1.0 - sig)).astype(BF16)
                dlb = dlb + jnp.sum(df * (1.0 - sig), axis=0, keepdims=True)
                if first:
                    dq_scr[rows, :] = dq
                    dv_scr[rows, :] = dv
                else:
                    dq_scr[rows, :] += dq
                    dv_scr[rows, :] += dv
                return gt, dlb

            return lax.fori_loop(0, n_gr, bwd_step, (jnp.zeros((LANES, LANES), F32), dlb0))[1]

        dlb = run(hff_ref, dhff_ref, False, True, jnp.zeros((1, LANES), F32))
        dlb = run(hfb_ref, dhfb_ref, True, False, dlb)

        @pl.when(pl.program_id(1) == 0)
        def _():
            dlb_ref[0] = dlb

        @pl.when(pl.program_id(1) > 0)
        def _():
            dlb_ref[0] += dlb

        def finish(i, carry):
            rows = pl.ds(pl.multiple_of(i * ts, ts), ts)
            hq = hq_ref[0, rows, :]
            sq = _sigmoid(hq)
            dhq_ref[0, rows, :] = (dq_scr[rows, :] * (sq * (1.0 + hq * (1.0 - sq)))).astype(BF16)
            dhi_ref[0, rows, :] = dv_scr[rows, :].astype(BF16)
            return carry

        lax.fori_loop(0, seq // ts, finish, 0)

    def col(g):
        return pl.BlockSpec((1, seq, LANES), lambda h, b: (b, 0, g + h))

    par = pl.BlockSpec((1, 1, LANES), lambda h, b: (h, 0, 0))
    wide = jax.ShapeDtypeStruct((bsz, seq, B_HEADS * LANES), BF16)
    small = jax.ShapeDtypeStruct((B_HEADS, 1, LANES), F32)
    args = [proj, proj, proj, proj, do, lb]
    in_specs = [col(HG_Q), col(HG_FF), col(HG_FB), col(HG_I), col(0), par]
    out_shape = [wide, wide, wide, wide, small]
    out_specs = [col(0), col(0), col(0), col(0), par]
    scratch = [pltpu.VMEM((n_ch, LANES, LANES), BF16), pltpu.VMEM((seq, LANES), F32), pltpu.VMEM((seq, LANES), F32)]
    aliases = {}
    if rider is not None:
        def first():
            return jnp.logical_and(pl.program_id(0) == 0, pl.program_id(1) == 0)

        def last():
            return jnp.logical_and(pl.program_id(0) == B_HEADS - 1, pl.program_id(1) == bsz - 1)

        body = rider.wrap(body, len(args), len(out_shape), first, last)
        args, in_specs, out_shape, out_specs, scratch, aliases = rider.extend(
            len(args), len(out_shape), args, in_specs, out_shape, out_specs, scratch)
    return pl.pallas_call(
        body, name="hgrn_bwd", out_shape=out_shape, grid=(B_HEADS, bsz),
        in_specs=in_specs, out_specs=out_specs, scratch_shapes=scratch, input_output_aliases=aliases,
        compiler_params=_cparams(("arbitrary", "arbitrary")),
    )(*args)


def _whole(shape):
    return pl.BlockSpec(shape, lambda: (0,) * len(shape))


def silu_small(x):
    def body(x_ref, o_ref):
        o_ref[...] = _silu(x_ref[...])

    return pl.pallas_call(body, name="silu_small", out_shape=jax.ShapeDtypeStruct(x.shape, F32),
                          in_specs=[_whole(x.shape)], out_specs=_whole(x.shape))(x)


def _softmax_rows(x):
    e = jnp.exp(x - jnp.max(x, axis=0, keepdims=True))
    return e / jnp.sum(e, axis=0, keepdims=True)


def lb_fwd(logits, layer):
    n, w = logits.shape

    def body(x_ref, o_ref):
        p = _softmax_rows(x_ref[...])
        row = lax.broadcasted_iota(jnp.int32, (n, w), 0)
        o_ref[...] = jnp.sum(jnp.where(row <= layer, p, 0.0), axis=0, keepdims=True)

    return pl.pallas_call(body, name="lb_fwd", out_shape=jax.ShapeDtypeStruct((1, w), F32),
                          in_specs=[_whole((n, w))], out_specs=_whole((1, w)))(logits)


def lb_bwd(logits, dlb, layer):
    n, w = logits.shape

    def body(x_ref, d_ref, o_ref):
        p = _softmax_rows(x_ref[...])
        row = lax.broadcasted_iota(jnp.int32, (n, w), 0)
        dp = jnp.where(row <= layer, d_ref[...], 0.0)
        o_ref[...] = p * (dp - jnp.sum(p * dp, axis=0, keepdims=True))

    return pl.pallas_call(body, name="lb_bwd", out_shape=jax.ShapeDtypeStruct((n, w), F32),
                          in_specs=[_whole((n, w)), _whole((1, w))], out_specs=_whole((n, w)))(logits, dlb)


def sum_parts(x, name):
    p, r, c = x.shape
    tr = _rows_tile(r, max(8, (1 << 23) // (4 * c * p) // 8 * 8))

    def body(x_ref, o_ref):
        acc = x_ref[0].astype(F32)
        for j in range(1, p):
            acc = acc + x_ref[j].astype(F32)
        o_ref[...] = acc

    return pl.pallas_call(
        body, name=name, out_shape=jax.ShapeDtypeStruct((r, c), F32),
        grid=(r // tr,),
        in_specs=[pl.BlockSpec((p, tr, c), lambda i: (0, i, 0))],
        out_specs=pl.BlockSpec((tr, c), lambda i: (i, 0)),
        compiler_params=_cparams(("parallel",)),
    )(x)


def adamw(w, g, m, v, name):
    r, c = w.shape
    tr = _rows_tile(r, max(8, (1 << 20) // (4 * c) // 8 * 8))
    c1 = 1.0 - ADAM_B1 ** ADAM_STEP
    c2 = 1.0 - ADAM_B2 ** ADAM_STEP

    def body(w_ref, g_ref, m_ref, v_ref, d_ref, nm_ref, nv_ref):
        g = g_ref[...]
        nm = ADAM_B1 * m_ref[...] + (1.0 - ADAM_B1) * g
        nv = ADAM_B2 * v_ref[...] + (1.0 - ADAM_B2) * (g * g)
        nm_ref[...] = nm
        nv_ref[...] = nv
        d_ref[...] = -ADAM_LR * ((nm / c1) / (jnp.sqrt(nv / c2) + ADAM_EPS) + ADAM_WD * w_ref[...])

    spec = pl.BlockSpec((tr, c), lambda i: (i, 0))
    sds = jax.ShapeDtypeStruct((r, c), F32)
    return pl.pallas_call(
        body, name=name, out_shape=[sds, sds, sds], grid=(r // tr,),
        in_specs=[spec, spec, spec, spec], out_specs=[spec, spec, spec],
        compiler_params=_cparams(("parallel",)),
    )(w, g, m, v)


ANY = pl.BlockSpec(memory_space=pl.ANY)


def _my_place():
    return lax.axis_index("x"), lax.axis_index("y"), lax.axis_index("c")


def all_gather(x, name):
    def body(x_ref, out_ref, send_sems, recv_sems, local_sem):
        mx, my, mc = _my_place()
        me, sibling = (mx, my, mc), (mx, my, 1 - mc)
        chips = [(1 - mx, my), (mx, 1 - my), (1 - mx, 1 - my)]

        def slot(px, py, pc):
            return out_ref.at[4 * px + 2 * py + pc]

        def copy(k, block, to, src=None):
            return pltpu.make_async_remote_copy(
                src_ref=slot(*block) if src is None else src, dst_ref=slot(*block),
                send_sem=send_sems.at[k], recv_sem=recv_sems.at[k],
                device_id=to, device_id_type=MESH_ID)

        mine = pltpu.make_async_copy(x_ref, slot(*me), local_sem)
        mine.start()
        first = [copy(0, me, sibling, src=x_ref)]
        first += [copy(1 + j, me, (*chip, mc), src=x_ref) for j, chip in enumerate(chips)]
        for cp in first:
            cp.start()
        passed = [copy(4 + j, (*chip, mc), sibling) for j, chip in enumerate(chips)]
        for j, chip in enumerate(chips):
            copy(1 + j, (*chip, mc), me).wait_recv()
            passed[j].start()
        copy(0, sibling, me).wait_recv()
        for j, chip in enumerate(chips):
            copy(4 + j, (*chip, 1 - mc), me).wait_recv()
        for cp in first + passed:
            cp.wait_send()
        mine.wait()

    return pl.pallas_call(
        body, name=name,
        out_shape=jax.ShapeDtypeStruct((N_DEV,) + x.shape, x.dtype),
        in_specs=[ANY], out_specs=ANY,
        scratch_shapes=[pltpu.SemaphoreType.DMA((7,)), pltpu.SemaphoreType.DMA((7,)),
                        pltpu.SemaphoreType.DMA(())],
    )(x)


def swap_with_sibling(x, name):
    def body(x_ref, out_ref, send_sems, recv_sems):
        copies = _sibling_swap_copies(x_ref, out_ref, send_sems, recv_sems)
        for cp, _ in copies:
            cp.start()
        for _, cp in copies:
            cp.wait()

    return pl.pallas_call(
        body, name=name,
        out_shape=jax.ShapeDtypeStruct((x.shape[0],) + x.shape[2:], x.dtype),
        in_specs=[ANY], out_specs=ANY,
        scratch_shapes=[pltpu.SemaphoreType.DMA((4,)), pltpu.SemaphoreType.DMA((4,))],
    )(x)


def pair_add(x, r, mc):
    n, _, rows, c = x.shape
    tr = _rows_tile(rows, 512)

    def body(mc_ref, x_ref, r_ref, o_ref):
        del mc_ref
        o_ref[0] = (x_ref[0, 0].astype(F32) + r_ref[0].astype(F32)).astype(BF16)

    return pl.pallas_call(
        body, name="pair_add",
        out_shape=jax.ShapeDtypeStruct((n, rows, c), BF16),
        grid_spec=pltpu.PrefetchScalarGridSpec(
            num_scalar_prefetch=1, grid=(n, rows // tr),
            in_specs=[pl.BlockSpec((1, 1, tr, c), lambda j, i, s: (j, s[0], i, 0)),
                      pl.BlockSpec((1, tr, c), lambda j, i, s: (j, i, 0))],
            out_specs=pl.BlockSpec((1, tr, c), lambda j, i, s: (j, i, 0))),
        compiler_params=_cparams(("parallel", "parallel")),
    )(mc.reshape(1).astype(jnp.int32), x, r)


def chip_all_to_all(x, name):
    def body(x_ref, out_ref, send_sems, recv_sems, local_sem):
        mx, my, mc = _my_place()
        me = 2 * mx + my
        mine = pltpu.make_async_copy(x_ref.at[me], out_ref.at[me], local_sem)
        mine.start()
        copies = []
        for k, (fx, fy) in enumerate(((1, 0), (0, 1), (1, 1))):
            px = 1 - mx if fx else mx
            py = 1 - my if fy else my
            peer = 2 * px + py
            cp = pltpu.make_async_remote_copy(
                src_ref=x_ref.at[peer], dst_ref=out_ref.at[me],
                send_sem=send_sems.at[k], recv_sem=recv_sems.at[k],
                device_id=(px, py, mc), device_id_type=MESH_ID)
            cp.start()
            copies.append((cp, peer, (px, py, mc), k))
        for cp, peer, to, k in copies:
            pltpu.make_async_remote_copy(
                src_ref=x_ref.at[peer], dst_ref=out_ref.at[peer],
                send_sem=send_sems.at[k], recv_sem=recv_sems.at[k],
                device_id=to, device_id_type=MESH_ID).wait_recv()
        for cp, _, _, _ in copies:
            cp.wait_send()
        mine.wait()

    return pl.pallas_call(
        body, name=name,
        out_shape=jax.ShapeDtypeStruct(x.shape, x.dtype),
        in_specs=[ANY], out_specs=ANY,
        scratch_shapes=[pltpu.SemaphoreType.DMA((3,)), pltpu.SemaphoreType.DMA((3,)),
                        pltpu.SemaphoreType.DMA(())],
    )(x)


ROW_W = D_MODEL
FF_SH = D_FF // N_DEV
FFN_PER_LAYER = 2
MDOWN_SEC_ROWS = 80
MDOWN_ROWS = 128 * 544 // ROW_W
PART_SECS = (
    (("gate", FFN_PER_LAYER * FF_SH), ("up", FFN_PER_LAYER * FF_SH), ("down", FFN_PER_LAYER * FF_SH),
     ("hin", 416), ("hout", 128)),
    (("gate", FFN_PER_LAYER * FF_SH), ("up", FFN_PER_LAYER * FF_SH), ("down", FFN_PER_LAYER * FF_SH),
     ("mout", 128), ("mdown", MDOWN_SEC_ROWS), ("uq", 48), ("ukv", 64)),
)
SEC, PART_ROWS = [], []
for _secs in PART_SECS:
    _table, _off = {}, 0
    for _name, _rows in _secs:
        _table[_name] = (_off, _rows)
        _off += _rows
    SEC.append(_table)
    PART_ROWS.append(_off)
A_Q_W = A_HEADS * A_HEAD_DIM
_KVHEAD = (np.arange(A_HEADS) // A_GROUP).reshape(A_HEADS, 1, 1)


def pack_shards(layer, sh, dtype):
    def rows(a):
        return a.astype(dtype).reshape(-1, ROW_W)

    def t(a):
        return rows(jnp.swapaxes(a, -1, -2))

    parts = [t(sh["ffn_w_gate"][layer]), t(sh["ffn_w_up"][layer]), rows(sh["ffn_w_down"][layer])]
    if layer == 0:
        parts += [t(sh["hyb_w_in"]), rows(sh["hyb_w_out"])]
    else:
        parts += [rows(sh["mla_w_out"]),
                  jnp.pad(rows(sh["mla_w_down"]), ((0, MDOWN_SEC_ROWS - MDOWN_ROWS), (0, 0))),
                  t(sh["mla_w_uq"]), t(sh["mla_w_ukv"])]
    return jnp.concatenate(parts, axis=0)


def unpack_shards(parts):
    def sec(layer, name):
        o, n = SEC[layer][name]
        return parts[layer][o:o + n]

    def col(layer, name, r, c):
        return jnp.swapaxes(sec(layer, name).reshape(-1, c, r), -1, -2)

    def both(f):
        return jnp.stack([f(0), f(1)])

    return dict(
        ffn_w_gate=both(lambda l: col(l, "gate", 1024, FF_SH)),
        ffn_w_up=both(lambda l: col(l, "up", 1024, FF_SH)),
        ffn_w_down=both(lambda l: sec(l, "down").reshape(FFN_PER_LAYER, FF_SH, 1024)),
        hyb_w_in=col(0, "hin", 1024, 416),
        hyb_w_out=sec(0, "hout").reshape(1, 128, 1024),
        mla_w_out=sec(1, "mout").reshape(1, 128, 1024),
        mla_w_down=sec(1, "mdown")[:MDOWN_ROWS].reshape(1, 128, 544),
        mla_w_uq=col(1, "uq", 256, 192),
        mla_w_ukv=col(1, "ukv", 256, 256),
    )


def _pad_qheads(w):
    a = w.reshape(A_HEADS, 1, A_HEAD_DIM, -1)
    kvh = _KVHEAD[..., None]
    both = jnp.concatenate([jnp.where(kvh == 0, a, 0), jnp.where(kvh == 1, a, 0)], axis=1)
    return both.reshape(A_HEADS * LANES, -1)


def _unpad_qheads(w):
    a = w.reshape(A_HEADS, 2, A_HEAD_DIM, -1)
    return jnp.where(_KVHEAD == 0, a[:, 0], a[:, 1]).reshape(A_Q_W, -1)


def unpack_full(layer, g):
    def sec(name):
        o, n = SEC[layer][name]
        return g[:, o:o + n]

    def ffn(name):
        return jnp.swapaxes(sec(name).reshape(N_DEV, FFN_PER_LAYER, FF_SH, ROW_W), 0, 1).reshape(
            FFN_PER_LAYER, D_FF, ROW_W)

    def z(*s):
        return jnp.zeros(s, g.dtype)

    def halves(a):
        return a.reshape(FFN_PER_LAYER, 2, 1, FF_HALF, ROW_W)

    gate_t, up_t, down = ffn("gate"), ffn("up"), ffn("down")
    w = dict(wgu_t=jnp.concatenate([halves(gate_t), halves(up_t)], axis=2).reshape(FFN_PER_LAYER, 2 * D_FF, ROW_W),
             wd=down)
    w.update(wgu=jnp.swapaxes(w["wgu_t"], 1, 2), wd_t=jnp.swapaxes(down, 1, 2))
    if layer == 0:
        hin_t = sec("hin").reshape(-1, ROW_W)
        hout = sec("hout").reshape(-1, ROW_W)
        w.update(hin_t=jnp.concatenate([_pad_qheads(hin_t[:A_Q_W]), hin_t[A_Q_W:]], axis=0),
                 hout=jnp.concatenate([_pad_qheads(hout[:A_Q_W]), hout[A_Q_W:]], axis=0))
        w.update(hin=w["hin_t"].T, hout_t=w["hout"].T)
    else:
        mout = sec("mout").reshape(C_HEADS, C_V, ROW_W)
        mdown = sec("mdown")[:, :MDOWN_ROWS].reshape(D_MODEL, 544)
        uq_t = sec("uq").reshape(C_HEADS, C_NOPE + C_ROPE, C_Q_RANK)
        w.update(mout=jnp.concatenate([z(C_HEADS, C_NOPE, ROW_W), mout], axis=1).reshape(MLA_HEAD_PAD, ROW_W),
                 mdown=jnp.concatenate([mdown[:, :512], z(D_MODEL, 64), mdown[:, 512:], z(D_MODEL, 32)], axis=1),
                 uq_t=jnp.concatenate([uq_t, z(C_HEADS, 32, C_Q_RANK)], axis=1).reshape(MLA_HEAD_PAD, C_Q_RANK),
                 ukv_t=sec("ukv").reshape(MLA_HEAD_PAD, C_KV_RANK))
        w.update(mout_t=w["mout"].T, mdown_t=w["mdown"].T, uq=w["uq_t"].T, ukv=w["ukv_t"].T)
    return w


def pack_full(layer, d):
    gu = jnp.stack(d["wgu_t"]).reshape(FFN_PER_LAYER, 2, 2, FF_HALF, ROW_W)

    def split(a):
        return a.reshape(N_DEV, -1, ROW_W)

    def ffn(a):
        return jnp.swapaxes(a.reshape(FFN_PER_LAYER, N_DEV, FF_SH, ROW_W), 0, 1).reshape(
            N_DEV, FFN_PER_LAYER * FF_SH, ROW_W)

    parts = [ffn(gu[:, :, 0]), ffn(gu[:, :, 1]), ffn(jnp.stack(d["wd"]))]
    if layer == 0:
        hin = jnp.concatenate([_unpad_qheads(d["hin_t"][:A_HEADS * LANES]), d["hin_t"][A_HEADS * LANES:]], axis=0)
        hout = jnp.concatenate([_unpad_qheads(d["hout"][:A_HEADS * LANES]), d["hout"][A_HEADS * LANES:]], axis=0)
        parts += [split(hin), split(hout)]
    else:
        mout = d["mout"].reshape(C_HEADS, LANES, ROW_W)[:, C_NOPE:].reshape(-1, ROW_W)
        mdown = jnp.concatenate([d["mdown"][:, :512], d["mdown"][:, 576:608]], axis=1)
        uq = d["uq_t"].reshape(C_HEADS, LANES, C_Q_RANK)[:, :C_NOPE + C_ROPE]
        parts += [split(mout), jnp.pad(split(mdown), ((0, 0), (0, MDOWN_SEC_ROWS - MDOWN_ROWS), (0, 0))),
                  split(uq), split(d["ukv_t"])]
    return jnp.concatenate(parts, axis=1)


def rope_tables(positions):
    half = C_ROPE // 2
    freqs = ROPE_THETA ** (-jnp.arange(half, dtype=F32) / half)
    ang = positions.astype(F32)[..., None] * freqs
    cos, sin = jnp.cos(ang), jnp.sin(ang)
    shape = positions.shape
    cq = jnp.concatenate([jnp.ones(shape + (C_NOPE,), F32), cos, cos, jnp.zeros(shape + (32,), F32)], axis=-1)
    sg = jnp.concatenate([jnp.zeros(shape + (C_NOPE,), F32), -sin, sin, jnp.zeros(shape + (32,), F32)], axis=-1)
    return cq, sg


GRAD_DT = BF16
BRANCH_DT = BF16
def ffn_fwd(h2, w, i):
    gu, a = ffn_up_fused(h2, w["wgu"][i])
    return matmul(a, w["wd"][i], "nn", BRANCH_DT, "ffn_down"), (gu, a)


def ffn_bwd(dy2, h2, w, i, saved, swap=None):
    gu, a = saved
    dgu = ffn_down_bwd_fused(dy2, w["wd_t"][i], gu)
    if swap is None:
        dwd, swapped = matmul(a, dy2, "tn", GRAD_DT, "ffn_down_dw", tm_cap=FF_HALF), None
    else:
        dwd, swapped = matmul(a, dy2, "tn", GRAD_DT, "ffn_down_dw_swap", tm_cap=FF_HALF,
                              rider=ride_swap_with_sibling(swap))
    dwgu_t = matmul(dgu, h2, "tn", GRAD_DT, "ffn_up_dw", tm_cap=FF_HALF)
    dh = matmul(dgu, w["wgu_t"][i], "nn", BRANCH_DT, "ffn_up_dx")
    return dh, dwgu_t, dwd, swapped


def hybrid_fwd(h2, shape, w, aux):
    bsz, seq = shape
    proj = matmul(h2, w["hin"], "nn", F32, "hyb_in").reshape(bsz, seq, HYB_PAD)
    pad = ((0, 0), (A_BLOCK, A_BLOCK), (0, 0))
    kp = jnp.pad(proj[:, :, 8 * LANES:9 * LANES].astype(BF16), pad)
    vp = jnp.pad(proj[:, :, 9 * LANES:10 * LANES].astype(BF16), pad)
    mo = wattn_fwd(proj, kp, vp, aux["posc"], aux["posr"], aux["sink"])
    mo, otot, gathered = hgrn_fwd(proj, mo, aux["lb"], aux["nw"], rider=ride_gather_level1(aux["next_packed"]))
    y, gathered = matmul(mo.reshape(bsz * seq, MO_PAD), w["hout"], "nn", BRANCH_DT, "hyb_out_gather",
                         rider=ride_gather_level2(gathered))
    return y, (proj, kp, vp, mo, otot), gathered


def hybrid_bwd(dy2, h2, shape, w, aux, saved):
    bsz, seq = shape
    proj, kp, vp, mo, otot = saved
    mo2 = mo.reshape(bsz * seq, MO_PAD)
    dmo = matmul(dy2, w["hout_t"], "nn", F32, "hyb_out_dx").reshape(bsz, seq, MO_PAD)
    dhout = matmul(mo2, dy2, "tn", GRAD_DT, "hyb_out_dw")
    dq, dkp, dvp, dsink = wattn_bwd(proj, kp, vp, aux["posc"], aux["posr"], aux["sink"], dmo)
    do, dhg, dnw = hgrn_out_bwd(proj, otot, dmo, aux["nw"])
    dhq, dhff, dhfb, dhi, dlb, aux["exchanged"] = hgrn_bwd(proj, do, aux["lb"],
                                                            rider=ride_chip_all_to_all(aux["pair_sums"]))
    dproj = jnp.concatenate([dq, dkp[:, A_BLOCK:-A_BLOCK].astype(BF16), dvp[:, A_BLOCK:-A_BLOCK].astype(BF16),
                             dhq, dhff, dhfb, dhi, dhg], axis=-1).reshape(bsz * seq, HYB_PAD)
    dhin_t = matmul(dproj, h2, "tn", GRAD_DT, "hyb_in_dw")
    dh = matmul(dproj, w["hin_t"], "nn", BRANCH_DT, "hyb_in_dx")
    return dh, dict(hin_t=dhin_t, hout=dhout), dict(sink=dsink, lb=dlb, nw=dnw)


def mla_fwd(h2, shape, w, aux):
    bsz, seq = shape
    t = bsz * seq
    cqkv = matmul(h2, w["mdown"], "nn", F32, "mla_down").reshape(bsz, seq, MLA_DOWN_PAD)
    nq, nkv, kr = mla_mid_fwd(cqkv, aux["qn"], aux["kvn"], aux["cq"], aux["sg"])
    q = matmul(nq.reshape(t, C_Q_RANK), w["uq"], "nn", F32, "mla_uq").reshape(bsz, seq, MLA_HEAD_PAD)
    qp = q_rope(q, aux["cq"], aux["sg"], False, "q_rope_fwd", out_scale=MLA_QSCALE)
    kv = matmul(nkv.reshape(t, C_KV_RANK), w["ukv"], "nn", BF16, "mla_ukv").reshape(bsz, seq, MLA_HEAD_PAD)
    o, lse = mla_attn_fwd(qp, kv, kr)
    y = matmul(o.reshape(t, MLA_HEAD_PAD), w["mout"], "nn", BRANCH_DT, "mla_out")
    return y, (cqkv, nq, nkv, kr, qp, kv, o, lse)


def mla_bwd(dy2, h2, shape, w, aux, saved):
    bsz, seq = shape
    t = bsz * seq
    cqkv, nq, nkv, kr, qp, kv, o, lse = saved
    do = matmul(dy2, w["mout_t"], "nn", BF16, "mla_out_dx").reshape(bsz, seq, MLA_HEAD_PAD)
    dmout = matmul(o.reshape(t, MLA_HEAD_PAD), dy2, "tn", GRAD_DT, "mla_out_dw")
    dqp, dkv, dkr = mla_attn_bwd(qp, kv, kr, o, do, lse)
    dq = q_rope(dqp, aux["cq"], aux["sg"], True, "q_rope_bwd").reshape(t, MLA_HEAD_PAD)
    dkv2 = dkv.reshape(t, MLA_HEAD_PAD)
    dnq = matmul(dq, w["uq_t"], "nn", F32, "mla_uq_dx").reshape(bsz, seq, C_Q_RANK)
    duq_t = matmul(dq, nq.reshape(t, C_Q_RANK), "tn", GRAD_DT, "mla_uq_dw")
    dnkv = matmul(dkv2, w["ukv_t"], "nn", F32, "mla_ukv_dx").reshape(bsz, seq, C_KV_RANK)
    dukv_t = matmul(dkv2, nkv.reshape(t, C_KV_RANK), "tn", GRAD_DT, "mla_ukv_dw")
    dcqkv, dqn, dkvn = mla_mid_bwd(dnq, dnkv, dkr, cqkv, aux["qn"], aux["kvn"], aux["cq"], aux["sg"])
    dcqkv2 = dcqkv.reshape(t, MLA_DOWN_PAD)
    dmdown = matmul(h2, dcqkv2, "tn", GRAD_DT, "mla_down_dw")
    dh = matmul(dcqkv2, w["mdown_t"], "nn", BRANCH_DT, "mla_down_dx")
    return dh, dict(mdown=dmdown, uq_t=duq_t, ukv_t=dukv_t, mout=dmout), dict(qn=dqn, kvn=dkvn)


W_NAMES = ['ada_w', 'ada_b', 'ln_g', 'ln_b', 'ffn_w_gate', 'ffn_w_up', 'ffn_w_down', 'hyb_w_in', 'hyb_w_out',
           'attn_sink', 'hgrn_lb_logits', 'hgrn_norm_w', 'mla_w_down', 'mla_q_norm', 'mla_kv_norm', 'mla_w_uq',
           'mla_w_ukv', 'mla_w_out']
SMALL_NAMES = ['ada_b', 'ln_g', 'ln_b', 'attn_sink', 'hgrn_lb_logits', 'hgrn_norm_w', 'mla_q_norm', 'mla_kv_norm']
MOD_W = N_SUB * 3 * D_MODEL
MOD_SH = MOD_W // N_DEV
RES_W = (0.5, 1.0, 0.5)


def _rows1024(a, rows=None):
    flat = a.astype(F32).reshape(-1)
    n = flat.shape[0]
    total = (-(-n // ROW_W) if rows is None else rows) * ROW_W
    return jnp.pad(flat, (0, total - n)).reshape(-1, ROW_W)


def kernel(x, c, positions, ada_w, ada_b, ln_g, ln_b, ffn_w_gate, ffn_w_up, ffn_w_down, hyb_w_in, hyb_w_out, attn_sink, hgrn_lb_logits, hgrn_norm_w, mla_w_down, mla_q_norm, mla_kv_norm, mla_w_uq, mla_w_ukv, mla_w_out, loss_target, m_ada_w, m_ada_b, m_ln_g, m_ln_b, m_ffn_w_gate, m_ffn_w_up, m_ffn_w_down, m_hyb_w_in, m_hyb_w_out, m_attn_sink, m_hgrn_lb_logits, m_hgrn_norm_w, m_mla_w_down, m_mla_q_norm, m_mla_kv_norm, m_mla_w_uq, m_mla_w_ukv, m_mla_w_out, v_ada_w, v_ada_b, v_ln_g, v_ln_b, v_ffn_w_gate, v_ffn_w_up, v_ffn_w_down, v_hyb_w_in, v_hyb_w_out, v_attn_sink, v_hgrn_lb_logits, v_hgrn_norm_w, v_mla_w_down, v_mla_q_norm, v_mla_kv_norm, v_mla_w_uq, v_mla_w_ukv, v_mla_w_out):
    weights = dict(zip(W_NAMES, (ada_w, ada_b, ln_g, ln_b, ffn_w_gate, ffn_w_up, ffn_w_down, hyb_w_in, hyb_w_out,
                                 attn_sink, hgrn_lb_logits, hgrn_norm_w, mla_w_down, mla_q_norm, mla_kv_norm,
                                 mla_w_uq, mla_w_ukv, mla_w_out)))
    mom1 = dict(zip(W_NAMES, (m_ada_w, m_ada_b, m_ln_g, m_ln_b, m_ffn_w_gate, m_ffn_w_up, m_ffn_w_down, m_hyb_w_in,
                              m_hyb_w_out, m_attn_sink, m_hgrn_lb_logits, m_hgrn_norm_w, m_mla_w_down, m_mla_q_norm,
                              m_mla_kv_norm, m_mla_w_uq, m_mla_w_ukv, m_mla_w_out)))
    mom2 = dict(zip(W_NAMES, (v_ada_w, v_ada_b, v_ln_g, v_ln_b, v_ffn_w_gate, v_ffn_w_up, v_ffn_w_down, v_hyb_w_in,
                              v_hyb_w_out, v_attn_sink, v_hgrn_lb_logits, v_hgrn_norm_w, v_mla_w_down, v_mla_q_norm,
                              v_mla_kv_norm, v_mla_w_uq, v_mla_w_ukv, v_mla_w_out)))
    bsz, seq, d = x.shape
    t = bsz * seq
    nb_tot = N_DEV * bsz
    me = 4 * lax.axis_index("x") + 2 * lax.axis_index("y") + lax.axis_index("c")

    c_all = all_gather(c, "gather_c").reshape(nb_tot, d)
    cond = silu_small(c_all)
    ada_b_mine = lax.dynamic_slice_in_dim(ada_b, me * MOD_SH, MOD_SH, axis=1)
    modp = jnp.concatenate([matmul(cond, ada_w[l], "nn", F32, "ada_fwd", bias=ada_b_mine[l])
                            for l in range(DEPTH)], axis=1)
    mod_rows = nb_tot * DEPTH * MOD_SH // ROW_W
    small1 = jnp.concatenate([_rows1024(modp), _rows1024(jnp.concatenate(
        [ln_g.reshape(-1), ln_b.reshape(-1), mla_q_norm.reshape(-1), mla_kv_norm.reshape(-1)]), rows=4)], axis=0)
    g1 = all_gather(small1, "gather_mod")
    mod_all = g1[:, :mod_rows].reshape(N_DEV, nb_tot, DEPTH, MOD_SH)
    mod_all = jnp.transpose(mod_all, (1, 2, 0, 3)).reshape(nb_tot, DEPTH, N_SUB, 3, d)
    mod = lax.dynamic_slice_in_dim(mod_all, me * bsz, bsz, axis=0)
    tail = g1[:, mod_rows:].reshape(N_DEV, -1)
    n_ln = DEPTH * N_SUB * LANES
    ln_g_full = jnp.transpose(tail[:, :n_ln].reshape(N_DEV, DEPTH, N_SUB, LANES), (1, 2, 0, 3)).reshape(DEPTH, N_SUB, d)
    ln_b_full = jnp.transpose(tail[:, n_ln:2 * n_ln].reshape(N_DEV, DEPTH, N_SUB, LANES), (1, 2, 0, 3)).reshape(DEPTH, N_SUB, d)
    qn_full = tail[:, 2 * n_ln:2 * n_ln + 32].reshape(C_Q_RANK)
    kvn_full = tail[:, 2 * n_ln + 32:2 * n_ln + 64].reshape(C_KV_RANK)

    def mvec(l, s, j):
        return mod[:, l, s, j].reshape(bsz, 1, d)

    packed = [pack_shards(layer, weights, BF16) for layer in range(DEPTH)]
    w = [unpack_full(0, all_gather(packed[0], "gather_weights_l0")), None]

    cq_tab, sg_tab = rope_tables(positions)
    lb0 = lb_fwd(hgrn_lb_logits, 0)
    aux = [dict(posc=positions.reshape(bsz, seq, 1),
                posr=jnp.pad(positions, ((0, 0), (A_BLOCK, A_BLOCK))).reshape(bsz, 1, seq + 2 * A_BLOCK),
                sink=attn_sink[0], lb=lb0.reshape(B_HEADS, 1, LANES), nw=hgrn_norm_w[0].reshape(B_HEADS, 1, LANES),
                next_packed=packed[1]),
           dict(qn=qn_full, kvn=kvn_full, cq=cq_tab, sg=sg_tab)]

    tape = []
    xin = x
    h = mod_fwd(x, mvec(0, 0, 1), mvec(0, 0, 0))
    for l in range(DEPTH):
        for s in range(N_SUB):
            h2 = h.reshape(t, d)
            if s != 1:
                y, saved = ffn_fwd(h2, w[l], s // 2)
            elif l == 0:
                y, saved, gathered = hybrid_fwd(h2, (bsz, seq), w[0], aux[0])
                w[1] = unpack_full(1, gathered)
            else:
                y, saved = mla_fwd(h2, (bsz, seq), w[1], aux[1])
            y = y.reshape(bsz, seq, d)
            last = l == DEPTH - 1 and s == N_SUB - 1
            ln, sn = (l, s + 1) if s + 1 < N_SUB else (l + 1, 0)
            nxt = None if last else (mvec(ln, sn, 1), mvec(ln, sn, 0))
            xn, hn = resln_fwd(xin, y, mvec(l, s, 2), ln_g_full[l, s], ln_b_full[l, s], RES_W[s], nxt)
            tape.append((l, s, xin, h2, y, saved))
            xin, h = xn, hn
    loss_part, gout = loss_fwd(xin, loss_target)
    loss = lax.psum(loss_part, ("x", "y", "c"))

    dmod = [[[None] * 3 for _ in range(N_SUB)] for _ in range(DEPTH)]
    dln_g = [[None] * N_SUB for _ in range(DEPTH)]
    dln_b = [[None] * N_SUB for _ in range(DEPTH)]
    big = [dict(wgu_t=[None] * FFN_PER_LAYER, wd=[None] * FFN_PER_LAYER) for _ in range(DEPTH)]
    small = {}
    mc = lax.axis_index("c")

    def outgoing(layer):
        return pack_full(layer, big[layer]).reshape(4, 2, PART_ROWS[layer], ROW_W)

    later = None
    for l, s, xs, h2, y, saved in reversed(tape):
        if later is None:
            dxa, dy, dgate, dg, db = resln_bwd(gout, xs, y, mvec(l, s, 2), ln_g_full[l, s], RES_W[s])
        else:
            ndxa, ndh, nl, ns = later
            dxa, dy, dgate, dg, db, dscale, dshift = resln_bwd(
                ndxa, xs, y, mvec(l, s, 2), ln_g_full[l, s], RES_W[s],
                upstream=(ndh, mvec(nl, ns, 1), ln_b_full[l, s]))
            dmod[nl][ns][0], dmod[nl][ns][1] = dshift, dscale
        dmod[l][s][2] = dgate
        dln_g[l][s], dln_b[l][s] = dg, db
        dy2 = dy.reshape(t, d)
        i = s // 2
        if s == 1 and l == 1:
            dh, dbig, dsmall = mla_bwd(dy2, h2, (bsz, seq), w[1], aux[1], saved)
        elif s == 1:
            dh, dbig, dsmall = hybrid_bwd(dy2, h2, (bsz, seq), w[0], aux[0], saved)
        elif l == 0 and s == 2:
            out1 = outgoing(1)
            dh, big[l]["wgu_t"][i], big[l]["wd"][i], swapped = ffn_bwd(dy2, h2, w[l], i, saved, swap=out1)
            aux[0]["pair_sums"] = pair_add(out1, swapped, mc)
        else:
            dh, big[l]["wgu_t"][i], big[l]["wd"][i], _ = ffn_bwd(dy2, h2, w[l], i, saved)
        if s == 1:
            big[l].update(dbig)
            small.update(dsmall)
        later = (dxa, dh.reshape(bsz, seq, d), l, s)
    grad_x, dmod[0][0][1], dmod[0][0][0] = mod_bwd(later[0], later[1], x, mvec(0, 0, 1))

    dmod_mine = jnp.stack([jnp.stack([jnp.concatenate(dmod[l][s], axis=1) for s in range(N_SUB)], axis=1)
                           for l in range(DEPTH)], axis=1)
    dmod_rows = bsz * DEPTH * MOD_W // ROW_W
    misc = jnp.concatenate([small["lb"].reshape(-1), small["nw"].reshape(-1), small["qn"].reshape(-1),
                            small["kvn"].reshape(-1), small["sink"].reshape(-1)[:A_HEADS]])
    small2 = jnp.concatenate([_rows1024(dmod_mine),
                              _rows1024(jnp.stack([jnp.stack(r) for r in dln_g])),
                              _rows1024(jnp.stack([jnp.stack(r) for r in dln_b])),
                              _rows1024(misc, rows=2)], axis=0)
    g2 = all_gather(small2, "gather_small_grads")
    dmod_all = g2[:, :dmod_rows].reshape(nb_tot, DEPTH * MOD_W // ROW_W, ROW_W)
    grad_ada_b = sum_parts(dmod_all, "sum_ada_b").reshape(DEPTH, MOD_W)
    dmod_cols = lax.dynamic_slice_in_dim(dmod_all.reshape(nb_tot, DEPTH, MOD_W), me * MOD_SH, MOD_SH, axis=2)
    grad_ada_w = jnp.stack([matmul(cond, dmod_cols[:, l], "tn", F32, "ada_dw") for l in range(DEPTH)])
    rest = sum_parts(g2[:, dmod_rows:], "sum_small")
    n6 = DEPTH * N_SUB
    gl_g = lax.dynamic_slice_in_dim(rest[:n6].reshape(DEPTH, N_SUB, d), me * LANES, LANES, axis=2)
    gl_b = lax.dynamic_slice_in_dim(rest[n6:2 * n6].reshape(DEPTH, N_SUB, d), me * LANES, LANES, axis=2)
    mrow = rest[2 * n6:].reshape(-1)
    dlb0 = mrow[:512].reshape(1, 512)
    g_nw = mrow[512:1024].reshape(1, B_HEADS, LANES)
    g_qn = lax.dynamic_slice_in_dim(mrow[1024:1280], me * 32, 32).reshape(1, 32)
    g_kvn = lax.dynamic_slice_in_dim(mrow[1280:1536], me * 32, 32).reshape(1, 32)
    g_sink = mrow[1536:1536 + A_HEADS].reshape(1, A_HEADS)
    g_lb = lb_bwd(hgrn_lb_logits, dlb0, 0)

    out0 = outgoing(0)
    pair0 = pair_add(out0, swap_with_sibling(out0, "exchange_d2d"), mc)
    reduced = [sum_parts(chip_all_to_all(pair0, "exchange_ici"), "sum_grads_l0"),
               sum_parts(aux[0]["exchanged"], "sum_grads_l1")]
    grads = unpack_shards(reduced)
    grads.update(ada_w=grad_ada_w, ada_b=grad_ada_b, ln_g=gl_g, ln_b=gl_b, attn_sink=g_sink,
                 hgrn_lb_logits=g_lb, hgrn_norm_w=g_nw, mla_q_norm=g_qn, mla_kv_norm=g_kvn)

    delta, new_m, new_v = {}, {}, {}
    for name in W_NAMES:
        if name in SMALL_NAMES:
            continue
        shp = weights[name].shape
        two_d = (-1, shp[-1])
        out = adamw(weights[name].reshape(two_d), grads[name].reshape(two_d), mom1[name].reshape(two_d),
                    mom2[name].reshape(two_d), "adamw_" + name)
        delta[name], new_m[name], new_v[name] = [o.reshape(shp) for o in out]

    def pack_small(src):
        flat = jnp.concatenate([src[n].reshape(-1) for n in SMALL_NAMES])
        return jnp.pad(flat, (0, -flat.shape[0] % (8 * LANES))).reshape(-1, LANES)

    outs = adamw(pack_small(weights), pack_small(grads), pack_small(mom1), pack_small(mom2), "adamw_small")
    off = 0
    for name in SMALL_NAMES:
        n = weights[name].size
        for dst, o in zip((delta, new_m, new_v), outs):
            dst[name] = o.reshape(-1)[off:off + n].reshape(weights[name].shape)
        off += n

    return (loss, grad_x, *[grads[n] for n in W_NAMES], *[delta[n] for n in W_NAMES],
            *[new_m[n] for n in W_NAMES], *[new_v[n] for n in W_NAMES])
```
